```python
import math
import jax, jax.numpy as jnp
from jax import lax
import numpy as np

D_MODEL = 2048
BATCH = 4
SEQ = 4096
DEPTH = 2

HEAD_DIM = 128
QBLK = 128
FOX_HEADS = 8
DIL_GROUPS = ((128, 1), (512, 4), (2048, 16))
DIL_HEADS_PER_GROUP = 4
DIL_HEADS = DIL_HEADS_PER_GROUP * len(DIL_GROUPS)
NSA_Q_HEADS = 8
NSA_KV_HEADS = 2
NSA_GQA = NSA_Q_HEADS // NSA_KV_HEADS
CMP_BLOCK = 32
CMP_STRIDE = 16
CMP_HIDDEN = 256
SLC_BLOCK = 64
SLC_COUNT = 16
SLC_QBLK = 64
WIN = 512
NSA_BRANCHES = 3
REL_BUCKETS = 32
REL_MAX_EXACT = 16
REL_MAX_DIST = 2048
N_BIAS_HEADS = DIL_HEADS + NSA_Q_HEADS
D_FF = 5632
N_EXPERTS = 8
TOP_K = 2
D_FF_EXPERT = 7168
N_DENSE = (DEPTH + 1) // 2
N_MOE = DEPTH // 2
N_BRANCHES = 3
RMS_EPS = 1e-6
W_FOX_QKV = 3 * FOX_HEADS * HEAD_DIM
W_FOX_F = FOX_HEADS
W_DIL_QKV = 3 * DIL_HEADS * HEAD_DIM
W_NSA_Q = NSA_Q_HEADS * HEAD_DIM
W_NSA_KV = NSA_BRANCHES * 2 * NSA_KV_HEADS * HEAD_DIM
W_NSA_G = NSA_BRANCHES * NSA_Q_HEADS
W_MERGE = N_BRANCHES * D_MODEL
IN_SPLITS = (W_FOX_QKV, W_FOX_F, W_DIL_QKV, W_NSA_Q, W_NSA_KV, W_NSA_G, W_MERGE)
D_IN = W_FOX_QKV + W_FOX_F + W_DIL_QKV + W_NSA_Q + W_NSA_KV + W_NSA_G + W_MERGE

kernel_name = "hybrid_fox_dilated_nsa_moe"


def rmsnorm(x, g):
    xf = x.astype(jnp.float32)
    inv = lax.rsqrt(jnp.mean(xf * xf, axis=-1, keepdims=True) + RMS_EPS)
    return (xf * inv).astype(x.dtype) * g


def masked_softmax(s, mask):
    s = jnp.where(mask, s.astype(jnp.float32), -jnp.inf)
    m = jnp.max(s, axis=-1, keepdims=True)
    m = jnp.where(jnp.isfinite(m), m, 0.0)
    e = jnp.exp(s - m)
    den = jnp.sum(e, axis=-1, keepdims=True)
    return e / jnp.where(den > 0, den, 1.0)


def t5_bucket(dist):
    dist = jnp.maximum(dist, 0)
    d = jnp.maximum(dist, 1).astype(jnp.float32)
    log_ratio = jnp.log(d / REL_MAX_EXACT) / math.log(REL_MAX_DIST / REL_MAX_EXACT)
    large = REL_MAX_EXACT + (log_ratio * (REL_BUCKETS - REL_MAX_EXACT)).astype(jnp.int32)
    large = jnp.minimum(large, REL_BUCKETS - 1)
    return jnp.where(dist < REL_MAX_EXACT, dist, large)


def split_columns(a, sizes):
    out, start = [], 0
    for n in sizes:
        out.append(a[..., start:start + n])
        start += n
    return out


def fox_attention(q, k, v, logf):
    B, S, H, D = q.shape
    nb = S // QBLK
    scale = HEAD_DIM ** -0.5
    c = jnp.cumsum(logf.astype(jnp.float32), axis=1).transpose(0, 2, 1)
    qb = q.reshape(B, nb, QBLK, H, D).transpose(1, 0, 2, 3, 4)
    cb = c.reshape(B, H, nb, QBLK).transpose(2, 0, 1, 3)
    kpos = jnp.arange(S)

    def block(args):
        i, qi, ci = args
        s = jnp.einsum("bqhd,bkhd->bhqk", qi, k).astype(jnp.float32) * scale
        s = s + (ci[..., :, None] - c[:, :, None, :])
        qpos = i * QBLK + jnp.arange(QBLK)
        p = masked_softmax(s, kpos[None, :] <= qpos[:, None])
        return jnp.einsum("bhqk,bkhd->bqhd", p.astype(v.dtype), v)

    o = lax.map(block, (jnp.arange(nb), qb, cb))
    return o.transpose(1, 0, 2, 3, 4).reshape(B, S, H * D)


def dilated_attention(q, k, v, rel_bias_b):
    B, S, H, D = q.shape
    G, Hg = len(DIL_GROUPS), DIL_HEADS_PER_GROUP
    nb = S // QBLK
    scale = HEAD_DIM ** -0.5
    qg = q.reshape(B, S, G, Hg, D)
    kg = k.reshape(B, S, G, Hg, D)
    vg = v.reshape(B, S, G, Hg, D)
    bias_g = rel_bias_b.reshape(REL_BUCKETS, G, Hg)
    offsets = [dil * np.arange(window // dil + 1) for window, dil in DIL_GROUPS]
    biases = [bias_g[t5_bucket(jnp.asarray(offsets[g])), g].T for g in range(G)]
    qb = qg.reshape(B, nb, QBLK, G, Hg, D).transpose(1, 0, 2, 3, 4, 5)

    def block(args):
        i, qi = args
        qpos = i * QBLK + jnp.arange(QBLK)
        outs, maxs, dens = [], [], []
        for g in range(G):
            idx = qpos[:, None] - jnp.asarray(offsets[g])[None, :]
            valid = idx >= 0
            idxc = jnp.maximum(idx, 0)
            kk = jnp.take(kg[:, :, g], idxc, axis=1)
            vv = jnp.take(vg[:, :, g], idxc, axis=1)
            s = jnp.einsum("bqhd,bqnhd->bhqn", qi[:, :, g], kk).astype(jnp.float32) * scale
            s = s + biases[g][None, :, None, :]
            s = jnp.where(valid[None, None], s, -jnp.inf)
            m = jnp.max(s, axis=-1, keepdims=True)
            e = jnp.exp(s - m)
            den = jnp.sum(e, axis=-1, keepdims=True)
            outs.append(jnp.einsum("bhqn,bqnhd->bqhd", (e / den).astype(v.dtype), vv))
            maxs.append(m[..., 0])
            dens.append(den[..., 0])
        m_all = jnp.stack(maxs, 0)
        w = jnp.stack(dens, 0) * jnp.exp(m_all - jnp.max(m_all, axis=0, keepdims=True))
        w = w / jnp.sum(w, axis=0, keepdims=True)
        return jnp.einsum("gbhq,gbqhd->bqhd", w.astype(v.dtype), jnp.stack(outs, 0))

    o = lax.map(block, (jnp.arange(nb), qb))
    return o.transpose(1, 0, 2, 3, 4).reshape(B, S, Hg * D)


def compress_tokens(x, pe, w1, w2):
    B, S, H, D = x.shape
    r = CMP_BLOCK // CMP_STRIDE
    nc = S // CMP_STRIDE - r + 1
    chunks = x.reshape(B, S // CMP_STRIDE, CMP_STRIDE, H, D)
    blocks = jnp.concatenate([chunks[:, j:j + nc] for j in range(r)], axis=2)
    blocks = blocks + pe[None, None, :, None, :]
    flat = blocks.transpose(0, 1, 3, 2, 4).reshape(B, nc, H, CMP_BLOCK * D)
    return jax.nn.gelu(flat @ w1) @ w2


def nsa_attention(q, k_cmp, v_cmp, k_slc, v_slc, k_win, v_win, gate_logits,
                  cmp_pe_k, cmp_w1_k, cmp_w2_k, cmp_pe_v, cmp_w1_v, cmp_w2_v, rel_bias_c):
    B, S, _, D = q.shape
    Hkv, G = NSA_KV_HEADS, NSA_GQA
    scale = HEAD_DIM ** -0.5
    pos = jnp.arange(S)
    qg = q.reshape(B, S, Hkv, G, D)
    bias_c = rel_bias_c.reshape(REL_BUCKETS, Hkv, G)

    kc = compress_tokens(k_cmp, cmp_pe_k, cmp_w1_k, cmp_w2_k)
    vc = compress_tokens(v_cmp, cmp_pe_v, cmp_w1_v, cmp_w2_v)
    nc = kc.shape[1]
    s_c = jnp.einsum("bshgd,bchd->bhgsc", qg, kc) * scale
    block_end = jnp.arange(nc) * CMP_STRIDE + CMP_BLOCK - 1
    p_cmp = masked_softmax(s_c, block_end[None, :] <= pos[:, None])
    o_cmp = jnp.einsum("bhgsc,bchd->bshgd", p_cmp.astype(vc.dtype), vc)

    n_slc = S // SLC_BLOCK
    ratio, n_inner = SLC_BLOCK // CMP_STRIDE, CMP_BLOCK // CMP_STRIDE
    off = np.array([m - n for m in range(ratio) for n in range(n_inner)])
    cidx = ratio * np.arange(n_slc)[:, None] + off[None, :]
    cvalid = (cidx >= 0) & (cidx < nc)
    imp = jnp.sum(p_cmp, axis=2)
    p_slc = jnp.sum(jnp.where(cvalid, jnp.take(imp, np.clip(cidx, 0, nc - 1), axis=-1), 0.0), axis=-1)
    blk = jnp.arange(n_slc)
    cur = pos // SLC_BLOCK
    forced = (blk[None, :] == 0) | (blk[None, :] == cur[:, None]) | (blk[None, :] == cur[:, None] - 1)
    allowed = blk[None, :] <= cur[:, None]
    score = jnp.where(forced, jnp.inf, jnp.where(allowed, p_slc, -jnp.inf))
    n_sel = min(SLC_COUNT, n_slc)
    _, sel = lax.top_k(score, n_sel)

    kb = k_slc.reshape(B, n_slc, SLC_BLOCK, Hkv, D).transpose(0, 3, 1, 2, 4)
    vb = v_slc.reshape(B, n_slc, SLC_BLOCK, Hkv, D).transpose(0, 3, 1, 2, 4)
    nq = S // SLC_QBLK
    qsb = qg.reshape(B, nq, SLC_QBLK, Hkv, G, D).transpose(1, 0, 3, 4, 2, 5)
    selb = sel.reshape(B, Hkv, nq, SLC_QBLK, n_sel).transpose(2, 0, 1, 3, 4)
    bi = jnp.arange(B)[:, None, None, None]
    hi = jnp.arange(Hkv)[None, :, None, None]
    n_keys = n_sel * SLC_BLOCK

    def slc_block(args):
        i, qi, si = args
        kk = kb[bi, hi, si].reshape(B, Hkv, SLC_QBLK, n_keys, D)
        vv = vb[bi, hi, si].reshape(B, Hkv, SLC_QBLK, n_keys, D)
        s = jnp.einsum("bhgqd,bhqnd->bhgqn", qi, kk).astype(jnp.float32) * scale
        qpos = i * SLC_QBLK + jnp.arange(SLC_QBLK)
        kpos = (si[..., None] * SLC_BLOCK + jnp.arange(SLC_BLOCK)).reshape(B, Hkv, SLC_QBLK, n_keys)
        dist = qpos[None, None, :, None] - kpos
        s = s + bias_c[t5_bucket(dist), hi].transpose(0, 1, 4, 2, 3)
        p = masked_softmax(s, (dist >= 0)[:, :, None])
        return jnp.einsum("bhgqn,bhqnd->bqhgd", p.astype(vv.dtype), vv)

    o_slc = lax.map(slc_block, (jnp.arange(nq), qsb, selb))
    o_slc = o_slc.transpose(1, 0, 2, 3, 4, 5).reshape(B, S, Hkv, G, D)

    span = WIN + QBLK
    kp = jnp.pad(k_win, ((0, 0), (WIN, 0), (0, 0), (0, 0)))
    vp = jnp.pad(v_win, ((0, 0), (WIN, 0), (0, 0), (0, 0)))
    nb = S // QBLK
    qwb = qg.reshape(B, nb, QBLK, Hkv, G, D).transpose(1, 0, 2, 3, 4, 5)
    kloc = jnp.arange(span)
    wdist = jnp.arange(QBLK)[:, None] + WIN - kloc[None, :]
    in_win = (wdist >= 0) & (wdist < WIN)
    bias_w = bias_c[t5_bucket(wdist)].transpose(2, 3, 0, 1)

    def win_block(args):
        i, qi = args
        kk = lax.dynamic_slice_in_dim(kp, i * QBLK, span, axis=1)
        vv = lax.dynamic_slice_in_dim(vp, i * QBLK, span, axis=1)
        s = jnp.einsum("bqhgd,bkhd->bhgqk", qi, kk).astype(jnp.float32) * scale + bias_w
        kpos = i * QBLK - WIN + kloc
        p = masked_softmax(s, in_win & (kpos >= 0)[None, :])
        return jnp.einsum("bhgqk,bkhd->bqhgd", p.astype(vv.dtype), vv)

    o_win = lax.map(win_block, (jnp.arange(nb), qwb))
    o_win = o_win.transpose(1, 0, 2, 3, 4, 5).reshape(B, S, Hkv, G, D)

    g = jax.nn.sigmoid(gate_logits).reshape(B, S, NSA_BRANCHES, Hkv, G)[..., None]
    o = g[:, :, 0] * o_cmp + g[:, :, 1] * o_slc + g[:, :, 2] * o_win
    return o.reshape(B, S, NSA_Q_HEADS * D)


def mixing_layer(h, w_in, fox_forget_bias, cmp_pe_k, cmp_w1_k, cmp_w2_k, cmp_pe_v, cmp_w1_v, cmp_w2_v,
                 w_branch_a, w_branch_b, w_branch_c, w_out, rel_bias):
    B, S, _ = h.shape
    D = HEAD_DIM
    fox_qkv, fox_f, dil_qkv, nsa_q, nsa_kv, nsa_g, merge_g = split_columns(h @ w_in, IN_SPLITS)
    qkv_a = fox_qkv.reshape(B, S, 3, FOX_HEADS, D)
    logf = jax.nn.log_sigmoid((fox_f + fox_forget_bias).astype(jnp.float32))
    y_a = fox_attention(qkv_a[:, :, 0], qkv_a[:, :, 1], qkv_a[:, :, 2], logf)
    qkv_b = dil_qkv.reshape(B, S, 3, DIL_HEADS, D)
    y_b = dilated_attention(qkv_b[:, :, 0], qkv_b[:, :, 1], qkv_b[:, :, 2], rel_bias[:, :DIL_HEADS])
    kv = nsa_kv.reshape(B, S, NSA_BRANCHES, 2, NSA_KV_HEADS, D)
    y_c = nsa_attention(nsa_q.reshape(B, S, NSA_Q_HEADS, D),
                        kv[:, :, 0, 0], kv[:, :, 0, 1], kv[:, :, 1, 0], kv[:, :, 1, 1],
                        kv[:, :, 2, 0], kv[:, :, 2, 1],
                        nsa_g.reshape(B, S, NSA_BRANCHES, NSA_Q_HEADS),
                        cmp_pe_k, cmp_w1_k, cmp_w2_k, cmp_pe_v, cmp_w1_v, cmp_w2_v,
                        rel_bias[:, DIL_HEADS:])
    gates = jax.nn.sigmoid(merge_g.reshape(B, S, N_BRANCHES, D_MODEL))
    merged = (gates[:, :, 0] * (y_a @ w_branch_a) + gates[:, :, 1] * (y_b @ w_branch_b)
              + gates[:, :, 2] * (y_c @ w_branch_c))
    return merged @ w_out


def swiglu(h, wg, wu, wd):
    return (jax.nn.silu(h @ wg) * (h @ wu)) @ wd


def moe_ffn(h, router, wg, wu, wd):
    logits = (h @ router).astype(jnp.float32)
    vals, idx = lax.top_k(logits, TOP_K)
    w = jax.nn.softmax(vals, axis=-1)
    combine = jnp.sum(jax.nn.one_hot(idx, N_EXPERTS, dtype=jnp.float32) * w[..., None], axis=-2)
    out = jnp.zeros_like(h)
    for e in range(N_EXPERTS):
        out = out + combine[..., e:e + 1].astype(h.dtype) * swiglu(h, wg[e], wu[e], wd[e])
    return out


def setup_inputs(seed: int = 0) -> dict:
    key = jax.random.key(seed)
    ks = jax.random.split(key, 24)
    f32 = jnp.float32
    D = D_MODEL

    def nrm(k, shape):
        return jax.random.normal(k, shape, f32)

    def w(k, shape, fan_in):
        return nrm(k, shape) * (fan_in ** -0.5)

    return {
        "x": nrm(ks[0], (BATCH, SEQ, D)),
        "rel_bias": 0.3 * nrm(ks[1], (REL_BUCKETS, N_BIAS_HEADS)),
        "norm_mix_g": 1.0 + 0.02 * nrm(ks[2], (DEPTH, D)),
        "norm_ffn_g": 1.0 + 0.02 * nrm(ks[3], (DEPTH, D)),
        "norm_final_g": 1.0 + 0.02 * nrm(ks[4], (D,)),
        "w_in": w(ks[5], (DEPTH, D, D_IN), D),
        "fox_forget_bias": 3.0 + 0.5 * nrm(ks[6], (DEPTH, FOX_HEADS)),
        "cmp_pe_k": 0.1 * nrm(ks[7], (DEPTH, CMP_BLOCK, HEAD_DIM)),
        "cmp_w1_k": w(ks[8], (DEPTH, CMP_BLOCK * HEAD_DIM, CMP_HIDDEN), CMP_BLOCK * HEAD_DIM),
        "cmp_w2_k": w(ks[9], (DEPTH, CMP_HIDDEN, HEAD_DIM), CMP_HIDDEN),
        "cmp_pe_v": 0.1 * nrm(ks[10], (DEPTH, CMP_BLOCK, HEAD_DIM)),
        "cmp_w1_v": w(ks[11], (DEPTH, CMP_BLOCK * HEAD_DIM, CMP_HIDDEN), CMP_BLOCK * HEAD_DIM),
        "cmp_w2_v": w(ks[12], (DEPTH, CMP_HIDDEN, HEAD_DIM), CMP_HIDDEN),
        "w_branch_a": w(ks[13], (DEPTH, FOX_HEADS * HEAD_DIM, D), FOX_HEADS * HEAD_DIM),
        "w_branch_b": w(ks[14], (DEPTH, DIL_HEADS_PER_GROUP * HEAD_DIM, D), DIL_HEADS_PER_GROUP * HEAD_DIM),
        "w_branch_c": w(ks[15], (DEPTH, NSA_Q_HEADS * HEAD_DIM, D), NSA_Q_HEADS * HEAD_DIM),
        "w_out": w(ks[16], (DEPTH, D, D), D),
        "ffn_w_gate": w(ks[17], (N_DENSE, D, D_FF), D),
        "ffn_w_up": w(ks[18], (N_DENSE, D, D_FF), D),
        "ffn_w_down": w(ks[19], (N_DENSE, D_FF, D), D_FF),
        "moe_router": w(ks[20], (N_MOE, D, N_EXPERTS), D),
        "moe_w_gate": w(ks[21], (N_MOE, N_EXPERTS, D, D_FF_EXPERT), D),
        "moe_w_up": w(ks[22], (N_MOE, N_EXPERTS, D, D_FF_EXPERT), D),
        "moe_w_down": w(ks[23], (N_MOE, N_EXPERTS, D_FF_EXPERT, D), D_FF_EXPERT),
    }


def reference(x, rel_bias, norm_mix_g, norm_ffn_g, norm_final_g, w_in, fox_forget_bias,
              cmp_pe_k, cmp_w1_k, cmp_w2_k, cmp_pe_v, cmp_w1_v, cmp_w2_v,
              w_branch_a, w_branch_b, w_branch_c, w_out,
              ffn_w_gate, ffn_w_up, ffn_w_down, moe_router, moe_w_gate, moe_w_up, moe_w_down):
    for l in range(DEPTH):
        h = rmsnorm(x, norm_mix_g[l])
        x = x + mixing_layer(h, w_in[l], fox_forget_bias[l], cmp_pe_k[l], cmp_w1_k[l], cmp_w2_k[l],
                             cmp_pe_v[l], cmp_w1_v[l], cmp_w2_v[l], w_branch_a[l], w_branch_b[l],
                             w_branch_c[l], w_out[l], rel_bias)
        h = rmsnorm(x, norm_ffn_g[l])
        if l % 2 == 0:
            j = l // 2
            x = x + swiglu(h, ffn_w_gate[j], ffn_w_up[j], ffn_w_down[j])
        else:
            j = l // 2
            x = x + moe_ffn(h, moe_router[j], moe_w_gate[j], moe_w_up[j], moe_w_down[j])
    return rmsnorm(x, norm_final_g)
```

```python
import functools
import math

import jax
import jax.numpy as jnp
import numpy as np
from jax import lax
from jax.experimental import pallas as pl
from jax.experimental.pallas import tpu as pltpu

F32 = jnp.float32
BF16 = jnp.bfloat16
HIGHEST = lax.Precision.HIGHEST
NEG_INF = float("-inf")

LANES = 128
HEAD_DIM = 128
ATTN_SCALE = HEAD_DIM ** -0.5
RMS_EPS = 1e-6
VMEM_LIMIT_BYTES = 56 * 1024 * 1024

FOX_HEADS = 8
DIL_GROUPS = ((128, 1), (512, 4), (2048, 16))
DIL_HEADS_PER_GROUP = 4
DIL_HEADS = DIL_HEADS_PER_GROUP * len(DIL_GROUPS)
DIL_TAPS = 128
NSA_Q_HEADS = 8
NSA_KV_HEADS = 2
NSA_GQA = NSA_Q_HEADS // NSA_KV_HEADS
NSA_BRANCHES = 3
CMP_BLOCK = 32
CMP_STRIDE = 16
CMP_HIDDEN = 256
SLC_BLOCK = 64
SLC_COUNT = 16
WIN = 512
REL_BUCKETS = 32
REL_MAX_EXACT = 16
REL_MAX_DIST = 2048
N_EXPERTS = 8
TOP_K = 2
N_BRANCHES = 3

CB_FOX = 0
CB_DIL = CB_FOX + 3 * FOX_HEADS
CB_NSAQ = CB_DIL + 3 * DIL_HEADS
CB_NSAKV = CB_NSAQ + NSA_Q_HEADS
CB_MERGE = CB_NSAKV + NSA_BRANCHES * 2 * NSA_KV_HEADS
SMALL_FOX_F = 0
SMALL_NSA_G = FOX_HEADS


def _cparams(semantics):
    return pltpu.CompilerParams(dimension_semantics=semantics,
                                vmem_limit_bytes=VMEM_LIMIT_BYTES)


def _t5_bucket_np(dist):
    dist = np.maximum(dist, 0)
    d = np.maximum(dist, 1).astype(np.float32)
    log_ratio = np.log(d / np.float32(REL_MAX_EXACT)) / np.float32(math.log(REL_MAX_DIST / REL_MAX_EXACT))
    large = REL_MAX_EXACT + (log_ratio * np.float32(REL_BUCKETS - REL_MAX_EXACT)).astype(np.int32)
    large = np.minimum(large, REL_BUCKETS - 1)
    return np.where(dist < REL_MAX_EXACT, dist, large).astype(np.int32)


def _rms_rows(x, g):
    inv = lax.rsqrt(jnp.mean(x * x, axis=-1, keepdims=True) + RMS_EPS)
    return (x * inv) * g


def _rms_proj_kernel(x_ref, g_ref, w_ref, o_ref, h_ref, *, ncb):
    @pl.when(pl.program_id(1) == 0)
    def _():
        h_ref[...] = _rms_rows(x_ref[...], g_ref[...]).astype(BF16)

    res = jnp.dot(h_ref[...], w_ref[...], preferred_element_type=F32)
    for c in range(ncb):
        o_ref[c] = res[:, c * LANES:(c + 1) * LANES].astype(o_ref.dtype)


def rms_proj(x, g, w, out_dtype, *, tm, tn):
    T, D = x.shape
    N = w.shape[1]
    tm, tn = min(tm, T), min(tn, N)
    ncb = tn // LANES
    return pl.pallas_call(
        functools.partial(_rms_proj_kernel, ncb=ncb),
        grid=(T // tm, N // tn),
        in_specs=[pl.BlockSpec((tm, D), lambda i, j: (i, 0)),
                  pl.BlockSpec((1, D), lambda i, j: (0, 0)),
                  pl.BlockSpec((D, tn), lambda i, j: (0, j))],
        out_specs=pl.BlockSpec((ncb, tm, LANES), lambda i, j: (j, i, 0)),
        out_shape=jax.ShapeDtypeStruct((N // LANES, T, LANES), out_dtype),
        scratch_shapes=[pltpu.VMEM((tm, D), BF16)],
        compiler_params=_cparams(("parallel", "arbitrary")),
        name="rms_proj",
    )(x, g, w)


CUMSUM_BLOCK = 256


def _logf_cumsum_kernel(f_ref, b_ref, c_ref, *, nblk):
    row = lax.broadcasted_iota(jnp.int32, (CUMSUM_BLOCK, CUMSUM_BLOCK), 0)
    col = lax.broadcasted_iota(jnp.int32, (CUMSUM_BLOCK, CUMSUM_BLOCK), 1)
    tri = jnp.where(col <= row, 1.0, 0.0).astype(F32)

    def body(i, carry):
        sl = pl.ds(pl.multiple_of(i * CUMSUM_BLOCK, CUMSUM_BLOCK), CUMSUM_BLOCK)
        z = f_ref[sl, :] + b_ref[...]
        logf = jnp.minimum(z, 0.0) - jnp.log1p(jnp.exp(-jnp.abs(z)))
        cs = jnp.dot(tri, logf, preferred_element_type=F32, precision=HIGHEST) + carry
        c_ref[sl, :] = cs
        return cs[CUMSUM_BLOCK - 1:CUMSUM_BLOCK, :]

    lax.fori_loop(0, nblk, body, jnp.zeros((1, LANES), F32))


def logf_cumsum(small, bias_row, B, S):
    T = B * S
    return pl.pallas_call(
        functools.partial(_logf_cumsum_kernel, nblk=S // CUMSUM_BLOCK),
        grid=(B,),
        in_specs=[pl.BlockSpec((S, LANES), lambda b: (b, 0)),
                  pl.BlockSpec((1, LANES), lambda b: (0, 0))],
        out_specs=pl.BlockSpec((S, LANES), lambda b: (b, 0)),
        out_shape=jax.ShapeDtypeStruct((T, LANES), F32),
        compiler_params=_cparams(("parallel",)),
        name="logf_cumsum",
    )(small, bias_row)


def _flash_update(s, v, m_ref, l_ref, acc_ref):
    m_old = m_ref[...]
    m_new = jnp.maximum(m_old, jnp.max(s, axis=-1, keepdims=True))
    m_safe = jnp.where(m_new == NEG_INF, 0.0, m_new)
    alpha = jnp.exp(m_old - m_safe)
    p = jnp.exp(s - m_safe)
    l_ref[...] = alpha * l_ref[...] + jnp.sum(p, axis=-1, keepdims=True)
    acc_ref[...] = alpha * acc_ref[...] + jnp.dot(p.astype(BF16), v, preferred_element_type=F32)
    m_ref[...] = m_new


def _flash_init(m_ref, l_ref, acc_ref):
    m_ref[...] = jnp.full(m_ref.shape, NEG_INF, F32)
    l_ref[...] = jnp.zeros(l_ref.shape, F32)
    acc_ref[...] = jnp.zeros(acc_ref.shape, F32)


def _flash_result(l_ref, acc_ref):
    l = l_ref[...]
    return acc_ref[...] / jnp.where(l > 0.0, l, 1.0)


def _qk(q, k):
    return lax.dot_general(q, k, (((1,), (1,)), ((), ())), preferred_element_type=F32)


def _fox_kernel(qi_ref, ki_ref, q_ref, k_ref, v_ref, cq_ref, ck_ref, o_ref, m_ref, l_ref, acc_ref, *, tq, tk):
    p = pl.program_id(2)
    qi, ki = qi_ref[p], ki_ref[p]

    @pl.when(ki == 0)
    def _():
        _flash_init(m_ref, l_ref, acc_ref)

    s = _qk(q_ref[...], k_ref[...]) * ATTN_SCALE + (cq_ref[...] - ck_ref[...])
    qpos = qi * tq + lax.broadcasted_iota(jnp.int32, (tq, tk), 0)
    kpos = ki * tk + lax.broadcasted_iota(jnp.int32, (tq, tk), 1)
    s = jnp.where(kpos <= qpos, s, NEG_INF)
    _flash_update(s, v_ref[...], m_ref, l_ref, acc_ref)

    @pl.when(ki == ((qi + 1) * tq - 1) // tk)
    def _():
        o_ref[...] = _flash_result(l_ref, acc_ref).astype(o_ref.dtype)


def _causal_pairs(nq, tq, tk):
    qs, ks = [], []
    for qi in range(nq):
        for ki in range(((qi + 1) * tq - 1) // tk + 1):
            qs.append(qi)
            ks.append(ki)
    return jnp.asarray(qs, jnp.int32), jnp.asarray(ks, jnp.int32)


def fox_attention(proj, c_col, c_row, B, S, *, tq=512, tk=512):
    tq, tk = min(tq, S), min(tk, S)
    nq, nk = S // tq, S // tk
    H = FOX_HEADS
    T = B * S
    qis, kis = _causal_pairs(nq, tq, tk)
    grid_spec = pltpu.PrefetchScalarGridSpec(
        num_scalar_prefetch=2,
        grid=(B, H, int(qis.shape[0])),
        in_specs=[
            pl.BlockSpec((None, tq, LANES), lambda b, h, p, qi, ki: (CB_FOX + h, b * nq + qi[p], 0)),
            pl.BlockSpec((None, tk, LANES), lambda b, h, p, qi, ki: (CB_FOX + H + h, b * nk + ki[p], 0)),
            pl.BlockSpec((None, tk, LANES), lambda b, h, p, qi, ki: (CB_FOX + 2 * H + h, b * nk + ki[p], 0)),
            pl.BlockSpec((None, tq, 1), lambda b, h, p, qi, ki: (h, b * nq + qi[p], 0)),
            pl.BlockSpec((None, 1, tk), lambda b, h, p, qi, ki: (h, 0, b * nk + ki[p])),
        ],
        out_specs=pl.BlockSpec((None, tq, LANES), lambda b, h, p, qi, ki: (h, b * nq + qi[p], 0)),
        scratch_shapes=[pltpu.VMEM((tq, 1), F32), pltpu.VMEM((tq, 1), F32), pltpu.VMEM((tq, LANES), F32)],
    )
    return pl.pallas_call(
        functools.partial(_fox_kernel, tq=tq, tk=tk),
        grid_spec=grid_spec,
        out_shape=jax.ShapeDtypeStruct((H, T, LANES), BF16),
        compiler_params=_cparams(("parallel", "parallel", "arbitrary")),
        name="fox_attention",
    )(qis, kis, proj, proj, proj, c_col, c_row)


def _dil_kernel(q_ref, kp_ref, kc_ref, vp_ref, vc_ref, bias_ref, o_ref, lse_ref, *, dil, tu):
    ui = pl.program_id(2)
    kcol = lax.broadcasted_iota(jnp.int32, (tu, 2 * tu), 1)
    first_ok = jnp.logical_or(ui > 0, kcol >= tu)
    bias = jnp.where(first_ok, bias_ref[...], NEG_INF)
    for r in range(dil):
        sl = slice(r * LANES, (r + 1) * LANES)
        k = jnp.concatenate([kp_ref[:, sl], kc_ref[:, sl]], axis=0)
        v = jnp.concatenate([vp_ref[:, sl], vc_ref[:, sl]], axis=0)
        s = _qk(q_ref[:, sl], k) * ATTN_SCALE + bias
        m = jnp.max(s, axis=-1, keepdims=True)
        e = jnp.exp(s - m)
        den = jnp.sum(e, axis=-1, keepdims=True)
        o = jnp.dot(e.astype(BF16), v, preferred_element_type=F32) / den
        o_ref[:, sl] = o.astype(o_ref.dtype)
        lse_ref[:, sl] = jnp.broadcast_to(m + jnp.log(den), (tu, LANES))


def dilated_group(proj, bias_tbl, group, B, S):
    dil = DIL_GROUPS[group][1]
    tu = DIL_TAPS
    Hg = DIL_HEADS_PER_GROUP
    T = B * S
    nu = S // dil // tu
    ncb = proj.shape[0]
    view = proj.reshape(ncb, T // dil, dil * LANES)
    cq = CB_DIL + group * Hg
    ck = cq + DIL_HEADS
    cv = ck + DIL_HEADS
    blk = (None, tu, dil * LANES)
    prev = lambda c: (lambda b, h, u: (c + h, b * nu + jnp.maximum(u - 1, 0), 0))
    cur = lambda c: (lambda b, h, u: (c + h, b * nu + u, 0))
    o, lse = pl.pallas_call(
        functools.partial(_dil_kernel, dil=dil, tu=tu),
        grid=(B, Hg, nu),
        in_specs=[pl.BlockSpec(blk, cur(cq)),
                  pl.BlockSpec(blk, prev(ck)), pl.BlockSpec(blk, cur(ck)),
                  pl.BlockSpec(blk, prev(cv)), pl.BlockSpec(blk, cur(cv)),
                  pl.BlockSpec((None, tu, 2 * tu), lambda b, h, u: (h, 0, 0))],
        out_specs=[pl.BlockSpec(blk, cur(0)), pl.BlockSpec(blk, cur(0))],
        out_shape=[jax.ShapeDtypeStruct((Hg, T // dil, dil * LANES), BF16),
                   jax.ShapeDtypeStruct((Hg, T // dil, dil * LANES), F32)],
        compiler_params=_cparams(("parallel", "parallel", "arbitrary")),
        name=f"dilated_group{group}",
    )(view, view, view, view, view, bias_tbl)
    return o.reshape(Hg, T, LANES), lse.reshape(Hg, T, LANES)


def dilated_bias_tables(rel_bias_b):
    tu = DIL_TAPS
    i = np.arange(tu)[:, None]
    j = np.arange(2 * tu)[None, :]
    taps = tu + i - j
    valid = (taps >= 0) & (taps <= DIL_TAPS)
    tables = []
    for g, (_, dil) in enumerate(DIL_GROUPS):
        bucket = _t5_bucket_np(np.where(valid, taps, 0) * dil)
        vals = rel_bias_b[:, g * DIL_HEADS_PER_GROUP:(g + 1) * DIL_HEADS_PER_GROUP][bucket]
        vals = jnp.where(valid[:, :, None], vals, NEG_INF)
        tables.append(jnp.transpose(vals, (2, 0, 1)).astype(F32))
    return tables


def _compress_kernel(x_ref, pelo_ref, pehi_ref, w1a_ref, w1b_ref, w2_ref, o_ref, *, nchunk):
    x = x_ref[...].astype(F32)
    u0 = jnp.dot((x + pelo_ref[...]).astype(BF16), w1a_ref[...], preferred_element_type=F32)
    u1 = jnp.dot((x + pehi_ref[...]).astype(BF16), w1b_ref[...], preferred_element_type=F32)
    pre = u0 + pltpu.roll(u1, nchunk - 1, 0)
    hid = jax.nn.gelu(pre)
    out = jnp.dot(hid.astype(BF16), w2_ref[...], preferred_element_type=F32)
    row = lax.broadcasted_iota(jnp.int32, out.shape, 0)
    o_ref[...] = jnp.where(row < nchunk - 1, out, 0.0).astype(o_ref.dtype)


def nsa_compress(proj, pe, w1, w2, B, S):
    nchunk = S // CMP_STRIDE
    half = CMP_STRIDE * HEAD_DIM
    Hkv = NSA_KV_HEADS
    x = proj[CB_NSAKV:CB_NSAKV + 2 * Hkv].reshape(2, Hkv, B, nchunk, half)
    pe_lo = pe[:, :CMP_STRIDE].reshape(2, 1, half)
    pe_hi = pe[:, CMP_STRIDE:].reshape(2, 1, half)
    return pl.pallas_call(
        functools.partial(_compress_kernel, nchunk=nchunk),
        grid=(2, B, Hkv),
        in_specs=[pl.BlockSpec((None, None, None, nchunk, half), lambda t, b, h: (t, h, b, 0, 0)),
                  pl.BlockSpec((None, 1, half), lambda t, b, h: (t, 0, 0)),
                  pl.BlockSpec((None, 1, half), lambda t, b, h: (t, 0, 0)),
                  pl.BlockSpec((None, half, CMP_HIDDEN), lambda t, b, h: (t, 0, 0)),
                  pl.BlockSpec((None, half, CMP_HIDDEN), lambda t, b, h: (t, 1, 0)),
                  pl.BlockSpec((None, CMP_HIDDEN, HEAD_DIM), lambda t, b, h: (t, 0, 0))],
        out_specs=pl.BlockSpec((None, None, None, nchunk, HEAD_DIM), lambda t, b, h: (t, b, h, 0, 0)),
        out_shape=jax.ShapeDtypeStruct((2, B, Hkv, nchunk, HEAD_DIM), BF16),
        compiler_params=_cparams(("parallel", "parallel", "parallel")),
        name="nsa_compress",
    )(x, pe_lo, pe_hi, w1, w1, w2)


def _masked_softmax(s, mask, axis):
    s = jnp.where(mask, s, NEG_INF)
    m = jnp.max(s, axis=axis, keepdims=True)
    m = jnp.where(m == NEG_INF, 0.0, m)
    e = jnp.exp(s - m)
    den = jnp.sum(e, axis=axis, keepdims=True)
    return e / jnp.where(den > 0.0, den, 1.0)


def _cmp_select_kernel(q_ref, kc_ref, vc_ref, mt_ref, o_ref, sel_ref, *, tq, nchunk, n_slc, n_sel):
    q0 = pl.program_id(2) * tq
    kc = kc_ref[...]
    vc = vc_ref[...]
    pos_r = q0 + lax.broadcasted_iota(jnp.int32, (tq, nchunk), 0)
    end_r = lax.broadcasted_iota(jnp.int32, (tq, nchunk), 1) * CMP_STRIDE + (CMP_BLOCK - 1)
    vis_r = end_r <= pos_r
    pos_c = q0 + lax.broadcasted_iota(jnp.int32, (nchunk, tq), 1)
    end_c = lax.broadcasted_iota(jnp.int32, (nchunk, tq), 0) * CMP_STRIDE + (CMP_BLOCK - 1)
    vis_c = end_c <= pos_c
    imp = jnp.zeros((nchunk, tq), F32)
    for g in range(NSA_GQA):
        q = q_ref[g]
        p = _masked_softmax(_qk(q, kc) * ATTN_SCALE, vis_r, -1)
        o_ref[g] = jnp.dot(p.astype(BF16), vc, preferred_element_type=F32).astype(o_ref.dtype)
        imp = imp + _masked_softmax(_qk(kc, q) * ATTN_SCALE, vis_c, 0)
    p_slc = jnp.dot(mt_ref[...], imp, preferred_element_type=F32, precision=HIGHEST)

    blk = lax.broadcasted_iota(jnp.int32, (n_slc, tq), 0)
    cur = (q0 + lax.broadcasted_iota(jnp.int32, (n_slc, tq), 1)) // SLC_BLOCK
    forced = (blk == 0) | (blk == cur) | (blk == cur - 1)
    allowed = blk <= cur
    score = jnp.where(forced, 1e30, jnp.where(allowed, p_slc, -1.0))
    chosen = jnp.zeros((n_slc, tq), F32)
    for _ in range(n_sel):
        top = jnp.max(score, axis=0, keepdims=True)
        first = jnp.min(jnp.where(score == top, blk, n_slc), axis=0, keepdims=True)
        hit = blk == first
        chosen = jnp.where(hit, 1.0, chosen)
        score = jnp.where(hit, -2.0, score)
    sel_ref[...] = jnp.where(allowed, chosen, 0.0)


def nsa_cmp_select(proj, kvc, B, S, *, tq=512):
    tq = min(tq, S)
    nq = S // tq
    T = B * S
    nchunk = S // CMP_STRIDE
    n_slc = S // SLC_BLOCK
    n_sel = min(SLC_COUNT, n_slc)
    ratio, n_inner = SLC_BLOCK // CMP_STRIDE, CMP_BLOCK // CMP_STRIDE
    mt = np.zeros((n_slc, nchunk), np.float32)
    for j in range(n_slc):
        for m in range(ratio):
            for n in range(n_inner):
                c = ratio * j + m - n
                if 0 <= c < nchunk - 1:
                    mt[j, c] += 1.0
    G = NSA_GQA
    kv_spec = lambda t: pl.BlockSpec((None, None, None, nchunk, HEAD_DIM), lambda b, h, i: (t, b, h, 0, 0))
    return pl.pallas_call(
        functools.partial(_cmp_select_kernel, tq=tq, nchunk=nchunk, n_slc=n_slc, n_sel=n_sel),
        grid=(B, NSA_KV_HEADS, nq),
        in_specs=[pl.BlockSpec((G, tq, LANES), lambda b, h, i: (CB_NSAQ // G + h, b * nq + i, 0)),
                  kv_spec(0), kv_spec(1),
                  pl.BlockSpec((n_slc, nchunk), lambda b, h, i: (0, 0))],
        out_specs=[pl.BlockSpec((G, tq, LANES), lambda b, h, i: (h, b * nq + i, 0)),
                   pl.BlockSpec((None, None, n_slc, tq), lambda b, h, i: (b, h, 0, i))],
        out_shape=[jax.ShapeDtypeStruct((NSA_Q_HEADS, T, LANES), BF16),
                   jax.ShapeDtypeStruct((B, NSA_KV_HEADS, n_slc, S), F32)],
        compiler_params=_cparams(("parallel", "parallel", "arbitrary")),
        name="nsa_cmp_select",
    )(proj, kvc, kvc, jnp.asarray(mt))


def _slc_kernel(qi_ref, ki_ref, q_ref, k_ref, v_ref, sel_ref, bias_ref, o_ref, m_ref, l_ref, acc_ref,
                *, tq, tk, n_slc):
    p = pl.program_id(2)
    qi, ki = qi_ref[p], ki_ref[p]

    @pl.when(ki == 0)
    def _():
        _flash_init(m_ref, l_ref, acc_ref)

    blk_of_key = ki * (tk // SLC_BLOCK) + lax.broadcasted_iota(jnp.int32, (n_slc, tk), 1) // SLC_BLOCK
    expand = jnp.where(lax.broadcasted_iota(jnp.int32, (n_slc, tk), 0) == blk_of_key, 1.0, 0.0).astype(BF16)
    picked = jnp.dot(sel_ref[...], expand, preferred_element_type=F32) > 0.5
    qpos = qi * tq + lax.broadcasted_iota(jnp.int32, (tq, tk), 0)
    kpos = ki * tk + lax.broadcasted_iota(jnp.int32, (tq, tk), 1)
    k = k_ref[...]
    v = v_ref[...]
    for g in range(NSA_GQA):
        s = _qk(q_ref[g], k) * ATTN_SCALE + bias_ref[g]
        s = jnp.where(picked, s, NEG_INF)
        s = jnp.where(kpos <= qpos, s, NEG_INF)
        _flash_update(s, v, m_ref.at[g], l_ref.at[g], acc_ref.at[g])

    @pl.when(ki == ((qi + 1) * tq - 1) // tk)
    def _():
        for g in range(NSA_GQA):
            o_ref[g] = _flash_result(l_ref.at[g], acc_ref.at[g]).astype(o_ref.dtype)


def slc_bias_table(rel_bias_c, S, tq, tk):
    n_delta = S // tq
    d = (np.arange(n_delta)[:, None, None] * tq + np.arange(tq)[None, :, None]
         - np.arange(tk)[None, None, :])
    vals = rel_bias_c[_t5_bucket_np(d)]
    return jnp.transpose(vals, (0, 3, 1, 2)).astype(F32)


def nsa_selected(proj, sel, bias_tbl, B, S, *, tq=256, tk=512):
    tq, tk = min(tq, S), min(tk, S)
    nq, nk = S // tq, S // tk
    T = B * S
    n_slc = S // SLC_BLOCK
    G = NSA_GQA
    ck = CB_NSAKV + (1 * 2 + 0) * NSA_KV_HEADS
    cv = CB_NSAKV + (1 * 2 + 1) * NSA_KV_HEADS
    qis, kis = _causal_pairs(nq, tq, tk)
    grid_spec = pltpu.PrefetchScalarGridSpec(
        num_scalar_prefetch=2,
        grid=(B, NSA_KV_HEADS, int(qis.shape[0])),
        in_specs=[
            pl.BlockSpec((G, tq, LANES), lambda b, h, p, qi, ki: (CB_NSAQ // G + h, b * nq + qi[p], 0)),
            pl.BlockSpec((None, tk, LANES), lambda b, h, p, qi, ki: (ck + h, b * nk + ki[p], 0)),
            pl.BlockSpec((None, tk, LANES), lambda b, h, p, qi, ki: (cv + h, b * nk + ki[p], 0)),
            pl.BlockSpec((None, None, tq, n_slc), lambda b, h, p, qi, ki: (b, h, qi[p], 0)),
            pl.BlockSpec((None, G, tq, tk),
                         lambda b, h, p, qi, ki: (qi[p] - ki[p] * (tk // tq), h, 0, 0)),
        ],
        out_specs=pl.BlockSpec((G, tq, LANES), lambda b, h, p, qi, ki: (h, b * nq + qi[p], 0)),
        scratch_shapes=[pltpu.VMEM((G, tq, 1), F32), pltpu.VMEM((G, tq, 1), F32),
                        pltpu.VMEM((G, tq, LANES), F32)],
    )
    return pl.pallas_call(
        functools.partial(_slc_kernel, tq=tq, tk=tk, n_slc=n_slc),
        grid_spec=grid_spec,
        out_shape=jax.ShapeDtypeStruct((NSA_Q_HEADS, T, LANES), BF16),
        compiler_params=_cparams(("parallel", "parallel", "arbitrary")),
        name="nsa_selected",
    )(qis, kis, proj, proj, proj, sel, bias_tbl)


def _win_kernel(q_ref, k0_ref, k1_ref, k2_ref, v0_ref, v1_ref, v2_ref, bias_ref, o_ref, *, tq, nkb):
    qi = pl.program_id(2)
    k = jnp.concatenate([r[...] for r in (k0_ref, k1_ref, k2_ref)][-nkb:], axis=0)
    v = jnp.concatenate([r[...] for r in (v0_ref, v1_ref, v2_ref)][-nkb:], axis=0)
    kpos = (qi - (nkb - 1)) * tq + lax.broadcasted_iota(jnp.int32, (tq, nkb * tq), 1)
    for g in range(NSA_GQA):
        s = _qk(q_ref[g], k) * ATTN_SCALE + bias_ref[g]
        p = _masked_softmax(s, kpos >= 0, -1)
        o_ref[g] = jnp.dot(p.astype(BF16), v, preferred_element_type=F32).astype(o_ref.dtype)


def win_bias_table(rel_bias_c, tq, nkb):
    i = np.arange(tq)[:, None]
    j = np.arange(nkb * tq)[None, :]
    dist = (nkb - 1) * tq + i - j
    valid = (dist >= 0) & (dist < WIN)
    vals = rel_bias_c[_t5_bucket_np(dist)]
    vals = jnp.where(valid[:, :, None], vals, NEG_INF)
    return jnp.transpose(vals, (2, 0, 1)).astype(F32)


def nsa_window(proj, bias_tbl, B, S, *, tq=256):
    tq = min(tq, S)
    nq = S // tq
    T = B * S
    G = NSA_GQA
    nkb = min(WIN // tq + 1, 3)
    ck = CB_NSAKV + (2 * 2 + 0) * NSA_KV_HEADS
    cv = CB_NSAKV + (2 * 2 + 1) * NSA_KV_HEADS
    kv = lambda c, back: pl.BlockSpec(
        (None, tq, LANES), lambda b, h, i: (c + h, b * nq + jnp.maximum(i - back, 0), 0))
    return pl.pallas_call(
        functools.partial(_win_kernel, tq=tq, nkb=nkb),
        grid=(B, NSA_KV_HEADS, nq),
        in_specs=[pl.BlockSpec((G, tq, LANES), lambda b, h, i: (CB_NSAQ // G + h, b * nq + i, 0)),
                  kv(ck, 2), kv(ck, 1), kv(ck, 0), kv(cv, 2), kv(cv, 1), kv(cv, 0),
                  pl.BlockSpec((G, tq, nkb * tq), lambda b, h, i: (h, 0, 0))],
        out_specs=pl.BlockSpec((G, tq, LANES), lambda b, h, i: (h, b * nq + i, 0)),
        out_shape=jax.ShapeDtypeStruct((NSA_Q_HEADS, T, LANES), BF16),
        compiler_params=_cparams(("parallel", "parallel", "arbitrary")),
        name="nsa_window",
    )(proj, proj, proj, proj, proj, proj, proj, bias_tbl)


def _sigmoid(x):
    return 1.0 / (1.0 + jnp.exp(-x))


def _merge_kernel(fox_ref, dil_ref, lse_ref, cmp_ref, slc_ref, win_ref, small_ref,
                  g0_ref, g1_ref, g2_ref, wa_ref, wb_ref, wc_ref, o_ref, ya_ref, yb_ref, yc_ref, *, ncb):
    @pl.when(pl.program_id(1) == 0)
    def _():
        for h in range(FOX_HEADS):
            ya_ref[:, h * LANES:(h + 1) * LANES] = fox_ref[h]
        for h in range(DIL_HEADS_PER_GROUP):
            lse = [lse_ref[g, h] for g in range(len(DIL_GROUPS))]
            top = jnp.maximum(jnp.maximum(lse[0], lse[1]), lse[2])
            w = [jnp.exp(x - top) for x in lse]
            tot = w[0] + w[1] + w[2]
            y = sum((w[g] / tot) * dil_ref[g, h].astype(F32) for g in range(len(DIL_GROUPS)))
            yb_ref[:, h * LANES:(h + 1) * LANES] = y.astype(BF16)
        gates = _sigmoid(small_ref[...])
        for h in range(NSA_Q_HEADS):
            y = jnp.zeros(cmp_ref.shape[1:], F32)
            for br, ref in enumerate((cmp_ref, slc_ref, win_ref)):
                col = SMALL_NSA_G + br * NSA_Q_HEADS + h
                y = y + gates[:, col:col + 1] * ref[h].astype(F32)
            yc_ref[:, h * LANES:(h + 1) * LANES] = y.astype(BF16)

    def gate(ref):
        return _sigmoid(jnp.concatenate([ref[c] for c in range(ncb)], axis=1).astype(F32))

    merged = (gate(g0_ref) * jnp.dot(ya_ref[...], wa_ref[...], preferred_element_type=F32)
              + gate(g1_ref) * jnp.dot(yb_ref[...], wb_ref[...], preferred_element_type=F32)
              + gate(g2_ref) * jnp.dot(yc_ref[...], wc_ref[...], preferred_element_type=F32))
    o_ref[...] = merged.astype(o_ref.dtype)


def merge_branches(proj, small, fox_o, dil_o, dil_lse, cmp_o, slc_o, win_o, wa, wb, wc, *, tm=512, tn=512):
    T = proj.shape[1]
    D = wa.shape[1]
    tm = min(tm, T)
    ncb = tn // LANES
    dpb = D // tn
    heads = lambda n: pl.BlockSpec((n, tm, LANES), lambda i, j: (0, i, 0))
    gate = lambda b: pl.BlockSpec((ncb, tm, LANES), lambda i, j: (CB_MERGE // ncb + b * dpb + j, i, 0))
    dil4 = pl.BlockSpec((len(DIL_GROUPS), DIL_HEADS_PER_GROUP, tm, LANES), lambda i, j: (0, 0, i, 0))
    wspec = lambda k: pl.BlockSpec((k, tn), lambda i, j: (0, j))
    ka, kb, kc = wa.shape[0], wb.shape[0], wc.shape[0]
    return pl.pallas_call(
        functools.partial(_merge_kernel, ncb=ncb),
        grid=(T // tm, D // tn),
        in_specs=[heads(FOX_HEADS), dil4, dil4, heads(NSA_Q_HEADS), heads(NSA_Q_HEADS), heads(NSA_Q_HEADS),
                  pl.BlockSpec((tm, LANES), lambda i, j: (i, 0)),
                  gate(0), gate(1), gate(2), wspec(ka), wspec(kb), wspec(kc)],
        out_specs=pl.BlockSpec((tm, tn), lambda i, j: (i, j)),
        out_shape=jax.ShapeDtypeStruct((T, D), BF16),
        scratch_shapes=[pltpu.VMEM((tm, ka), BF16), pltpu.VMEM((tm, kb), BF16), pltpu.VMEM((tm, kc), BF16)],
        compiler_params=_cparams(("parallel", "arbitrary")),
        name="merge_branches",
    )(fox_o, dil_o, dil_lse, cmp_o, slc_o, win_o, small, proj, proj, proj, wa, wb, wc)


def _matmul_residual_kernel(a_ref, w_ref, x_ref, o_ref):
    o_ref[...] = x_ref[...] + jnp.dot(a_ref[...], w_ref[...], preferred_element_type=F32)


def matmul_residual(a, w, x, *, tm=1024, tn=1024):
    T, K = a.shape
    N = w.shape[1]
    tm, tn = min(tm, T), min(tn, N)
    return pl.pallas_call(
        _matmul_residual_kernel,
        grid=(T // tm, N // tn),
        in_specs=[pl.BlockSpec((tm, K), lambda i, j: (i, 0)),
                  pl.BlockSpec((K, tn), lambda i, j: (0, j)),
                  pl.BlockSpec((tm, tn), lambda i, j: (i, j))],
        out_specs=pl.BlockSpec((tm, tn), lambda i, j: (i, j)),
        out_shape=jax.ShapeDtypeStruct((T, N), F32),
        compiler_params=_cparams(("parallel", "parallel")),
        name="matmul_residual",
    )(a, w, x)


def _silu(x):
    return x * _sigmoid(x)


def _ffn_kernel(x_ref, g_ref, comb_ref, wg_ref, wu_ref, wd_ref, gf_ref, o_ref, h_ref, acc_ref,
                *, weighted, final_norm):
    e, f = pl.program_id(1), pl.program_id(2)

    @pl.when((e == 0) & (f == 0))
    def _():
        h_ref[...] = _rms_rows(x_ref[...], g_ref[...]).astype(BF16)
        acc_ref[...] = jnp.zeros(acc_ref.shape, F32)

    h = h_ref[...]
    act = _silu(jnp.dot(h, wg_ref[...], preferred_element_type=F32)) * \
        jnp.dot(h, wu_ref[...], preferred_element_type=F32)
    y = jnp.dot(act.astype(BF16), wd_ref[...], preferred_element_type=F32)
    if weighted:
        lane = lax.broadcasted_iota(jnp.int32, comb_ref.shape, 1)
        y = y * jnp.sum(jnp.where(lane == e, comb_ref[...], 0.0), axis=1, keepdims=True)
    acc_ref[...] += y

    @pl.when((e == pl.num_programs(1) - 1) & (f == pl.num_programs(2) - 1))
    def _():
        out = x_ref[...] + acc_ref[...]
        if final_norm:
            out = _rms_rows(out, gf_ref[...])
        o_ref[...] = out


def ffn(x, g, combine, wg, wu, wd, g_final, *, weighted, final_norm, tm=512, tf=512):
    T, D = x.shape
    E, _, F = wg.shape
    tm, tf = min(tm, T), min(tf, F)
    return pl.pallas_call(
        functools.partial(_ffn_kernel, weighted=weighted, final_norm=final_norm),
        grid=(T // tm, E, F // tf),
        in_specs=[pl.BlockSpec((tm, D), lambda i, e, f: (i, 0)),
                  pl.BlockSpec((1, D), lambda i, e, f: (0, 0)),
                  pl.BlockSpec((tm, LANES), lambda i, e, f: (i, 0)),
                  pl.BlockSpec((None, D, tf), lambda i, e, f: (e, 0, f)),
                  pl.BlockSpec((None, D, tf), lambda i, e, f: (e, 0, f)),
                  pl.BlockSpec((None, tf, D), lambda i, e, f: (e, f, 0)),
                  pl.BlockSpec((1, D), lambda i, e, f: (0, 0))],
        out_specs=pl.BlockSpec((tm, D), lambda i, e, f: (i, 0)),
        out_shape=jax.ShapeDtypeStruct((T, D), F32),
        scratch_shapes=[pltpu.VMEM((tm, D), BF16), pltpu.VMEM((tm, D), F32)],
        compiler_params=_cparams(("parallel", "arbitrary", "arbitrary")),
        name="ffn",
    )(x, g, combine, wg, wu, wd, g_final)


def _route_kernel(x_ref, g_ref, r_ref, comb_ref):
    h = _rms_rows(x_ref[...], g_ref[...])
    logits = jnp.dot(h, r_ref[...], preferred_element_type=F32, precision=HIGHEST)
    lane = lax.broadcasted_iota(jnp.int32, logits.shape, 1)
    logits = jnp.where(lane < N_EXPERTS, logits, NEG_INF)
    v1 = jnp.max(logits, axis=1, keepdims=True)
    i1 = jnp.min(jnp.where(logits == v1, lane, LANES), axis=1, keepdims=True)
    rest = jnp.where(lane == i1, NEG_INF, logits)
    v2 = jnp.max(rest, axis=1, keepdims=True)
    i2 = jnp.min(jnp.where(rest == v2, lane, LANES), axis=1, keepdims=True)
    e2 = jnp.exp(v2 - v1)
    den = 1.0 + e2
    comb_ref[...] = jnp.where(lane == i1, 1.0 / den, jnp.where(lane == i2, e2 / den, 0.0))


def moe_route(x, g, router_pad, *, tm=512):
    T, D = x.shape
    tm = min(tm, T)
    return pl.pallas_call(
        _route_kernel,
        grid=(T // tm,),
        in_specs=[pl.BlockSpec((tm, D), lambda i: (i, 0)),
                  pl.BlockSpec((1, D), lambda i: (0, 0)),
                  pl.BlockSpec((D, LANES), lambda i: (0, 0))],
        out_specs=pl.BlockSpec((tm, LANES), lambda i: (i, 0)),
        out_shape=jax.ShapeDtypeStruct((T, LANES), F32),
        compiler_params=_cparams(("parallel",)),
        name="moe_route",
    )(x, g, router_pad)


def _split_w_in(w):
    D = w.shape[0]
    n_fox = 3 * FOX_HEADS * HEAD_DIM
    n_mid = (3 * DIL_HEADS + NSA_Q_HEADS + NSA_BRANCHES * 2 * NSA_KV_HEADS) * HEAD_DIM
    n_g = NSA_BRANCHES * NSA_Q_HEADS
    a0 = n_fox
    a1 = a0 + FOX_HEADS
    a2 = a1 + n_mid
    a3 = a2 + n_g
    main = jnp.concatenate([w[:, :a0], w[:, a1:a2], w[:, a3:]], axis=1).astype(BF16)
    small = jnp.concatenate([w[:, a0:a1], w[:, a2:a3],
                             jnp.zeros((D, LANES - FOX_HEADS - n_g), w.dtype)], axis=1).astype(BF16)
    return main, small


def mixing_block(x, B, S, norm_g, w_in, forget_bias, cmp_pe, cmp_w1, cmp_w2, wa, wb, wc, w_out, tables):
    T, D = x.shape
    w_main, w_small = _split_w_in(w_in)
    g = norm_g.reshape(1, D)
    proj = rms_proj(x, g, w_main, BF16, tm=1024, tn=1024)
    small = rms_proj(x, g, w_small, F32, tm=1024, tn=LANES)[0]

    bias_row = jnp.zeros((1, LANES), F32).at[0, :FOX_HEADS].set(forget_bias)
    c = logf_cumsum(small, bias_row, B, S)[:, :FOX_HEADS]
    c_col = c.T.reshape(FOX_HEADS, T, 1)
    c_row = c.T.reshape(FOX_HEADS, 1, T)
    fox_o = fox_attention(proj, c_col, c_row, B, S)

    dil = [dilated_group(proj, tables["dil"][gi], gi, B, S) for gi in range(len(DIL_GROUPS))]
    dil_o = jnp.stack([d[0] for d in dil])
    dil_lse = jnp.stack([d[1] for d in dil])

    kvc = nsa_compress(proj, cmp_pe, cmp_w1.astype(BF16), cmp_w2.astype(BF16), B, S)
    cmp_o, sel_t = nsa_cmp_select(proj, kvc, B, S)
    sel = jnp.swapaxes(sel_t, 2, 3).astype(BF16)
    slc_o = nsa_selected(proj, sel, tables["slc"], B, S, tq=tables["slc_tq"], tk=tables["slc_tk"])
    win_o = nsa_window(proj, tables["win"], B, S, tq=tables["win_tq"])

    merged = merge_branches(proj, small, fox_o, dil_o, dil_lse, cmp_o, slc_o, win_o,
                            wa.astype(BF16), wb.astype(BF16), wc.astype(BF16))
    return matmul_residual(merged, w_out.astype(BF16), x)


def bias_tables(rel_bias, S):
    slc_tq, slc_tk = min(256, S), min(512, S)
    win_tq = min(256, S)
    rel_c = rel_bias[:, DIL_HEADS:]
    return {
        "dil": dilated_bias_tables(rel_bias[:, :DIL_HEADS]),
        "slc": slc_bias_table(rel_c, S, slc_tq, slc_tk), "slc_tq": slc_tq, "slc_tk": slc_tk,
        "win": win_bias_table(rel_c, win_tq, min(WIN // win_tq + 1, 3)), "win_tq": win_tq,
    }


def kernel(x, rel_bias, norm_mix_g, norm_ffn_g, norm_final_g, w_in, fox_forget_bias, cmp_pe_k, cmp_w1_k, cmp_w2_k, cmp_pe_v, cmp_w1_v, cmp_w2_v, w_branch_a, w_branch_b, w_branch_c, w_out, ffn_w_gate, ffn_w_up, ffn_w_down, moe_router, moe_w_gate, moe_w_up, moe_w_down):
    B, S, D = x.shape
    T = B * S
    depth = w_in.shape[0]
    tables = bias_tables(rel_bias, S)
    g_final = norm_final_g.reshape(1, D)
    xt = x.reshape(T, D)
    for l in range(depth):
        xt = mixing_block(
            xt, B, S, norm_mix_g[l], w_in[l], fox_forget_bias[l],
            jnp.stack([cmp_pe_k[l], cmp_pe_v[l]]), jnp.stack([cmp_w1_k[l], cmp_w1_v[l]]),
            jnp.stack([cmp_w2_k[l], cmp_w2_v[l]]),
            w_branch_a[l], w_branch_b[l], w_branch_c[l], w_out[l], tables)
        g = norm_ffn_g[l].reshape(1, D)
        last = l == depth - 1
        j = l // 2
        if l % 2 == 0:
            dummy = jnp.zeros((T, LANES), F32)
            xt = ffn(xt, g, dummy, ffn_w_gate[j][None].astype(BF16), ffn_w_up[j][None].astype(BF16),
                     ffn_w_down[j][None].astype(BF16), g_final, weighted=False, final_norm=last)
        else:
            router_pad = jnp.zeros((D, LANES), F32).at[:, :N_EXPERTS].set(moe_router[j])
            combine = moe_route(xt, g, router_pad)
            xt = ffn(xt, g, combine, moe_w_gate[j].astype(BF16), moe_w_up[j].astype(BF16),
                     moe_w_down[j].astype(BF16), g_final, weighted=True, final_norm=last)
    if depth == 0:
        raise ValueError("depth must be positive")
    return xt.reshape(B, S, D)
```

```python
import functools
import math

import jax
import jax.numpy as jnp
import numpy as np
from jax import lax
from jax.experimental import pallas as pl
from jax.experimental.pallas import tpu as pltpu

F32 = jnp.float32
BF16 = jnp.bfloat16
HIGHEST = lax.Precision.HIGHEST
NEG_INF = float("-inf")

LANES = 128
HEAD_DIM = 128
ATTN_SCALE = HEAD_DIM ** -0.5
RMS_EPS = 1e-6
VMEM_LIMIT_BYTES = 56 * 1024 * 1024

FOX_HEADS = 8
DIL_GROUPS = ((128, 1), (512, 4), (2048, 16))
DIL_HEADS_PER_GROUP = 4
DIL_HEADS = DIL_HEADS_PER_GROUP * len(DIL_GROUPS)
DIL_TAPS = 128
NSA_Q_HEADS = 8
NSA_KV_HEADS = 2
NSA_GQA = NSA_Q_HEADS // NSA_KV_HEADS
NSA_BRANCHES = 3
CMP_BLOCK = 32
CMP_STRIDE = 16
CMP_HIDDEN = 256
SLC_BLOCK = 64
SLC_COUNT = 16
WIN = 512
REL_BUCKETS = 32
REL_MAX_EXACT = 16
REL_MAX_DIST = 2048
N_EXPERTS = 8
TOP_K = 2
N_BRANCHES = 3

CB_FOX = 0
CB_DIL = CB_FOX + 3 * FOX_HEADS
CB_NSAQ = CB_DIL + 3 * DIL_HEADS
CB_NSAKV = CB_NSAQ + NSA_Q_HEADS
CB_MERGE = CB_NSAKV + NSA_BRANCHES * 2 * NSA_KV_HEADS
SMALL_FOX_F = 0
SMALL_NSA_G = FOX_HEADS
ROUTE_CHOICE = 0
ROUTE_WEIGHT = N_EXPERTS


def _cparams(semantics):
    return pltpu.CompilerParams(dimension_semantics=semantics,
                                vmem_limit_bytes=VMEM_LIMIT_BYTES)


def _t5_bucket_np(dist):
    dist = np.maximum(dist, 0)
    d = np.maximum(dist, 1).astype(np.float32)
    log_ratio = np.log(d / np.float32(REL_MAX_EXACT)) / np.float32(math.log(REL_MAX_DIST / REL_MAX_EXACT))
    large = REL_MAX_EXACT + (log_ratio * np.float32(REL_BUCKETS - REL_MAX_EXACT)).astype(np.int32)
    large = np.minimum(large, REL_BUCKETS - 1)
    return np.where(dist < REL_MAX_EXACT, dist, large).astype(np.int32)


def _toeplitz_blocks(fn, offsets, rows, cols):
    lx = rows + cols
    m = np.arange(lx)
    rel = np.where(m < cols, -m, lx - m)
    out = []
    for c in offsets:
        v = fn(c + rel)
        flat = jnp.tile(v, (1, rows))[:, :rows * (lx - 1)]
        out.append(flat.reshape(-1, rows, lx - 1)[:, :, :cols])
    return jnp.stack(out).astype(F32)


def _rms_rows(x, g):
    inv = lax.rsqrt(jnp.mean(x * x, axis=-1, keepdims=True) + RMS_EPS)
    return (x * inv) * g


def _rms_proj_kernel(x_ref, g_ref, w_ref, o_ref, h_ref, *, ncb):
    @pl.when(pl.program_id(1) == 0)
    def _():
        h_ref[...] = _rms_rows(x_ref[...], g_ref[...]).astype(BF16)

    res = jnp.dot(h_ref[...], w_ref[...], preferred_element_type=F32)
    for c in range(ncb):
        o_ref[c] = res[:, c * LANES:(c + 1) * LANES].astype(o_ref.dtype)


def rms_proj(x, g, w, out_dtype, *, tm, tn):
    T, D = x.shape
    N = w.shape[1]
    tm, tn = min(tm, T), min(tn, N)
    ncb = tn // LANES
    return pl.pallas_call(
        functools.partial(_rms_proj_kernel, ncb=ncb),
        grid=(T // tm, N // tn),
        in_specs=[pl.BlockSpec((tm, D), lambda i, j: (i, 0)),
                  pl.BlockSpec((1, D), lambda i, j: (0, 0)),
                  pl.BlockSpec((D, tn), lambda i, j: (0, j))],
        out_specs=pl.BlockSpec((ncb, tm, LANES), lambda i, j: (j, i, 0)),
        out_shape=jax.ShapeDtypeStruct((N // LANES, T, LANES), out_dtype),
        scratch_shapes=[pltpu.VMEM((tm, D), BF16)],
        compiler_params=_cparams(("parallel", "arbitrary")),
        name="rms_proj",
    )(x, g, w)


CUMSUM_BLOCK = 256


def _logf_cumsum_kernel(f_ref, b_ref, c_ref, *, nblk):
    row = lax.broadcasted_iota(jnp.int32, (CUMSUM_BLOCK, CUMSUM_BLOCK), 0)
    col = lax.broadcasted_iota(jnp.int32, (CUMSUM_BLOCK, CUMSUM_BLOCK), 1)
    tri = jnp.where(col <= row, 1.0, 0.0).astype(F32)

    def body(i, carry):
        sl = pl.ds(pl.multiple_of(i * CUMSUM_BLOCK, CUMSUM_BLOCK), CUMSUM_BLOCK)
        z = f_ref[sl, :] + b_ref[...]
        logf = jnp.minimum(z, 0.0) - jnp.log1p(jnp.exp(-jnp.abs(z)))
        cs = jnp.dot(tri, logf, preferred_element_type=F32, precision=HIGHEST) + carry
        c_ref[sl, :] = cs
        return cs[CUMSUM_BLOCK - 1:CUMSUM_BLOCK, :]

    lax.fori_loop(0, nblk, body, jnp.zeros((1, LANES), F32))


def logf_cumsum(small, bias_row, B, S):
    T = B * S
    return pl.pallas_call(
        functools.partial(_logf_cumsum_kernel, nblk=S // CUMSUM_BLOCK),
        grid=(B,),
        in_specs=[pl.BlockSpec((S, LANES), lambda b: (b, 0)),
                  pl.BlockSpec((1, LANES), lambda b: (0, 0))],
        out_specs=pl.BlockSpec((S, LANES), lambda b: (b, 0)),
        out_shape=jax.ShapeDtypeStruct((T, LANES), F32),
        compiler_params=_cparams(("parallel",)),
        name="logf_cumsum",
    )(small, bias_row)


FLASH_ROW_CHUNK = 128


def _flash_update(s, v, m_ref, l_ref, acc_ref):
    m_old = m_ref[...]
    m_new = jnp.maximum(m_old, jnp.max(s, axis=-1, keepdims=True))
    m_safe = jnp.where(m_new == NEG_INF, 0.0, m_new)
    alpha = jnp.exp(m_old - m_safe)
    p = jnp.exp(s - m_safe)
    l_ref[...] = alpha * l_ref[...] + jnp.sum(p, axis=-1, keepdims=True)
    acc_ref[...] = alpha * acc_ref[...] + jnp.dot(p.astype(BF16), v, preferred_element_type=F32)
    m_ref[...] = m_new


def _flash_init(m_ref, l_ref, acc_ref):
    m_ref[...] = jnp.full(m_ref.shape, NEG_INF, F32)
    l_ref[...] = jnp.zeros(l_ref.shape, F32)
    acc_ref[...] = jnp.zeros(acc_ref.shape, F32)


def _flash_result(l_ref, acc_ref):
    l = l_ref[...]
    return acc_ref[...] / jnp.where(l > 0.0, l, 1.0)


def _qk(q, k):
    return lax.dot_general(q, k, (((1,), (1,)), ((), ())), preferred_element_type=F32)


def _fox_kernel(qi_ref, ki_ref, q_ref, k_ref, v_ref, cq_ref, ck_ref, o_ref, m_ref, l_ref, acc_ref,
                *, tq, tk, rc):
    p = pl.program_id(2)
    qi, ki = qi_ref[p], ki_ref[p]

    @pl.when(ki == 0)
    def _():
        _flash_init(m_ref, l_ref, acc_ref)

    def step(causal_mask):
        k, v, ck = k_ref[...], v_ref[...], ck_ref[...]
        chunks = [pl.ds(c * rc, rc) for c in range(tq // rc)]
        qk = [_qk(q_ref[rows, :], k) for rows in chunks]
        for c, rows in enumerate(chunks):
            s = qk[c] * ATTN_SCALE + (cq_ref[rows, :] - ck)
            if causal_mask:
                qpos = qi * tq + c * rc + lax.broadcasted_iota(jnp.int32, (rc, tk), 0)
                kpos = ki * tk + lax.broadcasted_iota(jnp.int32, (rc, tk), 1)
                s = jnp.where(kpos <= qpos, s, NEG_INF)
            _flash_update(s, v, m_ref.at[rows], l_ref.at[rows], acc_ref.at[rows])

    crosses_diagonal = (ki + 1) * tk - 1 > qi * tq

    @pl.when(crosses_diagonal)
    def _():
        step(True)

    @pl.when(jnp.logical_not(crosses_diagonal))
    def _():
        step(False)

    @pl.when(ki == ((qi + 1) * tq - 1) // tk)
    def _():
        o_ref[...] = _flash_result(l_ref, acc_ref).astype(o_ref.dtype)


def _causal_pairs(nq, tq, tk):
    qs, ks = [], []
    for qi in range(nq):
        for ki in range(((qi + 1) * tq - 1) // tk + 1):
            qs.append(qi)
            ks.append(ki)
    return jnp.asarray(qs, jnp.int32), jnp.asarray(ks, jnp.int32)


def fox_attention(proj, c_col, c_row, B, S, *, tq=512, tk=512):
    tq, tk = min(tq, S), min(tk, S)
    nq, nk = S // tq, S // tk
    H = FOX_HEADS
    T = B * S
    qis, kis = _causal_pairs(nq, tq, tk)
    grid_spec = pltpu.PrefetchScalarGridSpec(
        num_scalar_prefetch=2,
        grid=(B, H, int(qis.shape[0])),
        in_specs=[
            pl.BlockSpec((None, tq, LANES), lambda b, h, p, qi, ki: (CB_FOX + h, b * nq + qi[p], 0)),
            pl.BlockSpec((None, tk, LANES), lambda b, h, p, qi, ki: (CB_FOX + H + h, b * nk + ki[p], 0)),
            pl.BlockSpec((None, tk, LANES), lambda b, h, p, qi, ki: (CB_FOX + 2 * H + h, b * nk + ki[p], 0)),
            pl.BlockSpec((None, tq, 1), lambda b, h, p, qi, ki: (h, b * nq + qi[p], 0)),
            pl.BlockSpec((None, 1, tk), lambda b, h, p, qi, ki: (h, 0, b * nk + ki[p])),
        ],
        out_specs=pl.BlockSpec((None, tq, LANES), lambda b, h, p, qi, ki: (h, b * nq + qi[p], 0)),
        scratch_shapes=[pltpu.VMEM((tq, 1), F32), pltpu.VMEM((tq, 1), F32), pltpu.VMEM((tq, LANES), F32)],
    )
    return pl.pallas_call(
        functools.partial(_fox_kernel, tq=tq, tk=tk, rc=min(FLASH_ROW_CHUNK, tq)),
        grid_spec=grid_spec,
        out_shape=jax.ShapeDtypeStruct((H, T, LANES), BF16),
        compiler_params=_cparams(("parallel", "parallel", "arbitrary")),
        name="fox_attention",
    )(qis, kis, proj, proj, proj, c_col, c_row)


def _dil_kernel(q_ref, kp_ref, kc_ref, vp_ref, vc_ref, bias_ref, o_ref, lse_ref, *, dil, tu):
    ui = pl.program_id(2)
    kcol = lax.broadcasted_iota(jnp.int32, (tu, 2 * tu), 1)
    first_ok = jnp.logical_or(ui > 0, kcol >= tu)
    bias = jnp.where(first_ok, bias_ref[...], NEG_INF)
    lanes = [slice(r * LANES, (r + 1) * LANES) for r in range(dil)]
    qk = [_qk(q_ref[:, sl], jnp.concatenate([kp_ref[:, sl], kc_ref[:, sl]], axis=0)) for sl in lanes]
    for r, sl in enumerate(lanes):
        v = jnp.concatenate([vp_ref[:, sl], vc_ref[:, sl]], axis=0)
        s = qk[r] * ATTN_SCALE + bias
        m = jnp.max(s, axis=-1, keepdims=True)
        e = jnp.exp(s - m)
        den = jnp.sum(e, axis=-1, keepdims=True)
        o = jnp.dot(e.astype(BF16), v, preferred_element_type=F32) / den
        o_ref[:, sl] = o.astype(o_ref.dtype)
        lse_ref[:, sl] = jnp.broadcast_to(m + jnp.log(den), (tu, LANES))


def dilated_group(proj, bias_tbl, group, B, S):
    dil = DIL_GROUPS[group][1]
    tu = DIL_TAPS
    Hg = DIL_HEADS_PER_GROUP
    T = B * S
    nu = S // dil // tu
    c0 = CB_DIL + group * Hg
    slabs = jnp.concatenate([proj[c0 + t * DIL_HEADS:c0 + t * DIL_HEADS + Hg] for t in range(3)])
    view = slabs.reshape(3 * Hg, T // dil, dil * LANES)
    cq, ck, cv = 0, Hg, 2 * Hg
    blk = (None, tu, dil * LANES)
    prev = lambda c: (lambda b, h, u: (c + h, b * nu + jnp.maximum(u - 1, 0), 0))
    cur = lambda c: (lambda b, h, u: (c + h, b * nu + u, 0))
    o, lse = pl.pallas_call(
        functools.partial(_dil_kernel, dil=dil, tu=tu),
        grid=(B, Hg, nu),
        in_specs=[pl.BlockSpec(blk, cur(cq)),
                  pl.BlockSpec(blk, prev(ck)), pl.BlockSpec(blk, cur(ck)),
                  pl.BlockSpec(blk, prev(cv)), pl.BlockSpec(blk, cur(cv)),
                  pl.BlockSpec((None, tu, 2 * tu), lambda b, h, u: (h, 0, 0))],
        out_specs=[pl.BlockSpec(blk, cur(0)), pl.BlockSpec(blk, cur(0))],
        out_shape=[jax.ShapeDtypeStruct((Hg, T // dil, dil * LANES), BF16),
                   jax.ShapeDtypeStruct((Hg, T // dil, dil * LANES), F32)],
        compiler_params=_cparams(("parallel", "parallel", "arbitrary")),
        name=f"dilated_group{group}",
    )(view, view, view, view, view, bias_tbl)
    return o.reshape(Hg, T, LANES), lse.reshape(Hg, T, LANES)


def dilated_bias_tables(rel_bias_b):
    tu = DIL_TAPS
    tables = []
    for g, (_, dil) in enumerate(DIL_GROUPS):
        heads = rel_bias_b[:, g * DIL_HEADS_PER_GROUP:(g + 1) * DIL_HEADS_PER_GROUP]

        def fn(taps, heads=heads, dil=dil):
            valid = (taps >= 0) & (taps <= DIL_TAPS)
            vals = heads[_t5_bucket_np(np.where(valid, taps, 0) * dil)].T
            return jnp.where(valid[None, :], vals, NEG_INF)

        tables.append(_toeplitz_blocks(fn, [tu], tu, 2 * tu)[0])
    return tables


def _compress_kernel(x_ref, pelo_ref, pehi_ref, w1a_ref, w1b_ref, w2_ref, o_ref, *, nchunk):
    x = x_ref[...].astype(F32)
    u0 = jnp.dot((x + pelo_ref[...]).astype(BF16), w1a_ref[...], preferred_element_type=F32)
    u1 = jnp.dot((x + pehi_ref[...]).astype(BF16), w1b_ref[...], preferred_element_type=F32)
    pre = u0 + pltpu.roll(u1, nchunk - 1, 0)
    hid = jax.nn.gelu(pre)
    out = jnp.dot(hid.astype(BF16), w2_ref[...], preferred_element_type=F32)
    row = lax.broadcasted_iota(jnp.int32, out.shape, 0)
    o_ref[...] = jnp.where(row < nchunk - 1, out, 0.0).astype(o_ref.dtype)


def nsa_compress(proj, pe, w1, w2, B, S):
    nchunk = S // CMP_STRIDE
    half = CMP_STRIDE * HEAD_DIM
    Hkv = NSA_KV_HEADS
    x = proj[CB_NSAKV:CB_NSAKV + 2 * Hkv].reshape(2, Hkv, B, nchunk, half)
    pe_lo = pe[:, :CMP_STRIDE].reshape(2, 1, half)
    pe_hi = pe[:, CMP_STRIDE:].reshape(2, 1, half)
    return pl.pallas_call(
        functools.partial(_compress_kernel, nchunk=nchunk),
        grid=(2, B, Hkv),
        in_specs=[pl.BlockSpec((None, None, None, nchunk, half), lambda t, b, h: (t, h, b, 0, 0)),
                  pl.BlockSpec((None, 1, half), lambda t, b, h: (t, 0, 0)),
                  pl.BlockSpec((None, 1, half), lambda t, b, h: (t, 0, 0)),
                  pl.BlockSpec((None, half, CMP_HIDDEN), lambda t, b, h: (t, 0, 0)),
                  pl.BlockSpec((None, half, CMP_HIDDEN), lambda t, b, h: (t, 1, 0)),
                  pl.BlockSpec((None, CMP_HIDDEN, HEAD_DIM), lambda t, b, h: (t, 0, 0))],
        out_specs=pl.BlockSpec((None, None, None, nchunk, HEAD_DIM), lambda t, b, h: (t, b, h, 0, 0)),
        out_shape=jax.ShapeDtypeStruct((2, B, Hkv, nchunk, HEAD_DIM), BF16),
        compiler_params=_cparams(("parallel", "parallel", "parallel")),
        name="nsa_compress",
    )(x, pe_lo, pe_hi, w1, w1, w2)


def _masked_softmax(s, mask, axis):
    s = jnp.where(mask, s, NEG_INF)
    m = jnp.max(s, axis=axis, keepdims=True)
    m = jnp.where(m == NEG_INF, 0.0, m)
    e = jnp.exp(s - m)
    den = jnp.sum(e, axis=axis, keepdims=True)
    return e / jnp.where(den > 0.0, den, 1.0)


def _cmp_select_kernel(q_ref, kc_ref, vc_ref, mt_ref, o_ref, sel_ref, *, tq, nchunk, n_slc, n_sel):
    q0 = pl.program_id(2) * tq
    kc = kc_ref[...]
    vc = vc_ref[...]
    pos_r = q0 + lax.broadcasted_iota(jnp.int32, (tq, nchunk), 0)
    end_r = lax.broadcasted_iota(jnp.int32, (tq, nchunk), 1) * CMP_STRIDE + (CMP_BLOCK - 1)
    vis_r = end_r <= pos_r
    pos_c = q0 + lax.broadcasted_iota(jnp.int32, (nchunk, tq), 1)
    end_c = lax.broadcasted_iota(jnp.int32, (nchunk, tq), 0) * CMP_STRIDE + (CMP_BLOCK - 1)
    vis_c = end_c <= pos_c
    imp = jnp.zeros((nchunk, tq), F32)
    for g in range(NSA_GQA):
        q = q_ref[g]
        p = _masked_softmax(_qk(q, kc) * ATTN_SCALE, vis_r, -1)
        o_ref[g] = jnp.dot(p.astype(BF16), vc, preferred_element_type=F32).astype(o_ref.dtype)
        imp = imp + _masked_softmax(_qk(kc, q) * ATTN_SCALE, vis_c, 0)
    p_slc = jnp.dot(mt_ref[...], imp, preferred_element_type=F32, precision=HIGHEST)

    blk = lax.broadcasted_iota(jnp.int32, (n_slc, tq), 0)
    cur = (q0 + lax.broadcasted_iota(jnp.int32, (n_slc, tq), 1)) // SLC_BLOCK
    forced = (blk == 0) | (blk == cur) | (blk == cur - 1)
    allowed = blk <= cur
    score = jnp.where(forced, 1e30, jnp.where(allowed, p_slc, -1.0))
    chosen = jnp.zeros((n_slc, tq), F32)
    for _ in range(n_sel):
        top = jnp.max(score, axis=0, keepdims=True)
        first = jnp.min(jnp.where(score == top, blk, n_slc), axis=0, keepdims=True)
        hit = blk == first
        chosen = jnp.where(hit, 1.0, chosen)
        score = jnp.where(hit, -2.0, score)
    sel_ref[...] = jnp.where(allowed, chosen, 0.0)


def nsa_cmp_select(proj, kvc, B, S, *, tq=512):
    tq = min(tq, S)
    nq = S // tq
    T = B * S
    nchunk = S // CMP_STRIDE
    n_slc = S // SLC_BLOCK
    n_sel = min(SLC_COUNT, n_slc)
    ratio, n_inner = SLC_BLOCK // CMP_STRIDE, CMP_BLOCK // CMP_STRIDE
    mt = np.zeros((n_slc, nchunk), np.float32)
    for j in range(n_slc):
        for m in range(ratio):
            for n in range(n_inner):
                c = ratio * j + m - n
                if 0 <= c < nchunk - 1:
                    mt[j, c] += 1.0
    G = NSA_GQA
    kv_spec = lambda t: pl.BlockSpec((None, None, None, nchunk, HEAD_DIM), lambda b, h, i: (t, b, h, 0, 0))
    return pl.pallas_call(
        functools.partial(_cmp_select_kernel, tq=tq, nchunk=nchunk, n_slc=n_slc, n_sel=n_sel),
        grid=(B, NSA_KV_HEADS, nq),
        in_specs=[pl.BlockSpec((G, tq, LANES), lambda b, h, i: (CB_NSAQ // G + h, b * nq + i, 0)),
                  kv_spec(0), kv_spec(1),
                  pl.BlockSpec((n_slc, nchunk), lambda b, h, i: (0, 0))],
        out_specs=[pl.BlockSpec((G, tq, LANES), lambda b, h, i: (h, b * nq + i, 0)),
                   pl.BlockSpec((None, None, n_slc, tq), lambda b, h, i: (b, h, 0, i))],
        out_shape=[jax.ShapeDtypeStruct((NSA_Q_HEADS, T, LANES), BF16),
                   jax.ShapeDtypeStruct((B, NSA_KV_HEADS, n_slc, S), F32)],
        compiler_params=_cparams(("parallel", "parallel", "arbitrary")),
        name="nsa_cmp_select",
    )(proj, kvc, kvc, jnp.asarray(mt))


def _slc_kernel(qi_ref, ki_ref, q_ref, k_ref, v_ref, sel_ref, bias_ref, o_ref, m_ref, l_ref, acc_ref,
                *, tq, tk, n_slc):
    p = pl.program_id(2)
    qi, ki = qi_ref[p], ki_ref[p]

    @pl.when(ki == 0)
    def _():
        _flash_init(m_ref, l_ref, acc_ref)

    blk_of_key = ki * (tk // SLC_BLOCK) + lax.broadcasted_iota(jnp.int32, (n_slc, tk), 1) // SLC_BLOCK
    expand = jnp.where(lax.broadcasted_iota(jnp.int32, (n_slc, tk), 0) == blk_of_key, 1.0, 0.0).astype(BF16)
    picked = jnp.dot(sel_ref[...], expand, preferred_element_type=F32)

    def step(causal_mask):
        keep = picked > 0.5
        if causal_mask:
            qpos = qi * tq + lax.broadcasted_iota(jnp.int32, (tq, tk), 0)
            kpos = ki * tk + lax.broadcasted_iota(jnp.int32, (tq, tk), 1)
            keep = jnp.where(kpos <= qpos, picked, 0.0) > 0.5
        k, v = k_ref[...], v_ref[...]
        qk = [_qk(q_ref[g], k) for g in range(NSA_GQA)]
        for g in range(NSA_GQA):
            s = qk[g] * ATTN_SCALE + bias_ref[g]
            s = jnp.where(keep, s, NEG_INF)
            _flash_update(s, v, m_ref.at[g], l_ref.at[g], acc_ref.at[g])

    crosses_diagonal = (ki + 1) * tk - 1 > qi * tq

    @pl.when(crosses_diagonal)
    def _():
        step(True)

    @pl.when(jnp.logical_not(crosses_diagonal))
    def _():
        step(False)

    @pl.when(ki == ((qi + 1) * tq - 1) // tk)
    def _():
        for g in range(NSA_GQA):
            o_ref[g] = _flash_result(l_ref.at[g], acc_ref.at[g]).astype(o_ref.dtype)


def slc_bias_table(rel_bias_c, S, tq, tk):
    buckets = _t5_bucket_np(np.arange(S + tk))
    not_last = np.nonzero(buckets != REL_BUCKETS - 1)[0]
    far_start = int(not_last[-1]) + 1 if not_last.size else 0
    n_delta = min(S // tq, -(-(far_start + tk - 1) // tq) + 1)

    def fn(d):
        return rel_bias_c[_t5_bucket_np(d)].T

    return _toeplitz_blocks(fn, [dl * tq for dl in range(n_delta)], tq, tk)


def nsa_selected(proj, sel, bias_tbl, B, S, *, tq=256, tk=512):
    tq, tk = min(tq, S), min(tk, S)
    nq, nk = S // tq, S // tk
    T = B * S
    n_slc = S // SLC_BLOCK
    G = NSA_GQA
    ck = CB_NSAKV + (1 * 2 + 0) * NSA_KV_HEADS
    cv = CB_NSAKV + (1 * 2 + 1) * NSA_KV_HEADS
    qis, kis = _causal_pairs(nq, tq, tk)
    n_delta = bias_tbl.shape[0]
    grid_spec = pltpu.PrefetchScalarGridSpec(
        num_scalar_prefetch=2,
        grid=(B, NSA_KV_HEADS, int(qis.shape[0])),
        in_specs=[
            pl.BlockSpec((G, tq, LANES), lambda b, h, p, qi, ki: (CB_NSAQ // G + h, b * nq + qi[p], 0)),
            pl.BlockSpec((None, tk, LANES), lambda b, h, p, qi, ki: (ck + h, b * nk + ki[p], 0)),
            pl.BlockSpec((None, tk, LANES), lambda b, h, p, qi, ki: (cv + h, b * nk + ki[p], 0)),
            pl.BlockSpec((None, None, tq, n_slc), lambda b, h, p, qi, ki: (b, h, qi[p], 0)),
            pl.BlockSpec((None, G, tq, tk),
                         lambda b, h, p, qi, ki: (jnp.minimum(qi[p] - ki[p] * (tk // tq), n_delta - 1), h, 0, 0)),
        ],
        out_specs=pl.BlockSpec((G, tq, LANES), lambda b, h, p, qi, ki: (h, b * nq + qi[p], 0)),
        scratch_shapes=[pltpu.VMEM((G, tq, 1), F32), pltpu.VMEM((G, tq, 1), F32),
                        pltpu.VMEM((G, tq, LANES), F32)],
    )
    return pl.pallas_call(
        functools.partial(_slc_kernel, tq=tq, tk=tk, n_slc=n_slc),
        grid_spec=grid_spec,
        out_shape=jax.ShapeDtypeStruct((NSA_Q_HEADS, T, LANES), BF16),
        compiler_params=_cparams(("parallel", "parallel", "arbitrary")),
        name="nsa_selected",
    )(qis, kis, proj, proj, proj, sel, bias_tbl)


def _win_kernel(q_ref, k0_ref, k1_ref, k2_ref, v0_ref, v1_ref, v2_ref, bias_ref, o_ref, *, tq, nkb):
    qi = pl.program_id(2)
    k = jnp.concatenate([r[...] for r in (k0_ref, k1_ref, k2_ref)][-nkb:], axis=0)
    v = jnp.concatenate([r[...] for r in (v0_ref, v1_ref, v2_ref)][-nkb:], axis=0)
    kpos = (qi - (nkb - 1)) * tq + lax.broadcasted_iota(jnp.int32, (tq, nkb * tq), 1)
    qk = [_qk(q_ref[g], k) for g in range(NSA_GQA)]
    for g in range(NSA_GQA):
        s = qk[g] * ATTN_SCALE + bias_ref[g]
        p = _masked_softmax(s, kpos >= 0, -1)
        o_ref[g] = jnp.dot(p.astype(BF16), v, preferred_element_type=F32).astype(o_ref.dtype)


def win_bias_table(rel_bias_c, tq, nkb):
    def fn(dist):
        valid = (dist >= 0) & (dist < WIN)
        return jnp.where(valid[None, :], rel_bias_c[_t5_bucket_np(dist)].T, NEG_INF)

    return _toeplitz_blocks(fn, [(nkb - 1) * tq], tq, nkb * tq)[0]


def nsa_window(proj, bias_tbl, B, S, *, tq=256):
    tq = min(tq, S)
    nq = S // tq
    T = B * S
    G = NSA_GQA
    nkb = min(WIN // tq + 1, 3)
    ck = CB_NSAKV + (2 * 2 + 0) * NSA_KV_HEADS
    cv = CB_NSAKV + (2 * 2 + 1) * NSA_KV_HEADS
    kv = lambda c, back: pl.BlockSpec(
        (None, tq, LANES), lambda b, h, i: (c + h, b * nq + jnp.maximum(i - back, 0), 0))
    return pl.pallas_call(
        functools.partial(_win_kernel, tq=tq, nkb=nkb),
        grid=(B, NSA_KV_HEADS, nq),
        in_specs=[pl.BlockSpec((G, tq, LANES), lambda b, h, i: (CB_NSAQ // G + h, b * nq + i, 0)),
                  kv(ck, 2), kv(ck, 1), kv(ck, 0), kv(cv, 2), kv(cv, 1), kv(cv, 0),
                  pl.BlockSpec((G, tq, nkb * tq), lambda b, h, i: (h, 0, 0))],
        out_specs=pl.BlockSpec((G, tq, LANES), lambda b, h, i: (h, b * nq + i, 0)),
        out_shape=jax.ShapeDtypeStruct((NSA_Q_HEADS, T, LANES), BF16),
        compiler_params=_cparams(("parallel", "parallel", "arbitrary")),
        name="nsa_window",
    )(proj, proj, proj, proj, proj, proj, proj, bias_tbl)


def _sigmoid(x):
    return 1.0 / (1.0 + jnp.exp(-x))


def _merge_kernel(fox_ref, dil_ref, lse_ref, cmp_ref, slc_ref, win_ref, small_ref,
                  g0_ref, g1_ref, g2_ref, wa_ref, wb_ref, wc_ref, o_ref, ya_ref, yb_ref, yc_ref, *, ncb):
    @pl.when(pl.program_id(1) == 0)
    def _():
        for h in range(FOX_HEADS):
            ya_ref[:, h * LANES:(h + 1) * LANES] = fox_ref[h]
        for h in range(DIL_HEADS_PER_GROUP):
            lse = [lse_ref[g, h] for g in range(len(DIL_GROUPS))]
            top = jnp.maximum(jnp.maximum(lse[0], lse[1]), lse[2])
            w = [jnp.exp(x - top) for x in lse]
            tot = w[0] + w[1] + w[2]
            y = sum((w[g] / tot) * dil_ref[g, h].astype(F32) for g in range(len(DIL_GROUPS)))
            yb_ref[:, h * LANES:(h + 1) * LANES] = y.astype(BF16)
        gates = _sigmoid(small_ref[...])
        for h in range(NSA_Q_HEADS):
            y = jnp.zeros(cmp_ref.shape[1:], F32)
            for br, ref in enumerate((cmp_ref, slc_ref, win_ref)):
                col = SMALL_NSA_G + br * NSA_Q_HEADS + h
                y = y + gates[:, col:col + 1] * ref[h].astype(F32)
            yc_ref[:, h * LANES:(h + 1) * LANES] = y.astype(BF16)

    def gate(ref):
        return _sigmoid(jnp.concatenate([ref[c] for c in range(ncb)], axis=1).astype(F32))

    merged = (gate(g0_ref) * jnp.dot(ya_ref[...], wa_ref[...], preferred_element_type=F32)
              + gate(g1_ref) * jnp.dot(yb_ref[...], wb_ref[...], preferred_element_type=F32)
              + gate(g2_ref) * jnp.dot(yc_ref[...], wc_ref[...], preferred_element_type=F32))
    o_ref[...] = merged.astype(o_ref.dtype)


def merge_branches(proj, small, fox_o, dil_o, dil_lse, cmp_o, slc_o, win_o, wa, wb, wc, *, tm=512, tn=512):
    T = proj.shape[1]
    D = wa.shape[1]
    tm = min(tm, T)
    ncb = tn // LANES
    dpb = D // tn
    heads = lambda n: pl.BlockSpec((n, tm, LANES), lambda i, j: (0, i, 0))
    gate = lambda b: pl.BlockSpec((ncb, tm, LANES), lambda i, j: (CB_MERGE // ncb + b * dpb + j, i, 0))
    dil4 = pl.BlockSpec((len(DIL_GROUPS), DIL_HEADS_PER_GROUP, tm, LANES), lambda i, j: (0, 0, i, 0))
    wspec = lambda k: pl.BlockSpec((k, tn), lambda i, j: (0, j))
    ka, kb, kc = wa.shape[0], wb.shape[0], wc.shape[0]
    return pl.pallas_call(
        functools.partial(_merge_kernel, ncb=ncb),
        grid=(T // tm, D // tn),
        in_specs=[heads(FOX_HEADS), dil4, dil4, heads(NSA_Q_HEADS), heads(NSA_Q_HEADS), heads(NSA_Q_HEADS),
                  pl.BlockSpec((tm, LANES), lambda i, j: (i, 0)),
                  gate(0), gate(1), gate(2), wspec(ka), wspec(kb), wspec(kc)],
        out_specs=pl.BlockSpec((tm, tn), lambda i, j: (i, j)),
        out_shape=jax.ShapeDtypeStruct((T, D), BF16),
        scratch_shapes=[pltpu.VMEM((tm, ka), BF16), pltpu.VMEM((tm, kb), BF16), pltpu.VMEM((tm, kc), BF16)],
        compiler_params=_cparams(("parallel", "arbitrary")),
        name="merge_branches",
    )(fox_o, dil_o, dil_lse, cmp_o, slc_o, win_o, small, proj, proj, proj, wa, wb, wc)


def _matmul_residual_kernel(a_ref, w_ref, x_ref, o_ref):
    o_ref[...] = x_ref[...] + jnp.dot(a_ref[...], w_ref[...], preferred_element_type=F32)


def matmul_residual(a, w, x, *, tm=1024, tn=1024):
    T, K = a.shape
    N = w.shape[1]
    tm, tn = min(tm, T), min(tn, N)
    return pl.pallas_call(
        _matmul_residual_kernel,
        grid=(T // tm, N // tn),
        in_specs=[pl.BlockSpec((tm, K), lambda i, j: (i, 0)),
                  pl.BlockSpec((K, tn), lambda i, j: (0, j)),
                  pl.BlockSpec((tm, tn), lambda i, j: (i, j))],
        out_specs=pl.BlockSpec((tm, tn), lambda i, j: (i, j)),
        out_shape=jax.ShapeDtypeStruct((T, N), F32),
        compiler_params=_cparams(("parallel", "parallel")),
        name="matmul_residual",
    )(a, w, x)


def _silu(x):
    return x * _sigmoid(x)


def _swiglu_partial(h, wg_ref, wu_ref, wd_ref):
    act = _silu(jnp.dot(h, wg_ref[...], preferred_element_type=F32)) * \
        jnp.dot(h, wu_ref[...], preferred_element_type=F32)
    return jnp.dot(act.astype(BF16), wd_ref[...], preferred_element_type=F32)


def _ffn_kernel(x_ref, g_ref, wg_ref, wu_ref, wd_ref, gf_ref, o_ref, h_ref, acc_ref, *, final_norm):
    f = pl.program_id(1)

    @pl.when(f == 0)
    def _():
        h_ref[...] = _rms_rows(x_ref[...], g_ref[...]).astype(BF16)
        acc_ref[...] = jnp.zeros(acc_ref.shape, F32)

    acc_ref[...] += _swiglu_partial(h_ref[...], wg_ref, wu_ref, wd_ref)

    @pl.when(f == pl.num_programs(1) - 1)
    def _():
        out = x_ref[...] + acc_ref[...]
        if final_norm:
            out = _rms_rows(out, gf_ref[...])
        o_ref[...] = out


def ffn(x, g, wg, wu, wd, g_final, *, final_norm, tm=512, tf=512):
    T, D = x.shape
    F = wg.shape[1]
    tm, tf = min(tm, T), min(tf, F)
    return pl.pallas_call(
        functools.partial(_ffn_kernel, final_norm=final_norm),
        grid=(T // tm, F // tf),
        in_specs=[pl.BlockSpec((tm, D), lambda i, f: (i, 0)),
                  pl.BlockSpec((1, D), lambda i, f: (0, 0)),
                  pl.BlockSpec((D, tf), lambda i, f: (0, f)),
                  pl.BlockSpec((D, tf), lambda i, f: (0, f)),
                  pl.BlockSpec((tf, D), lambda i, f: (f, 0)),
                  pl.BlockSpec((1, D), lambda i, f: (0, 0))],
        out_specs=pl.BlockSpec((tm, D), lambda i, f: (i, 0)),
        out_shape=jax.ShapeDtypeStruct((T, D), F32),
        scratch_shapes=[pltpu.VMEM((tm, D), BF16), pltpu.VMEM((tm, D), F32)],
        compiler_params=_cparams(("parallel", "arbitrary")),
        name="ffn",
    )(x, g, wg, wu, wd, g_final)


def _route_kernel(x_ref, g_ref, r_ref, o_ref):
    h = _rms_rows(x_ref[...], g_ref[...])
    logits = jnp.dot(h, r_ref[...], preferred_element_type=F32, precision=HIGHEST)
    lane = lax.broadcasted_iota(jnp.int32, logits.shape, 1)
    logits = jnp.where(lane < N_EXPERTS, logits, NEG_INF)
    v1 = jnp.max(logits, axis=1, keepdims=True)
    i1 = jnp.min(jnp.where(logits == v1, lane, LANES), axis=1, keepdims=True)
    rest = jnp.where(lane == i1, NEG_INF, logits)
    v2 = jnp.max(rest, axis=1, keepdims=True)
    i2 = jnp.min(jnp.where(rest == v2, lane, LANES), axis=1, keepdims=True)
    e2 = jnp.exp(v2 - v1)
    den = 1.0 + e2
    rec = jnp.where(lane == ROUTE_CHOICE + i1, 1.0, jnp.where(lane == ROUTE_CHOICE + i2, 2.0, 0.0))
    rec = jnp.where(lane == ROUTE_WEIGHT, 1.0 / den, jnp.where(lane == ROUTE_WEIGHT + 1, e2 / den, rec))
    o_ref[...] = rec


def moe_route(x, g, router_pad, *, tm=512):
    T, D = x.shape
    tm = min(tm, T)
    return pl.pallas_call(
        _route_kernel,
        grid=(T // tm,),
        in_specs=[pl.BlockSpec((tm, D), lambda i: (i, 0)),
                  pl.BlockSpec((1, D), lambda i: (0, 0)),
                  pl.BlockSpec((D, LANES), lambda i: (0, 0))],
        out_specs=pl.BlockSpec((tm, LANES), lambda i: (i, 0)),
        out_shape=jax.ShapeDtypeStruct((T, LANES), F32),
        compiler_params=_cparams(("parallel",)),
        name="moe_route",
    )(x, g, router_pad)


def _moe_plan(route, T, tm):
    E = N_EXPERTS
    n_slots = TOP_K * T + E * tm
    n_tiles = n_slots // tm
    choice = route[:, ROUTE_CHOICE:ROUTE_CHOICE + E]
    sel = choice > 0.5
    seli = sel.astype(jnp.int32)
    padded = ((jnp.sum(seli, axis=0) + tm - 1) // tm) * tm
    seg_end = jnp.cumsum(padded)
    slot = (seg_end - padded)[None, :] + jnp.cumsum(seli, axis=0) - 1
    idx = jnp.where(sel, slot, n_slots).reshape(-1)
    tok = jnp.broadcast_to(jnp.arange(T, dtype=jnp.int32)[:, None], (T, E))
    dest = (choice.astype(jnp.int32) - 1) * T + tok
    src = jnp.zeros((n_slots,), jnp.int32).at[idx].set(tok.reshape(-1), mode="drop")
    dst = jnp.full((n_slots,), -1, jnp.int32).at[idx].set(dest.reshape(-1), mode="drop")
    n_used = (seg_end[-1] // tm).astype(jnp.int32)
    tile_start = jnp.arange(n_tiles, dtype=jnp.int32) * tm
    tile_expert = jnp.sum((seg_end[None, :] <= tile_start[:, None]).astype(jnp.int32), axis=1)
    tile_expert = jnp.minimum(tile_expert, E - 1)
    last_expert = jnp.take(tile_expert, n_used - 1)
    tile_expert = jnp.where(jnp.arange(n_tiles) < n_used, tile_expert, last_expert)
    return tile_expert, n_used.reshape(1), src, dst


def _moe_ffn_kernel(te_ref, nt_ref, src_ref, dst_ref, x_hbm, g_ref, wg_ref, wu_ref, wd_ref, y_hbm,
                    xbuf, h_ref, acc_ref, stage_ref, gsem, ssem, *, tm):
    i, f = pl.program_id(0), pl.program_id(1)
    nf = pl.num_programs(1)
    n_used = nt_ref[0]
    cur = lax.rem(i, 2)

    def gather_row(tile, r, buf):
        tok = src_ref[tile * tm + r]
        return pltpu.make_async_copy(x_hbm.at[pl.ds(tok, 1)], xbuf.at[buf, pl.ds(r, 1)], gsem.at[buf])

    def scatter_rows(tile, start):
        def one(r):
            row = dst_ref[tile * tm + r]

            @pl.when(row >= 0)
            def _():
                cp = pltpu.make_async_copy(stage_ref.at[pl.ds(r, 1)], y_hbm.at[pl.ds(row, 1)], ssem.at[0])
                cp.start() if start else cp.wait()

        for_rows(one)

    def for_rows(fn):
        def body(r, carry):
            fn(r)
            return carry
        lax.fori_loop(0, tm, body, 0, unroll=8)

    @pl.when(i < n_used)
    def _():
        @pl.when(f == 0)
        def _():
            @pl.when(i == 0)
            def _():
                for_rows(lambda r: gather_row(0, r, 0).start())

            for_rows(lambda r: gather_row(i, r, cur).wait())

            @pl.when(i + 1 < n_used)
            def _():
                for_rows(lambda r: gather_row(i + 1, r, 1 - cur).start())

            h_ref[...] = _rms_rows(xbuf[cur], g_ref[...]).astype(BF16)
            acc_ref[...] = jnp.zeros(acc_ref.shape, F32)

        acc_ref[...] += _swiglu_partial(h_ref[...], wg_ref, wu_ref, wd_ref)

        @pl.when(f == nf - 1)
        def _():
            @pl.when(i > 0)
            def _():
                scatter_rows(i - 1, start=False)

            stage_ref[...] = acc_ref[...]
            scatter_rows(i, start=True)

            @pl.when(i == n_used - 1)
            def _():
                scatter_rows(i, start=False)


def moe_experts(x, g, plan, wg, wu, wd, *, tm, tf=512):
    tile_expert, n_used, src, dst = plan
    T, D = x.shape
    F = wg.shape[2]
    tf = min(tf, F)
    nf = F // tf
    n_tiles = tile_expert.shape[0]

    def fidx(i, f, nt):
        return jnp.where(i < nt[0], f, nf - 1)

    grid_spec = pltpu.PrefetchScalarGridSpec(
        num_scalar_prefetch=4,
        grid=(n_tiles, nf),
        in_specs=[
            pl.BlockSpec(memory_space=pl.ANY),
            pl.BlockSpec((1, D), lambda i, f, te, nt, s, d: (0, 0)),
            pl.BlockSpec((None, D, tf), lambda i, f, te, nt, s, d: (te[i], 0, fidx(i, f, nt))),
            pl.BlockSpec((None, D, tf), lambda i, f, te, nt, s, d: (te[i], 0, fidx(i, f, nt))),
            pl.BlockSpec((None, tf, D), lambda i, f, te, nt, s, d: (te[i], fidx(i, f, nt), 0)),
        ],
        out_specs=pl.BlockSpec(memory_space=pl.ANY),
        scratch_shapes=[pltpu.VMEM((2, tm, D), F32), pltpu.VMEM((tm, D), BF16), pltpu.VMEM((tm, D), F32),
                        pltpu.VMEM((tm, D), F32), pltpu.SemaphoreType.DMA((2,)), pltpu.SemaphoreType.DMA((1,))],
    )
    return pl.pallas_call(
        functools.partial(_moe_ffn_kernel, tm=tm),
        grid_spec=grid_spec,
        out_shape=jax.ShapeDtypeStruct((TOP_K * T, D), F32),
        compiler_params=_cparams(("arbitrary", "arbitrary")),
        name="moe_experts",
    )(tile_expert, n_used, src, dst, x, g, wg, wu, wd)


def _moe_combine_kernel(x_ref, y1_ref, y2_ref, route_ref, gf_ref, o_ref, *, final_norm):
    w = route_ref[...]
    out = (x_ref[...] + w[:, ROUTE_WEIGHT:ROUTE_WEIGHT + 1] * y1_ref[...]
           + w[:, ROUTE_WEIGHT + 1:ROUTE_WEIGHT + 2] * y2_ref[...])
    if final_norm:
        out = _rms_rows(out, gf_ref[...])
    o_ref[...] = out


def moe_combine(x, y, route, g_final, *, final_norm, tc=256):
    T, D = x.shape
    tc = min(tc, T)
    second = T // tc
    return pl.pallas_call(
        functools.partial(_moe_combine_kernel, final_norm=final_norm),
        grid=(T // tc,),
        in_specs=[pl.BlockSpec((tc, D), lambda i: (i, 0)),
                  pl.BlockSpec((tc, D), lambda i: (i, 0)),
                  pl.BlockSpec((tc, D), lambda i: (second + i, 0)),
                  pl.BlockSpec((tc, LANES), lambda i: (i, 0)),
                  pl.BlockSpec((1, D), lambda i: (0, 0))],
        out_specs=pl.BlockSpec((tc, D), lambda i: (i, 0)),
        out_shape=jax.ShapeDtypeStruct((T, D), F32),
        compiler_params=_cparams(("parallel",)),
        name="moe_combine",
    )(x, y, y, route, g_final)


def moe_block(x, g, router, wg, wu, wd, g_final, *, final_norm, tm=512):
    T, D = x.shape
    tm = min(tm, T)
    router_pad = jnp.zeros((D, LANES), F32).at[:, :N_EXPERTS].set(router)
    route = moe_route(x, g, router_pad)
    plan = _moe_plan(route, T, tm)
    y = moe_experts(x, g, plan, wg.astype(BF16), wu.astype(BF16), wd.astype(BF16), tm=tm)
    return moe_combine(x, y, route, g_final, final_norm=final_norm)


def _split_w_in(w):
    D = w.shape[0]
    n_fox = 3 * FOX_HEADS * HEAD_DIM
    n_mid = (3 * DIL_HEADS + NSA_Q_HEADS + NSA_BRANCHES * 2 * NSA_KV_HEADS) * HEAD_DIM
    n_g = NSA_BRANCHES * NSA_Q_HEADS
    a0 = n_fox
    a1 = a0 + FOX_HEADS
    a2 = a1 + n_mid
    a3 = a2 + n_g
    main = jnp.concatenate([w[:, :a0], w[:, a1:a2], w[:, a3:]], axis=1).astype(BF16)
    small = jnp.concatenate([w[:, a0:a1], w[:, a2:a3],
                             jnp.zeros((D, LANES - FOX_HEADS - n_g), w.dtype)], axis=1).astype(BF16)
    return main, small


def mixing_block(x, B, S, norm_g, w_in, forget_bias, cmp_pe, cmp_w1, cmp_w2, wa, wb, wc, w_out, tables):
    T, D = x.shape
    w_main, w_small = _split_w_in(w_in)
    g = norm_g.reshape(1, D)
    proj = rms_proj(x, g, w_main, BF16, tm=1024, tn=1024)
    small = rms_proj(x, g, w_small, F32, tm=1024, tn=LANES)[0]

    bias_row = jnp.zeros((1, LANES), F32).at[0, :FOX_HEADS].set(forget_bias)
    c = logf_cumsum(small, bias_row, B, S)[:, :FOX_HEADS]
    c_col = c.T.reshape(FOX_HEADS, T, 1)
    c_row = c.T.reshape(FOX_HEADS, 1, T)
    fox_o = fox_attention(proj, c_col, c_row, B, S)

    dil = [dilated_group(proj, tables["dil"][gi], gi, B, S) for gi in range(len(DIL_GROUPS))]
    dil_o = jnp.stack([d[0] for d in dil])
    dil_lse = jnp.stack([d[1] for d in dil])

    kvc = nsa_compress(proj, cmp_pe, cmp_w1.astype(BF16), cmp_w2.astype(BF16), B, S)
    cmp_o, sel_t = nsa_cmp_select(proj, kvc, B, S)
    sel = jnp.swapaxes(sel_t, 2, 3).astype(BF16)
    slc_o = nsa_selected(proj, sel, tables["slc"], B, S, tq=tables["slc_tq"], tk=tables["slc_tk"])
    win_o = nsa_window(proj, tables["win"], B, S, tq=tables["win_tq"])

    merged = merge_branches(proj, small, fox_o, dil_o, dil_lse, cmp_o, slc_o, win_o,
                            wa.astype(BF16), wb.astype(BF16), wc.astype(BF16))
    return matmul_residual(merged, w_out.astype(BF16), x)


def bias_tables(rel_bias, S):
    slc_tq, slc_tk = min(256, S), min(512, S)
    win_tq = min(256, S)
    rel_c = rel_bias[:, DIL_HEADS:]
    return {
        "dil": dilated_bias_tables(rel_bias[:, :DIL_HEADS]),
        "slc": slc_bias_table(rel_c, S, slc_tq, slc_tk), "slc_tq": slc_tq, "slc_tk": slc_tk,
        "win": win_bias_table(rel_c, win_tq, min(WIN // win_tq + 1, 3)), "win_tq": win_tq,
    }


def kernel(x, rel_bias, norm_mix_g, norm_ffn_g, norm_final_g, w_in, fox_forget_bias, cmp_pe_k, cmp_w1_k, cmp_w2_k, cmp_pe_v, cmp_w1_v, cmp_w2_v, w_branch_a, w_branch_b, w_branch_c, w_out, ffn_w_gate, ffn_w_up, ffn_w_down, moe_router, moe_w_gate, moe_w_up, moe_w_down):
    B, S, D = x.shape
    T = B * S
    depth = w_in.shape[0]
    tables = bias_tables(rel_bias, S)
    g_final = norm_final_g.reshape(1, D)
    xt = x.reshape(T, D)
    for l in range(depth):
        xt = mixing_block(
            xt, B, S, norm_mix_g[l], w_in[l], fox_forget_bias[l],
            jnp.stack([cmp_pe_k[l], cmp_pe_v[l]]), jnp.stack([cmp_w1_k[l], cmp_w1_v[l]]),
            jnp.stack([cmp_w2_k[l], cmp_w2_v[l]]),
            w_branch_a[l], w_branch_b[l], w_branch_c[l], w_out[l], tables)
        g = norm_ffn_g[l].reshape(1, D)
        last = l == depth - 1
        j = l // 2
        if l % 2 == 0:
            xt = ffn(xt, g, ffn_w_gate[j].astype(BF16), ffn_w_up[j].astype(BF16),
                     ffn_w_down[j].astype(BF16), g_final, final_norm=last)
        else:
            xt = moe_block(xt, g, moe_router[j], moe_w_gate[j], moe_w_up[j], moe_w_down[j], g_final,
                           final_norm=last)
    return xt.reshape(B, S, D)
```

```python
import functools
import math

import jax
import jax.numpy as jnp
import numpy as np
from jax import lax
from jax.experimental import pallas as pl
from jax.experimental.pallas import tpu as pltpu

F32 = jnp.float32
BF16 = jnp.bfloat16
HIGHEST = lax.Precision.HIGHEST
NEG_INF = float("-inf")

LANES = 128
HEAD_DIM = 128
ATTN_SCALE = HEAD_DIM ** -0.5
RMS_EPS = 1e-6
VMEM_LIMIT_BYTES = 56 * 1024 * 1024

FOX_HEADS = 8
DIL_GROUPS = ((128, 1), (512, 4), (2048, 16))
DIL_HEADS_PER_GROUP = 4
DIL_HEADS = DIL_HEADS_PER_GROUP * len(DIL_GROUPS)
DIL_TAPS = 128
NSA_Q_HEADS = 8
NSA_KV_HEADS = 2
NSA_GQA = NSA_Q_HEADS // NSA_KV_HEADS
NSA_BRANCHES = 3
CMP_BLOCK = 32
CMP_STRIDE = 16
CMP_HIDDEN = 256
SLC_BLOCK = 64
SLC_COUNT = 16
WIN = 512
REL_BUCKETS = 32
REL_MAX_EXACT = 16
REL_MAX_DIST = 2048
N_EXPERTS = 8
TOP_K = 2
N_BRANCHES = 3

CB_FOX = 0
CB_DIL = CB_FOX + 3 * FOX_HEADS
CB_NSAQ = CB_DIL + 3 * DIL_HEADS
CB_NSAKV = CB_NSAQ + NSA_Q_HEADS
CB_MERGE = CB_NSAKV + NSA_BRANCHES * 2 * NSA_KV_HEADS
SMALL_FOX_F = 0
SMALL_NSA_G = FOX_HEADS
ROUTE_CHOICE = 0
ROUTE_WEIGHT = N_EXPERTS
MOE_ZERO_ROWS = 64


def _cparams(semantics):
    return pltpu.CompilerParams(dimension_semantics=semantics,
                                vmem_limit_bytes=VMEM_LIMIT_BYTES)


def _t5_bucket_np(dist):
    dist = np.maximum(dist, 0)
    d = np.maximum(dist, 1).astype(np.float32)
    log_ratio = np.log(d / np.float32(REL_MAX_EXACT)) / np.float32(math.log(REL_MAX_DIST / REL_MAX_EXACT))
    large = REL_MAX_EXACT + (log_ratio * np.float32(REL_BUCKETS - REL_MAX_EXACT)).astype(np.int32)
    large = np.minimum(large, REL_BUCKETS - 1)
    return np.where(dist < REL_MAX_EXACT, dist, large).astype(np.int32)


def _toeplitz_blocks(fn, offsets, rows, cols):
    lx = rows + cols
    m = np.arange(lx)
    rel = np.where(m < cols, -m, lx - m)
    out = []
    for c in offsets:
        v = fn(c + rel)
        flat = jnp.tile(v, (1, rows))[:, :rows * (lx - 1)]
        out.append(flat.reshape(-1, rows, lx - 1)[:, :, :cols])
    return jnp.stack(out).astype(F32)


def _rms_rows(x, g):
    inv = lax.rsqrt(jnp.mean(x * x, axis=-1, keepdims=True) + RMS_EPS)
    return (x * inv) * g


def _rms_proj_kernel(x_ref, g_ref, w_ref, o_ref, h_ref, *, ncb):
    @pl.when(pl.program_id(1) == 0)
    def _():
        h_ref[...] = _rms_rows(x_ref[...], g_ref[...]).astype(BF16)

    res = jnp.dot(h_ref[...], w_ref[...], preferred_element_type=F32)
    for c in range(ncb):
        o_ref[c] = res[:, c * LANES:(c + 1) * LANES].astype(o_ref.dtype)


def rms_proj(x, g, w, out_dtype, *, tm, tn):
    T, D = x.shape
    N = w.shape[1]
    tm, tn = min(tm, T), min(tn, N)
    ncb = tn // LANES
    return pl.pallas_call(
        functools.partial(_rms_proj_kernel, ncb=ncb),
        grid=(T // tm, N // tn),
        in_specs=[pl.BlockSpec((tm, D), lambda i, j: (i, 0)),
                  pl.BlockSpec((1, D), lambda i, j: (0, 0)),
                  pl.BlockSpec((D, tn), lambda i, j: (0, j))],
        out_specs=pl.BlockSpec((ncb, tm, LANES), lambda i, j: (j, i, 0)),
        out_shape=jax.ShapeDtypeStruct((N // LANES, T, LANES), out_dtype),
        scratch_shapes=[pltpu.VMEM((tm, D), BF16)],
        compiler_params=_cparams(("parallel", "arbitrary")),
        name="rms_proj",
    )(x, g, w)


CUMSUM_BLOCK = 256


def _logf_cumsum_kernel(f_ref, b_ref, c_ref, *, nblk):
    row = lax.broadcasted_iota(jnp.int32, (CUMSUM_BLOCK, CUMSUM_BLOCK), 0)
    col = lax.broadcasted_iota(jnp.int32, (CUMSUM_BLOCK, CUMSUM_BLOCK), 1)
    tri = jnp.where(col <= row, 1.0, 0.0).astype(F32)

    def body(i, carry):
        sl = pl.ds(pl.multiple_of(i * CUMSUM_BLOCK, CUMSUM_BLOCK), CUMSUM_BLOCK)
        z = f_ref[sl, :] + b_ref[...]
        logf = jnp.minimum(z, 0.0) - jnp.log1p(jnp.exp(-jnp.abs(z)))
        cs = jnp.dot(tri, logf, preferred_element_type=F32, precision=HIGHEST) + carry
        c_ref[sl, :] = cs
        return cs[CUMSUM_BLOCK - 1:CUMSUM_BLOCK, :]

    lax.fori_loop(0, nblk, body, jnp.zeros((1, LANES), F32))


def logf_cumsum(small, bias_row, B, S):
    T = B * S
    return pl.pallas_call(
        functools.partial(_logf_cumsum_kernel, nblk=S // CUMSUM_BLOCK),
        grid=(B,),
        in_specs=[pl.BlockSpec((S, LANES), lambda b: (b, 0)),
                  pl.BlockSpec((1, LANES), lambda b: (0, 0))],
        out_specs=pl.BlockSpec((S, LANES), lambda b: (b, 0)),
        out_shape=jax.ShapeDtypeStruct((T, LANES), F32),
        compiler_params=_cparams(("parallel",)),
        name="logf_cumsum",
    )(small, bias_row)


FLASH_ROW_CHUNK = 128


LOG2E = math.log2(math.e)


def _lane_tile(x, n):
    return jnp.concatenate([x] * n, axis=1)


def _flash_update(s2, v, m_ref, l_ref, acc_ref):
    m_old = m_ref[...]
    m_new = jnp.maximum(m_old, jnp.max(s2, axis=-1, keepdims=True))
    m_safe = jnp.where(m_new == NEG_INF, 0.0, m_new)
    alpha = jnp.exp2(m_old - m_safe)
    p = jnp.exp2(s2 - _lane_tile(m_safe, s2.shape[1] // LANES))
    l_ref[...] = alpha * l_ref[...] + jnp.sum(p, axis=-1, keepdims=True)
    acc_ref[...] = alpha * acc_ref[...] + jnp.dot(p.astype(BF16), v, preferred_element_type=F32)
    m_ref[...] = m_new


def _flash_init(m_ref, l_ref, acc_ref):
    m_ref[...] = jnp.full(m_ref.shape, NEG_INF, F32)
    l_ref[...] = jnp.zeros(l_ref.shape, F32)
    acc_ref[...] = jnp.zeros(acc_ref.shape, F32)


def _flash_result(l_ref, acc_ref):
    l = l_ref[...]
    return acc_ref[...] / jnp.where(l > 0.0, l, 1.0)


def _qk(q, k):
    return lax.dot_general(q, k, (((1,), (1,)), ((), ())), preferred_element_type=F32)


def _fox_kernel(qi_ref, ki_ref, q_ref, k_ref, v_ref, cq_ref, ck_ref, o_ref, m_ref, l_ref, acc_ref, cqb_ref,
                *, tq, tk, rc):
    p = pl.program_id(2)
    qi, ki = qi_ref[p], ki_ref[p]

    @pl.when(ki == 0)
    def _():
        _flash_init(m_ref, l_ref, acc_ref)
        cqb_ref[...] = jnp.broadcast_to(cq_ref[...] * LOG2E, cqb_ref.shape)

    def step(causal_mask):
        k, v = k_ref[...], v_ref[...]
        ck = ck_ref[...] * LOG2E
        chunks = [pl.ds(c * rc, rc) for c in range(tq // rc)]
        qk = [_qk(q_ref[rows, :], k) for rows in chunks]
        for c, rows in enumerate(chunks):
            s = qk[c] * (ATTN_SCALE * LOG2E) + (_lane_tile(cqb_ref[rows, :], tk // LANES) - ck)
            if causal_mask:
                qpos = qi * tq + c * rc + lax.broadcasted_iota(jnp.int32, (rc, tk), 0)
                kpos = ki * tk + lax.broadcasted_iota(jnp.int32, (rc, tk), 1)
                s = jnp.where(kpos <= qpos, s, NEG_INF)
            _flash_update(s, v, m_ref.at[rows], l_ref.at[rows], acc_ref.at[rows])

    crosses_diagonal = (ki + 1) * tk - 1 > qi * tq

    @pl.when(crosses_diagonal)
    def _():
        step(True)

    @pl.when(jnp.logical_not(crosses_diagonal))
    def _():
        step(False)

    @pl.when(ki == ((qi + 1) * tq - 1) // tk)
    def _():
        o_ref[...] = _flash_result(l_ref, acc_ref).astype(o_ref.dtype)


def _causal_pairs(nq, tq, tk):
    qs, ks = [], []
    for qi in range(nq):
        for ki in range(((qi + 1) * tq - 1) // tk + 1):
            qs.append(qi)
            ks.append(ki)
    return jnp.asarray(qs, jnp.int32), jnp.asarray(ks, jnp.int32)


def fox_attention(proj, c_col, c_row, B, S, *, tq=512, tk=512):
    tq, tk = min(tq, S), min(tk, S)
    nq, nk = S // tq, S // tk
    H = FOX_HEADS
    T = B * S
    qis, kis = _causal_pairs(nq, tq, tk)
    grid_spec = pltpu.PrefetchScalarGridSpec(
        num_scalar_prefetch=2,
        grid=(B, H, int(qis.shape[0])),
        in_specs=[
            pl.BlockSpec((None, tq, LANES), lambda b, h, p, qi, ki: (CB_FOX + h, b * nq + qi[p], 0)),
            pl.BlockSpec((None, tk, LANES), lambda b, h, p, qi, ki: (CB_FOX + H + h, b * nk + ki[p], 0)),
            pl.BlockSpec((None, tk, LANES), lambda b, h, p, qi, ki: (CB_FOX + 2 * H + h, b * nk + ki[p], 0)),
            pl.BlockSpec((None, tq, 1), lambda b, h, p, qi, ki: (h, b * nq + qi[p], 0)),
            pl.BlockSpec((None, 1, tk), lambda b, h, p, qi, ki: (h, 0, b * nk + ki[p])),
        ],
        out_specs=pl.BlockSpec((None, tq, LANES), lambda b, h, p, qi, ki: (h, b * nq + qi[p], 0)),
        scratch_shapes=[pltpu.VMEM((tq, LANES), F32)] * 4,
    )
    return pl.pallas_call(
        functools.partial(_fox_kernel, tq=tq, tk=tk, rc=min(FLASH_ROW_CHUNK, tq)),
        grid_spec=grid_spec,
        out_shape=jax.ShapeDtypeStruct((H, T, LANES), BF16),
        compiler_params=_cparams(("parallel", "parallel", "arbitrary")),
        name="fox_attention",
    )(qis, kis, proj, proj, proj, c_col, c_row)


def _dil_kernel(q_ref, kp_ref, kc_ref, vp_ref, vc_ref, bias_ref, o_ref, lse_ref, *, dil, tu):
    ui = pl.program_id(2)
    kcol = lax.broadcasted_iota(jnp.int32, (tu, 2 * tu), 1)
    first_ok = jnp.logical_or(ui > 0, kcol >= tu)
    bias = jnp.where(first_ok, bias_ref[...], NEG_INF)
    lanes = [slice(r * LANES, (r + 1) * LANES) for r in range(dil)]
    qk = [_qk(q_ref[:, sl], jnp.concatenate([kp_ref[:, sl], kc_ref[:, sl]], axis=0)) for sl in lanes]
    for r, sl in enumerate(lanes):
        v = jnp.concatenate([vp_ref[:, sl], vc_ref[:, sl]], axis=0)
        s = qk[r] * ATTN_SCALE + bias
        m = jnp.max(s, axis=-1, keepdims=True)
        e = jnp.exp(s - m)
        den = jnp.sum(e, axis=-1, keepdims=True)
        o = jnp.dot(e.astype(BF16), v, preferred_element_type=F32) / den
        o_ref[:, sl] = o.astype(o_ref.dtype)
        lse_ref[:, sl] = jnp.broadcast_to(m + jnp.log(den), (tu, LANES))


def dilated_group(proj, bias_tbl, group, B, S):
    dil = DIL_GROUPS[group][1]
    tu = DIL_TAPS
    Hg = DIL_HEADS_PER_GROUP
    T = B * S
    nu = S // dil // tu
    c0 = CB_DIL + group * Hg
    slabs = jnp.concatenate([proj[c0 + t * DIL_HEADS:c0 + t * DIL_HEADS + Hg] for t in range(3)])
    view = slabs.reshape(3 * Hg, T // dil, dil * LANES)
    cq, ck, cv = 0, Hg, 2 * Hg
    blk = (None, tu, dil * LANES)
    prev = lambda c: (lambda b, h, u: (c + h, b * nu + jnp.maximum(u - 1, 0), 0))
    cur = lambda c: (lambda b, h, u: (c + h, b * nu + u, 0))
    o, lse = pl.pallas_call(
        functools.partial(_dil_kernel, dil=dil, tu=tu),
        grid=(B, Hg, nu),
        in_specs=[pl.BlockSpec(blk, cur(cq)),
                  pl.BlockSpec(blk, prev(ck)), pl.BlockSpec(blk, cur(ck)),
                  pl.BlockSpec(blk, prev(cv)), pl.BlockSpec(blk, cur(cv)),
                  pl.BlockSpec((None, tu, 2 * tu), lambda b, h, u: (h, 0, 0))],
        out_specs=[pl.BlockSpec(blk, cur(0)), pl.BlockSpec(blk, cur(0))],
        out_shape=[jax.ShapeDtypeStruct((Hg, T // dil, dil * LANES), BF16),
                   jax.ShapeDtypeStruct((Hg, T // dil, dil * LANES), F32)],
        compiler_params=_cparams(("parallel", "parallel", "arbitrary")),
        name=f"dilated_group{group}",
    )(view, view, view, view, view, bias_tbl)
    return o.reshape(Hg, T, LANES), lse.reshape(Hg, T, LANES)


def dilated_bias_tables(rel_bias_b):
    tu = DIL_TAPS
    tables = []
    for g, (_, dil) in enumerate(DIL_GROUPS):
        heads = rel_bias_b[:, g * DIL_HEADS_PER_GROUP:(g + 1) * DIL_HEADS_PER_GROUP]

        def fn(taps, heads=heads, dil=dil):
            valid = (taps >= 0) & (taps <= DIL_TAPS)
            vals = heads[_t5_bucket_np(np.where(valid, taps, 0) * dil)].T
            return jnp.where(valid[None, :], vals, NEG_INF)

        tables.append(_toeplitz_blocks(fn, [tu], tu, 2 * tu)[0])
    return tables


def _compress_kernel(x_ref, pelo_ref, pehi_ref, w1a_ref, w1b_ref, w2_ref, o_ref, *, nchunk):
    x = x_ref[...].astype(F32)
    u0 = jnp.dot((x + pelo_ref[...]).astype(BF16), w1a_ref[...], preferred_element_type=F32)
    u1 = jnp.dot((x + pehi_ref[...]).astype(BF16), w1b_ref[...], preferred_element_type=F32)
    pre = u0 + pltpu.roll(u1, nchunk - 1, 0)
    hid = jax.nn.gelu(pre)
    out = jnp.dot(hid.astype(BF16), w2_ref[...], preferred_element_type=F32)
    row = lax.broadcasted_iota(jnp.int32, out.shape, 0)
    o_ref[...] = jnp.where(row < nchunk - 1, out, 0.0).astype(o_ref.dtype)


def nsa_compress(proj, pe, w1, w2, B, S):
    nchunk = S // CMP_STRIDE
    half = CMP_STRIDE * HEAD_DIM
    Hkv = NSA_KV_HEADS
    x = proj[CB_NSAKV:CB_NSAKV + 2 * Hkv].reshape(2, Hkv, B, nchunk, half)
    pe_lo = pe[:, :CMP_STRIDE].reshape(2, 1, half)
    pe_hi = pe[:, CMP_STRIDE:].reshape(2, 1, half)
    return pl.pallas_call(
        functools.partial(_compress_kernel, nchunk=nchunk),
        grid=(2, B, Hkv),
        in_specs=[pl.BlockSpec((None, None, None, nchunk, half), lambda t, b, h: (t, h, b, 0, 0)),
                  pl.BlockSpec((None, 1, half), lambda t, b, h: (t, 0, 0)),
                  pl.BlockSpec((None, 1, half), lambda t, b, h: (t, 0, 0)),
                  pl.BlockSpec((None, half, CMP_HIDDEN), lambda t, b, h: (t, 0, 0)),
                  pl.BlockSpec((None, half, CMP_HIDDEN), lambda t, b, h: (t, 1, 0)),
                  pl.BlockSpec((None, CMP_HIDDEN, HEAD_DIM), lambda t, b, h: (t, 0, 0))],
        out_specs=pl.BlockSpec((None, None, None, nchunk, HEAD_DIM), lambda t, b, h: (t, b, h, 0, 0)),
        out_shape=jax.ShapeDtypeStruct((2, B, Hkv, nchunk, HEAD_DIM), BF16),
        compiler_params=_cparams(("parallel", "parallel", "parallel")),
        name="nsa_compress",
    )(x, pe_lo, pe_hi, w1, w1, w2)


def _masked_softmax(s, mask, axis):
    s = jnp.where(mask, s, NEG_INF)
    m = jnp.max(s, axis=axis, keepdims=True)
    m = jnp.where(m == NEG_INF, 0.0, m)
    e = jnp.exp(s - m)
    den = jnp.sum(e, axis=axis, keepdims=True)
    return e / jnp.where(den > 0.0, den, 1.0)


def _cmp_select_kernel(q_ref, kc_ref, vc_ref, mt_ref, o_ref, sel_ref, *, tq, nchunk, n_slc, n_sel):
    q0 = pl.program_id(2) * tq
    kc = kc_ref[...]
    vc = vc_ref[...]
    pos_r = q0 + lax.broadcasted_iota(jnp.int32, (tq, nchunk), 0)
    end_r = lax.broadcasted_iota(jnp.int32, (tq, nchunk), 1) * CMP_STRIDE + (CMP_BLOCK - 1)
    vis_r = end_r <= pos_r
    pos_c = q0 + lax.broadcasted_iota(jnp.int32, (nchunk, tq), 1)
    end_c = lax.broadcasted_iota(jnp.int32, (nchunk, tq), 0) * CMP_STRIDE + (CMP_BLOCK - 1)
    vis_c = end_c <= pos_c
    imp = jnp.zeros((nchunk, tq), F32)
    for g in range(NSA_GQA):
        q = q_ref[g]
        p = _masked_softmax(_qk(q, kc) * ATTN_SCALE, vis_r, -1)
        o_ref[g] = jnp.dot(p.astype(BF16), vc, preferred_element_type=F32).astype(o_ref.dtype)
        imp = imp + _masked_softmax(_qk(kc, q) * ATTN_SCALE, vis_c, 0)
    p_slc = jnp.dot(mt_ref[...], imp, preferred_element_type=F32, precision=HIGHEST)

    blk = lax.broadcasted_iota(jnp.int32, (n_slc, tq), 0)
    cur = (q0 + lax.broadcasted_iota(jnp.int32, (n_slc, tq), 1)) // SLC_BLOCK
    forced = (blk == 0) | (blk == cur) | (blk == cur - 1)
    allowed = blk <= cur
    score = jnp.where(forced, 1e30, jnp.where(allowed, p_slc, -1.0))
    chosen = jnp.zeros((n_slc, tq), F32)
    for _ in range(n_sel):
        top = jnp.max(score, axis=0, keepdims=True)
        first = jnp.min(jnp.where(score == top, blk, n_slc), axis=0, keepdims=True)
        hit = blk == first
        chosen = jnp.where(hit, 1.0, chosen)
        score = jnp.where(hit, -2.0, score)
    sel_ref[...] = jnp.where(allowed, chosen, 0.0)


def nsa_cmp_select(proj, kvc, B, S, *, tq=512):
    tq = min(tq, S)
    nq = S // tq
    T = B * S
    nchunk = S // CMP_STRIDE
    n_slc = S // SLC_BLOCK
    n_sel = min(SLC_COUNT, n_slc)
    ratio, n_inner = SLC_BLOCK // CMP_STRIDE, CMP_BLOCK // CMP_STRIDE
    mt = np.zeros((n_slc, nchunk), np.float32)
    for j in range(n_slc):
        for m in range(ratio):
            for n in range(n_inner):
                c = ratio * j + m - n
                if 0 <= c < nchunk - 1:
                    mt[j, c] += 1.0
    G = NSA_GQA
    kv_spec = lambda t: pl.BlockSpec((None, None, None, nchunk, HEAD_DIM), lambda b, h, i: (t, b, h, 0, 0))
    return pl.pallas_call(
        functools.partial(_cmp_select_kernel, tq=tq, nchunk=nchunk, n_slc=n_slc, n_sel=n_sel),
        grid=(B, NSA_KV_HEADS, nq),
        in_specs=[pl.BlockSpec((G, tq, LANES), lambda b, h, i: (CB_NSAQ // G + h, b * nq + i, 0)),
                  kv_spec(0), kv_spec(1),
                  pl.BlockSpec((n_slc, nchunk), lambda b, h, i: (0, 0))],
        out_specs=[pl.BlockSpec((G, tq, LANES), lambda b, h, i: (h, b * nq + i, 0)),
                   pl.BlockSpec((None, None, n_slc, tq), lambda b, h, i: (b, h, 0, i))],
        out_shape=[jax.ShapeDtypeStruct((NSA_Q_HEADS, T, LANES), BF16),
                   jax.ShapeDtypeStruct((B, NSA_KV_HEADS, n_slc, S), F32)],
        compiler_params=_cparams(("parallel", "parallel", "arbitrary")),
        name="nsa_cmp_select",
    )(proj, kvc, kvc, jnp.asarray(mt))


def _slc_kernel(qi_ref, ki_ref, q_ref, k_ref, v_ref, sel_ref, bias_ref, o_ref, m_ref, l_ref, acc_ref,
                *, tq, tk, n_slc):
    p = pl.program_id(2)
    qi, ki = qi_ref[p], ki_ref[p]

    @pl.when(ki == 0)
    def _():
        _flash_init(m_ref, l_ref, acc_ref)

    blk_of_key = ki * (tk // SLC_BLOCK) + lax.broadcasted_iota(jnp.int32, (n_slc, tk), 1) // SLC_BLOCK
    expand = jnp.where(lax.broadcasted_iota(jnp.int32, (n_slc, tk), 0) == blk_of_key, 1.0, 0.0).astype(BF16)
    picked = jnp.dot(sel_ref[...], expand, preferred_element_type=F32)

    def step(causal_mask):
        keep = picked > 0.5
        if causal_mask:
            qpos = qi * tq + lax.broadcasted_iota(jnp.int32, (tq, tk), 0)
            kpos = ki * tk + lax.broadcasted_iota(jnp.int32, (tq, tk), 1)
            keep = jnp.where(kpos <= qpos, picked, 0.0) > 0.5
        k, v = k_ref[...], v_ref[...]
        qk = [_qk(q_ref[g], k) for g in range(NSA_GQA)]
        for g in range(NSA_GQA):
            s = qk[g] * (ATTN_SCALE * LOG2E) + bias_ref[g]
            s = jnp.where(keep, s, NEG_INF)
            _flash_update(s, v, m_ref.at[g], l_ref.at[g], acc_ref.at[g])

    crosses_diagonal = (ki + 1) * tk - 1 > qi * tq

    @pl.when(crosses_diagonal)
    def _():
        step(True)

    @pl.when(jnp.logical_not(crosses_diagonal))
    def _():
        step(False)

    @pl.when(ki == ((qi + 1) * tq - 1) // tk)
    def _():
        for g in range(NSA_GQA):
            o_ref[g] = _flash_result(l_ref.at[g], acc_ref.at[g]).astype(o_ref.dtype)


def slc_bias_table(rel_bias_c, S, tq, tk):
    buckets = _t5_bucket_np(np.arange(S + tk))
    not_last = np.nonzero(buckets != REL_BUCKETS - 1)[0]
    far_start = int(not_last[-1]) + 1 if not_last.size else 0
    n_delta = min(S // tq, -(-(far_start + tk - 1) // tq) + 1)

    def fn(d):
        return rel_bias_c[_t5_bucket_np(d)].T * LOG2E

    return _toeplitz_blocks(fn, [dl * tq for dl in range(n_delta)], tq, tk)


def nsa_selected(proj, sel, bias_tbl, B, S, *, tq=256, tk=512):
    tq, tk = min(tq, S), min(tk, S)
    nq, nk = S // tq, S // tk
    T = B * S
    n_slc = S // SLC_BLOCK
    G = NSA_GQA
    ck = CB_NSAKV + (1 * 2 + 0) * NSA_KV_HEADS
    cv = CB_NSAKV + (1 * 2 + 1) * NSA_KV_HEADS
    qis, kis = _causal_pairs(nq, tq, tk)
    n_delta = bias_tbl.shape[0]
    grid_spec = pltpu.PrefetchScalarGridSpec(
        num_scalar_prefetch=2,
        grid=(B, NSA_KV_HEADS, int(qis.shape[0])),
        in_specs=[
            pl.BlockSpec((G, tq, LANES), lambda b, h, p, qi, ki: (CB_NSAQ // G + h, b * nq + qi[p], 0)),
            pl.BlockSpec((None, tk, LANES), lambda b, h, p, qi, ki: (ck + h, b * nk + ki[p], 0)),
            pl.BlockSpec((None, tk, LANES), lambda b, h, p, qi, ki: (cv + h, b * nk + ki[p], 0)),
            pl.BlockSpec((None, None, tq, n_slc), lambda b, h, p, qi, ki: (b, h, qi[p], 0)),
            pl.BlockSpec((None, G, tq, tk),
                         lambda b, h, p, qi, ki: (jnp.minimum(qi[p] - ki[p] * (tk // tq), n_delta - 1), h, 0, 0)),
        ],
        out_specs=pl.BlockSpec((G, tq, LANES), lambda b, h, p, qi, ki: (h, b * nq + qi[p], 0)),
        scratch_shapes=[pltpu.VMEM((G, tq, LANES), F32)] * 3,
    )
    return pl.pallas_call(
        functools.partial(_slc_kernel, tq=tq, tk=tk, n_slc=n_slc),
        grid_spec=grid_spec,
        out_shape=jax.ShapeDtypeStruct((NSA_Q_HEADS, T, LANES), BF16),
        compiler_params=_cparams(("parallel", "parallel", "arbitrary")),
        name="nsa_selected",
    )(qis, kis, proj, proj, proj, sel, bias_tbl)


def _win_kernel(q_ref, k0_ref, k1_ref, k2_ref, v0_ref, v1_ref, v2_ref, bias_ref, o_ref, *, tq, nkb):
    qi = pl.program_id(2)
    k = jnp.concatenate([r[...] for r in (k0_ref, k1_ref, k2_ref)][-nkb:], axis=0)
    v = jnp.concatenate([r[...] for r in (v0_ref, v1_ref, v2_ref)][-nkb:], axis=0)
    kpos = (qi - (nkb - 1)) * tq + lax.broadcasted_iota(jnp.int32, (tq, nkb * tq), 1)
    qk = [_qk(q_ref[g], k) for g in range(NSA_GQA)]
    for g in range(NSA_GQA):
        s = qk[g] * ATTN_SCALE + bias_ref[g]
        p = _masked_softmax(s, kpos >= 0, -1)
        o_ref[g] = jnp.dot(p.astype(BF16), v, preferred_element_type=F32).astype(o_ref.dtype)


def win_bias_table(rel_bias_c, tq, nkb):
    def fn(dist):
        valid = (dist >= 0) & (dist < WIN)
        return jnp.where(valid[None, :], rel_bias_c[_t5_bucket_np(dist)].T, NEG_INF)

    return _toeplitz_blocks(fn, [(nkb - 1) * tq], tq, nkb * tq)[0]


def nsa_window(proj, bias_tbl, B, S, *, tq=256):
    tq = min(tq, S)
    nq = S // tq
    T = B * S
    G = NSA_GQA
    nkb = min(WIN // tq + 1, 3)
    ck = CB_NSAKV + (2 * 2 + 0) * NSA_KV_HEADS
    cv = CB_NSAKV + (2 * 2 + 1) * NSA_KV_HEADS
    kv = lambda c, back: pl.BlockSpec(
        (None, tq, LANES), lambda b, h, i: (c + h, b * nq + jnp.maximum(i - back, 0), 0))
    return pl.pallas_call(
        functools.partial(_win_kernel, tq=tq, nkb=nkb),
        grid=(B, NSA_KV_HEADS, nq),
        in_specs=[pl.BlockSpec((G, tq, LANES), lambda b, h, i: (CB_NSAQ // G + h, b * nq + i, 0)),
                  kv(ck, 2), kv(ck, 1), kv(ck, 0), kv(cv, 2), kv(cv, 1), kv(cv, 0),
                  pl.BlockSpec((G, tq, nkb * tq), lambda b, h, i: (h, 0, 0))],
        out_specs=pl.BlockSpec((G, tq, LANES), lambda b, h, i: (h, b * nq + i, 0)),
        out_shape=jax.ShapeDtypeStruct((NSA_Q_HEADS, T, LANES), BF16),
        compiler_params=_cparams(("parallel", "parallel", "arbitrary")),
        name="nsa_window",
    )(proj, proj, proj, proj, proj, proj, proj, bias_tbl)


def _sigmoid(x):
    return 1.0 / (1.0 + jnp.exp(-x))


def _merge_kernel(fox_ref, dil_ref, lse_ref, cmp_ref, slc_ref, win_ref, small_ref,
                  g0_ref, g1_ref, g2_ref, wa_ref, wb_ref, wc_ref, o_ref, ya_ref, yb_ref, yc_ref, *, ncb):
    @pl.when(pl.program_id(1) == 0)
    def _():
        for h in range(FOX_HEADS):
            ya_ref[:, h * LANES:(h + 1) * LANES] = fox_ref[h]
        for h in range(DIL_HEADS_PER_GROUP):
            lse = [lse_ref[g, h] for g in range(len(DIL_GROUPS))]
            top = jnp.maximum(jnp.maximum(lse[0], lse[1]), lse[2])
            w = [jnp.exp(x - top) for x in lse]
            tot = w[0] + w[1] + w[2]
            y = sum((w[g] / tot) * dil_ref[g, h].astype(F32) for g in range(len(DIL_GROUPS)))
            yb_ref[:, h * LANES:(h + 1) * LANES] = y.astype(BF16)
        gates = _sigmoid(small_ref[...])
        for h in range(NSA_Q_HEADS):
            y = jnp.zeros(cmp_ref.shape[1:], F32)
            for br, ref in enumerate((cmp_ref, slc_ref, win_ref)):
                col = SMALL_NSA_G + br * NSA_Q_HEADS + h
                y = y + gates[:, col:col + 1] * ref[h].astype(F32)
            yc_ref[:, h * LANES:(h + 1) * LANES] = y.astype(BF16)

    def gate(ref):
        return _sigmoid(jnp.concatenate([ref[c] for c in range(ncb)], axis=1).astype(F32))

    merged = (gate(g0_ref) * jnp.dot(ya_ref[...], wa_ref[...], preferred_element_type=F32)
              + gate(g1_ref) * jnp.dot(yb_ref[...], wb_ref[...], preferred_element_type=F32)
              + gate(g2_ref) * jnp.dot(yc_ref[...], wc_ref[...], preferred_element_type=F32))
    o_ref[...] = merged.astype(o_ref.dtype)


def merge_branches(proj, small, fox_o, dil_o, dil_lse, cmp_o, slc_o, win_o, wa, wb, wc, *, tm=512, tn=512):
    T = proj.shape[1]
    D = wa.shape[1]
    tm = min(tm, T)
    ncb = tn // LANES
    dpb = D // tn
    heads = lambda n: pl.BlockSpec((n, tm, LANES), lambda i, j: (0, i, 0))
    gate = lambda b: pl.BlockSpec((ncb, tm, LANES), lambda i, j: (CB_MERGE // ncb + b * dpb + j, i, 0))
    dil4 = pl.BlockSpec((len(DIL_GROUPS), DIL_HEADS_PER_GROUP, tm, LANES), lambda i, j: (0, 0, i, 0))
    wspec = lambda k: pl.BlockSpec((k, tn), lambda i, j: (0, j))
    ka, kb, kc = wa.shape[0], wb.shape[0], wc.shape[0]
    return pl.pallas_call(
        functools.partial(_merge_kernel, ncb=ncb),
        grid=(T // tm, D // tn),
        in_specs=[heads(FOX_HEADS), dil4, dil4, heads(NSA_Q_HEADS), heads(NSA_Q_HEADS), heads(NSA_Q_HEADS),
                  pl.BlockSpec((tm, LANES), lambda i, j: (i, 0)),
                  gate(0), gate(1), gate(2), wspec(ka), wspec(kb), wspec(kc)],
        out_specs=pl.BlockSpec((tm, tn), lambda i, j: (i, j)),
        out_shape=jax.ShapeDtypeStruct((T, D), BF16),
        scratch_shapes=[pltpu.VMEM((tm, ka), BF16), pltpu.VMEM((tm, kb), BF16), pltpu.VMEM((tm, kc), BF16)],
        compiler_params=_cparams(("parallel", "arbitrary")),
        name="merge_branches",
    )(fox_o, dil_o, dil_lse, cmp_o, slc_o, win_o, small, proj, proj, proj, wa, wb, wc)


def _matmul_residual_kernel(a_ref, w_ref, x_ref, o_ref):
    o_ref[...] = x_ref[...] + jnp.dot(a_ref[...], w_ref[...], preferred_element_type=F32)


def matmul_residual(a, w, x, *, tm=1024, tn=1024):
    T, K = a.shape
    N = w.shape[1]
    tm, tn = min(tm, T), min(tn, N)
    return pl.pallas_call(
        _matmul_residual_kernel,
        grid=(T // tm, N // tn),
        in_specs=[pl.BlockSpec((tm, K), lambda i, j: (i, 0)),
                  pl.BlockSpec((K, tn), lambda i, j: (0, j)),
                  pl.BlockSpec((tm, tn), lambda i, j: (i, j))],
        out_specs=pl.BlockSpec((tm, tn), lambda i, j: (i, j)),
        out_shape=jax.ShapeDtypeStruct((T, N), F32),
        compiler_params=_cparams(("parallel", "parallel")),
        name="matmul_residual",
    )(a, w, x)


def _silu(x):
    return x * _sigmoid(x)


def _swiglu_partial(h, wg_ref, wu_ref, wd_ref):
    act = _silu(jnp.dot(h, wg_ref[...], preferred_element_type=F32)) * \
        jnp.dot(h, wu_ref[...], preferred_element_type=F32)
    return jnp.dot(act.astype(BF16), wd_ref[...], preferred_element_type=F32)


def _ffn_kernel(x_ref, g_ref, wg_ref, wu_ref, wd_ref, gf_ref, o_ref, h_ref, acc_ref, *, final_norm):
    f = pl.program_id(1)

    @pl.when(f == 0)
    def _():
        h_ref[...] = _rms_rows(x_ref[...], g_ref[...]).astype(BF16)
        acc_ref[...] = jnp.zeros(acc_ref.shape, F32)

    acc_ref[...] += _swiglu_partial(h_ref[...], wg_ref, wu_ref, wd_ref)

    @pl.when(f == pl.num_programs(1) - 1)
    def _():
        out = x_ref[...] + acc_ref[...]
        if final_norm:
            out = _rms_rows(out, gf_ref[...])
        o_ref[...] = out


def ffn(x, g, wg, wu, wd, g_final, *, final_norm, tm=512, tf=512):
    T, D = x.shape
    F = wg.shape[1]
    tm, tf = min(tm, T), min(tf, F)
    return pl.pallas_call(
        functools.partial(_ffn_kernel, final_norm=final_norm),
        grid=(T // tm, F // tf),
        in_specs=[pl.BlockSpec((tm, D), lambda i, f: (i, 0)),
                  pl.BlockSpec((1, D), lambda i, f: (0, 0)),
                  pl.BlockSpec((D, tf), lambda i, f: (0, f)),
                  pl.BlockSpec((D, tf), lambda i, f: (0, f)),
                  pl.BlockSpec((tf, D), lambda i, f: (f, 0)),
                  pl.BlockSpec((1, D), lambda i, f: (0, 0))],
        out_specs=pl.BlockSpec((tm, D), lambda i, f: (i, 0)),
        out_shape=jax.ShapeDtypeStruct((T, D), F32),
        scratch_shapes=[pltpu.VMEM((tm, D), BF16), pltpu.VMEM((tm, D), F32)],
        compiler_params=_cparams(("parallel", "arbitrary")),
        name="ffn",
    )(x, g, wg, wu, wd, g_final)


def _route_kernel(x_ref, g_ref, r_ref, o_ref):
    h = _rms_rows(x_ref[...], g_ref[...])
    logits = jnp.dot(h, r_ref[...], preferred_element_type=F32, precision=HIGHEST)
    lane = lax.broadcasted_iota(jnp.int32, logits.shape, 1)
    logits = jnp.where(lane < N_EXPERTS, logits, NEG_INF)
    v1 = jnp.max(logits, axis=1, keepdims=True)
    i1 = jnp.min(jnp.where(logits == v1, lane, LANES), axis=1, keepdims=True)
    rest = jnp.where(lane == i1, NEG_INF, logits)
    v2 = jnp.max(rest, axis=1, keepdims=True)
    i2 = jnp.min(jnp.where(rest == v2, lane, LANES), axis=1, keepdims=True)
    e2 = jnp.exp(v2 - v1)
    den = 1.0 + e2
    rec = jnp.where(lane == ROUTE_CHOICE + i1, 1.0, jnp.where(lane == ROUTE_CHOICE + i2, 2.0, 0.0))
    rec = jnp.where(lane == ROUTE_WEIGHT, 1.0 / den, jnp.where(lane == ROUTE_WEIGHT + 1, e2 / den, rec))
    o_ref[...] = rec


def moe_route(x, g, router_pad, *, tm=512):
    T, D = x.shape
    tm = min(tm, T)
    return pl.pallas_call(
        _route_kernel,
        grid=(T // tm,),
        in_specs=[pl.BlockSpec((tm, D), lambda i: (i, 0)),
                  pl.BlockSpec((1, D), lambda i: (0, 0)),
                  pl.BlockSpec((D, LANES), lambda i: (0, 0))],
        out_specs=pl.BlockSpec((tm, LANES), lambda i: (i, 0)),
        out_shape=jax.ShapeDtypeStruct((T, LANES), F32),
        compiler_params=_cparams(("parallel",)),
        name="moe_route",
    )(x, g, router_pad)


def _moe_plan(route, T, tm):
    E = N_EXPERTS
    n_tiles = (TOP_K * T + E * tm) // tm
    choice = route[:, ROUTE_CHOICE:ROUTE_CHOICE + E]
    seli = (choice > 0.5).astype(jnp.int32)
    counts = jnp.sum(seli, axis=0)
    padded = ((counts + tm - 1) // tm) * tm
    seg_end = jnp.cumsum(padded)
    seg_start = seg_end - padded
    slot = seg_start[None, :] + jnp.cumsum(seli, axis=0) - 1
    slot_of = jnp.stack([jnp.sum(jnp.where(choice == k + 1.0, slot, 0), axis=1) for k in range(TOP_K)])
    n_used = (seg_end[-1] // tm).astype(jnp.int32)
    tile_start = jnp.arange(n_tiles, dtype=jnp.int32) * tm
    tile_expert = jnp.sum((seg_end[None, :] <= tile_start[:, None]).astype(jnp.int32), axis=1)
    tile_expert = jnp.minimum(tile_expert, E - 1)
    last_expert = jnp.take(tile_expert, n_used - 1)
    tile_expert = jnp.where(jnp.arange(n_tiles) < n_used, tile_expert, last_expert)
    pad_lo = (seg_start + counts).astype(jnp.int32)
    return (tile_expert.astype(jnp.int32), n_used.reshape(1), slot_of.reshape(-1).astype(jnp.int32),
            pad_lo, seg_end.astype(jnp.int32))


def _for_range(lo, hi, fn, unroll=None):
    def body(r, carry):
        fn(r)
        return carry
    lax.fori_loop(lo, hi, body, 0, unroll=unroll)


def _moe_dispatch_kernel(slot_ref, lo_ref, hi_ref, nt_ref, x_ref, xs_hbm, zero_ref, sem, zsem,
                         *, tc, T, chunks_per_tile, n_chunks):
    i = pl.program_id(0)

    def token_row(k, r):
        s = slot_ref[k * T + i * tc + r]
        return pltpu.make_async_copy(x_ref.at[pl.ds(r, 1)], xs_hbm.at[pl.ds(s, 1)], sem.at[0])

    def zero_row(s):
        return pltpu.make_async_copy(zero_ref.at[pl.ds(0, 1)], xs_hbm.at[pl.ds(s, 1)], zsem.at[0])

    def zero_chunk(c):
        rows = pl.ds(pl.multiple_of(c * MOE_ZERO_ROWS, MOE_ZERO_ROWS), MOE_ZERO_ROWS)
        return pltpu.make_async_copy(zero_ref, xs_hbm.at[rows], zsem.at[0])

    @pl.when(i == 0)
    def _():
        zero_ref[...] = jnp.zeros(zero_ref.shape, F32)
        for start in (True, False):
            for e in range(N_EXPERTS):
                _for_range(lo_ref[e], hi_ref[e], lambda s: zero_row(s).start() if start else zero_row(s).wait())
            _for_range(nt_ref[0] * chunks_per_tile, n_chunks,
                       lambda c: zero_chunk(c).start() if start else zero_chunk(c).wait())

    for start in (True, False):
        for k in range(TOP_K):
            _for_range(0, tc, lambda r: token_row(k, r).start() if start else token_row(k, r).wait(), unroll=8)


def moe_dispatch(x, plan, *, tm, tc=512):
    _, n_used, slot_of, pad_lo, pad_hi = plan
    T, D = x.shape
    tc = min(tc, T)
    n_slots = TOP_K * T + N_EXPERTS * tm
    grid_spec = pltpu.PrefetchScalarGridSpec(
        num_scalar_prefetch=4,
        grid=(T // tc,),
        in_specs=[pl.BlockSpec((tc, D), lambda i, s, lo, hi, nt: (i, 0))],
        out_specs=pl.BlockSpec(memory_space=pl.ANY),
        scratch_shapes=[pltpu.VMEM((MOE_ZERO_ROWS, D), F32),
                        pltpu.SemaphoreType.DMA((1,)), pltpu.SemaphoreType.DMA((1,))],
    )
    return pl.pallas_call(
        functools.partial(_moe_dispatch_kernel, tc=tc, T=T, chunks_per_tile=tm // MOE_ZERO_ROWS,
                          n_chunks=n_slots // MOE_ZERO_ROWS),
        grid_spec=grid_spec,
        out_shape=jax.ShapeDtypeStruct((n_slots, D), F32),
        compiler_params=_cparams(("arbitrary",)),
        name="moe_dispatch",
    )(slot_of, pad_lo, pad_hi, n_used, x)


def _moe_ffn_kernel(te_ref, nt_ref, xs_ref, g_ref, wg_ref, wu_ref, wd_ref, y_ref, h_ref):
    i, f = pl.program_id(0), pl.program_id(1)
    used = i < nt_ref[0]

    @pl.when(used & (f == 0))
    def _():
        h_ref[...] = _rms_rows(xs_ref[...], g_ref[...]).astype(BF16)
        y_ref[...] = _swiglu_partial(h_ref[...], wg_ref, wu_ref, wd_ref)

    @pl.when(used & (f > 0))
    def _():
        y_ref[...] += _swiglu_partial(h_ref[...], wg_ref, wu_ref, wd_ref)

    @pl.when(jnp.logical_not(used) & (f == 0))
    def _():
        y_ref[...] = jnp.zeros(y_ref.shape, F32)


def moe_experts(xs, g, plan, wg, wu, wd, *, tm, tf=512):
    tile_expert, n_used = plan[0], plan[1]
    n_slots, D = xs.shape
    F = wg.shape[2]
    tf = min(tf, F)
    nf = F // tf

    def fidx(i, f, nt):
        return jnp.where(i < nt[0], f, nf - 1)

    grid_spec = pltpu.PrefetchScalarGridSpec(
        num_scalar_prefetch=2,
        grid=(n_slots // tm, nf),
        in_specs=[
            pl.BlockSpec((tm, D), lambda i, f, te, nt: (jnp.minimum(i, nt[0] - 1), 0)),
            pl.BlockSpec((1, D), lambda i, f, te, nt: (0, 0)),
            pl.BlockSpec((None, D, tf), lambda i, f, te, nt: (te[i], 0, fidx(i, f, nt))),
            pl.BlockSpec((None, D, tf), lambda i, f, te, nt: (te[i], 0, fidx(i, f, nt))),
            pl.BlockSpec((None, tf, D), lambda i, f, te, nt: (te[i], fidx(i, f, nt), 0)),
        ],
        out_specs=pl.BlockSpec((tm, D), lambda i, f, te, nt: (i, 0)),
        scratch_shapes=[pltpu.VMEM((tm, D), BF16)],
    )
    return pl.pallas_call(
        _moe_ffn_kernel,
        grid_spec=grid_spec,
        out_shape=jax.ShapeDtypeStruct((n_slots, D), F32),
        compiler_params=_cparams(("parallel", "arbitrary")),
        name="moe_experts",
    )(tile_expert, n_used, xs, g, wg, wu, wd)


def _moe_combine_kernel(slot_ref, x_ref, y_hbm, route_ref, gf_ref, o_ref, ybuf, sem, *, tc, T, final_norm):
    i = pl.program_id(0)

    def slot_row(k, r):
        s = slot_ref[k * T + i * tc + r]
        return pltpu.make_async_copy(y_hbm.at[pl.ds(s, 1)], ybuf.at[k, pl.ds(r, 1)], sem.at[0])

    for start in (True, False):
        for k in range(TOP_K):
            _for_range(0, tc, lambda r: slot_row(k, r).start() if start else slot_row(k, r).wait(), unroll=8)

    w = route_ref[...]
    out = x_ref[...]
    for k in range(TOP_K):
        out = out + w[:, ROUTE_WEIGHT + k:ROUTE_WEIGHT + k + 1] * ybuf[k]
    if final_norm:
        out = _rms_rows(out, gf_ref[...])
    o_ref[...] = out


def moe_combine(x, y, route, plan, g_final, *, final_norm, tc=256):
    T, D = x.shape
    tc = min(tc, T)
    grid_spec = pltpu.PrefetchScalarGridSpec(
        num_scalar_prefetch=1,
        grid=(T // tc,),
        in_specs=[pl.BlockSpec((tc, D), lambda i, s: (i, 0)),
                  pl.BlockSpec(memory_space=pl.ANY),
                  pl.BlockSpec((tc, LANES), lambda i, s: (i, 0)),
                  pl.BlockSpec((1, D), lambda i, s: (0, 0))],
        out_specs=pl.BlockSpec((tc, D), lambda i, s: (i, 0)),
        scratch_shapes=[pltpu.VMEM((TOP_K, tc, D), F32), pltpu.SemaphoreType.DMA((1,))],
    )
    return pl.pallas_call(
        functools.partial(_moe_combine_kernel, tc=tc, T=T, final_norm=final_norm),
        grid_spec=grid_spec,
        out_shape=jax.ShapeDtypeStruct((T, D), F32),
        compiler_params=_cparams(("arbitrary",)),
        name="moe_combine",
    )(plan[2], x, y, route, g_final)


def moe_block(x, g, router, wg, wu, wd, g_final, *, final_norm, tm=512):
    T, D = x.shape
    tm = min(tm, T)
    router_pad = jnp.zeros((D, LANES), F32).at[:, :N_EXPERTS].set(router)
    route = moe_route(x, g, router_pad)
    plan = _moe_plan(route, T, tm)
    xs = moe_dispatch(x, plan, tm=tm)
    y = moe_experts(xs, g, plan, wg.astype(BF16), wu.astype(BF16), wd.astype(BF16), tm=tm)
    return moe_combine(x, y, route, plan, g_final, final_norm=final_norm)


def _split_w_in(w):
    D = w.shape[0]
    n_fox = 3 * FOX_HEADS * HEAD_DIM
    n_mid = (3 * DIL_HEADS + NSA_Q_HEADS + NSA_BRANCHES * 2 * NSA_KV_HEADS) * HEAD_DIM
    n_g = NSA_BRANCHES * NSA_Q_HEADS
    a0 = n_fox
    a1 = a0 + FOX_HEADS
    a2 = a1 + n_mid
    a3 = a2 + n_g
    main = jnp.concatenate([w[:, :a0], w[:, a1:a2], w[:, a3:]], axis=1).astype(BF16)
    small = jnp.concatenate([w[:, a0:a1], w[:, a2:a3],
                             jnp.zeros((D, LANES - FOX_HEADS - n_g), w.dtype)], axis=1).astype(BF16)
    return main, small


def mixing_block(x, B, S, norm_g, w_in, forget_bias, cmp_pe, cmp_w1, cmp_w2, wa, wb, wc, w_out, tables):
    T, D = x.shape
    w_main, w_small = _split_w_in(w_in)
    g = norm_g.reshape(1, D)
    proj = rms_proj(x, g, w_main, BF16, tm=1024, tn=1024)
    small = rms_proj(x, g, w_small, F32, tm=1024, tn=LANES)[0]

    bias_row = jnp.zeros((1, LANES), F32).at[0, :FOX_HEADS].set(forget_bias)
    c = logf_cumsum(small, bias_row, B, S)[:, :FOX_HEADS]
    c_col = c.T.reshape(FOX_HEADS, T, 1)
    c_row = c.T.reshape(FOX_HEADS, 1, T)
    fox_o = fox_attention(proj, c_col, c_row, B, S)

    dil = [dilated_group(proj, tables["dil"][gi], gi, B, S) for gi in range(len(DIL_GROUPS))]
    dil_o = jnp.stack([d[0] for d in dil])
    dil_lse = jnp.stack([d[1] for d in dil])

    kvc = nsa_compress(proj, cmp_pe, cmp_w1.astype(BF16), cmp_w2.astype(BF16), B, S)
    cmp_o, sel_t = nsa_cmp_select(proj, kvc, B, S)
    sel = jnp.swapaxes(sel_t, 2, 3).astype(BF16)
    slc_o = nsa_selected(proj, sel, tables["slc"], B, S, tq=tables["slc_tq"], tk=tables["slc_tk"])
    win_o = nsa_window(proj, tables["win"], B, S, tq=tables["win_tq"])

    merged = merge_branches(proj, small, fox_o, dil_o, dil_lse, cmp_o, slc_o, win_o,
                            wa.astype(BF16), wb.astype(BF16), wc.astype(BF16))
    return matmul_residual(merged, w_out.astype(BF16), x)


def bias_tables(rel_bias, S):
    slc_tq, slc_tk = min(256, S), min(512, S)
    win_tq = min(256, S)
    rel_c = rel_bias[:, DIL_HEADS:]
    return {
        "dil": dilated_bias_tables(rel_bias[:, :DIL_HEADS]),
        "slc": slc_bias_table(rel_c, S, slc_tq, slc_tk), "slc_tq": slc_tq, "slc_tk": slc_tk,
        "win": win_bias_table(rel_c, win_tq, min(WIN // win_tq + 1, 3)), "win_tq": win_tq,
    }


def kernel(x, rel_bias, norm_mix_g, norm_ffn_g, norm_final_g, w_in, fox_forget_bias, cmp_pe_k, cmp_w1_k, cmp_w2_k, cmp_pe_v, cmp_w1_v, cmp_w2_v, w_branch_a, w_branch_b, w_branch_c, w_out, ffn_w_gate, ffn_w_up, ffn_w_down, moe_router, moe_w_gate, moe_w_up, moe_w_down):
    B, S, D = x.shape
    T = B * S
    depth = w_in.shape[0]
    tables = bias_tables(rel_bias, S)
    g_final = norm_final_g.reshape(1, D)
    xt = x.reshape(T, D)
    for l in range(depth):
        xt = mixing_block(
            xt, B, S, norm_mix_g[l], w_in[l], fox_forget_bias[l],
            jnp.stack([cmp_pe_k[l], cmp_pe_v[l]]), jnp.stack([cmp_w1_k[l], cmp_w1_v[l]]),
            jnp.stack([cmp_w2_k[l], cmp_w2_v[l]]),
            w_branch_a[l], w_branch_b[l], w_branch_c[l], w_out[l], tables)
        g = norm_ffn_g[l].reshape(1, D)
        last = l == depth - 1
        j = l // 2
        if l % 2 == 0:
            xt = ffn(xt, g, ffn_w_gate[j].astype(BF16), ffn_w_up[j].astype(BF16),
                     ffn_w_down[j].astype(BF16), g_final, final_norm=last)
        else:
            xt = moe_block(xt, g, moe_router[j], moe_w_gate[j], moe_w_up[j], moe_w_down[j], g_final,
                           final_norm=last)
    return xt.reshape(B, S, D)
```

```python
import functools
import math

import jax
import jax.numpy as jnp
import numpy as np
from jax import lax
from jax.experimental import pallas as pl
from jax.experimental.pallas import tpu as pltpu

F32 = jnp.float32
BF16 = jnp.bfloat16
HIGHEST = lax.Precision.HIGHEST
NEG_INF = float("-inf")

LANES = 128
HEAD_DIM = 128
ATTN_SCALE = HEAD_DIM ** -0.5
RMS_EPS = 1e-6
VMEM_LIMIT_BYTES = 56 * 1024 * 1024

FOX_HEADS = 8
DIL_GROUPS = ((128, 1), (512, 4), (2048, 16))
DIL_HEADS_PER_GROUP = 4
DIL_HEADS = DIL_HEADS_PER_GROUP * len(DIL_GROUPS)
DIL_TAPS = 128
NSA_Q_HEADS = 8
NSA_KV_HEADS = 2
NSA_GQA = NSA_Q_HEADS // NSA_KV_HEADS
NSA_BRANCHES = 3
CMP_BLOCK = 32
CMP_STRIDE = 16
CMP_HIDDEN = 256
SLC_BLOCK = 64
SLC_COUNT = 16
WIN = 512
REL_BUCKETS = 32
REL_MAX_EXACT = 16
REL_MAX_DIST = 2048
N_EXPERTS = 8
TOP_K = 2
N_BRANCHES = 3

CB_FOX = 0
CB_DIL = CB_FOX + 3 * FOX_HEADS
CB_NSAQ = CB_DIL + 3 * DIL_HEADS
CB_NSAKV = CB_NSAQ + NSA_Q_HEADS
CB_MERGE = CB_NSAKV + NSA_BRANCHES * 2 * NSA_KV_HEADS
SMALL_FOX_F = 0
SMALL_NSA_G = FOX_HEADS
ROUTE_CHOICE = 0
ROUTE_WEIGHT = N_EXPERTS
MOE_ZERO_ROWS = 64


def _cparams(semantics):
    return pltpu.CompilerParams(dimension_semantics=semantics,
                                vmem_limit_bytes=VMEM_LIMIT_BYTES)


def _t5_bucket_np(dist):
    dist = np.maximum(dist, 0)
    d = np.maximum(dist, 1).astype(np.float32)
    log_ratio = np.log(d / np.float32(REL_MAX_EXACT)) / np.float32(math.log(REL_MAX_DIST / REL_MAX_EXACT))
    large = REL_MAX_EXACT + (log_ratio * np.float32(REL_BUCKETS - REL_MAX_EXACT)).astype(np.int32)
    large = np.minimum(large, REL_BUCKETS - 1)
    return np.where(dist < REL_MAX_EXACT, dist, large).astype(np.int32)


def _toeplitz_kernel(v_ref, o_ref, *, rows, cols):
    x = jnp.broadcast_to(v_ref[...], (rows, v_ref.shape[-1]))
    o_ref[...] = pltpu.roll(x, 0, 1, stride=1, stride_axis=0)[:, :cols]


def _toeplitz_blocks(fn, offsets, rows, cols):
    lx = -(-(rows + cols) // LANES) * LANES
    m = np.arange(lx)
    rel = np.where(m < cols, -m, lx - m)
    v = jnp.stack([fn(c + rel) for c in offsets]).astype(F32)
    n_off, H = v.shape[:2]
    return pl.pallas_call(
        functools.partial(_toeplitz_kernel, rows=rows, cols=cols),
        grid=(n_off, H),
        in_specs=[pl.BlockSpec((None, None, 1, lx), lambda o, h: (o, h, 0, 0))],
        out_specs=pl.BlockSpec((None, None, rows, cols), lambda o, h: (o, h, 0, 0)),
        out_shape=jax.ShapeDtypeStruct((n_off, H, rows, cols), F32),
        compiler_params=_cparams(("parallel", "parallel")),
        name="toeplitz_table",
    )(v.reshape(n_off, H, 1, lx))


def _rms_rows(x, g):
    inv = lax.rsqrt(jnp.mean(x * x, axis=-1, keepdims=True) + RMS_EPS)
    return (x * inv) * g


def _rms_proj_kernel(x_ref, g_ref, w_ref, o_ref, h_ref, *, ncb):
    @pl.when(pl.program_id(1) == 0)
    def _():
        h_ref[...] = _rms_rows(x_ref[...], g_ref[...]).astype(BF16)

    res = jnp.dot(h_ref[...], w_ref[...], preferred_element_type=F32)
    for c in range(ncb):
        o_ref[c] = res[:, c * LANES:(c + 1) * LANES].astype(o_ref.dtype)


def rms_proj(x, g, w, out_dtype, *, tm, tn):
    T, D = x.shape
    N = w.shape[1]
    tm, tn = min(tm, T), min(tn, N)
    ncb = tn // LANES
    return pl.pallas_call(
        functools.partial(_rms_proj_kernel, ncb=ncb),
        grid=(T // tm, N // tn),
        in_specs=[pl.BlockSpec((tm, D), lambda i, j: (i, 0)),
                  pl.BlockSpec((1, D), lambda i, j: (0, 0)),
                  pl.BlockSpec((D, tn), lambda i, j: (0, j))],
        out_specs=pl.BlockSpec((ncb, tm, LANES), lambda i, j: (j, i, 0)),
        out_shape=jax.ShapeDtypeStruct((N // LANES, T, LANES), out_dtype),
        scratch_shapes=[pltpu.VMEM((tm, D), BF16)],
        compiler_params=_cparams(("parallel", "arbitrary")),
        name="rms_proj",
    )(x, g, w)


CUMSUM_BLOCK = 256


def _logf_cumsum_kernel(f_ref, b_ref, c_ref, *, nblk):
    row = lax.broadcasted_iota(jnp.int32, (CUMSUM_BLOCK, CUMSUM_BLOCK), 0)
    col = lax.broadcasted_iota(jnp.int32, (CUMSUM_BLOCK, CUMSUM_BLOCK), 1)
    tri = jnp.where(col <= row, 1.0, 0.0).astype(F32)

    def body(i, carry):
        sl = pl.ds(pl.multiple_of(i * CUMSUM_BLOCK, CUMSUM_BLOCK), CUMSUM_BLOCK)
        z = f_ref[sl, :] + b_ref[...]
        logf = jnp.minimum(z, 0.0) - jnp.log1p(jnp.exp(-jnp.abs(z)))
        cs = jnp.dot(tri, logf, preferred_element_type=F32, precision=HIGHEST) + carry
        c_ref[sl, :] = cs
        return cs[CUMSUM_BLOCK - 1:CUMSUM_BLOCK, :]

    lax.fori_loop(0, nblk, body, jnp.zeros((1, LANES), F32))


def logf_cumsum(small, bias_row, B, S):
    T = B * S
    return pl.pallas_call(
        functools.partial(_logf_cumsum_kernel, nblk=S // CUMSUM_BLOCK),
        grid=(B,),
        in_specs=[pl.BlockSpec((S, LANES), lambda b: (b, 0)),
                  pl.BlockSpec((1, LANES), lambda b: (0, 0))],
        out_specs=pl.BlockSpec((S, LANES), lambda b: (b, 0)),
        out_shape=jax.ShapeDtypeStruct((T, LANES), F32),
        compiler_params=_cparams(("parallel",)),
        name="logf_cumsum",
    )(small, bias_row)


FLASH_ROW_CHUNK = 128


LOG2E = math.log2(math.e)


def _lane_tile(x, n):
    return jnp.concatenate([x] * n, axis=1)


def _flash_update(s2, v, m_ref, l_ref, acc_ref):
    m_old = m_ref[...]
    m_new = jnp.maximum(m_old, jnp.max(s2, axis=-1, keepdims=True))
    m_safe = jnp.where(m_new == NEG_INF, 0.0, m_new)
    alpha = jnp.exp2(m_old - m_safe)
    p = jnp.exp2(s2 - _lane_tile(m_safe, s2.shape[1] // LANES))
    l_ref[...] = alpha * l_ref[...] + jnp.sum(p, axis=-1, keepdims=True)
    acc_ref[...] = alpha * acc_ref[...] + jnp.dot(p.astype(BF16), v, preferred_element_type=F32)
    m_ref[...] = m_new


def _flash_init(m_ref, l_ref, acc_ref):
    m_ref[...] = jnp.full(m_ref.shape, NEG_INF, F32)
    l_ref[...] = jnp.zeros(l_ref.shape, F32)
    acc_ref[...] = jnp.zeros(acc_ref.shape, F32)


def _flash_result(l_ref, acc_ref):
    l = l_ref[...]
    return acc_ref[...] / jnp.where(l > 0.0, l, 1.0)


def _qk(q, k):
    return lax.dot_general(q, k, (((1,), (1,)), ((), ())), preferred_element_type=F32)


def _fox_kernel(qi_ref, ki_ref, q_ref, k_ref, v_ref, cq_ref, ck_ref, o_ref, m_ref, l_ref, acc_ref, cqb_ref,
                *, tq, tk, rc, nb):
    p = pl.program_id(1)
    qi, ki = qi_ref[p], ki_ref[p]

    @pl.when(ki == 0)
    def _():
        _flash_init(m_ref, l_ref, acc_ref)
        cqb_ref[...] = jnp.broadcast_to(cq_ref[...] * LOG2E, cqb_ref.shape)

    def step(b, causal_mask):
        k, v = k_ref[b], v_ref[b]
        ck = ck_ref[b] * LOG2E
        chunks = [pl.ds(c * rc, rc) for c in range(tq // rc)]
        qk = [_qk(q_ref[b, rows, :], k) for rows in chunks]
        for c, rows in enumerate(chunks):
            s = qk[c] * (ATTN_SCALE * LOG2E) + (_lane_tile(cqb_ref[b, rows, :], tk // LANES) - ck)
            if causal_mask:
                qpos = qi * tq + c * rc + lax.broadcasted_iota(jnp.int32, (rc, tk), 0)
                kpos = ki * tk + lax.broadcasted_iota(jnp.int32, (rc, tk), 1)
                s = jnp.where(kpos <= qpos, s, NEG_INF)
            _flash_update(s, v, m_ref.at[b, rows], l_ref.at[b, rows], acc_ref.at[b, rows])

    crosses_diagonal = (ki + 1) * tk - 1 > qi * tq

    @pl.when(crosses_diagonal)
    def _():
        _for_range(0, nb, lambda b: step(b, True))

    @pl.when(jnp.logical_not(crosses_diagonal))
    def _():
        _for_range(0, nb, lambda b: step(b, False))

    @pl.when(ki == ((qi + 1) * tq - 1) // tk)
    def _():
        o_ref[...] = _flash_result(l_ref, acc_ref).astype(o_ref.dtype)


def _causal_pairs(nq, tq, tk):
    qs, ks = [], []
    for qi in range(nq):
        for ki in range(((qi + 1) * tq - 1) // tk + 1):
            qs.append(qi)
            ks.append(ki)
    return jnp.asarray(qs, jnp.int32), jnp.asarray(ks, jnp.int32)


def fox_attention(proj, c_col, c_row, B, S, *, tq=512, tk=512):
    tq, tk = min(tq, S), min(tk, S)
    H = FOX_HEADS
    T = B * S
    qis, kis = _causal_pairs(S // tq, tq, tk)
    proj4 = proj.reshape(proj.shape[0], B, S, LANES)
    grid_spec = pltpu.PrefetchScalarGridSpec(
        num_scalar_prefetch=2,
        grid=(H, int(qis.shape[0])),
        in_specs=[
            pl.BlockSpec((None, B, tq, LANES), lambda h, p, qi, ki: (CB_FOX + h, 0, qi[p], 0)),
            pl.BlockSpec((None, B, tk, LANES), lambda h, p, qi, ki: (CB_FOX + H + h, 0, ki[p], 0)),
            pl.BlockSpec((None, B, tk, LANES), lambda h, p, qi, ki: (CB_FOX + 2 * H + h, 0, ki[p], 0)),
            pl.BlockSpec((None, B, tq, 1), lambda h, p, qi, ki: (h, 0, qi[p], 0)),
            pl.BlockSpec((None, B, 1, tk), lambda h, p, qi, ki: (h, 0, 0, ki[p])),
        ],
        out_specs=pl.BlockSpec((None, B, tq, LANES), lambda h, p, qi, ki: (h, 0, qi[p], 0)),
        scratch_shapes=[pltpu.VMEM((B, tq, LANES), F32)] * 4,
    )
    out = pl.pallas_call(
        functools.partial(_fox_kernel, tq=tq, tk=tk, rc=min(FLASH_ROW_CHUNK, tq), nb=B),
        grid_spec=grid_spec,
        out_shape=jax.ShapeDtypeStruct((H, B, S, LANES), BF16),
        compiler_params=_cparams(("parallel", "arbitrary")),
        name="fox_attention",
    )(qis, kis, proj4, proj4, proj4, c_col.reshape(H, B, S, 1), c_row.reshape(H, B, 1, S))
    return out.reshape(H, T, LANES)


DIL_ROWS_PER_STEP = 512


def _dil_kernel(q_ref, kp_ref, kc_ref, vp_ref, vc_ref, bias_ref, o_ref, lse_ref, *, dil, tu, nsub, hp):
    ui = pl.program_id(2)
    kcol = lax.broadcasted_iota(jnp.int32, (tu, 2 * tu), 1)
    first_ok = jnp.logical_or(ui > 0, kcol >= tu)
    units = [(c, slice(r * LANES, (r + 1) * LANES)) for c in range(nsub) for r in range(dil)]

    def window(prev_ref, cur_ref, h, c, sl):
        if c == 0:
            return jnp.concatenate([prev_ref[h, :, sl], cur_ref[h, 0:tu, sl]], axis=0)
        return cur_ref[h, (c - 1) * tu:(c + 1) * tu, sl]

    def head(h, carry):
        bias = bias_ref[h]
        bias_first = jnp.where(first_ok, bias, NEG_INF)
        qk = [_qk(q_ref[h, c * tu:(c + 1) * tu, sl], window(kp_ref, kc_ref, h, c, sl)) for c, sl in units]
        for i, (c, sl) in enumerate(units):
            rows = slice(c * tu, (c + 1) * tu)
            s = qk[i] * ATTN_SCALE + (bias_first if c == 0 else bias)
            m = jnp.max(s, axis=-1, keepdims=True)
            e = jnp.exp(s - m)
            den = jnp.sum(e, axis=-1, keepdims=True)
            o = jnp.dot(e.astype(BF16), window(vp_ref, vc_ref, h, c, sl), preferred_element_type=F32) / den
            o_ref[h, rows, sl] = o.astype(o_ref.dtype)
            lse_ref[h, rows, sl] = jnp.broadcast_to(m + jnp.log(den), (tu, LANES))
        return carry

    lax.fori_loop(0, hp, head, 0)


def dilated_group(proj, bias_tbl, group, B, S):
    dil = DIL_GROUPS[group][1]
    tu = DIL_TAPS
    Hg = DIL_HEADS_PER_GROUP
    T = B * S
    rows = min(DIL_ROWS_PER_STEP, S // dil)
    nsub = rows // tu
    nstep = S // dil // rows
    hp = max(1, Hg // max(1, dil // 4))
    c0 = CB_DIL + group * Hg
    slabs = jnp.concatenate([proj[c0 + t * DIL_HEADS:c0 + t * DIL_HEADS + Hg] for t in range(3)])
    view = slabs.reshape(3 * Hg, T // dil, dil * LANES)
    cq, ck, cv = 0, Hg // hp, 2 * Hg // hp
    cur_blk = (hp, rows, dil * LANES)
    prev_blk = (hp, tu, dil * LANES)
    prev = lambda c: (lambda b, h, u: (c + h, jnp.maximum((b * nstep + u) * nsub - 1, 0), 0))
    cur = lambda c: (lambda b, h, u: (c + h, b * nstep + u, 0))
    o, lse = pl.pallas_call(
        functools.partial(_dil_kernel, dil=dil, tu=tu, nsub=nsub, hp=hp),
        grid=(B, Hg // hp, nstep),
        in_specs=[pl.BlockSpec(cur_blk, cur(cq)),
                  pl.BlockSpec(prev_blk, prev(ck)), pl.BlockSpec(cur_blk, cur(ck)),
                  pl.BlockSpec(prev_blk, prev(cv)), pl.BlockSpec(cur_blk, cur(cv)),
                  pl.BlockSpec((hp, tu, 2 * tu), lambda b, h, u: (h, 0, 0))],
        out_specs=[pl.BlockSpec(cur_blk, cur(0)), pl.BlockSpec(cur_blk, cur(0))],
        out_shape=[jax.ShapeDtypeStruct((Hg, T // dil, dil * LANES), BF16),
                   jax.ShapeDtypeStruct((Hg, T // dil, dil * LANES), F32)],
        compiler_params=_cparams(("parallel", "parallel", "arbitrary")),
        name=f"dilated_group{group}",
    )(view, view, view, view, view, bias_tbl)
    return o.reshape(Hg, T, LANES), lse.reshape(Hg, T, LANES)


def dilated_bias_tables(rel_bias_b):
    tu = DIL_TAPS
    tables = []
    for g, (_, dil) in enumerate(DIL_GROUPS):
        heads = rel_bias_b[:, g * DIL_HEADS_PER_GROUP:(g + 1) * DIL_HEADS_PER_GROUP]

        def fn(taps, heads=heads, dil=dil):
            valid = (taps >= 0) & (taps <= DIL_TAPS)
            vals = heads[_t5_bucket_np(np.where(valid, taps, 0) * dil)].T
            return jnp.where(valid[None, :], vals, NEG_INF)

        tables.append(_toeplitz_blocks(fn, [tu], tu, 2 * tu)[0])
    return tables


def _compress_kernel(x_ref, pelo_ref, pehi_ref, w1a_ref, w1b_ref, w2_ref, o_ref, *, nchunk):
    x = x_ref[...].astype(F32)
    u0 = jnp.dot((x + pelo_ref[...]).astype(BF16), w1a_ref[...], preferred_element_type=F32)
    u1 = jnp.dot((x + pehi_ref[...]).astype(BF16), w1b_ref[...], preferred_element_type=F32)
    pre = u0 + pltpu.roll(u1, nchunk - 1, 0)
    hid = jax.nn.gelu(pre)
    out = jnp.dot(hid.astype(BF16), w2_ref[...], preferred_element_type=F32)
    row = lax.broadcasted_iota(jnp.int32, out.shape, 0)
    o_ref[...] = jnp.where(row < nchunk - 1, out, 0.0).astype(o_ref.dtype)


def nsa_compress(proj, pe, w1, w2, B, S):
    nchunk = S // CMP_STRIDE
    half = CMP_STRIDE * HEAD_DIM
    Hkv = NSA_KV_HEADS
    x = proj[CB_NSAKV:CB_NSAKV + 2 * Hkv].reshape(2, Hkv, B, nchunk, half)
    pe_lo = pe[:, :CMP_STRIDE].reshape(2, 1, half)
    pe_hi = pe[:, CMP_STRIDE:].reshape(2, 1, half)
    return pl.pallas_call(
        functools.partial(_compress_kernel, nchunk=nchunk),
        grid=(2, B, Hkv),
        in_specs=[pl.BlockSpec((None, None, None, nchunk, half), lambda t, b, h: (t, h, b, 0, 0)),
                  pl.BlockSpec((None, 1, half), lambda t, b, h: (t, 0, 0)),
                  pl.BlockSpec((None, 1, half), lambda t, b, h: (t, 0, 0)),
                  pl.BlockSpec((None, half, CMP_HIDDEN), lambda t, b, h: (t, 0, 0)),
                  pl.BlockSpec((None, half, CMP_HIDDEN), lambda t, b, h: (t, 1, 0)),
                  pl.BlockSpec((None, CMP_HIDDEN, HEAD_DIM), lambda t, b, h: (t, 0, 0))],
        out_specs=pl.BlockSpec((None, None, None, nchunk, HEAD_DIM), lambda t, b, h: (t, b, h, 0, 0)),
        out_shape=jax.ShapeDtypeStruct((2, B, Hkv, nchunk, HEAD_DIM), BF16),
        compiler_params=_cparams(("parallel", "parallel", "parallel")),
        name="nsa_compress",
    )(x, pe_lo, pe_hi, w1, w1, w2)


def _masked_softmax(s, mask, axis):
    s = jnp.where(mask, s, NEG_INF)
    m = jnp.max(s, axis=axis, keepdims=True)
    m = jnp.where(m == NEG_INF, 0.0, m)
    e = jnp.exp(s - m)
    den = jnp.sum(e, axis=axis, keepdims=True)
    return e / jnp.where(den > 0.0, den, 1.0)


def _cmp_select_kernel(q_ref, kc_ref, vc_ref, mt_ref, o_ref, sel_ref, *, tq, nchunk, n_slc, n_sel):
    q0 = pl.program_id(2) * tq
    kc = kc_ref[...]
    vc = vc_ref[...]
    pos_r = q0 + lax.broadcasted_iota(jnp.int32, (tq, nchunk), 0)
    end_r = lax.broadcasted_iota(jnp.int32, (tq, nchunk), 1) * CMP_STRIDE + (CMP_BLOCK - 1)
    vis_r = end_r <= pos_r
    pos_c = q0 + lax.broadcasted_iota(jnp.int32, (nchunk, tq), 1)
    end_c = lax.broadcasted_iota(jnp.int32, (nchunk, tq), 0) * CMP_STRIDE + (CMP_BLOCK - 1)
    vis_c = end_c <= pos_c
    imp = jnp.zeros((nchunk, tq), F32)
    for g in range(NSA_GQA):
        q = q_ref[g]
        p = _masked_softmax(_qk(q, kc) * ATTN_SCALE, vis_r, -1)
        o_ref[g] = jnp.dot(p.astype(BF16), vc, preferred_element_type=F32).astype(o_ref.dtype)
        imp = imp + _masked_softmax(_qk(kc, q) * ATTN_SCALE, vis_c, 0)
    p_slc = jnp.dot(mt_ref[...], imp, preferred_element_type=F32, precision=HIGHEST)

    blk = lax.broadcasted_iota(jnp.int32, (n_slc, tq), 0)
    cur = (q0 + lax.broadcasted_iota(jnp.int32, (n_slc, tq), 1)) // SLC_BLOCK
    forced = (blk == 0) | (blk == cur) | (blk == cur - 1)
    allowed = blk <= cur
    score = jnp.where(forced, 1e30, jnp.where(allowed, p_slc, -1.0))
    chosen = jnp.zeros((n_slc, tq), F32)
    for _ in range(n_sel):
        top = jnp.max(score, axis=0, keepdims=True)
        first = jnp.min(jnp.where(score == top, blk, n_slc), axis=0, keepdims=True)
        hit = blk == first
        chosen = jnp.where(hit, 1.0, chosen)
        score = jnp.where(hit, -2.0, score)
    sel_ref[...] = jnp.where(allowed, chosen, 0.0)


def nsa_cmp_select(proj, kvc, B, S, *, tq=512):
    tq = min(tq, S)
    nq = S // tq
    T = B * S
    nchunk = S // CMP_STRIDE
    n_slc = S // SLC_BLOCK
    n_sel = min(SLC_COUNT, n_slc)
    ratio, n_inner = SLC_BLOCK // CMP_STRIDE, CMP_BLOCK // CMP_STRIDE
    mt = np.zeros((n_slc, nchunk), np.float32)
    for j in range(n_slc):
        for m in range(ratio):
            for n in range(n_inner):
                c = ratio * j + m - n
                if 0 <= c < nchunk - 1:
                    mt[j, c] += 1.0
    G = NSA_GQA
    kv_spec = lambda t: pl.BlockSpec((None, None, None, nchunk, HEAD_DIM), lambda b, h, i: (t, b, h, 0, 0))
    return pl.pallas_call(
        functools.partial(_cmp_select_kernel, tq=tq, nchunk=nchunk, n_slc=n_slc, n_sel=n_sel),
        grid=(B, NSA_KV_HEADS, nq),
        in_specs=[pl.BlockSpec((G, tq, LANES), lambda b, h, i: (CB_NSAQ // G + h, b * nq + i, 0)),
                  kv_spec(0), kv_spec(1),
                  pl.BlockSpec((n_slc, nchunk), lambda b, h, i: (0, 0))],
        out_specs=[pl.BlockSpec((G, tq, LANES), lambda b, h, i: (h, b * nq + i, 0)),
                   pl.BlockSpec((None, None, n_slc, tq), lambda b, h, i: (b, h, 0, i))],
        out_shape=[jax.ShapeDtypeStruct((NSA_Q_HEADS, T, LANES), BF16),
                   jax.ShapeDtypeStruct((B, NSA_KV_HEADS, n_slc, S), F32)],
        compiler_params=_cparams(("parallel", "parallel", "arbitrary")),
        name="nsa_cmp_select",
    )(proj, kvc, kvc, jnp.asarray(mt))


def _slc_kernel(qi_ref, ki_ref, q_ref, k_ref, v_ref, sel_ref, bias_ref, o_ref, m_ref, l_ref, acc_ref,
                *, tq, tk, n_slc, nb):
    p = pl.program_id(1)
    qi, ki = qi_ref[p], ki_ref[p]

    @pl.when(ki == 0)
    def _():
        _flash_init(m_ref, l_ref, acc_ref)

    blk_of_key = ki * (tk // SLC_BLOCK) + lax.broadcasted_iota(jnp.int32, (n_slc, tk), 1) // SLC_BLOCK
    expand = jnp.where(lax.broadcasted_iota(jnp.int32, (n_slc, tk), 0) == blk_of_key, 1.0, 0.0).astype(BF16)

    def step(b, causal_mask):
        picked = jnp.dot(sel_ref[b], expand, preferred_element_type=F32)
        keep = picked > 0.5
        if causal_mask:
            qpos = qi * tq + lax.broadcasted_iota(jnp.int32, (tq, tk), 0)
            kpos = ki * tk + lax.broadcasted_iota(jnp.int32, (tq, tk), 1)
            keep = jnp.where(kpos <= qpos, picked, 0.0) > 0.5
        k, v = k_ref[b], v_ref[b]
        qk = [_qk(q_ref[g, b], k) for g in range(NSA_GQA)]
        for g in range(NSA_GQA):
            s = qk[g] * (ATTN_SCALE * LOG2E) + bias_ref[g]
            s = jnp.where(keep, s, NEG_INF)
            _flash_update(s, v, m_ref.at[g, b], l_ref.at[g, b], acc_ref.at[g, b])

    crosses_diagonal = (ki + 1) * tk - 1 > qi * tq

    @pl.when(crosses_diagonal)
    def _():
        _for_range(0, nb, lambda b: step(b, True))

    @pl.when(jnp.logical_not(crosses_diagonal))
    def _():
        _for_range(0, nb, lambda b: step(b, False))

    @pl.when(ki == ((qi + 1) * tq - 1) // tk)
    def _():
        o_ref[...] = _flash_result(l_ref, acc_ref).astype(o_ref.dtype)


def slc_bias_table(rel_bias_c, S, tq, tk):
    buckets = _t5_bucket_np(np.arange(S + tk))
    not_last = np.nonzero(buckets != REL_BUCKETS - 1)[0]
    far_start = int(not_last[-1]) + 1 if not_last.size else 0
    n_delta = min(S // tq, -(-(far_start + tk - 1) // tq) + 1)

    def fn(d):
        return rel_bias_c[_t5_bucket_np(d)].T * LOG2E

    return _toeplitz_blocks(fn, [dl * tq for dl in range(n_delta)], tq, tk)


def nsa_selected(proj, sel, bias_tbl, B, S, *, tq=256, tk=512):
    tq, tk = min(tq, S), min(tk, S)
    T = B * S
    n_slc = S // SLC_BLOCK
    G = NSA_GQA
    ck = CB_NSAKV + (1 * 2 + 0) * NSA_KV_HEADS
    cv = CB_NSAKV + (1 * 2 + 1) * NSA_KV_HEADS
    qis, kis = _causal_pairs(S // tq, tq, tk)
    n_delta = bias_tbl.shape[0]
    proj4 = proj.reshape(proj.shape[0], B, S, LANES)
    grid_spec = pltpu.PrefetchScalarGridSpec(
        num_scalar_prefetch=2,
        grid=(NSA_KV_HEADS, int(qis.shape[0])),
        in_specs=[
            pl.BlockSpec((G, B, tq, LANES), lambda h, p, qi, ki: (CB_NSAQ // G + h, 0, qi[p], 0)),
            pl.BlockSpec((None, B, tk, LANES), lambda h, p, qi, ki: (ck + h, 0, ki[p], 0)),
            pl.BlockSpec((None, B, tk, LANES), lambda h, p, qi, ki: (cv + h, 0, ki[p], 0)),
            pl.BlockSpec((B, None, tq, n_slc), lambda h, p, qi, ki: (0, h, qi[p], 0)),
            pl.BlockSpec((None, G, tq, tk),
                         lambda h, p, qi, ki: (jnp.minimum(qi[p] - ki[p] * (tk // tq), n_delta - 1), h, 0, 0)),
        ],
        out_specs=pl.BlockSpec((G, B, tq, LANES), lambda h, p, qi, ki: (h, 0, qi[p], 0)),
        scratch_shapes=[pltpu.VMEM((G, B, tq, LANES), F32)] * 3,
    )
    out = pl.pallas_call(
        functools.partial(_slc_kernel, tq=tq, tk=tk, n_slc=n_slc, nb=B),
        grid_spec=grid_spec,
        out_shape=jax.ShapeDtypeStruct((NSA_Q_HEADS, B, S, LANES), BF16),
        compiler_params=_cparams(("parallel", "arbitrary")),
        name="nsa_selected",
    )(qis, kis, proj4, proj4, proj4, sel, bias_tbl)
    return out.reshape(NSA_Q_HEADS, T, LANES)


def _win_kernel(q_ref, k0_ref, k1_ref, k2_ref, v0_ref, v1_ref, v2_ref, bias_ref, o_ref, *, tq, nkb):
    qi = pl.program_id(2)
    k = jnp.concatenate([r[...] for r in (k0_ref, k1_ref, k2_ref)][-nkb:], axis=0)
    v = jnp.concatenate([r[...] for r in (v0_ref, v1_ref, v2_ref)][-nkb:], axis=0)
    kpos = (qi - (nkb - 1)) * tq + lax.broadcasted_iota(jnp.int32, (tq, nkb * tq), 1)
    qk = [_qk(q_ref[g], k) for g in range(NSA_GQA)]
    for g in range(NSA_GQA):
        s = qk[g] * ATTN_SCALE + bias_ref[g]
        p = _masked_softmax(s, kpos >= 0, -1)
        o_ref[g] = jnp.dot(p.astype(BF16), v, preferred_element_type=F32).astype(o_ref.dtype)


def win_bias_table(rel_bias_c, tq, nkb):
    def fn(dist):
        valid = (dist >= 0) & (dist < WIN)
        return jnp.where(valid[None, :], rel_bias_c[_t5_bucket_np(dist)].T, NEG_INF)

    return _toeplitz_blocks(fn, [(nkb - 1) * tq], tq, nkb * tq)[0]


def nsa_window(proj, bias_tbl, B, S, *, tq=256):
    tq = min(tq, S)
    nq = S // tq
    T = B * S
    G = NSA_GQA
    nkb = min(WIN // tq + 1, 3)
    ck = CB_NSAKV + (2 * 2 + 0) * NSA_KV_HEADS
    cv = CB_NSAKV + (2 * 2 + 1) * NSA_KV_HEADS
    kv = lambda c, back: pl.BlockSpec(
        (None, tq, LANES), lambda b, h, i: (c + h, b * nq + jnp.maximum(i - back, 0), 0))
    return pl.pallas_call(
        functools.partial(_win_kernel, tq=tq, nkb=nkb),
        grid=(B, NSA_KV_HEADS, nq),
        in_specs=[pl.BlockSpec((G, tq, LANES), lambda b, h, i: (CB_NSAQ // G + h, b * nq + i, 0)),
                  kv(ck, 2), kv(ck, 1), kv(ck, 0), kv(cv, 2), kv(cv, 1), kv(cv, 0),
                  pl.BlockSpec((G, tq, nkb * tq), lambda b, h, i: (h, 0, 0))],
        out_specs=pl.BlockSpec((G, tq, LANES), lambda b, h, i: (h, b * nq + i, 0)),
        out_shape=jax.ShapeDtypeStruct((NSA_Q_HEADS, T, LANES), BF16),
        compiler_params=_cparams(("parallel", "parallel", "arbitrary")),
        name="nsa_window",
    )(proj, proj, proj, proj, proj, proj, proj, bias_tbl)


def _sigmoid(x):
    return 1.0 / (1.0 + jnp.exp(-x))


def _merge_kernel(fox_ref, dil_ref, lse_ref, cmp_ref, slc_ref, win_ref, small_ref,
                  g0_ref, g1_ref, g2_ref, wa_ref, wb_ref, wc_ref, o_ref, *, ncb):
    ya = jnp.concatenate([fox_ref[h] for h in range(FOX_HEADS)], axis=1)
    yb = []
    for h in range(DIL_HEADS_PER_GROUP):
        lse = [lse_ref[g, h] for g in range(len(DIL_GROUPS))]
        top = jnp.maximum(jnp.maximum(lse[0], lse[1]), lse[2])
        w = [jnp.exp(x - top) for x in lse]
        tot = w[0] + w[1] + w[2]
        y = sum((w[g] / tot) * dil_ref[g, h].astype(F32) for g in range(len(DIL_GROUPS)))
        yb.append(y.astype(BF16))
    gates = _sigmoid(small_ref[...])
    yc = []
    for h in range(NSA_Q_HEADS):
        y = jnp.zeros(cmp_ref.shape[1:], F32)
        for br, ref in enumerate((cmp_ref, slc_ref, win_ref)):
            col = SMALL_NSA_G + br * NSA_Q_HEADS + h
            y = y + gates[:, col:col + 1] * ref[h].astype(F32)
        yc.append(y.astype(BF16))

    def gate(ref):
        return _sigmoid(jnp.concatenate([ref[c] for c in range(ncb)], axis=1).astype(F32))

    merged = (gate(g0_ref) * jnp.dot(ya, wa_ref[...], preferred_element_type=F32)
              + gate(g1_ref) * jnp.dot(jnp.concatenate(yb, axis=1), wb_ref[...], preferred_element_type=F32)
              + gate(g2_ref) * jnp.dot(jnp.concatenate(yc, axis=1), wc_ref[...], preferred_element_type=F32))
    o_ref[...] = merged.astype(o_ref.dtype)


def merge_branches(proj, small, fox_o, dil_o, dil_lse, cmp_o, slc_o, win_o, wa, wb, wc, *, tm=256):
    T = proj.shape[1]
    D = wa.shape[1]
    tm = min(tm, T)
    ncb = D // LANES
    heads = lambda n: pl.BlockSpec((n, tm, LANES), lambda i: (0, i, 0))
    gate = lambda b: pl.BlockSpec((ncb, tm, LANES), lambda i: (CB_MERGE // ncb + b, i, 0))
    dil4 = pl.BlockSpec((len(DIL_GROUPS), DIL_HEADS_PER_GROUP, tm, LANES), lambda i: (0, 0, i, 0))
    wspec = lambda k: pl.BlockSpec((k, D), lambda i: (0, 0))
    return pl.pallas_call(
        functools.partial(_merge_kernel, ncb=ncb),
        grid=(T // tm,),
        in_specs=[heads(FOX_HEADS), dil4, dil4, heads(NSA_Q_HEADS), heads(NSA_Q_HEADS), heads(NSA_Q_HEADS),
                  pl.BlockSpec((tm, LANES), lambda i: (i, 0)),
                  gate(0), gate(1), gate(2), wspec(wa.shape[0]), wspec(wb.shape[0]), wspec(wc.shape[0])],
        out_specs=pl.BlockSpec((tm, D), lambda i: (i, 0)),
        out_shape=jax.ShapeDtypeStruct((T, D), BF16),
        compiler_params=_cparams(("parallel",)),
        name="merge_branches",
    )(fox_o, dil_o, dil_lse, cmp_o, slc_o, win_o, small, proj, proj, proj, wa, wb, wc)


def _matmul_residual_kernel(a_ref, w_ref, x_ref, o_ref):
    o_ref[...] = x_ref[...] + jnp.dot(a_ref[...], w_ref[...], preferred_element_type=F32)


def matmul_residual(a, w, x, *, tm=1024, tn=1024):
    T, K = a.shape
    N = w.shape[1]
    tm, tn = min(tm, T), min(tn, N)
    return pl.pallas_call(
        _matmul_residual_kernel,
        grid=(T // tm, N // tn),
        in_specs=[pl.BlockSpec((tm, K), lambda i, j: (i, 0)),
                  pl.BlockSpec((K, tn), lambda i, j: (0, j)),
                  pl.BlockSpec((tm, tn), lambda i, j: (i, j))],
        out_specs=pl.BlockSpec((tm, tn), lambda i, j: (i, j)),
        out_shape=jax.ShapeDtypeStruct((T, N), F32),
        compiler_params=_cparams(("parallel", "parallel")),
        name="matmul_residual",
    )(a, w, x)


def _silu(x):
    return x * _sigmoid(x)


def _swiglu_partial(h, wg_ref, wu_ref, wd_ref):
    act = _silu(jnp.dot(h, wg_ref[...], preferred_element_type=F32)) * \
        jnp.dot(h, wu_ref[...], preferred_element_type=F32)
    return jnp.dot(act.astype(BF16), wd_ref[...], preferred_element_type=F32)


def _ffn_kernel(x_ref, g_ref, wg_ref, wu_ref, wd_ref, gf_ref, o_ref, h_ref, acc_ref, *, final_norm):
    f = pl.program_id(1)

    @pl.when(f == 0)
    def _():
        h_ref[...] = _rms_rows(x_ref[...], g_ref[...]).astype(BF16)
        acc_ref[...] = jnp.zeros(acc_ref.shape, F32)

    acc_ref[...] += _swiglu_partial(h_ref[...], wg_ref, wu_ref, wd_ref)

    @pl.when(f == pl.num_programs(1) - 1)
    def _():
        out = x_ref[...] + acc_ref[...]
        if final_norm:
            out = _rms_rows(out, gf_ref[...])
        o_ref[...] = out


def ffn(x, g, wg, wu, wd, g_final, *, final_norm, tm=512, tf=512):
    T, D = x.shape
    F = wg.shape[1]
    tm, tf = min(tm, T), min(tf, F)
    return pl.pallas_call(
        functools.partial(_ffn_kernel, final_norm=final_norm),
        grid=(T // tm, F // tf),
        in_specs=[pl.BlockSpec((tm, D), lambda i, f: (i, 0)),
                  pl.BlockSpec((1, D), lambda i, f: (0, 0)),
                  pl.BlockSpec((D, tf), lambda i, f: (0, f)),
                  pl.BlockSpec((D, tf), lambda i, f: (0, f)),
                  pl.BlockSpec((tf, D), lambda i, f: (f, 0)),
                  pl.BlockSpec((1, D), lambda i, f: (0, 0))],
        out_specs=pl.BlockSpec((tm, D), lambda i, f: (i, 0)),
        out_shape=jax.ShapeDtypeStruct((T, D), F32),
        scratch_shapes=[pltpu.VMEM((tm, D), BF16), pltpu.VMEM((tm, D), F32)],
        compiler_params=_cparams(("parallel", "arbitrary")),
        name="ffn",
    )(x, g, wg, wu, wd, g_final)


def _route_kernel(x_ref, g_ref, r_ref, o_ref):
    h = _rms_rows(x_ref[...], g_ref[...])
    logits = jnp.dot(h, r_ref[...], preferred_element_type=F32, precision=HIGHEST)
    lane = lax.broadcasted_iota(jnp.int32, logits.shape, 1)
    logits = jnp.where(lane < N_EXPERTS, logits, NEG_INF)
    v1 = jnp.max(logits, axis=1, keepdims=True)
    i1 = jnp.min(jnp.where(logits == v1, lane, LANES), axis=1, keepdims=True)
    rest = jnp.where(lane == i1, NEG_INF, logits)
    v2 = jnp.max(rest, axis=1, keepdims=True)
    i2 = jnp.min(jnp.where(rest == v2, lane, LANES), axis=1, keepdims=True)
    e2 = jnp.exp(v2 - v1)
    den = 1.0 + e2
    rec = jnp.where(lane == ROUTE_CHOICE + i1, 1.0, jnp.where(lane == ROUTE_CHOICE + i2, 2.0, 0.0))
    rec = jnp.where(lane == ROUTE_WEIGHT, 1.0 / den, jnp.where(lane == ROUTE_WEIGHT + 1, e2 / den, rec))
    o_ref[...] = rec


def moe_route(x, g, router_pad, *, tm=512):
    T, D = x.shape
    tm = min(tm, T)
    return pl.pallas_call(
        _route_kernel,
        grid=(T // tm,),
        in_specs=[pl.BlockSpec((tm, D), lambda i: (i, 0)),
                  pl.BlockSpec((1, D), lambda i: (0, 0)),
                  pl.BlockSpec((D, LANES), lambda i: (0, 0))],
        out_specs=pl.BlockSpec((tm, LANES), lambda i: (i, 0)),
        out_shape=jax.ShapeDtypeStruct((T, LANES), F32),
        compiler_params=_cparams(("parallel",)),
        name="moe_route",
    )(x, g, router_pad)


def _moe_plan(route, T, tm):
    E = N_EXPERTS
    n_tiles = (TOP_K * T + E * tm) // tm
    choice = route[:, ROUTE_CHOICE:ROUTE_CHOICE + E]
    seli = (choice > 0.5).astype(jnp.int32)
    counts = jnp.sum(seli, axis=0)
    padded = ((counts + tm - 1) // tm) * tm
    seg_end = jnp.cumsum(padded)
    seg_start = seg_end - padded
    slot = seg_start[None, :] + jnp.cumsum(seli, axis=0) - 1
    slot_of = jnp.stack([jnp.sum(jnp.where(choice == k + 1.0, slot, 0), axis=1) for k in range(TOP_K)])
    n_used = (seg_end[-1] // tm).astype(jnp.int32)
    tile_start = jnp.arange(n_tiles, dtype=jnp.int32) * tm
    tile_expert = jnp.sum((seg_end[None, :] <= tile_start[:, None]).astype(jnp.int32), axis=1)
    tile_expert = jnp.minimum(tile_expert, E - 1)
    last_expert = jnp.take(tile_expert, n_used - 1)
    tile_expert = jnp.where(jnp.arange(n_tiles) < n_used, tile_expert, last_expert)
    pad_lo = (seg_start + counts).astype(jnp.int32)
    return (tile_expert.astype(jnp.int32), n_used.reshape(1), slot_of.reshape(-1).astype(jnp.int32),
            pad_lo, seg_end.astype(jnp.int32))


def _for_range(lo, hi, fn, unroll=None):
    def body(r, carry):
        fn(r)
        return carry
    lax.fori_loop(lo, hi, body, 0, unroll=unroll)


def _moe_dispatch_kernel(slot_ref, lo_ref, hi_ref, nt_ref, x_ref, xs_hbm, zero_ref, sem, zsem,
                         *, tc, T, chunks_per_tile, n_chunks):
    i = pl.program_id(0)

    def token_row(k, r):
        s = slot_ref[k * T + i * tc + r]
        return pltpu.make_async_copy(x_ref.at[pl.ds(r, 1)], xs_hbm.at[pl.ds(s, 1)], sem.at[0])

    def zero_row(s):
        return pltpu.make_async_copy(zero_ref.at[pl.ds(0, 1)], xs_hbm.at[pl.ds(s, 1)], zsem.at[0])

    def zero_chunk(c):
        rows = pl.ds(pl.multiple_of(c * MOE_ZERO_ROWS, MOE_ZERO_ROWS), MOE_ZERO_ROWS)
        return pltpu.make_async_copy(zero_ref, xs_hbm.at[rows], zsem.at[0])

    @pl.when(i == 0)
    def _():
        zero_ref[...] = jnp.zeros(zero_ref.shape, F32)
        for start in (True, False):
            for e in range(N_EXPERTS):
                _for_range(lo_ref[e], hi_ref[e], lambda s: zero_row(s).start() if start else zero_row(s).wait())
            _for_range(nt_ref[0] * chunks_per_tile, n_chunks,
                       lambda c: zero_chunk(c).start() if start else zero_chunk(c).wait())

    for start in (True, False):
        for k in range(TOP_K):
            _for_range(0, tc, lambda r: token_row(k, r).start() if start else token_row(k, r).wait(), unroll=8)


def moe_dispatch(x, plan, *, tm, tc=512):
    _, n_used, slot_of, pad_lo, pad_hi = plan
    T, D = x.shape
    tc = min(tc, T)
    n_slots = TOP_K * T + N_EXPERTS * tm
    grid_spec = pltpu.PrefetchScalarGridSpec(
        num_scalar_prefetch=4,
        grid=(T // tc,),
        in_specs=[pl.BlockSpec((tc, D), lambda i, s, lo, hi, nt: (i, 0))],
        out_specs=pl.BlockSpec(memory_space=pl.ANY),
        scratch_shapes=[pltpu.VMEM((MOE_ZERO_ROWS, D), F32),
                        pltpu.SemaphoreType.DMA((1,)), pltpu.SemaphoreType.DMA((1,))],
    )
    return pl.pallas_call(
        functools.partial(_moe_dispatch_kernel, tc=tc, T=T, chunks_per_tile=tm // MOE_ZERO_ROWS,
                          n_chunks=n_slots // MOE_ZERO_ROWS),
        grid_spec=grid_spec,
        out_shape=jax.ShapeDtypeStruct((n_slots, D), F32),
        compiler_params=_cparams(("arbitrary",)),
        name="moe_dispatch",
    )(slot_of, pad_lo, pad_hi, n_used, x)


def _moe_ffn_kernel(te_ref, nt_ref, xs_ref, g_ref, wg_ref, wu_ref, wd_ref, y_ref, h_ref):
    i, f = pl.program_id(0), pl.program_id(1)
    used = i < nt_ref[0]

    @pl.when(used & (f == 0))
    def _():
        h_ref[...] = _rms_rows(xs_ref[...], g_ref[...]).astype(BF16)
        y_ref[...] = _swiglu_partial(h_ref[...], wg_ref, wu_ref, wd_ref)

    @pl.when(used & (f > 0))
    def _():
        y_ref[...] += _swiglu_partial(h_ref[...], wg_ref, wu_ref, wd_ref)

    @pl.when(jnp.logical_not(used) & (f == 0))
    def _():
        y_ref[...] = jnp.zeros(y_ref.shape, F32)


def moe_experts(xs, g, plan, wg, wu, wd, *, tm, tf=512):
    tile_expert, n_used = plan[0], plan[1]
    n_slots, D = xs.shape
    F = wg.shape[2]
    tf = min(tf, F)
    nf = F // tf

    def fidx(i, f, nt):
        return jnp.where(i < nt[0], f, nf - 1)

    grid_spec = pltpu.PrefetchScalarGridSpec(
        num_scalar_prefetch=2,
        grid=(n_slots // tm, nf),
        in_specs=[
            pl.BlockSpec((tm, D), lambda i, f, te, nt: (jnp.minimum(i, nt[0] - 1), 0)),
            pl.BlockSpec((1, D), lambda i, f, te, nt: (0, 0)),
            pl.BlockSpec((None, D, tf), lambda i, f, te, nt: (te[i], 0, fidx(i, f, nt))),
            pl.BlockSpec((None, D, tf), lambda i, f, te, nt: (te[i], 0, fidx(i, f, nt))),
            pl.BlockSpec((None, tf, D), lambda i, f, te, nt: (te[i], fidx(i, f, nt), 0)),
        ],
        out_specs=pl.BlockSpec((tm, D), lambda i, f, te, nt: (i, 0)),
        scratch_shapes=[pltpu.VMEM((tm, D), BF16)],
    )
    return pl.pallas_call(
        _moe_ffn_kernel,
        grid_spec=grid_spec,
        out_shape=jax.ShapeDtypeStruct((n_slots, D), F32),
        compiler_params=_cparams(("parallel", "arbitrary")),
        name="moe_experts",
    )(tile_expert, n_used, xs, g, wg, wu, wd)


def _moe_combine_kernel(slot_ref, x_ref, y_hbm, route_ref, gf_ref, o_ref, ybuf, sem, *, tc, T, final_norm):
    i = pl.program_id(0)

    def slot_row(k, r):
        s = slot_ref[k * T + i * tc + r]
        return pltpu.make_async_copy(y_hbm.at[pl.ds(s, 1)], ybuf.at[k, pl.ds(r, 1)], sem.at[0])

    for start in (True, False):
        for k in range(TOP_K):
            _for_range(0, tc, lambda r: slot_row(k, r).start() if start else slot_row(k, r).wait(), unroll=8)

    w = route_ref[...]
    out = x_ref[...]
    for k in range(TOP_K):
        out = out + w[:, ROUTE_WEIGHT + k:ROUTE_WEIGHT + k + 1] * ybuf[k]
    if final_norm:
        out = _rms_rows(out, gf_ref[...])
    o_ref[...] = out


def moe_combine(x, y, route, plan, g_final, *, final_norm, tc=256):
    T, D = x.shape
    tc = min(tc, T)
    grid_spec = pltpu.PrefetchScalarGridSpec(
        num_scalar_prefetch=1,
        grid=(T // tc,),
        in_specs=[pl.BlockSpec((tc, D), lambda i, s: (i, 0)),
                  pl.BlockSpec(memory_space=pl.ANY),
                  pl.BlockSpec((tc, LANES), lambda i, s: (i, 0)),
                  pl.BlockSpec((1, D), lambda i, s: (0, 0))],
        out_specs=pl.BlockSpec((tc, D), lambda i, s: (i, 0)),
        scratch_shapes=[pltpu.VMEM((TOP_K, tc, D), F32), pltpu.SemaphoreType.DMA((1,))],
    )
    return pl.pallas_call(
        functools.partial(_moe_combine_kernel, tc=tc, T=T, final_norm=final_norm),
        grid_spec=grid_spec,
        out_shape=jax.ShapeDtypeStruct((T, D), F32),
        compiler_params=_cparams(("arbitrary",)),
        name="moe_combine",
    )(plan[2], x, y, route, g_final)


def moe_block(x, g, router, wg, wu, wd, g_final, *, final_norm, tm=512):
    T, D = x.shape
    tm = min(tm, T)
    router_pad = jnp.zeros((D, LANES), F32).at[:, :N_EXPERTS].set(router)
    route = moe_route(x, g, router_pad)
    plan = _moe_plan(route, T, tm)
    xs = moe_dispatch(x, plan, tm=tm)
    y = moe_experts(xs, g, plan, wg.astype(BF16), wu.astype(BF16), wd.astype(BF16), tm=tm)
    return moe_combine(x, y, route, plan, g_final, final_norm=final_norm)


def _split_w_in(w):
    D = w.shape[0]
    n_fox = 3 * FOX_HEADS * HEAD_DIM
    n_mid = (3 * DIL_HEADS + NSA_Q_HEADS + NSA_BRANCHES * 2 * NSA_KV_HEADS) * HEAD_DIM
    n_g = NSA_BRANCHES * NSA_Q_HEADS
    a0 = n_fox
    a1 = a0 + FOX_HEADS
    a2 = a1 + n_mid
    a3 = a2 + n_g
    main = jnp.concatenate([w[:, :a0], w[:, a1:a2], w[:, a3:]], axis=1).astype(BF16)
    small = jnp.concatenate([w[:, a0:a1], w[:, a2:a3],
                             jnp.zeros((D, LANES - FOX_HEADS - n_g), w.dtype)], axis=1).astype(BF16)
    return main, small


def mixing_block(x, B, S, norm_g, w_in, forget_bias, cmp_pe, cmp_w1, cmp_w2, wa, wb, wc, w_out, tables):
    T, D = x.shape
    w_main, w_small = _split_w_in(w_in)
    g = norm_g.reshape(1, D)
    proj = rms_proj(x, g, w_main, BF16, tm=1024, tn=1024)
    small = rms_proj(x, g, w_small, F32, tm=1024, tn=LANES)[0]

    bias_row = jnp.zeros((1, LANES), F32).at[0, :FOX_HEADS].set(forget_bias)
    c = logf_cumsum(small, bias_row, B, S)[:, :FOX_HEADS]
    c_col = c.T.reshape(FOX_HEADS, T, 1)
    c_row = c.T.reshape(FOX_HEADS, 1, T)
    fox_o = fox_attention(proj, c_col, c_row, B, S)

    dil = [dilated_group(proj, tables["dil"][gi], gi, B, S) for gi in range(len(DIL_GROUPS))]
    dil_o = jnp.stack([d[0] for d in dil])
    dil_lse = jnp.stack([d[1] for d in dil])

    kvc = nsa_compress(proj, cmp_pe, cmp_w1.astype(BF16), cmp_w2.astype(BF16), B, S)
    cmp_o, sel_t = nsa_cmp_select(proj, kvc, B, S)
    sel = jnp.swapaxes(sel_t, 2, 3).astype(BF16)
    slc_o = nsa_selected(proj, sel, tables["slc"], B, S, tq=tables["slc_tq"], tk=tables["slc_tk"])
    win_o = nsa_window(proj, tables["win"], B, S, tq=tables["win_tq"])

    merged = merge_branches(proj, small, fox_o, dil_o, dil_lse, cmp_o, slc_o, win_o,
                            wa.astype(BF16), wb.astype(BF16), wc.astype(BF16))
    return matmul_residual(merged, w_out.astype(BF16), x)


def bias_tables(rel_bias, S):
    slc_tq, slc_tk = min(256, S), min(512, S)
    win_tq = min(256, S)
    rel_c = rel_bias[:, DIL_HEADS:]
    return {
        "dil": dilated_bias_tables(rel_bias[:, :DIL_HEADS]),
        "slc": slc_bias_table(rel_c, S, slc_tq, slc_tk), "slc_tq": slc_tq, "slc_tk": slc_tk,
        "win": win_bias_table(rel_c, win_tq, min(WIN // win_tq + 1, 3)), "win_tq": win_tq,
    }


def kernel(x, rel_bias, norm_mix_g, norm_ffn_g, norm_final_g, w_in, fox_forget_bias, cmp_pe_k, cmp_w1_k, cmp_w2_k, cmp_pe_v, cmp_w1_v, cmp_w2_v, w_branch_a, w_branch_b, w_branch_c, w_out, ffn_w_gate, ffn_w_up, ffn_w_down, moe_router, moe_w_gate, moe_w_up, moe_w_down):
    B, S, D = x.shape
    T = B * S
    depth = w_in.shape[0]
    tables = bias_tables(rel_bias, S)
    g_final = norm_final_g.reshape(1, D)
    xt = x.reshape(T, D)
    for l in range(depth):
        xt = mixing_block(
            xt, B, S, norm_mix_g[l], w_in[l], fox_forget_bias[l],
            jnp.stack([cmp_pe_k[l], cmp_pe_v[l]]), jnp.stack([cmp_w1_k[l], cmp_w1_v[l]]),
            jnp.stack([cmp_w2_k[l], cmp_w2_v[l]]),
            w_branch_a[l], w_branch_b[l], w_branch_c[l], w_out[l], tables)
        g = norm_ffn_g[l].reshape(1, D)
        last = l == depth - 1
        j = l // 2
        if l % 2 == 0:
            xt = ffn(xt, g, ffn_w_gate[j].astype(BF16), ffn_w_up[j].astype(BF16),
                     ffn_w_down[j].astype(BF16), g_final, final_norm=last)
        else:
            xt = moe_block(xt, g, moe_router[j], moe_w_gate[j], moe_w_up[j], moe_w_down[j], g_final,
                           final_norm=last)
    return xt.reshape(B, S, D)
```

```python
import functools
import math

import jax
import jax.numpy as jnp
import numpy as np
from jax import lax
from jax.experimental import pallas as pl
from jax.experimental.pallas import tpu as pltpu

F32 = jnp.float32
BF16 = jnp.bfloat16
HIGHEST = lax.Precision.HIGHEST
NEG_INF = float("-inf")

LANES = 128
HEAD_DIM = 128
ATTN_SCALE = HEAD_DIM ** -0.5
RMS_EPS = 1e-6
VMEM_LIMIT_BYTES = 58 * 1024 * 1024

D_MODEL = 2048
FOX_HEADS = 8
DIL_GROUPS = ((128, 1), (512, 4), (2048, 16))
DIL_HEADS_PER_GROUP = 4
DIL_HEADS = DIL_HEADS_PER_GROUP * len(DIL_GROUPS)
DIL_TAPS = 128
NSA_Q_HEADS = 8
NSA_KV_HEADS = 2
NSA_GQA = NSA_Q_HEADS // NSA_KV_HEADS
NSA_BRANCHES = 3
CMP_BLOCK = 32
CMP_STRIDE = 16
CMP_HIDDEN = 256
SLC_BLOCK = 64
SLC_COUNT = 16
WIN = 512
REL_BUCKETS = 32
REL_MAX_EXACT = 16
REL_MAX_DIST = 2048
N_EXPERTS = 8
TOP_K = 2
N_BRANCHES = 3

CB_MERGE = 0
CB_FOX = CB_MERGE + N_BRANCHES * (D_MODEL // LANES)
CB_DIL = CB_FOX + 3 * FOX_HEADS
CB_NSAQ = CB_DIL + 3 * DIL_HEADS_PER_GROUP
CB_NSAKV = CB_NSAQ + NSA_Q_HEADS
CB_END = CB_NSAKV + NSA_BRANCHES * 2 * NSA_KV_HEADS
SMALL_FOX_F = 0
SMALL_NSA_G = FOX_HEADS
ROUTE_CHOICE = 0
ROUTE_WEIGHT = N_EXPERTS
MOE_ZERO_ROWS = 64


def _cparams(semantics):
    return pltpu.CompilerParams(dimension_semantics=semantics,
                                vmem_limit_bytes=VMEM_LIMIT_BYTES)


def _t5_bucket_np(dist):
    dist = np.maximum(dist, 0)
    d = np.maximum(dist, 1).astype(np.float32)
    log_ratio = np.log(d / np.float32(REL_MAX_EXACT)) / np.float32(math.log(REL_MAX_DIST / REL_MAX_EXACT))
    large = REL_MAX_EXACT + (log_ratio * np.float32(REL_BUCKETS - REL_MAX_EXACT)).astype(np.int32)
    large = np.minimum(large, REL_BUCKETS - 1)
    return np.where(dist < REL_MAX_EXACT, dist, large).astype(np.int32)


def _toeplitz_kernel(v_ref, o_ref, *, rows, cols):
    x = jnp.broadcast_to(v_ref[...], (rows, v_ref.shape[-1]))
    o_ref[...] = pltpu.roll(x, 0, 1, stride=1, stride_axis=0)[:, :cols]


def _toeplitz_blocks(fn, offsets, rows, cols):
    lx = -(-(rows + cols) // LANES) * LANES
    m = np.arange(lx)
    rel = np.where(m < cols, -m, lx - m)
    v = jnp.stack([fn(c + rel) for c in offsets]).astype(F32)
    n_off, H = v.shape[:2]
    return pl.pallas_call(
        functools.partial(_toeplitz_kernel, rows=rows, cols=cols),
        grid=(n_off, H),
        in_specs=[pl.BlockSpec((None, None, 1, lx), lambda o, h: (o, h, 0, 0))],
        out_specs=pl.BlockSpec((None, None, rows, cols), lambda o, h: (o, h, 0, 0)),
        out_shape=jax.ShapeDtypeStruct((n_off, H, rows, cols), F32),
        compiler_params=_cparams(("parallel", "parallel")),
        name="toeplitz_table",
    )(v.reshape(n_off, H, 1, lx))


def _rms_rows(x, g):
    inv = lax.rsqrt(jnp.mean(x * x, axis=-1, keepdims=True) + RMS_EPS)
    return (x * inv) * g


def _rms_proj_kernel(x_ref, g_ref, w_ref, o_ref, h_ref, *, ncb):
    @pl.when(pl.program_id(1) == 0)
    def _():
        h_ref[...] = _rms_rows(x_ref[...], g_ref[...]).astype(BF16)

    res = jnp.dot(h_ref[...], w_ref[...], preferred_element_type=F32)
    for c in range(ncb):
        o_ref[c] = res[:, c * LANES:(c + 1) * LANES].astype(o_ref.dtype)


def rms_proj(x, g, w, out_dtype, *, tm, tn):
    T, D = x.shape
    N = w.shape[1]
    tm, tn = min(tm, T), min(tn, N)
    ncb = tn // LANES
    return pl.pallas_call(
        functools.partial(_rms_proj_kernel, ncb=ncb),
        grid=(T // tm, N // tn),
        in_specs=[pl.BlockSpec((tm, D), lambda i, j: (i, 0)),
                  pl.BlockSpec((1, D), lambda i, j: (0, 0)),
                  pl.BlockSpec((D, tn), lambda i, j: (0, j))],
        out_specs=pl.BlockSpec((ncb, tm, LANES), lambda i, j: (j, i, 0)),
        out_shape=jax.ShapeDtypeStruct((N // LANES, T, LANES), out_dtype),
        scratch_shapes=[pltpu.VMEM((tm, D), BF16)],
        compiler_params=_cparams(("parallel", "arbitrary")),
        name="rms_proj",
    )(x, g, w)


def _rms_proj_view_kernel(x_ref, g_ref, w_ref, o_ref, h_ref, res_ref, *, ncb, dil):
    @pl.when(pl.program_id(1) == 0)
    def _():
        h_ref[...] = _rms_rows(x_ref[...], g_ref[...]).astype(BF16)

    res = jnp.dot(h_ref[...], w_ref[...], preferred_element_type=F32)
    rows = res_ref.shape[1] // dil
    for c in range(ncb):
        res_ref[c] = res[:, c * LANES:(c + 1) * LANES]
        for r in range(dil):
            part = res_ref[c, pl.ds(r, rows, stride=dil), :]
            o_ref[c, :, r * LANES:(r + 1) * LANES] = part.astype(o_ref.dtype)


def rms_proj_view(x, g, w, dil, *, tm, tn):
    T, D = x.shape
    N = w.shape[1]
    tm, tn = min(tm, T), min(tn, N)
    ncb = tn // LANES
    return pl.pallas_call(
        functools.partial(_rms_proj_view_kernel, ncb=ncb, dil=dil),
        grid=(T // tm, N // tn),
        in_specs=[pl.BlockSpec((tm, D), lambda i, j: (i, 0)),
                  pl.BlockSpec((1, D), lambda i, j: (0, 0)),
                  pl.BlockSpec((D, tn), lambda i, j: (0, j))],
        out_specs=pl.BlockSpec((ncb, tm // dil, dil * LANES), lambda i, j: (j, i, 0)),
        out_shape=jax.ShapeDtypeStruct((N // LANES, T // dil, dil * LANES), BF16),
        scratch_shapes=[pltpu.VMEM((tm, D), BF16), pltpu.VMEM((ncb, tm, LANES), F32)],
        compiler_params=_cparams(("parallel", "arbitrary")),
        name="rms_proj_view",
    )(x, g, w)


CUMSUM_BLOCK = 256


def _logf_cumsum_kernel(f_ref, b_ref, c_ref, *, nblk):
    row = lax.broadcasted_iota(jnp.int32, (CUMSUM_BLOCK, CUMSUM_BLOCK), 0)
    col = lax.broadcasted_iota(jnp.int32, (CUMSUM_BLOCK, CUMSUM_BLOCK), 1)
    tri = jnp.where(col <= row, 1.0, 0.0).astype(F32)

    def body(i, carry):
        sl = pl.ds(pl.multiple_of(i * CUMSUM_BLOCK, CUMSUM_BLOCK), CUMSUM_BLOCK)
        z = f_ref[sl, :] + b_ref[...]
        logf = jnp.minimum(z, 0.0) - jnp.log1p(jnp.exp(-jnp.abs(z)))
        cs = jnp.dot(tri, logf, preferred_element_type=F32, precision=HIGHEST) + carry
        c_ref[sl, :] = cs
        return cs[CUMSUM_BLOCK - 1:CUMSUM_BLOCK, :]

    lax.fori_loop(0, nblk, body, jnp.zeros((1, LANES), F32))


def logf_cumsum(small, bias_row, B, S):
    T = B * S
    return pl.pallas_call(
        functools.partial(_logf_cumsum_kernel, nblk=S // CUMSUM_BLOCK),
        grid=(B,),
        in_specs=[pl.BlockSpec((S, LANES), lambda b: (b, 0)),
                  pl.BlockSpec((1, LANES), lambda b: (0, 0))],
        out_specs=pl.BlockSpec((S, LANES), lambda b: (b, 0)),
        out_shape=jax.ShapeDtypeStruct((T, LANES), F32),
        compiler_params=_cparams(("parallel",)),
        name="logf_cumsum",
    )(small, bias_row)


FLASH_ROW_CHUNK = 128


LOG2E = math.log2(math.e)


def _lane_tile(x, n):
    return jnp.concatenate([x] * n, axis=1)


def _flash_update(s2, v, m_ref, l_ref, acc_ref):
    m_old = m_ref[...]
    m_new = jnp.maximum(m_old, jnp.max(s2, axis=-1, keepdims=True))
    m_safe = jnp.where(m_new == NEG_INF, 0.0, m_new)
    alpha = jnp.exp2(m_old - m_safe)
    p = jnp.exp2(s2 - _lane_tile(m_safe, s2.shape[1] // LANES))
    l_ref[...] = alpha * l_ref[...] + jnp.sum(p, axis=-1, keepdims=True)
    acc_ref[...] = alpha * acc_ref[...] + jnp.dot(p.astype(BF16), v, preferred_element_type=F32)
    m_ref[...] = m_new


def _flash_init(m_ref, l_ref, acc_ref):
    m_ref[...] = jnp.full(m_ref.shape, NEG_INF, F32)
    l_ref[...] = jnp.zeros(l_ref.shape, F32)
    acc_ref[...] = jnp.zeros(acc_ref.shape, F32)


def _flash_result(l_ref, acc_ref):
    l = l_ref[...]
    return acc_ref[...] / jnp.where(l > 0.0, l, 1.0)


def _qk(q, k):
    return lax.dot_general(q, k, (((1,), (1,)), ((), ())), preferred_element_type=F32)


def _fox_kernel(qi_ref, ki_ref, q_ref, k_ref, v_ref, cq_ref, ck_ref, o_ref, m_ref, l_ref, acc_ref, cqb_ref,
                *, tq, tk, rc, nb):
    p = pl.program_id(1)
    qi, ki = qi_ref[p], ki_ref[p]

    @pl.when(ki == 0)
    def _():
        _flash_init(m_ref, l_ref, acc_ref)
        cqb_ref[...] = jnp.broadcast_to(cq_ref[...] * LOG2E, cqb_ref.shape)

    def step(b, causal_mask):
        k, v = k_ref[b], v_ref[b]
        ck = ck_ref[b] * LOG2E
        chunks = [pl.ds(c * rc, rc) for c in range(tq // rc)]
        qk = [_qk(q_ref[b, rows, :], k) for rows in chunks]
        for c, rows in enumerate(chunks):
            s = qk[c] * (ATTN_SCALE * LOG2E) + (_lane_tile(cqb_ref[b, rows, :], tk // LANES) - ck)
            if causal_mask:
                qpos = qi * tq + c * rc + lax.broadcasted_iota(jnp.int32, (rc, tk), 0)
                kpos = ki * tk + lax.broadcasted_iota(jnp.int32, (rc, tk), 1)
                s = jnp.where(kpos <= qpos, s, NEG_INF)
            _flash_update(s, v, m_ref.at[b, rows], l_ref.at[b, rows], acc_ref.at[b, rows])

    crosses_diagonal = (ki + 1) * tk - 1 > qi * tq

    @pl.when(crosses_diagonal)
    def _():
        _for_range(0, nb, lambda b: step(b, True))

    @pl.when(jnp.logical_not(crosses_diagonal))
    def _():
        _for_range(0, nb, lambda b: step(b, False))

    @pl.when(ki == ((qi + 1) * tq - 1) // tk)
    def _():
        o_ref[...] = _flash_result(l_ref, acc_ref).astype(o_ref.dtype)


def _causal_pairs(nq, tq, tk):
    qs, ks = [], []
    for qi in range(nq):
        for ki in range(((qi + 1) * tq - 1) // tk + 1):
            qs.append(qi)
            ks.append(ki)
    return jnp.asarray(qs, jnp.int32), jnp.asarray(ks, jnp.int32)


def fox_attention(proj, c_col, c_row, B, S, *, tq=512, tk=512):
    tq, tk = min(tq, S), min(tk, S)
    H = FOX_HEADS
    T = B * S
    qis, kis = _causal_pairs(S // tq, tq, tk)
    proj4 = proj.reshape(proj.shape[0], B, S, LANES)
    grid_spec = pltpu.PrefetchScalarGridSpec(
        num_scalar_prefetch=2,
        grid=(H, int(qis.shape[0])),
        in_specs=[
            pl.BlockSpec((None, B, tq, LANES), lambda h, p, qi, ki: (CB_FOX + h, 0, qi[p], 0)),
            pl.BlockSpec((None, B, tk, LANES), lambda h, p, qi, ki: (CB_FOX + H + h, 0, ki[p], 0)),
            pl.BlockSpec((None, B, tk, LANES), lambda h, p, qi, ki: (CB_FOX + 2 * H + h, 0, ki[p], 0)),
            pl.BlockSpec((None, B, tq, 1), lambda h, p, qi, ki: (h, 0, qi[p], 0)),
            pl.BlockSpec((None, B, 1, tk), lambda h, p, qi, ki: (h, 0, 0, ki[p])),
        ],
        out_specs=pl.BlockSpec((None, B, tq, LANES), lambda h, p, qi, ki: (h, 0, qi[p], 0)),
        scratch_shapes=[pltpu.VMEM((B, tq, LANES), F32)] * 4,
    )
    out = pl.pallas_call(
        functools.partial(_fox_kernel, tq=tq, tk=tk, rc=min(FLASH_ROW_CHUNK, tq), nb=B),
        grid_spec=grid_spec,
        out_shape=jax.ShapeDtypeStruct((H, B, S, LANES), BF16),
        compiler_params=_cparams(("parallel", "arbitrary")),
        name="fox_attention",
    )(qis, kis, proj4, proj4, proj4, c_col.reshape(H, B, S, 1), c_row.reshape(H, B, 1, S))
    return out.reshape(H, T, LANES)


DIL_ROWS_PER_STEP = 512


def _dil_kernel(q_ref, kp_ref, kc_ref, vp_ref, vc_ref, bias_ref, o_ref, lse_ref, *, dil, tu, nsub, hp):
    ui = pl.program_id(2)
    kcol = lax.broadcasted_iota(jnp.int32, (tu, 2 * tu), 1)
    first_ok = jnp.logical_or(ui > 0, kcol >= tu)
    units = [(c, slice(r * LANES, (r + 1) * LANES)) for c in range(nsub) for r in range(dil)]

    def window(prev_ref, cur_ref, h, c, sl):
        if c == 0:
            return jnp.concatenate([prev_ref[h, :, sl], cur_ref[h, 0:tu, sl]], axis=0)
        return cur_ref[h, (c - 1) * tu:(c + 1) * tu, sl]

    def head(h, carry):
        bias = bias_ref[h]
        bias_first = jnp.where(first_ok, bias, NEG_INF)
        qk = [_qk(q_ref[h, c * tu:(c + 1) * tu, sl], window(kp_ref, kc_ref, h, c, sl)) for c, sl in units]
        for i, (c, sl) in enumerate(units):
            rows = slice(c * tu, (c + 1) * tu)
            s = qk[i] * ATTN_SCALE + (bias_first if c == 0 else bias)
            m = jnp.max(s, axis=-1, keepdims=True)
            e = jnp.exp(s - m)
            den = jnp.sum(e, axis=-1, keepdims=True)
            o = jnp.dot(e.astype(BF16), window(vp_ref, vc_ref, h, c, sl), preferred_element_type=F32) / den
            o_ref[h, rows, sl] = o.astype(o_ref.dtype)
            lse_ref[h, rows, sl] = jnp.broadcast_to(m + jnp.log(den), (tu, LANES))
        return carry

    lax.fori_loop(0, hp, head, 0)


def dilated_group(view, c0, bias_tbl, group, B, S):
    dil = DIL_GROUPS[group][1]
    tu = DIL_TAPS
    Hg = DIL_HEADS_PER_GROUP
    T = B * S
    rows = min(DIL_ROWS_PER_STEP, S // dil)
    nsub = rows // tu
    nstep = S // dil // rows
    hp = max(1, Hg // max(1, dil // 4))
    cq, ck, cv = c0 // hp, (c0 + Hg) // hp, (c0 + 2 * Hg) // hp
    cur_blk = (hp, rows, dil * LANES)
    prev_blk = (hp, tu, dil * LANES)
    prev = lambda c: (lambda b, h, u: (c + h, jnp.maximum((b * nstep + u) * nsub - 1, 0), 0))
    cur = lambda c: (lambda b, h, u: (c + h, b * nstep + u, 0))
    o, lse = pl.pallas_call(
        functools.partial(_dil_kernel, dil=dil, tu=tu, nsub=nsub, hp=hp),
        grid=(B, Hg // hp, nstep),
        in_specs=[pl.BlockSpec(cur_blk, cur(cq)),
                  pl.BlockSpec(prev_blk, prev(ck)), pl.BlockSpec(cur_blk, cur(ck)),
                  pl.BlockSpec(prev_blk, prev(cv)), pl.BlockSpec(cur_blk, cur(cv)),
                  pl.BlockSpec((hp, tu, 2 * tu), lambda b, h, u: (h, 0, 0))],
        out_specs=[pl.BlockSpec(cur_blk, cur(0)), pl.BlockSpec(cur_blk, cur(0))],
        out_shape=[jax.ShapeDtypeStruct((Hg, T // dil, dil * LANES), BF16),
                   jax.ShapeDtypeStruct((Hg, T // dil, dil * LANES), F32)],
        compiler_params=_cparams(("parallel", "parallel", "arbitrary")),
        name=f"dilated_group{group}",
    )(view, view, view, view, view, bias_tbl)
    return o, lse


def dilated_bias_tables(rel_bias_b):
    tu = DIL_TAPS
    tables = []
    for g, (_, dil) in enumerate(DIL_GROUPS):
        heads = rel_bias_b[:, g * DIL_HEADS_PER_GROUP:(g + 1) * DIL_HEADS_PER_GROUP]

        def fn(taps, heads=heads, dil=dil):
            valid = (taps >= 0) & (taps <= DIL_TAPS)
            vals = heads[_t5_bucket_np(np.where(valid, taps, 0) * dil)].T
            return jnp.where(valid[None, :], vals, NEG_INF)

        tables.append(_toeplitz_blocks(fn, [tu], tu, 2 * tu)[0])
    return tables


def _compress_kernel(x_ref, pelo_ref, pehi_ref, w1a_ref, w1b_ref, w2_ref, o_ref, *, nchunk):
    x = x_ref[...].astype(F32)
    u0 = jnp.dot((x + pelo_ref[...]).astype(BF16), w1a_ref[...], preferred_element_type=F32)
    u1 = jnp.dot((x + pehi_ref[...]).astype(BF16), w1b_ref[...], preferred_element_type=F32)
    pre = u0 + pltpu.roll(u1, nchunk - 1, 0)
    hid = jax.nn.gelu(pre)
    out = jnp.dot(hid.astype(BF16), w2_ref[...], preferred_element_type=F32)
    row = lax.broadcasted_iota(jnp.int32, out.shape, 0)
    o_ref[...] = jnp.where(row < nchunk - 1, out, 0.0).astype(o_ref.dtype)


def nsa_compress(proj, pe, w1, w2, B, S):
    nchunk = S // CMP_STRIDE
    half = CMP_STRIDE * HEAD_DIM
    Hkv = NSA_KV_HEADS
    x = proj[CB_NSAKV:CB_NSAKV + 2 * Hkv].reshape(2, Hkv, B, nchunk, half)
    pe_lo = pe[:, :CMP_STRIDE].reshape(2, 1, half)
    pe_hi = pe[:, CMP_STRIDE:].reshape(2, 1, half)
    return pl.pallas_call(
        functools.partial(_compress_kernel, nchunk=nchunk),
        grid=(2, B, Hkv),
        in_specs=[pl.BlockSpec((None, None, None, nchunk, half), lambda t, b, h: (t, h, b, 0, 0)),
                  pl.BlockSpec((None, 1, half), lambda t, b, h: (t, 0, 0)),
                  pl.BlockSpec((None, 1, half), lambda t, b, h: (t, 0, 0)),
                  pl.BlockSpec((None, half, CMP_HIDDEN), lambda t, b, h: (t, 0, 0)),
                  pl.BlockSpec((None, half, CMP_HIDDEN), lambda t, b, h: (t, 1, 0)),
                  pl.BlockSpec((None, CMP_HIDDEN, HEAD_DIM), lambda t, b, h: (t, 0, 0))],
        out_specs=pl.BlockSpec((None, None, None, nchunk, HEAD_DIM), lambda t, b, h: (t, b, h, 0, 0)),
        out_shape=jax.ShapeDtypeStruct((2, B, Hkv, nchunk, HEAD_DIM), BF16),
        compiler_params=_cparams(("parallel", "parallel", "parallel")),
        name="nsa_compress",
    )(x, pe_lo, pe_hi, w1, w1, w2)


def _masked_softmax(s, mask, axis):
    s = jnp.where(mask, s, NEG_INF)
    m = jnp.max(s, axis=axis, keepdims=True)
    m = jnp.where(m == NEG_INF, 0.0, m)
    e = jnp.exp(s - m)
    den = jnp.sum(e, axis=axis, keepdims=True)
    return e / jnp.where(den > 0.0, den, 1.0)


def _cmp_select_kernel(q_ref, kc_ref, vc_ref, mt_ref, o_ref, sel_ref, *, tq, nchunk, n_slc, n_sel):
    q0 = pl.program_id(2) * tq
    kc = kc_ref[...]
    vc = vc_ref[...]
    pos_r = q0 + lax.broadcasted_iota(jnp.int32, (tq, nchunk), 0)
    end_r = lax.broadcasted_iota(jnp.int32, (tq, nchunk), 1) * CMP_STRIDE + (CMP_BLOCK - 1)
    vis_r = end_r <= pos_r
    pos_c = q0 + lax.broadcasted_iota(jnp.int32, (nchunk, tq), 1)
    end_c = lax.broadcasted_iota(jnp.int32, (nchunk, tq), 0) * CMP_STRIDE + (CMP_BLOCK - 1)
    vis_c = end_c <= pos_c
    imp = jnp.zeros((nchunk, tq), F32)
    for g in range(NSA_GQA):
        q = q_ref[g]
        p = _masked_softmax(_qk(q, kc) * ATTN_SCALE, vis_r, -1)
        o_ref[g] = jnp.dot(p.astype(BF16), vc, preferred_element_type=F32).astype(o_ref.dtype)
        imp = imp + _masked_softmax(_qk(kc, q) * ATTN_SCALE, vis_c, 0)
    p_slc = jnp.dot(mt_ref[...], imp, preferred_element_type=F32, precision=HIGHEST)

    blk = lax.broadcasted_iota(jnp.int32, (n_slc, tq), 0)
    cur = (q0 + lax.broadcasted_iota(jnp.int32, (n_slc, tq), 1)) // SLC_BLOCK
    forced = (blk == 0) | (blk == cur) | (blk == cur - 1)
    allowed = blk <= cur
    score = jnp.where(forced, 1e30, jnp.where(allowed, p_slc, -1.0))
    chosen = jnp.zeros((n_slc, tq), F32)
    for _ in range(n_sel):
        top = jnp.max(score, axis=0, keepdims=True)
        first = jnp.min(jnp.where(score == top, blk, n_slc), axis=0, keepdims=True)
        hit = blk == first
        chosen = jnp.where(hit, 1.0, chosen)
        score = jnp.where(hit, -2.0, score)
    sel_ref[...] = jnp.where(allowed, chosen, 0.0)


def nsa_cmp_select(proj, kvc, B, S, *, tq=512):
    tq = min(tq, S)
    nq = S // tq
    T = B * S
    nchunk = S // CMP_STRIDE
    n_slc = S // SLC_BLOCK
    n_sel = min(SLC_COUNT, n_slc)
    ratio, n_inner = SLC_BLOCK // CMP_STRIDE, CMP_BLOCK // CMP_STRIDE
    mt = np.zeros((n_slc, nchunk), np.float32)
    for j in range(n_slc):
        for m in range(ratio):
            for n in range(n_inner):
                c = ratio * j + m - n
                if 0 <= c < nchunk - 1:
                    mt[j, c] += 1.0
    G = NSA_GQA
    kv_spec = lambda t: pl.BlockSpec((None, None, None, nchunk, HEAD_DIM), lambda b, h, i: (t, b, h, 0, 0))
    return pl.pallas_call(
        functools.partial(_cmp_select_kernel, tq=tq, nchunk=nchunk, n_slc=n_slc, n_sel=n_sel),
        grid=(B, NSA_KV_HEADS, nq),
        in_specs=[pl.BlockSpec((G, tq, LANES), lambda b, h, i: (CB_NSAQ // G + h, b * nq + i, 0)),
                  kv_spec(0), kv_spec(1),
                  pl.BlockSpec((n_slc, nchunk), lambda b, h, i: (0, 0))],
        out_specs=[pl.BlockSpec((G, tq, LANES), lambda b, h, i: (h, b * nq + i, 0)),
                   pl.BlockSpec((None, None, n_slc, tq), lambda b, h, i: (b, h, 0, i))],
        out_shape=[jax.ShapeDtypeStruct((NSA_Q_HEADS, T, LANES), BF16),
                   jax.ShapeDtypeStruct((B, NSA_KV_HEADS, n_slc, S), F32)],
        compiler_params=_cparams(("parallel", "parallel", "arbitrary")),
        name="nsa_cmp_select",
    )(proj, kvc, kvc, jnp.asarray(mt))


def _slc_kernel(qi_ref, ki_ref, q_ref, k_ref, v_ref, sel_ref, bias_ref, o_ref, m_ref, l_ref, acc_ref,
                *, tq, tk, n_slc, nb):
    p = pl.program_id(1)
    qi, ki = qi_ref[p], ki_ref[p]

    @pl.when(ki == 0)
    def _():
        _flash_init(m_ref, l_ref, acc_ref)

    blk_of_key = ki * (tk // SLC_BLOCK) + lax.broadcasted_iota(jnp.int32, (n_slc, tk), 1) // SLC_BLOCK
    expand = jnp.where(lax.broadcasted_iota(jnp.int32, (n_slc, tk), 0) == blk_of_key, 1.0, 0.0).astype(BF16)

    def step(b, causal_mask):
        picked = jnp.dot(sel_ref[b], expand, preferred_element_type=F32)
        keep = picked > 0.5
        if causal_mask:
            qpos = qi * tq + lax.broadcasted_iota(jnp.int32, (tq, tk), 0)
            kpos = ki * tk + lax.broadcasted_iota(jnp.int32, (tq, tk), 1)
            keep = jnp.where(kpos <= qpos, picked, 0.0) > 0.5
        k, v = k_ref[b], v_ref[b]
        qk = [_qk(q_ref[g, b], k) for g in range(NSA_GQA)]
        for g in range(NSA_GQA):
            s = qk[g] * (ATTN_SCALE * LOG2E) + bias_ref[g]
            s = jnp.where(keep, s, NEG_INF)
            _flash_update(s, v, m_ref.at[g, b], l_ref.at[g, b], acc_ref.at[g, b])

    crosses_diagonal = (ki + 1) * tk - 1 > qi * tq

    @pl.when(crosses_diagonal)
    def _():
        _for_range(0, nb, lambda b: step(b, True))

    @pl.when(jnp.logical_not(crosses_diagonal))
    def _():
        _for_range(0, nb, lambda b: step(b, False))

    @pl.when(ki == ((qi + 1) * tq - 1) // tk)
    def _():
        o_ref[...] = _flash_result(l_ref, acc_ref).astype(o_ref.dtype)


def slc_bias_table(rel_bias_c, S, tq, tk):
    buckets = _t5_bucket_np(np.arange(S + tk))
    not_last = np.nonzero(buckets != REL_BUCKETS - 1)[0]
    far_start = int(not_last[-1]) + 1 if not_last.size else 0
    n_delta = min(S // tq, -(-(far_start + tk - 1) // tq) + 1)

    def fn(d):
        return rel_bias_c[_t5_bucket_np(d)].T * LOG2E

    return _toeplitz_blocks(fn, [dl * tq for dl in range(n_delta)], tq, tk)


def nsa_selected(proj, sel, bias_tbl, B, S, *, tq=256, tk=512):
    tq, tk = min(tq, S), min(tk, S)
    T = B * S
    n_slc = S // SLC_BLOCK
    G = NSA_GQA
    ck = CB_NSAKV + (1 * 2 + 0) * NSA_KV_HEADS
    cv = CB_NSAKV + (1 * 2 + 1) * NSA_KV_HEADS
    qis, kis = _causal_pairs(S // tq, tq, tk)
    n_delta = bias_tbl.shape[0]
    proj4 = proj.reshape(proj.shape[0], B, S, LANES)
    grid_spec = pltpu.PrefetchScalarGridSpec(
        num_scalar_prefetch=2,
        grid=(NSA_KV_HEADS, int(qis.shape[0])),
        in_specs=[
            pl.BlockSpec((G, B, tq, LANES), lambda h, p, qi, ki: (CB_NSAQ // G + h, 0, qi[p], 0)),
            pl.BlockSpec((None, B, tk, LANES), lambda h, p, qi, ki: (ck + h, 0, ki[p], 0)),
            pl.BlockSpec((None, B, tk, LANES), lambda h, p, qi, ki: (cv + h, 0, ki[p], 0)),
            pl.BlockSpec((B, None, tq, n_slc), lambda h, p, qi, ki: (0, h, qi[p], 0)),
            pl.BlockSpec((None, G, tq, tk),
                         lambda h, p, qi, ki: (jnp.minimum(qi[p] - ki[p] * (tk // tq), n_delta - 1), h, 0, 0)),
        ],
        out_specs=pl.BlockSpec((G, B, tq, LANES), lambda h, p, qi, ki: (h, 0, qi[p], 0)),
        scratch_shapes=[pltpu.VMEM((G, B, tq, LANES), F32)] * 3,
    )
    out = pl.pallas_call(
        functools.partial(_slc_kernel, tq=tq, tk=tk, n_slc=n_slc, nb=B),
        grid_spec=grid_spec,
        out_shape=jax.ShapeDtypeStruct((NSA_Q_HEADS, B, S, LANES), BF16),
        compiler_params=_cparams(("parallel", "arbitrary")),
        name="nsa_selected",
    )(qis, kis, proj4, proj4, proj4, sel, bias_tbl)
    return out.reshape(NSA_Q_HEADS, T, LANES)


def _win_kernel(q_ref, k0_ref, k1_ref, k2_ref, v0_ref, v1_ref, v2_ref, bias_ref, o_ref, *, tq, nkb):
    qi = pl.program_id(2)
    k = jnp.concatenate([r[...] for r in (k0_ref, k1_ref, k2_ref)][-nkb:], axis=0)
    v = jnp.concatenate([r[...] for r in (v0_ref, v1_ref, v2_ref)][-nkb:], axis=0)
    kpos = (qi - (nkb - 1)) * tq + lax.broadcasted_iota(jnp.int32, (tq, nkb * tq), 1)
    qk = [_qk(q_ref[g], k) for g in range(NSA_GQA)]
    for g in range(NSA_GQA):
        s = qk[g] * ATTN_SCALE + bias_ref[g]
        p = _masked_softmax(s, kpos >= 0, -1)
        o_ref[g] = jnp.dot(p.astype(BF16), v, preferred_element_type=F32).astype(o_ref.dtype)


def win_bias_table(rel_bias_c, tq, nkb):
    def fn(dist):
        valid = (dist >= 0) & (dist < WIN)
        return jnp.where(valid[None, :], rel_bias_c[_t5_bucket_np(dist)].T, NEG_INF)

    return _toeplitz_blocks(fn, [(nkb - 1) * tq], tq, nkb * tq)[0]


def nsa_window(proj, bias_tbl, B, S, *, tq=256):
    tq = min(tq, S)
    nq = S // tq
    T = B * S
    G = NSA_GQA
    nkb = min(WIN // tq + 1, 3)
    ck = CB_NSAKV + (2 * 2 + 0) * NSA_KV_HEADS
    cv = CB_NSAKV + (2 * 2 + 1) * NSA_KV_HEADS
    kv = lambda c, back: pl.BlockSpec(
        (None, tq, LANES), lambda b, h, i: (c + h, b * nq + jnp.maximum(i - back, 0), 0))
    return pl.pallas_call(
        functools.partial(_win_kernel, tq=tq, nkb=nkb),
        grid=(B, NSA_KV_HEADS, nq),
        in_specs=[pl.BlockSpec((G, tq, LANES), lambda b, h, i: (CB_NSAQ // G + h, b * nq + i, 0)),
                  kv(ck, 2), kv(ck, 1), kv(ck, 0), kv(cv, 2), kv(cv, 1), kv(cv, 0),
                  pl.BlockSpec((G, tq, nkb * tq), lambda b, h, i: (h, 0, 0))],
        out_specs=pl.BlockSpec((G, tq, LANES), lambda b, h, i: (h, b * nq + i, 0)),
        out_shape=jax.ShapeDtypeStruct((NSA_Q_HEADS, T, LANES), BF16),
        compiler_params=_cparams(("parallel", "parallel", "arbitrary")),
        name="nsa_window",
    )(proj, proj, proj, proj, proj, proj, proj, bias_tbl)


def _sigmoid(x):
    return 1.0 / (1.0 + jnp.exp(-x))


def _merge_kernel(fox_ref, d0_ref, d1_ref, d2_ref, l0_ref, l1_ref, l2_ref, cmp_ref, slc_ref, win_ref, small_ref,
                  g0_ref, g1_ref, g2_ref, wa_ref, wb_ref, wc_ref, o_ref, nat_ref, *, ncb):
    tm = o_ref.shape[0]

    def natural(ref, h, dil, slot):
        if dil == 1:
            return ref[h].astype(F32)
        for r in range(dil):
            nat_ref[slot, pl.ds(r, tm // dil, stride=dil), :] = ref[h, :, r * LANES:(r + 1) * LANES].astype(F32)
        return nat_ref[slot]

    ya = jnp.concatenate([fox_ref[h] for h in range(FOX_HEADS)], axis=1)
    yb = []
    for h in range(DIL_HEADS_PER_GROUP):
        outs, lse = [], []
        for g, (o_g, l_g) in enumerate(((d0_ref, l0_ref), (d1_ref, l1_ref), (d2_ref, l2_ref))):
            dil = DIL_GROUPS[g][1]
            slot = (h * len(DIL_GROUPS) + g) * 2
            outs.append(natural(o_g, h, dil, slot))
            lse.append(natural(l_g, h, dil, slot + 1))
        top = jnp.maximum(jnp.maximum(lse[0], lse[1]), lse[2])
        w = [jnp.exp(x - top) for x in lse]
        tot = w[0] + w[1] + w[2]
        y = sum((w[g] / tot) * outs[g] for g in range(len(DIL_GROUPS)))
        yb.append(y.astype(BF16))
    gates = _sigmoid(small_ref[...])
    yc = []
    for h in range(NSA_Q_HEADS):
        y = jnp.zeros(cmp_ref.shape[1:], F32)
        for br, ref in enumerate((cmp_ref, slc_ref, win_ref)):
            col = SMALL_NSA_G + br * NSA_Q_HEADS + h
            y = y + gates[:, col:col + 1] * ref[h].astype(F32)
        yc.append(y.astype(BF16))

    def gate(ref):
        return _sigmoid(jnp.concatenate([ref[c] for c in range(ncb)], axis=1).astype(F32))

    merged = (gate(g0_ref) * jnp.dot(ya, wa_ref[...], preferred_element_type=F32)
              + gate(g1_ref) * jnp.dot(jnp.concatenate(yb, axis=1), wb_ref[...], preferred_element_type=F32)
              + gate(g2_ref) * jnp.dot(jnp.concatenate(yc, axis=1), wc_ref[...], preferred_element_type=F32))
    o_ref[...] = merged.astype(o_ref.dtype)


def merge_branches(proj, small, fox_o, dil_o, dil_lse, cmp_o, slc_o, win_o, wa, wb, wc, *, tm=256):
    T = proj.shape[1]
    D = wa.shape[1]
    tm = min(tm, T)
    ncb = D // LANES
    Hg = DIL_HEADS_PER_GROUP
    heads = lambda n: pl.BlockSpec((n, tm, LANES), lambda i: (0, i, 0))
    gate = lambda b: pl.BlockSpec((ncb, tm, LANES), lambda i: (CB_MERGE // ncb + b, i, 0))
    dil = [pl.BlockSpec((Hg, tm // d, d * LANES), lambda i: (0, i, 0)) for _, d in DIL_GROUPS]
    wspec = lambda k: pl.BlockSpec((k, D), lambda i: (0, 0))
    return pl.pallas_call(
        functools.partial(_merge_kernel, ncb=ncb),
        grid=(T // tm,),
        in_specs=[heads(FOX_HEADS), *dil, *dil, heads(NSA_Q_HEADS), heads(NSA_Q_HEADS), heads(NSA_Q_HEADS),
                  pl.BlockSpec((tm, LANES), lambda i: (i, 0)),
                  gate(0), gate(1), gate(2), wspec(wa.shape[0]), wspec(wb.shape[0]), wspec(wc.shape[0])],
        out_specs=pl.BlockSpec((tm, D), lambda i: (i, 0)),
        out_shape=jax.ShapeDtypeStruct((T, D), BF16),
        scratch_shapes=[pltpu.VMEM((2 * Hg * len(DIL_GROUPS), tm, LANES), F32)],
        compiler_params=_cparams(("parallel",)),
        name="merge_branches",
    )(fox_o, *dil_o, *dil_lse, cmp_o, slc_o, win_o, small, proj, proj, proj, wa, wb, wc)


def _matmul_residual_kernel(a_ref, w_ref, x_ref, o_ref):
    o_ref[...] = x_ref[...] + jnp.dot(a_ref[...], w_ref[...], preferred_element_type=F32)


def matmul_residual(a, w, x, *, tm=1024, tn=1024):
    T, K = a.shape
    N = w.shape[1]
    tm, tn = min(tm, T), min(tn, N)
    return pl.pallas_call(
        _matmul_residual_kernel,
        grid=(T // tm, N // tn),
        in_specs=[pl.BlockSpec((tm, K), lambda i, j: (i, 0)),
                  pl.BlockSpec((K, tn), lambda i, j: (0, j)),
                  pl.BlockSpec((tm, tn), lambda i, j: (i, j))],
        out_specs=pl.BlockSpec((tm, tn), lambda i, j: (i, j)),
        out_shape=jax.ShapeDtypeStruct((T, N), F32),
        compiler_params=_cparams(("parallel", "parallel")),
        name="matmul_residual",
    )(a, w, x)


def _silu(x):
    return x * _sigmoid(x)


def _swiglu_partial(h, wg_ref, wu_ref, wd_ref):
    act = _silu(jnp.dot(h, wg_ref[...].astype(BF16), preferred_element_type=F32)) * \
        jnp.dot(h, wu_ref[...].astype(BF16), preferred_element_type=F32)
    return jnp.dot(act.astype(BF16), wd_ref[...].astype(BF16), preferred_element_type=F32)


def _ffn_kernel(x_ref, g_ref, wg_ref, wu_ref, wd_ref, gf_ref, o_ref, h_ref, *, final_norm):
    f = pl.program_id(1)

    @pl.when(f == 0)
    def _():
        h_ref[...] = _rms_rows(x_ref[...], g_ref[...]).astype(BF16)
        o_ref[...] = x_ref[...]

    o_ref[...] += _swiglu_partial(h_ref[...], wg_ref, wu_ref, wd_ref)

    if final_norm:
        @pl.when(f == pl.num_programs(1) - 1)
        def _():
            o_ref[...] = _rms_rows(o_ref[...], gf_ref[...])


def ffn(x, g, wg, wu, wd, g_final, *, final_norm, tm=1024, tf=256):
    T, D = x.shape
    F = wg.shape[1]
    tm, tf = min(tm, T), min(tf, F)
    return pl.pallas_call(
        functools.partial(_ffn_kernel, final_norm=final_norm),
        grid=(T // tm, F // tf),
        in_specs=[pl.BlockSpec((tm, D), lambda i, f: (i, 0)),
                  pl.BlockSpec((1, D), lambda i, f: (0, 0)),
                  pl.BlockSpec((D, tf), lambda i, f: (0, f)),
                  pl.BlockSpec((D, tf), lambda i, f: (0, f)),
                  pl.BlockSpec((tf, D), lambda i, f: (f, 0)),
                  pl.BlockSpec((1, D), lambda i, f: (0, 0))],
        out_specs=pl.BlockSpec((tm, D), lambda i, f: (i, 0)),
        out_shape=jax.ShapeDtypeStruct((T, D), F32),
        scratch_shapes=[pltpu.VMEM((tm, D), BF16)],
        compiler_params=_cparams(("parallel", "arbitrary")),
        name="ffn",
    )(x, g, wg, wu, wd, g_final)


def _route_kernel(x_ref, g_ref, r_ref, o_ref):
    h = _rms_rows(x_ref[...], g_ref[...])
    logits = jnp.dot(h, r_ref[...], preferred_element_type=F32, precision=HIGHEST)
    lane = lax.broadcasted_iota(jnp.int32, logits.shape, 1)
    logits = jnp.where(lane < N_EXPERTS, logits, NEG_INF)
    v1 = jnp.max(logits, axis=1, keepdims=True)
    i1 = jnp.min(jnp.where(logits == v1, lane, LANES), axis=1, keepdims=True)
    rest = jnp.where(lane == i1, NEG_INF, logits)
    v2 = jnp.max(rest, axis=1, keepdims=True)
    i2 = jnp.min(jnp.where(rest == v2, lane, LANES), axis=1, keepdims=True)
    e2 = jnp.exp(v2 - v1)
    den = 1.0 + e2
    rec = jnp.where(lane == ROUTE_CHOICE + i1, 1.0, jnp.where(lane == ROUTE_CHOICE + i2, 2.0, 0.0))
    rec = jnp.where(lane == ROUTE_WEIGHT, 1.0 / den, jnp.where(lane == ROUTE_WEIGHT + 1, e2 / den, rec))
    o_ref[...] = rec


def moe_route(x, g, router_pad, *, tm=512):
    T, D = x.shape
    tm = min(tm, T)
    return pl.pallas_call(
        _route_kernel,
        grid=(T // tm,),
        in_specs=[pl.BlockSpec((tm, D), lambda i: (i, 0)),
                  pl.BlockSpec((1, D), lambda i: (0, 0)),
                  pl.BlockSpec((D, LANES), lambda i: (0, 0))],
        out_specs=pl.BlockSpec((tm, LANES), lambda i: (i, 0)),
        out_shape=jax.ShapeDtypeStruct((T, LANES), F32),
        compiler_params=_cparams(("parallel",)),
        name="moe_route",
    )(x, g, router_pad)


def _moe_plan(route, T, tm):
    E = N_EXPERTS
    n_tiles = (TOP_K * T + E * tm) // tm
    choice = route[:, ROUTE_CHOICE:ROUTE_CHOICE + E]
    seli = (choice > 0.5).astype(jnp.int32)
    counts = jnp.sum(seli, axis=0)
    padded = ((counts + tm - 1) // tm) * tm
    seg_end = jnp.cumsum(padded)
    seg_start = seg_end - padded
    slot = seg_start[None, :] + jnp.cumsum(seli, axis=0) - 1
    slot_of = jnp.stack([jnp.sum(jnp.where(choice == k + 1.0, slot, 0), axis=1) for k in range(TOP_K)])
    n_used = (seg_end[-1] // tm).astype(jnp.int32)
    tile_start = jnp.arange(n_tiles, dtype=jnp.int32) * tm
    tile_expert = jnp.sum((seg_end[None, :] <= tile_start[:, None]).astype(jnp.int32), axis=1)
    tile_expert = jnp.minimum(tile_expert, E - 1)
    last_expert = jnp.take(tile_expert, n_used - 1)
    tile_expert = jnp.where(jnp.arange(n_tiles) < n_used, tile_expert, last_expert)
    pad_lo = (seg_start + counts).astype(jnp.int32)
    return (tile_expert.astype(jnp.int32), n_used.reshape(1), slot_of.reshape(-1).astype(jnp.int32),
            pad_lo, seg_end.astype(jnp.int32))


def _for_range(lo, hi, fn, unroll=None):
    def body(r, carry):
        fn(r)
        return carry
    lax.fori_loop(lo, hi, body, 0, unroll=unroll)


def _moe_dispatch_kernel(slot_ref, lo_ref, hi_ref, nt_ref, x_ref, xs_hbm, zero_ref, sem, zsem,
                         *, tc, T, chunks_per_tile, n_chunks):
    i = pl.program_id(0)

    def token_row(k, r):
        s = slot_ref[k * T + i * tc + r]
        return pltpu.make_async_copy(x_ref.at[pl.ds(r, 1)], xs_hbm.at[pl.ds(s, 1)], sem.at[0])

    def zero_row(s):
        return pltpu.make_async_copy(zero_ref.at[pl.ds(0, 1)], xs_hbm.at[pl.ds(s, 1)], zsem.at[0])

    def zero_chunk(c):
        rows = pl.ds(pl.multiple_of(c * MOE_ZERO_ROWS, MOE_ZERO_ROWS), MOE_ZERO_ROWS)
        return pltpu.make_async_copy(zero_ref, xs_hbm.at[rows], zsem.at[0])

    @pl.when(i == 0)
    def _():
        zero_ref[...] = jnp.zeros(zero_ref.shape, F32)
        for start in (True, False):
            for e in range(N_EXPERTS):
                _for_range(lo_ref[e], hi_ref[e], lambda s: zero_row(s).start() if start else zero_row(s).wait())
            _for_range(nt_ref[0] * chunks_per_tile, n_chunks,
                       lambda c: zero_chunk(c).start() if start else zero_chunk(c).wait())

    for start in (True, False):
        for k in range(TOP_K):
            _for_range(0, tc, lambda r: token_row(k, r).start() if start else token_row(k, r).wait(), unroll=8)


def moe_dispatch(x, plan, *, tm, tc=512):
    _, n_used, slot_of, pad_lo, pad_hi = plan
    T, D = x.shape
    tc = min(tc, T)
    n_slots = TOP_K * T + N_EXPERTS * tm
    grid_spec = pltpu.PrefetchScalarGridSpec(
        num_scalar_prefetch=4,
        grid=(T // tc,),
        in_specs=[pl.BlockSpec((tc, D), lambda i, s, lo, hi, nt: (i, 0))],
        out_specs=pl.BlockSpec(memory_space=pl.ANY),
        scratch_shapes=[pltpu.VMEM((MOE_ZERO_ROWS, D), F32),
                        pltpu.SemaphoreType.DMA((1,)), pltpu.SemaphoreType.DMA((1,))],
    )
    return pl.pallas_call(
        functools.partial(_moe_dispatch_kernel, tc=tc, T=T, chunks_per_tile=tm // MOE_ZERO_ROWS,
                          n_chunks=n_slots // MOE_ZERO_ROWS),
        grid_spec=grid_spec,
        out_shape=jax.ShapeDtypeStruct((n_slots, D), F32),
        compiler_params=_cparams(("arbitrary",)),
        name="moe_dispatch",
    )(slot_of, pad_lo, pad_hi, n_used, x)


def _moe_ffn_kernel(te_ref, nt_ref, xs_ref, g_ref, wg_ref, wu_ref, wd_ref, y_ref, h_ref):
    i, f = pl.program_id(0), pl.program_id(1)
    used = i < nt_ref[0]

    @pl.when(used & (f == 0))
    def _():
        h_ref[...] = _rms_rows(xs_ref[...], g_ref[...]).astype(BF16)
        y_ref[...] = _swiglu_partial(h_ref[...], wg_ref, wu_ref, wd_ref)

    @pl.when(used & (f > 0))
    def _():
        y_ref[...] += _swiglu_partial(h_ref[...], wg_ref, wu_ref, wd_ref)

    @pl.when(jnp.logical_not(used) & (f == 0))
    def _():
        y_ref[...] = jnp.zeros(y_ref.shape, F32)


def moe_experts(xs, g, plan, wg, wu, wd, *, tm, tf=1024):
    tile_expert, n_used = plan[0], plan[1]
    n_slots, D = xs.shape
    F = wg.shape[2]
    tf = min(tf, F)
    nf = F // tf

    def fidx(i, f, nt):
        return jnp.where(i < nt[0], f, nf - 1)

    grid_spec = pltpu.PrefetchScalarGridSpec(
        num_scalar_prefetch=2,
        grid=(n_slots // tm, nf),
        in_specs=[
            pl.BlockSpec((tm, D), lambda i, f, te, nt: (jnp.minimum(i, nt[0] - 1), 0)),
            pl.BlockSpec((1, D), lambda i, f, te, nt: (0, 0)),
            pl.BlockSpec((None, D, tf), lambda i, f, te, nt: (te[i], 0, fidx(i, f, nt))),
            pl.BlockSpec((None, D, tf), lambda i, f, te, nt: (te[i], 0, fidx(i, f, nt))),
            pl.BlockSpec((None, tf, D), lambda i, f, te, nt: (te[i], fidx(i, f, nt), 0)),
        ],
        out_specs=pl.BlockSpec((tm, D), lambda i, f, te, nt: (i, 0)),
        scratch_shapes=[pltpu.VMEM((tm, D), BF16)],
    )
    return pl.pallas_call(
        _moe_ffn_kernel,
        grid_spec=grid_spec,
        out_shape=jax.ShapeDtypeStruct((n_slots, D), F32),
        compiler_params=_cparams(("parallel", "arbitrary")),
        name="moe_experts",
    )(tile_expert, n_used, xs, g, wg, wu, wd)


def _moe_combine_kernel(slot_ref, x_ref, y_hbm, route_ref, gf_ref, o_ref, ybuf, sem, *, tc, T, final_norm):
    i = pl.program_id(0)

    def slot_row(k, r):
        s = slot_ref[k * T + i * tc + r]
        return pltpu.make_async_copy(y_hbm.at[pl.ds(s, 1)], ybuf.at[k, pl.ds(r, 1)], sem.at[0])

    for start in (True, False):
        for k in range(TOP_K):
            _for_range(0, tc, lambda r: slot_row(k, r).start() if start else slot_row(k, r).wait(), unroll=8)

    w = route_ref[...]
    out = x_ref[...]
    for k in range(TOP_K):
        out = out + w[:, ROUTE_WEIGHT + k:ROUTE_WEIGHT + k + 1] * ybuf[k]
    if final_norm:
        out = _rms_rows(out, gf_ref[...])
    o_ref[...] = out


def moe_combine(x, y, route, plan, g_final, *, final_norm, tc=256):
    T, D = x.shape
    tc = min(tc, T)
    grid_spec = pltpu.PrefetchScalarGridSpec(
        num_scalar_prefetch=1,
        grid=(T // tc,),
        in_specs=[pl.BlockSpec((tc, D), lambda i, s: (i, 0)),
                  pl.BlockSpec(memory_space=pl.ANY),
                  pl.BlockSpec((tc, LANES), lambda i, s: (i, 0)),
                  pl.BlockSpec((1, D), lambda i, s: (0, 0))],
        out_specs=pl.BlockSpec((tc, D), lambda i, s: (i, 0)),
        scratch_shapes=[pltpu.VMEM((TOP_K, tc, D), F32), pltpu.SemaphoreType.DMA((1,))],
    )
    return pl.pallas_call(
        functools.partial(_moe_combine_kernel, tc=tc, T=T, final_norm=final_norm),
        grid_spec=grid_spec,
        out_shape=jax.ShapeDtypeStruct((T, D), F32),
        compiler_params=_cparams(("arbitrary",)),
        name="moe_combine",
    )(plan[2], x, y, route, g_final)


def moe_block(x, g, router, wg, wu, wd, g_final, *, final_norm, tm=512):
    T, D = x.shape
    tm = min(tm, T)
    router_pad = jnp.zeros((D, LANES), F32).at[:, :N_EXPERTS].set(router)
    route = moe_route(x, g, router_pad)
    plan = _moe_plan(route, T, tm)
    xs = moe_dispatch(x, plan, tm=tm)
    y = moe_experts(xs, g, plan, wg.astype(BF16), wu.astype(BF16), wd.astype(BF16), tm=tm)
    return moe_combine(x, y, route, plan, g_final, final_norm=final_norm)


def _split_w_in(w):
    D = w.shape[0]
    n_fox = 3 * FOX_HEADS * HEAD_DIM
    n_dil = 3 * DIL_HEADS * HEAD_DIM
    n_nsa = (NSA_Q_HEADS + NSA_BRANCHES * 2 * NSA_KV_HEADS) * HEAD_DIM
    n_g = NSA_BRANCHES * NSA_Q_HEADS
    a0 = n_fox
    a1 = a0 + FOX_HEADS
    a2 = a1 + n_dil
    a3 = a2 + n_nsa
    a4 = a3 + n_g
    group_cols = DIL_HEADS_PER_GROUP * HEAD_DIM
    dil = w[:, a1:a2].reshape(D, 3, len(DIL_GROUPS), group_cols)
    dil_sets = [dil[:, :, gi].reshape(D, 3 * group_cols).astype(BF16) for gi in range(len(DIL_GROUPS))]
    main = jnp.concatenate([w[:, a4:].astype(BF16), w[:, :a0].astype(BF16), dil_sets[0],
                            w[:, a2:a3].astype(BF16)], axis=1)
    small = jnp.concatenate([w[:, a0:a1], w[:, a3:a4],
                             jnp.zeros((D, LANES - FOX_HEADS - n_g), w.dtype)], axis=1).astype(BF16)
    return main, dil_sets[1:], small


def mixing_block(x, B, S, norm_g, w_in, forget_bias, cmp_pe, cmp_w1, cmp_w2, wa, wb, wc, w_out, tables):
    T, D = x.shape
    w_main, w_dil, w_small = _split_w_in(w_in)
    g = norm_g.reshape(1, D)
    proj = rms_proj(x, g, w_main, BF16, tm=1024, tn=1024)
    small = rms_proj(x, g, w_small, F32, tm=1024, tn=LANES)[0]
    dil_views = [rms_proj_view(x, g, w, DIL_GROUPS[gi + 1][1], tm=1024, tn=4 * LANES)
                 for gi, w in enumerate(w_dil)]

    bias_row = jnp.zeros((1, LANES), F32).at[0, :FOX_HEADS].set(forget_bias)
    c = logf_cumsum(small, bias_row, B, S)[:, :FOX_HEADS]
    c_col = c.T.reshape(FOX_HEADS, T, 1)
    c_row = c.T.reshape(FOX_HEADS, 1, T)
    fox_o = fox_attention(proj, c_col, c_row, B, S)

    dil = [dilated_group(proj, CB_DIL, tables["dil"][0], 0, B, S)]
    dil += [dilated_group(v, 0, tables["dil"][gi + 1], gi + 1, B, S) for gi, v in enumerate(dil_views)]
    dil_o = [d[0] for d in dil]
    dil_lse = [d[1] for d in dil]

    kvc = nsa_compress(proj, cmp_pe, cmp_w1.astype(BF16), cmp_w2.astype(BF16), B, S)
    cmp_o, sel_t = nsa_cmp_select(proj, kvc, B, S)
    sel = jnp.swapaxes(sel_t, 2, 3).astype(BF16)
    slc_o = nsa_selected(proj, sel, tables["slc"], B, S, tq=tables["slc_tq"], tk=tables["slc_tk"])
    win_o = nsa_window(proj, tables["win"], B, S, tq=tables["win_tq"])

    merged = merge_branches(proj, small, fox_o, dil_o, dil_lse, cmp_o, slc_o, win_o,
                            wa.astype(BF16), wb.astype(BF16), wc.astype(BF16))
    return matmul_residual(merged, w_out.astype(BF16), x)


def bias_tables(rel_bias, S):
    slc_tq, slc_tk = min(256, S), min(512, S)
    win_tq = min(256, S)
    rel_c = rel_bias[:, DIL_HEADS:]
    return {
        "dil": dilated_bias_tables(rel_bias[:, :DIL_HEADS]),
        "slc": slc_bias_table(rel_c, S, slc_tq, slc_tk), "slc_tq": slc_tq, "slc_tk": slc_tk,
        "win": win_bias_table(rel_c, win_tq, min(WIN // win_tq + 1, 3)), "win_tq": win_tq,
    }


def kernel(x, rel_bias, norm_mix_g, norm_ffn_g, norm_final_g, w_in, fox_forget_bias, cmp_pe_k, cmp_w1_k, cmp_w2_k, cmp_pe_v, cmp_w1_v, cmp_w2_v, w_branch_a, w_branch_b, w_branch_c, w_out, ffn_w_gate, ffn_w_up, ffn_w_down, moe_router, moe_w_gate, moe_w_up, moe_w_down):
    B, S, D = x.shape
    if D != D_MODEL:
        raise ValueError(f"column-block layout is built for d_model={D_MODEL}, got {D}")
    T = B * S
    depth = w_in.shape[0]
    tables = bias_tables(rel_bias, S)
    g_final = norm_final_g.reshape(1, D)
    xt = x.reshape(T, D)
    for l in range(depth):
        xt = mixing_block(
            xt, B, S, norm_mix_g[l], w_in[l], fox_forget_bias[l],
            jnp.stack([cmp_pe_k[l], cmp_pe_v[l]]), jnp.stack([cmp_w1_k[l], cmp_w1_v[l]]),
            jnp.stack([cmp_w2_k[l], cmp_w2_v[l]]),
            w_branch_a[l], w_branch_b[l], w_branch_c[l], w_out[l], tables)
        g = norm_ffn_g[l].reshape(1, D)
        last = l == depth - 1
        j = l // 2
        if l % 2 == 0:
            xt = ffn(xt, g, ffn_w_gate[j], ffn_w_up[j], ffn_w_down[j], g_final, final_norm=last)
        else:
            xt = moe_block(xt, g, moe_router[j], moe_w_gate[j], moe_w_up[j], moe_w_down[j], g_final,
                           final_norm=last)
    return xt.reshape(B, S, D)
```

```python
import functools
import math

import jax
import jax.numpy as jnp
import numpy as np
from jax import lax
from jax.experimental import pallas as pl
from jax.experimental.pallas import tpu as pltpu

F32 = jnp.float32
BF16 = jnp.bfloat16
HIGHEST = lax.Precision.HIGHEST
NEG_INF = float("-inf")

LANES = 128
HEAD_DIM = 128
ATTN_SCALE = HEAD_DIM ** -0.5
RMS_EPS = 1e-6
VMEM_LIMIT_BYTES = 58 * 1024 * 1024

D_MODEL = 2048
FOX_HEADS = 8
DIL_GROUPS = ((128, 1), (512, 4), (2048, 16))
DIL_HEADS_PER_GROUP = 4
DIL_HEADS = DIL_HEADS_PER_GROUP * len(DIL_GROUPS)
DIL_TAPS = 128
NSA_Q_HEADS = 8
NSA_KV_HEADS = 2
NSA_GQA = NSA_Q_HEADS // NSA_KV_HEADS
NSA_BRANCHES = 3
CMP_BLOCK = 32
CMP_STRIDE = 16
CMP_HIDDEN = 256
SLC_BLOCK = 64
SLC_COUNT = 16
WIN = 512
REL_BUCKETS = 32
REL_MAX_EXACT = 16
REL_MAX_DIST = 2048
N_EXPERTS = 8
TOP_K = 2
N_BRANCHES = 3

CB_MERGE = 0
CB_FOX = CB_MERGE + N_BRANCHES * (D_MODEL // LANES)
CB_DIL = CB_FOX + 3 * FOX_HEADS
CB_NSAQ = CB_DIL + 3 * DIL_HEADS_PER_GROUP
CB_NSAKV = CB_NSAQ + NSA_Q_HEADS
CB_END = CB_NSAKV + NSA_BRANCHES * 2 * NSA_KV_HEADS
SMALL_FOX_F = 0
SMALL_NSA_G = FOX_HEADS
ROUTE_CHOICE = 0
ROUTE_WEIGHT = N_EXPERTS
MOE_ZERO_ROWS = 64


def _cparams(semantics):
    return pltpu.CompilerParams(dimension_semantics=semantics,
                                vmem_limit_bytes=VMEM_LIMIT_BYTES)


def _t5_bucket_np(dist):
    dist = np.maximum(dist, 0)
    d = np.maximum(dist, 1).astype(np.float32)
    log_ratio = np.log(d / np.float32(REL_MAX_EXACT)) / np.float32(math.log(REL_MAX_DIST / REL_MAX_EXACT))
    large = REL_MAX_EXACT + (log_ratio * np.float32(REL_BUCKETS - REL_MAX_EXACT)).astype(np.int32)
    large = np.minimum(large, REL_BUCKETS - 1)
    return np.where(dist < REL_MAX_EXACT, dist, large).astype(np.int32)


def _toeplitz_kernel(v_ref, o_ref, *, rows, cols):
    x = jnp.broadcast_to(v_ref[...], (rows, v_ref.shape[-1]))
    o_ref[...] = pltpu.roll(x, 0, 1, stride=1, stride_axis=0)[:, :cols]


def _toeplitz_blocks(fn, offsets, rows, cols):
    lx = -(-(rows + cols) // LANES) * LANES
    m = np.arange(lx)
    rel = np.where(m < cols, -m, lx - m)
    v = jnp.stack([fn(c + rel) for c in offsets]).astype(F32)
    n_off, H = v.shape[:2]
    return pl.pallas_call(
        functools.partial(_toeplitz_kernel, rows=rows, cols=cols),
        grid=(n_off, H),
        in_specs=[pl.BlockSpec((None, None, 1, lx), lambda o, h: (o, h, 0, 0))],
        out_specs=pl.BlockSpec((None, None, rows, cols), lambda o, h: (o, h, 0, 0)),
        out_shape=jax.ShapeDtypeStruct((n_off, H, rows, cols), F32),
        compiler_params=_cparams(("parallel", "parallel")),
        name="toeplitz_table",
    )(v.reshape(n_off, H, 1, lx))


def _rms_rows(x, g):
    inv = lax.rsqrt(jnp.mean(x * x, axis=-1, keepdims=True) + RMS_EPS)
    return (x * inv) * g


def _rms_proj_kernel(x_ref, g_ref, w_ref, o_ref, h_ref, *, ncb):
    @pl.when(pl.program_id(1) == 0)
    def _():
        h_ref[...] = _rms_rows(x_ref[...], g_ref[...]).astype(BF16)

    res = jnp.dot(h_ref[...], w_ref[...], preferred_element_type=F32)
    for c in range(ncb):
        o_ref[c] = res[:, c * LANES:(c + 1) * LANES].astype(o_ref.dtype)


def rms_proj(x, g, w, out_dtype, *, tm, tn):
    T, D = x.shape
    N = w.shape[1]
    tm, tn = min(tm, T), min(tn, N)
    ncb = tn // LANES
    return pl.pallas_call(
        functools.partial(_rms_proj_kernel, ncb=ncb),
        grid=(T // tm, N // tn),
        in_specs=[pl.BlockSpec((tm, D), lambda i, j: (i, 0)),
                  pl.BlockSpec((1, D), lambda i, j: (0, 0)),
                  pl.BlockSpec((D, tn), lambda i, j: (0, j))],
        out_specs=[pl.BlockSpec((ncb, tm, LANES), lambda i, j: (j, i, 0)),
                   pl.BlockSpec((tm, D), lambda i, j: (i, 0))],
        out_shape=[jax.ShapeDtypeStruct((N // LANES, T, LANES), out_dtype),
                   jax.ShapeDtypeStruct((T, D), BF16)],
        compiler_params=_cparams(("parallel", "arbitrary")),
        name="rms_proj",
    )(x, g, w)


def _proj_kernel(h_ref, w_ref, o_ref, *, ncb):
    res = jnp.dot(h_ref[...], w_ref[...], preferred_element_type=F32)
    for c in range(ncb):
        o_ref[c] = res[:, c * LANES:(c + 1) * LANES].astype(o_ref.dtype)


def slab_proj(h, w, out_dtype, *, tm, tn):
    T, D = h.shape
    N = w.shape[1]
    tm, tn = min(tm, T), min(tn, N)
    ncb = tn // LANES
    return pl.pallas_call(
        functools.partial(_proj_kernel, ncb=ncb),
        grid=(T // tm, N // tn),
        in_specs=[pl.BlockSpec((tm, D), lambda i, j: (i, 0)),
                  pl.BlockSpec((D, tn), lambda i, j: (0, j))],
        out_specs=pl.BlockSpec((ncb, tm, LANES), lambda i, j: (j, i, 0)),
        out_shape=jax.ShapeDtypeStruct((N // LANES, T, LANES), out_dtype),
        compiler_params=_cparams(("parallel", "parallel")),
        name="slab_proj",
    )(h, w)


def _proj_view_kernel(h_ref, w_ref, o_ref, res_ref, *, ncb, dil):
    res = jnp.dot(h_ref[...], w_ref[...], preferred_element_type=F32)
    rows = res_ref.shape[1] // dil
    for c in range(ncb):
        res_ref[c] = res[:, c * LANES:(c + 1) * LANES]
        for r in range(dil):
            part = res_ref[c, pl.ds(r, rows, stride=dil), :]
            o_ref[c, :, r * LANES:(r + 1) * LANES] = part.astype(o_ref.dtype)


def proj_view(h, w, dil, *, tm, tn):
    T, D = h.shape
    N = w.shape[1]
    tm, tn = min(tm, T), min(tn, N)
    ncb = tn // LANES
    return pl.pallas_call(
        functools.partial(_proj_view_kernel, ncb=ncb, dil=dil),
        grid=(T // tm, N // tn),
        in_specs=[pl.BlockSpec((tm, D), lambda i, j: (i, 0)),
                  pl.BlockSpec((D, tn), lambda i, j: (0, j))],
        out_specs=pl.BlockSpec((ncb, tm // dil, dil * LANES), lambda i, j: (j, i, 0)),
        out_shape=jax.ShapeDtypeStruct((N // LANES, T // dil, dil * LANES), BF16),
        scratch_shapes=[pltpu.VMEM((ncb, tm, LANES), F32)],
        compiler_params=_cparams(("parallel", "parallel")),
        name="proj_view",
    )(h, w)


CUMSUM_BLOCK = 256


def _logf_cumsum_kernel(f_ref, b_ref, c_ref, *, nblk):
    row = lax.broadcasted_iota(jnp.int32, (CUMSUM_BLOCK, CUMSUM_BLOCK), 0)
    col = lax.broadcasted_iota(jnp.int32, (CUMSUM_BLOCK, CUMSUM_BLOCK), 1)
    tri = jnp.where(col <= row, 1.0, 0.0).astype(F32)

    def body(i, carry):
        sl = pl.ds(pl.multiple_of(i * CUMSUM_BLOCK, CUMSUM_BLOCK), CUMSUM_BLOCK)
        z = f_ref[sl, :] + b_ref[...]
        logf = jnp.minimum(z, 0.0) - jnp.log1p(jnp.exp(-jnp.abs(z)))
        cs = jnp.dot(tri, logf, preferred_element_type=F32, precision=HIGHEST) + carry
        c_ref[sl, :] = cs
        return cs[CUMSUM_BLOCK - 1:CUMSUM_BLOCK, :]

    lax.fori_loop(0, nblk, body, jnp.zeros((1, LANES), F32))


def logf_cumsum(small, bias_row, B, S):
    T = B * S
    return pl.pallas_call(
        functools.partial(_logf_cumsum_kernel, nblk=S // CUMSUM_BLOCK),
        grid=(B,),
        in_specs=[pl.BlockSpec((S, LANES), lambda b: (b, 0)),
                  pl.BlockSpec((1, LANES), lambda b: (0, 0))],
        out_specs=pl.BlockSpec((S, LANES), lambda b: (b, 0)),
        out_shape=jax.ShapeDtypeStruct((T, LANES), F32),
        compiler_params=_cparams(("parallel",)),
        name="logf_cumsum",
    )(small, bias_row)


FLASH_ROW_CHUNK = 128


LOG2E = math.log2(math.e)


def _lane_tile(x, n):
    return jnp.concatenate([x] * n, axis=1)


def _flash_update(s2, v, m_ref, l_ref, acc_ref):
    m_old = m_ref[...]
    m_new = jnp.maximum(m_old, jnp.max(s2, axis=-1, keepdims=True))
    m_safe = jnp.where(m_new == NEG_INF, 0.0, m_new)
    alpha = jnp.exp2(m_old - m_safe)
    p = jnp.exp2(s2 - _lane_tile(m_safe, s2.shape[1] // LANES))
    l_ref[...] = alpha * l_ref[...] + jnp.sum(p, axis=-1, keepdims=True)
    acc_ref[...] = alpha * acc_ref[...] + jnp.dot(p.astype(BF16), v, preferred_element_type=F32)
    m_ref[...] = m_new


def _flash_init(m_ref, l_ref, acc_ref):
    m_ref[...] = jnp.full(m_ref.shape, NEG_INF, F32)
    l_ref[...] = jnp.zeros(l_ref.shape, F32)
    acc_ref[...] = jnp.zeros(acc_ref.shape, F32)


def _flash_result(l_ref, acc_ref):
    l = l_ref[...]
    return acc_ref[...] / jnp.where(l > 0.0, l, 1.0)


def _qk(q, k):
    return lax.dot_general(q, k, (((1,), (1,)), ((), ())), preferred_element_type=F32)


def _fox_kernel(qi_ref, ki_ref, q_ref, k_ref, v_ref, cq_ref, ck_ref, o_ref, m_ref, l_ref, acc_ref, cqb_ref,
                *, tq, tk, rc, nb):
    p = pl.program_id(1)
    qi, ki = qi_ref[p], ki_ref[p]

    @pl.when(ki == 0)
    def _():
        _flash_init(m_ref, l_ref, acc_ref)
        lane = lax.broadcasted_iota(jnp.int32, cq_ref.shape, 2)
        mine = jnp.sum(jnp.where(lane == pl.program_id(0), cq_ref[...], 0.0), axis=-1, keepdims=True)
        cqb_ref[...] = jnp.broadcast_to(mine * LOG2E, cqb_ref.shape)

    def step(b, causal_mask):
        ck = ck_ref[b] * LOG2E
        chunks = [pl.ds(c * rc, rc) for c in range(tq // rc)]
        widths = [min(tk, (c + 1) * rc) if causal_mask and tq == tk else tk for c in range(tq // rc)]
        qk = [_qk(q_ref[b, rows, :], k_ref[b, 0:w, :]) for rows, w in zip(chunks, widths)]
        for c, (rows, w) in enumerate(zip(chunks, widths)):
            s = qk[c] * (ATTN_SCALE * LOG2E) + (_lane_tile(cqb_ref[b, rows, :], w // LANES) - ck[:, 0:w])
            if causal_mask:
                qpos = qi * tq + c * rc + lax.broadcasted_iota(jnp.int32, (rc, w), 0)
                kpos = ki * tk + lax.broadcasted_iota(jnp.int32, (rc, w), 1)
                s = jnp.where(kpos <= qpos, s, NEG_INF)
            _flash_update(s, v_ref[b, 0:w, :], m_ref.at[b, rows], l_ref.at[b, rows], acc_ref.at[b, rows])

    crosses_diagonal = (ki + 1) * tk - 1 > qi * tq

    @pl.when(crosses_diagonal)
    def _():
        _for_range(0, nb, lambda b: step(b, True))

    @pl.when(jnp.logical_not(crosses_diagonal))
    def _():
        _for_range(0, nb, lambda b: step(b, False))

    @pl.when(ki == ((qi + 1) * tq - 1) // tk)
    def _():
        o_ref[...] = _flash_result(l_ref, acc_ref).astype(o_ref.dtype)


def _causal_pairs(nq, tq, tk):
    qs, ks = [], []
    for qi in range(nq):
        for ki in range(((qi + 1) * tq - 1) // tk + 1):
            qs.append(qi)
            ks.append(ki)
    return jnp.asarray(qs, jnp.int32), jnp.asarray(ks, jnp.int32)


def fox_attention(proj, c_packed, c_row, B, S, *, tq=512, tk=512):
    tq, tk = min(tq, S), min(tk, S)
    H = FOX_HEADS
    T = B * S
    qis, kis = _causal_pairs(S // tq, tq, tk)
    proj4 = proj.reshape(proj.shape[0], B, S, LANES)
    grid_spec = pltpu.PrefetchScalarGridSpec(
        num_scalar_prefetch=2,
        grid=(H, int(qis.shape[0])),
        in_specs=[
            pl.BlockSpec((None, B, tq, LANES), lambda h, p, qi, ki: (CB_FOX + h, 0, qi[p], 0)),
            pl.BlockSpec((None, B, tk, LANES), lambda h, p, qi, ki: (CB_FOX + H + h, 0, ki[p], 0)),
            pl.BlockSpec((None, B, tk, LANES), lambda h, p, qi, ki: (CB_FOX + 2 * H + h, 0, ki[p], 0)),
            pl.BlockSpec((B, tq, LANES), lambda h, p, qi, ki: (0, qi[p], 0)),
            pl.BlockSpec((None, B, 1, tk), lambda h, p, qi, ki: (h, 0, 0, ki[p])),
        ],
        out_specs=pl.BlockSpec((None, B, tq, LANES), lambda h, p, qi, ki: (h, 0, qi[p], 0)),
        scratch_shapes=[pltpu.VMEM((B, tq, LANES), F32)] * 4,
    )
    out = pl.pallas_call(
        functools.partial(_fox_kernel, tq=tq, tk=tk, rc=min(FLASH_ROW_CHUNK, tq), nb=B),
        grid_spec=grid_spec,
        out_shape=jax.ShapeDtypeStruct((H, B, S, LANES), BF16),
        compiler_params=_cparams(("parallel", "arbitrary")),
        name="fox_attention",
    )(qis, kis, proj4, proj4, proj4, c_packed.reshape(B, S, LANES), c_row.reshape(H, B, 1, S))
    return out.reshape(H, T, LANES)


DIL_ROWS_PER_STEP = 512


def _dil_kernel(q_ref, kp_ref, kc_ref, vp_ref, vc_ref, bias_ref, o_ref, lse_ref, *, dil, tu, nsub, hp):
    ui = pl.program_id(2)
    kcol = lax.broadcasted_iota(jnp.int32, (tu, 2 * tu), 1)
    first_ok = jnp.logical_or(ui > 0, kcol >= tu)
    units = [(c, slice(r * LANES, (r + 1) * LANES)) for c in range(nsub) for r in range(dil)]

    def window(prev_ref, cur_ref, h, c, sl):
        if c == 0:
            return jnp.concatenate([prev_ref[h, :, sl], cur_ref[h, 0:tu, sl]], axis=0)
        return cur_ref[h, (c - 1) * tu:(c + 1) * tu, sl]

    def head(h, carry):
        bias = bias_ref[h]
        bias_first = jnp.where(first_ok, bias, NEG_INF)
        qk = [_qk(q_ref[h, c * tu:(c + 1) * tu, sl], window(kp_ref, kc_ref, h, c, sl)) for c, sl in units]
        for i, (c, sl) in enumerate(units):
            rows = slice(c * tu, (c + 1) * tu)
            s = qk[i] * ATTN_SCALE + (bias_first if c == 0 else bias)
            m = jnp.max(s, axis=-1, keepdims=True)
            e = jnp.exp(s - m)
            den = jnp.sum(e, axis=-1, keepdims=True)
            o = jnp.dot(e.astype(BF16), window(vp_ref, vc_ref, h, c, sl), preferred_element_type=F32) / den
            o_ref[h, rows, sl] = o.astype(o_ref.dtype)
            lse_ref[h, rows, sl] = jnp.broadcast_to(m + jnp.log(den), (tu, LANES))
        return carry

    lax.fori_loop(0, hp, head, 0)


def dilated_group(view, c0, bias_tbl, group, B, S):
    dil = DIL_GROUPS[group][1]
    tu = DIL_TAPS
    Hg = DIL_HEADS_PER_GROUP
    T = B * S
    rows = min(DIL_ROWS_PER_STEP, S // dil)
    nsub = rows // tu
    nstep = S // dil // rows
    hp = max(1, Hg // max(1, dil // 4))
    cq, ck, cv = c0 // hp, (c0 + Hg) // hp, (c0 + 2 * Hg) // hp
    cur_blk = (hp, rows, dil * LANES)
    prev_blk = (hp, tu, dil * LANES)
    prev = lambda c: (lambda b, h, u: (c + h, jnp.maximum((b * nstep + u) * nsub - 1, 0), 0))
    cur = lambda c: (lambda b, h, u: (c + h, b * nstep + u, 0))
    o, lse = pl.pallas_call(
        functools.partial(_dil_kernel, dil=dil, tu=tu, nsub=nsub, hp=hp),
        grid=(B, Hg // hp, nstep),
        in_specs=[pl.BlockSpec(cur_blk, cur(cq)),
                  pl.BlockSpec(prev_blk, prev(ck)), pl.BlockSpec(cur_blk, cur(ck)),
                  pl.BlockSpec(prev_blk, prev(cv)), pl.BlockSpec(cur_blk, cur(cv)),
                  pl.BlockSpec((hp, tu, 2 * tu), lambda b, h, u: (h, 0, 0))],
        out_specs=[pl.BlockSpec(cur_blk, cur(0)), pl.BlockSpec(cur_blk, cur(0))],
        out_shape=[jax.ShapeDtypeStruct((Hg, T // dil, dil * LANES), BF16),
                   jax.ShapeDtypeStruct((Hg, T // dil, dil * LANES), F32)],
        compiler_params=_cparams(("parallel", "parallel", "arbitrary")),
        name=f"dilated_group{group}",
    )(view, view, view, view, view, bias_tbl)
    return o, lse


def dilated_bias_tables(rel_bias_b):
    tu = DIL_TAPS
    tables = []
    for g, (_, dil) in enumerate(DIL_GROUPS):
        heads = rel_bias_b[:, g * DIL_HEADS_PER_GROUP:(g + 1) * DIL_HEADS_PER_GROUP]

        def fn(taps, heads=heads, dil=dil):
            valid = (taps >= 0) & (taps <= DIL_TAPS)
            vals = heads[_t5_bucket_np(np.where(valid, taps, 0) * dil)].T
            return jnp.where(valid[None, :], vals, NEG_INF)

        tables.append(_toeplitz_blocks(fn, [tu], tu, 2 * tu)[0])
    return tables


def _compress_kernel(x_ref, pelo_ref, pehi_ref, w1a_ref, w1b_ref, w2_ref, o_ref, *, nchunk):
    x = x_ref[...].astype(F32)
    u0 = jnp.dot((x + pelo_ref[...]).astype(BF16), w1a_ref[...], preferred_element_type=F32)
    u1 = jnp.dot((x + pehi_ref[...]).astype(BF16), w1b_ref[...], preferred_element_type=F32)
    pre = u0 + pltpu.roll(u1, nchunk - 1, 0)
    hid = jax.nn.gelu(pre)
    out = jnp.dot(hid.astype(BF16), w2_ref[...], preferred_element_type=F32)
    row = lax.broadcasted_iota(jnp.int32, out.shape, 0)
    o_ref[...] = jnp.where(row < nchunk - 1, out, 0.0).astype(o_ref.dtype)


def nsa_compress(proj, pe, w1, w2, B, S):
    nchunk = S // CMP_STRIDE
    half = CMP_STRIDE * HEAD_DIM
    Hkv = NSA_KV_HEADS
    x = proj[CB_NSAKV:CB_NSAKV + 2 * Hkv].reshape(2, Hkv, B, nchunk, half)
    pe_lo = pe[:, :CMP_STRIDE].reshape(2, 1, half)
    pe_hi = pe[:, CMP_STRIDE:].reshape(2, 1, half)
    return pl.pallas_call(
        functools.partial(_compress_kernel, nchunk=nchunk),
        grid=(2, B, Hkv),
        in_specs=[pl.BlockSpec((None, None, None, nchunk, half), lambda t, b, h: (t, h, b, 0, 0)),
                  pl.BlockSpec((None, 1, half), lambda t, b, h: (t, 0, 0)),
                  pl.BlockSpec((None, 1, half), lambda t, b, h: (t, 0, 0)),
                  pl.BlockSpec((None, half, CMP_HIDDEN), lambda t, b, h: (t, 0, 0)),
                  pl.BlockSpec((None, half, CMP_HIDDEN), lambda t, b, h: (t, 1, 0)),
                  pl.BlockSpec((None, CMP_HIDDEN, HEAD_DIM), lambda t, b, h: (t, 0, 0))],
        out_specs=pl.BlockSpec((None, None, None, nchunk, HEAD_DIM), lambda t, b, h: (t, b, h, 0, 0)),
        out_shape=jax.ShapeDtypeStruct((2, B, Hkv, nchunk, HEAD_DIM), BF16),
        compiler_params=_cparams(("parallel", "parallel", "parallel")),
        name="nsa_compress",
    )(x, pe_lo, pe_hi, w1, w1, w2)


def _masked_softmax(s, mask, axis):
    s = jnp.where(mask, s, NEG_INF)
    m = jnp.max(s, axis=axis, keepdims=True)
    m = jnp.where(m == NEG_INF, 0.0, m)
    e = jnp.exp(s - m)
    den = jnp.sum(e, axis=axis, keepdims=True)
    return e / jnp.where(den > 0.0, den, 1.0)


def _cmp_select_kernel(q_ref, kc_ref, vc_ref, mt_ref, o_ref, sel_ref, *, tq, nchunk, n_slc, n_sel):
    q0 = pl.program_id(2) * tq
    kc = kc_ref[...]
    vc = vc_ref[...]
    pos_r = q0 + lax.broadcasted_iota(jnp.int32, (tq, nchunk), 0)
    end_r = lax.broadcasted_iota(jnp.int32, (tq, nchunk), 1) * CMP_STRIDE + (CMP_BLOCK - 1)
    vis_r = end_r <= pos_r
    pos_c = q0 + lax.broadcasted_iota(jnp.int32, (nchunk, tq), 1)
    end_c = lax.broadcasted_iota(jnp.int32, (nchunk, tq), 0) * CMP_STRIDE + (CMP_BLOCK - 1)
    vis_c = end_c <= pos_c
    imp = jnp.zeros((nchunk, tq), F32)
    for g in range(NSA_GQA):
        q = q_ref[g]
        p = _masked_softmax(_qk(q, kc) * ATTN_SCALE, vis_r, -1)
        o_ref[g] = jnp.dot(p.astype(BF16), vc, preferred_element_type=F32).astype(o_ref.dtype)
        imp = imp + _masked_softmax(_qk(kc, q) * ATTN_SCALE, vis_c, 0)
    p_slc = jnp.dot(mt_ref[...], imp, preferred_element_type=F32, precision=HIGHEST)

    blk = lax.broadcasted_iota(jnp.int32, (n_slc, tq), 0)
    cur = (q0 + lax.broadcasted_iota(jnp.int32, (n_slc, tq), 1)) // SLC_BLOCK
    forced = (blk == 0) | (blk == cur) | (blk == cur - 1)
    allowed = blk <= cur
    score = jnp.where(forced, 1e30, jnp.where(allowed, p_slc, -1.0))
    chosen = jnp.zeros((n_slc, tq), F32)
    for _ in range(n_sel):
        top = jnp.max(score, axis=0, keepdims=True)
        first = jnp.min(jnp.where(score == top, blk, n_slc), axis=0, keepdims=True)
        hit = blk == first
        chosen = jnp.where(hit, 1.0, chosen)
        score = jnp.where(hit, -2.0, score)
    sel_ref[...] = jnp.where(allowed, chosen, 0.0)


def nsa_cmp_select(proj, kvc, B, S, *, tq=512):
    tq = min(tq, S)
    nq = S // tq
    T = B * S
    nchunk = S // CMP_STRIDE
    n_slc = S // SLC_BLOCK
    n_sel = min(SLC_COUNT, n_slc)
    ratio, n_inner = SLC_BLOCK // CMP_STRIDE, CMP_BLOCK // CMP_STRIDE
    mt = np.zeros((n_slc, nchunk), np.float32)
    for j in range(n_slc):
        for m in range(ratio):
            for n in range(n_inner):
                c = ratio * j + m - n
                if 0 <= c < nchunk - 1:
                    mt[j, c] += 1.0
    G = NSA_GQA
    kv_spec = lambda t: pl.BlockSpec((None, None, None, nchunk, HEAD_DIM), lambda b, h, i: (t, b, h, 0, 0))
    return pl.pallas_call(
        functools.partial(_cmp_select_kernel, tq=tq, nchunk=nchunk, n_slc=n_slc, n_sel=n_sel),
        grid=(B, NSA_KV_HEADS, nq),
        in_specs=[pl.BlockSpec((G, tq, LANES), lambda b, h, i: (CB_NSAQ // G + h, b * nq + i, 0)),
                  kv_spec(0), kv_spec(1),
                  pl.BlockSpec((n_slc, nchunk), lambda b, h, i: (0, 0))],
        out_specs=[pl.BlockSpec((G, tq, LANES), lambda b, h, i: (h, b * nq + i, 0)),
                   pl.BlockSpec((None, None, n_slc, tq), lambda b, h, i: (b, h, 0, i))],
        out_shape=[jax.ShapeDtypeStruct((NSA_Q_HEADS, T, LANES), BF16),
                   jax.ShapeDtypeStruct((B, NSA_KV_HEADS, n_slc, S), F32)],
        compiler_params=_cparams(("parallel", "parallel", "arbitrary")),
        name="nsa_cmp_select",
    )(proj, kvc, kvc, jnp.asarray(mt))


def _slc_kernel(qi_ref, ki_ref, q_ref, k_ref, v_ref, sel_ref, bias_ref, o_ref, m_ref, l_ref, acc_ref,
                *, tq, tk, n_slc, nb):
    p = pl.program_id(1)
    qi, ki = qi_ref[p], ki_ref[p]

    @pl.when(ki == 0)
    def _():
        _flash_init(m_ref, l_ref, acc_ref)

    blk_of_key = ki * (tk // SLC_BLOCK) + lax.broadcasted_iota(jnp.int32, (n_slc, tk), 1) // SLC_BLOCK
    expand = jnp.where(lax.broadcasted_iota(jnp.int32, (n_slc, tk), 0) == blk_of_key, 1.0, 0.0).astype(BF16)

    def step(b, causal_mask):
        picked = jnp.dot(sel_ref[b], expand, preferred_element_type=F32)
        keep = picked > 0.5
        if causal_mask:
            qpos = qi * tq + lax.broadcasted_iota(jnp.int32, (tq, tk), 0)
            kpos = ki * tk + lax.broadcasted_iota(jnp.int32, (tq, tk), 1)
            keep = jnp.where(kpos <= qpos, picked, 0.0) > 0.5
        k, v = k_ref[b], v_ref[b]
        qk = [_qk(q_ref[g, b], k) for g in range(NSA_GQA)]
        for g in range(NSA_GQA):
            s = qk[g] * (ATTN_SCALE * LOG2E) + bias_ref[g]
            s = jnp.where(keep, s, NEG_INF)
            _flash_update(s, v, m_ref.at[g, b], l_ref.at[g, b], acc_ref.at[g, b])

    crosses_diagonal = (ki + 1) * tk - 1 > qi * tq

    @pl.when(crosses_diagonal)
    def _():
        _for_range(0, nb, lambda b: step(b, True))

    @pl.when(jnp.logical_not(crosses_diagonal))
    def _():
        _for_range(0, nb, lambda b: step(b, False))

    @pl.when(ki == ((qi + 1) * tq - 1) // tk)
    def _():
        o_ref[...] = _flash_result(l_ref, acc_ref).astype(o_ref.dtype)


def slc_bias_table(rel_bias_c, S, tq, tk):
    buckets = _t5_bucket_np(np.arange(S + tk))
    not_last = np.nonzero(buckets != REL_BUCKETS - 1)[0]
    far_start = int(not_last[-1]) + 1 if not_last.size else 0
    n_delta = min(S // tq, -(-(far_start + tk - 1) // tq) + 1)

    def fn(d):
        return rel_bias_c[_t5_bucket_np(d)].T * LOG2E

    return _toeplitz_blocks(fn, [dl * tq for dl in range(n_delta)], tq, tk)


def nsa_selected(proj, sel, bias_tbl, B, S, *, tq=256, tk=512):
    tq, tk = min(tq, S), min(tk, S)
    T = B * S
    n_slc = S // SLC_BLOCK
    G = NSA_GQA
    ck = CB_NSAKV + (1 * 2 + 0) * NSA_KV_HEADS
    cv = CB_NSAKV + (1 * 2 + 1) * NSA_KV_HEADS
    qis, kis = _causal_pairs(S // tq, tq, tk)
    n_delta = bias_tbl.shape[0]
    proj4 = proj.reshape(proj.shape[0], B, S, LANES)
    grid_spec = pltpu.PrefetchScalarGridSpec(
        num_scalar_prefetch=2,
        grid=(NSA_KV_HEADS, int(qis.shape[0])),
        in_specs=[
            pl.BlockSpec((G, B, tq, LANES), lambda h, p, qi, ki: (CB_NSAQ // G + h, 0, qi[p], 0)),
            pl.BlockSpec((None, B, tk, LANES), lambda h, p, qi, ki: (ck + h, 0, ki[p], 0)),
            pl.BlockSpec((None, B, tk, LANES), lambda h, p, qi, ki: (cv + h, 0, ki[p], 0)),
            pl.BlockSpec((B, None, tq, n_slc), lambda h, p, qi, ki: (0, h, qi[p], 0)),
            pl.BlockSpec((None, G, tq, tk),
                         lambda h, p, qi, ki: (jnp.minimum(qi[p] - ki[p] * (tk // tq), n_delta - 1), h, 0, 0)),
        ],
        out_specs=pl.BlockSpec((G, B, tq, LANES), lambda h, p, qi, ki: (h, 0, qi[p], 0)),
        scratch_shapes=[pltpu.VMEM((G, B, tq, LANES), F32)] * 3,
    )
    out = pl.pallas_call(
        functools.partial(_slc_kernel, tq=tq, tk=tk, n_slc=n_slc, nb=B),
        grid_spec=grid_spec,
        out_shape=jax.ShapeDtypeStruct((NSA_Q_HEADS, B, S, LANES), BF16),
        compiler_params=_cparams(("parallel", "arbitrary")),
        name="nsa_selected",
    )(qis, kis, proj4, proj4, proj4, sel, bias_tbl)
    return out.reshape(NSA_Q_HEADS, T, LANES)


def _win_kernel(q_ref, k0_ref, k1_ref, k2_ref, v0_ref, v1_ref, v2_ref, bias_ref, o_ref, *, tq, nkb):
    qi = pl.program_id(2)
    k = jnp.concatenate([r[...] for r in (k0_ref, k1_ref, k2_ref)][-nkb:], axis=0)
    v = jnp.concatenate([r[...] for r in (v0_ref, v1_ref, v2_ref)][-nkb:], axis=0)
    kpos = (qi - (nkb - 1)) * tq + lax.broadcasted_iota(jnp.int32, (tq, nkb * tq), 1)
    qk = [_qk(q_ref[g], k) for g in range(NSA_GQA)]
    for g in range(NSA_GQA):
        s = qk[g] * ATTN_SCALE + bias_ref[g]
        p = _masked_softmax(s, kpos >= 0, -1)
        o_ref[g] = jnp.dot(p.astype(BF16), v, preferred_element_type=F32).astype(o_ref.dtype)


def win_bias_table(rel_bias_c, tq, nkb):
    def fn(dist):
        valid = (dist >= 0) & (dist < WIN)
        return jnp.where(valid[None, :], rel_bias_c[_t5_bucket_np(dist)].T, NEG_INF)

    return _toeplitz_blocks(fn, [(nkb - 1) * tq], tq, nkb * tq)[0]


def nsa_window(proj, bias_tbl, B, S, *, tq=256):
    tq = min(tq, S)
    nq = S // tq
    T = B * S
    G = NSA_GQA
    nkb = min(WIN // tq + 1, 3)
    ck = CB_NSAKV + (2 * 2 + 0) * NSA_KV_HEADS
    cv = CB_NSAKV + (2 * 2 + 1) * NSA_KV_HEADS
    kv = lambda c, back: pl.BlockSpec(
        (None, tq, LANES), lambda b, h, i: (c + h, b * nq + jnp.maximum(i - back, 0), 0))
    return pl.pallas_call(
        functools.partial(_win_kernel, tq=tq, nkb=nkb),
        grid=(B, NSA_KV_HEADS, nq),
        in_specs=[pl.BlockSpec((G, tq, LANES), lambda b, h, i: (CB_NSAQ // G + h, b * nq + i, 0)),
                  kv(ck, 2), kv(ck, 1), kv(ck, 0), kv(cv, 2), kv(cv, 1), kv(cv, 0),
                  pl.BlockSpec((G, tq, nkb * tq), lambda b, h, i: (h, 0, 0))],
        out_specs=pl.BlockSpec((G, tq, LANES), lambda b, h, i: (h, b * nq + i, 0)),
        out_shape=jax.ShapeDtypeStruct((NSA_Q_HEADS, T, LANES), BF16),
        compiler_params=_cparams(("parallel", "parallel", "arbitrary")),
        name="nsa_window",
    )(proj, proj, proj, proj, proj, proj, proj, bias_tbl)


def _sigmoid(x):
    return 1.0 / (1.0 + jnp.exp(-x))


def _merge_kernel(fox_ref, d0_ref, d1_ref, d2_ref, l0_ref, l1_ref, l2_ref, cmp_ref, slc_ref, win_ref, small_ref,
                  g0_ref, g1_ref, g2_ref, wa_ref, wb_ref, wc_ref, o_ref, nat_ref, *, ncb):
    tm = o_ref.shape[0]

    def natural(ref, h, dil, slot):
        if dil == 1:
            return ref[h].astype(F32)
        for r in range(dil):
            nat_ref[slot, pl.ds(r, tm // dil, stride=dil), :] = ref[h, :, r * LANES:(r + 1) * LANES].astype(F32)
        return nat_ref[slot]

    ya = jnp.concatenate([fox_ref[h] for h in range(FOX_HEADS)], axis=1)
    yb = []
    for h in range(DIL_HEADS_PER_GROUP):
        outs, lse = [], []
        for g, (o_g, l_g) in enumerate(((d0_ref, l0_ref), (d1_ref, l1_ref), (d2_ref, l2_ref))):
            dil = DIL_GROUPS[g][1]
            slot = (h * len(DIL_GROUPS) + g) * 2
            outs.append(natural(o_g, h, dil, slot))
            lse.append(natural(l_g, h, dil, slot + 1))
        top = jnp.maximum(jnp.maximum(lse[0], lse[1]), lse[2])
        w = [jnp.exp(x - top) for x in lse]
        tot = w[0] + w[1] + w[2]
        y = sum((w[g] / tot) * outs[g] for g in range(len(DIL_GROUPS)))
        yb.append(y.astype(BF16))
    gates = _sigmoid(small_ref[...])
    yc = []
    for h in range(NSA_Q_HEADS):
        y = jnp.zeros(cmp_ref.shape[1:], F32)
        for br, ref in enumerate((cmp_ref, slc_ref, win_ref)):
            col = SMALL_NSA_G + br * NSA_Q_HEADS + h
            y = y + gates[:, col:col + 1] * ref[h].astype(F32)
        yc.append(y.astype(BF16))

    def gate(ref):
        return _sigmoid(jnp.concatenate([ref[c] for c in range(ncb)], axis=1).astype(F32))

    merged = (gate(g0_ref) * jnp.dot(ya, wa_ref[...], preferred_element_type=F32)
              + gate(g1_ref) * jnp.dot(jnp.concatenate(yb, axis=1), wb_ref[...], preferred_element_type=F32)
              + gate(g2_ref) * jnp.dot(jnp.concatenate(yc, axis=1), wc_ref[...], preferred_element_type=F32))
    o_ref[...] = merged.astype(o_ref.dtype)


def merge_branches(proj, small, fox_o, dil_o, dil_lse, cmp_o, slc_o, win_o, wa, wb, wc, *, tm=256):
    T = proj.shape[1]
    D = wa.shape[1]
    tm = min(tm, T)
    ncb = D // LANES
    Hg = DIL_HEADS_PER_GROUP
    heads = lambda n: pl.BlockSpec((n, tm, LANES), lambda i: (0, i, 0))
    gate = lambda b: pl.BlockSpec((ncb, tm, LANES), lambda i: (CB_MERGE // ncb + b, i, 0))
    dil = [pl.BlockSpec((Hg, tm // d, d * LANES), lambda i: (0, i, 0)) for _, d in DIL_GROUPS]
    wspec = lambda k: pl.BlockSpec((k, D), lambda i: (0, 0))
    return pl.pallas_call(
        functools.partial(_merge_kernel, ncb=ncb),
        grid=(T // tm,),
        in_specs=[heads(FOX_HEADS), *dil, *dil, heads(NSA_Q_HEADS), heads(NSA_Q_HEADS), heads(NSA_Q_HEADS),
                  pl.BlockSpec((tm, LANES), lambda i: (i, 0)),
                  gate(0), gate(1), gate(2), wspec(wa.shape[0]), wspec(wb.shape[0]), wspec(wc.shape[0])],
        out_specs=pl.BlockSpec((tm, D), lambda i: (i, 0)),
        out_shape=jax.ShapeDtypeStruct((T, D), BF16),
        scratch_shapes=[pltpu.VMEM((2 * Hg * len(DIL_GROUPS), tm, LANES), F32)],
        compiler_params=_cparams(("parallel",)),
        name="merge_branches",
    )(fox_o, *dil_o, *dil_lse, cmp_o, slc_o, win_o, small, proj, proj, proj, wa, wb, wc)


def _matmul_residual_kernel(a_ref, w_ref, x_ref, o_ref):
    o_ref[...] = x_ref[...] + jnp.dot(a_ref[...], w_ref[...], preferred_element_type=F32)


def matmul_residual(a, w, x, *, tm=1024, tn=1024):
    T, K = a.shape
    N = w.shape[1]
    tm, tn = min(tm, T), min(tn, N)
    return pl.pallas_call(
        _matmul_residual_kernel,
        grid=(T // tm, N // tn),
        in_specs=[pl.BlockSpec((tm, K), lambda i, j: (i, 0)),
                  pl.BlockSpec((K, tn), lambda i, j: (0, j)),
                  pl.BlockSpec((tm, tn), lambda i, j: (i, j))],
        out_specs=pl.BlockSpec((tm, tn), lambda i, j: (i, j)),
        out_shape=jax.ShapeDtypeStruct((T, N), F32),
        compiler_params=_cparams(("parallel", "parallel")),
        name="matmul_residual",
    )(a, w, x)


def _silu(x):
    return x * _sigmoid(x)


def _swiglu_partial(h, wg_ref, wu_ref, wd_ref):
    act = _silu(jnp.dot(h, wg_ref[...].astype(BF16), preferred_element_type=F32)) * \
        jnp.dot(h, wu_ref[...].astype(BF16), preferred_element_type=F32)
    return jnp.dot(act.astype(BF16), wd_ref[...].astype(BF16), preferred_element_type=F32)


def _ffn_kernel(x_ref, g_ref, wg_ref, wu_ref, wd_ref, gf_ref, o_ref, h_ref, *, final_norm):
    f = pl.program_id(1)

    @pl.when(f == 0)
    def _():
        h_ref[...] = _rms_rows(x_ref[...], g_ref[...]).astype(BF16)
        o_ref[...] = x_ref[...]

    o_ref[...] += _swiglu_partial(h_ref[...], wg_ref, wu_ref, wd_ref)

    if final_norm:
        @pl.when(f == pl.num_programs(1) - 1)
        def _():
            o_ref[...] = _rms_rows(o_ref[...], gf_ref[...])


def ffn(x, g, wg, wu, wd, g_final, *, final_norm, tm=1024, tf=256):
    T, D = x.shape
    F = wg.shape[1]
    tm, tf = min(tm, T), min(tf, F)
    return pl.pallas_call(
        functools.partial(_ffn_kernel, final_norm=final_norm),
        grid=(T // tm, F // tf),
        in_specs=[pl.BlockSpec((tm, D), lambda i, f: (i, 0)),
                  pl.BlockSpec((1, D), lambda i, f: (0, 0)),
                  pl.BlockSpec((D, tf), lambda i, f: (0, f)),
                  pl.BlockSpec((D, tf), lambda i, f: (0, f)),
                  pl.BlockSpec((tf, D), lambda i, f: (f, 0)),
                  pl.BlockSpec((1, D), lambda i, f: (0, 0))],
        out_specs=pl.BlockSpec((tm, D), lambda i, f: (i, 0)),
        out_shape=jax.ShapeDtypeStruct((T, D), F32),
        scratch_shapes=[pltpu.VMEM((tm, D), BF16)],
        compiler_params=_cparams(("parallel", "arbitrary")),
        name="ffn",
    )(x, g, wg, wu, wd, g_final)


def _route_kernel(x_ref, g_ref, r_ref, o_ref):
    h = _rms_rows(x_ref[...], g_ref[...])
    logits = jnp.dot(h, r_ref[...], preferred_element_type=F32, precision=HIGHEST)
    lane = lax.broadcasted_iota(jnp.int32, logits.shape, 1)
    logits = jnp.where(lane < N_EXPERTS, logits, NEG_INF)
    v1 = jnp.max(logits, axis=1, keepdims=True)
    i1 = jnp.min(jnp.where(logits == v1, lane, LANES), axis=1, keepdims=True)
    rest = jnp.where(lane == i1, NEG_INF, logits)
    v2 = jnp.max(rest, axis=1, keepdims=True)
    i2 = jnp.min(jnp.where(rest == v2, lane, LANES), axis=1, keepdims=True)
    e2 = jnp.exp(v2 - v1)
    den = 1.0 + e2
    rec = jnp.where(lane == ROUTE_CHOICE + i1, 1.0, jnp.where(lane == ROUTE_CHOICE + i2, 2.0, 0.0))
    rec = jnp.where(lane == ROUTE_WEIGHT, 1.0 / den, jnp.where(lane == ROUTE_WEIGHT + 1, e2 / den, rec))
    o_ref[...] = rec


def moe_route(x, g, router_pad, *, tm=512):
    T, D = x.shape
    tm = min(tm, T)
    return pl.pallas_call(
        _route_kernel,
        grid=(T // tm,),
        in_specs=[pl.BlockSpec((tm, D), lambda i: (i, 0)),
                  pl.BlockSpec((1, D), lambda i: (0, 0)),
                  pl.BlockSpec((D, LANES), lambda i: (0, 0))],
        out_specs=pl.BlockSpec((tm, LANES), lambda i: (i, 0)),
        out_shape=jax.ShapeDtypeStruct((T, LANES), F32),
        compiler_params=_cparams(("parallel",)),
        name="moe_route",
    )(x, g, router_pad)


def _moe_plan(route, T, tm):
    E = N_EXPERTS
    n_tiles = (TOP_K * T + E * tm) // tm
    choice = route[:, ROUTE_CHOICE:ROUTE_CHOICE + E]
    seli = (choice > 0.5).astype(jnp.int32)
    counts = jnp.sum(seli, axis=0)
    padded = ((counts + tm - 1) // tm) * tm
    seg_end = jnp.cumsum(padded)
    seg_start = seg_end - padded
    slot = seg_start[None, :] + jnp.cumsum(seli, axis=0) - 1
    slot_of = jnp.stack([jnp.sum(jnp.where(choice == k + 1.0, slot, 0), axis=1) for k in range(TOP_K)])
    n_used = (seg_end[-1] // tm).astype(jnp.int32)
    tile_start = jnp.arange(n_tiles, dtype=jnp.int32) * tm
    tile_expert = jnp.sum((seg_end[None, :] <= tile_start[:, None]).astype(jnp.int32), axis=1)
    tile_expert = jnp.minimum(tile_expert, E - 1)
    last_expert = jnp.take(tile_expert, n_used - 1)
    tile_expert = jnp.where(jnp.arange(n_tiles) < n_used, tile_expert, last_expert)
    pad_lo = (seg_start + counts).astype(jnp.int32)
    return (tile_expert.astype(jnp.int32), n_used.reshape(1), slot_of.reshape(-1).astype(jnp.int32),
            pad_lo, seg_end.astype(jnp.int32))


def _for_range(lo, hi, fn, unroll=None):
    def body(r, carry):
        fn(r)
        return carry
    lax.fori_loop(lo, hi, body, 0, unroll=unroll)


def _moe_dispatch_kernel(slot_ref, lo_ref, hi_ref, nt_ref, x_ref, xs_hbm, zero_ref, sem, zsem,
                         *, tc, T, chunks_per_tile, n_chunks):
    i = pl.program_id(0)

    def token_row(k, r):
        s = slot_ref[k * T + i * tc + r]
        return pltpu.make_async_copy(x_ref.at[pl.ds(r, 1)], xs_hbm.at[pl.ds(s, 1)], sem.at[0])

    def zero_row(s):
        return pltpu.make_async_copy(zero_ref.at[pl.ds(0, 1)], xs_hbm.at[pl.ds(s, 1)], zsem.at[0])

    def zero_chunk(c):
        rows = pl.ds(pl.multiple_of(c * MOE_ZERO_ROWS, MOE_ZERO_ROWS), MOE_ZERO_ROWS)
        return pltpu.make_async_copy(zero_ref, xs_hbm.at[rows], zsem.at[0])

    @pl.when(i == 0)
    def _():
        zero_ref[...] = jnp.zeros(zero_ref.shape, F32)
        for start in (True, False):
            for e in range(N_EXPERTS):
                _for_range(lo_ref[e], hi_ref[e], lambda s: zero_row(s).start() if start else zero_row(s).wait())
            _for_range(nt_ref[0] * chunks_per_tile, n_chunks,
                       lambda c: zero_chunk(c).start() if start else zero_chunk(c).wait())

    for start in (True, False):
        for k in range(TOP_K):
            _for_range(0, tc, lambda r: token_row(k, r).start() if start else token_row(k, r).wait(), unroll=8)


def moe_dispatch(x, plan, *, tm, tc=512):
    _, n_used, slot_of, pad_lo, pad_hi = plan
    T, D = x.shape
    tc = min(tc, T)
    n_slots = TOP_K * T + N_EXPERTS * tm
    grid_spec = pltpu.PrefetchScalarGridSpec(
        num_scalar_prefetch=4,
        grid=(T // tc,),
        in_specs=[pl.BlockSpec((tc, D), lambda i, s, lo, hi, nt: (i, 0))],
        out_specs=pl.BlockSpec(memory_space=pl.ANY),
        scratch_shapes=[pltpu.VMEM((MOE_ZERO_ROWS, D), F32),
                        pltpu.SemaphoreType.DMA((1,)), pltpu.SemaphoreType.DMA((1,))],
    )
    return pl.pallas_call(
        functools.partial(_moe_dispatch_kernel, tc=tc, T=T, chunks_per_tile=tm // MOE_ZERO_ROWS,
                          n_chunks=n_slots // MOE_ZERO_ROWS),
        grid_spec=grid_spec,
        out_shape=jax.ShapeDtypeStruct((n_slots, D), F32),
        compiler_params=_cparams(("arbitrary",)),
        name="moe_dispatch",
    )(slot_of, pad_lo, pad_hi, n_used, x)


def _moe_ffn_kernel(te_ref, nt_ref, xs_ref, g_ref, wg_ref, wu_ref, wd_ref, y_ref, h_ref):
    i, f = pl.program_id(0), pl.program_id(1)
    used = i < nt_ref[0]

    @pl.when(used & (f == 0))
    def _():
        h_ref[...] = _rms_rows(xs_ref[...], g_ref[...]).astype(BF16)
        y_ref[...] = _swiglu_partial(h_ref[...], wg_ref, wu_ref, wd_ref)

    @pl.when(used & (f > 0))
    def _():
        y_ref[...] += _swiglu_partial(h_ref[...], wg_ref, wu_ref, wd_ref)

    @pl.when(jnp.logical_not(used) & (f == 0))
    def _():
        y_ref[...] = jnp.zeros(y_ref.shape, F32)


def moe_experts(xs, g, plan, wg, wu, wd, *, tm, tf=1024):
    tile_expert, n_used = plan[0], plan[1]
    n_slots, D = xs.shape
    F = wg.shape[2]
    tf = min(tf, F)
    nf = F // tf

    def fidx(i, f, nt):
        return jnp.where(i < nt[0], f, nf - 1)

    grid_spec = pltpu.PrefetchScalarGridSpec(
        num_scalar_prefetch=2,
        grid=(n_slots // tm, nf),
        in_specs=[
            pl.BlockSpec((tm, D), lambda i, f, te, nt: (jnp.minimum(i, nt[0] - 1), 0)),
            pl.BlockSpec((1, D), lambda i, f, te, nt: (0, 0)),
            pl.BlockSpec((None, D, tf), lambda i, f, te, nt: (te[i], 0, fidx(i, f, nt))),
            pl.BlockSpec((None, D, tf), lambda i, f, te, nt: (te[i], 0, fidx(i, f, nt))),
            pl.BlockSpec((None, tf, D), lambda i, f, te, nt: (te[i], fidx(i, f, nt), 0)),
        ],
        out_specs=pl.BlockSpec((tm, D), lambda i, f, te, nt: (i, 0)),
        scratch_shapes=[pltpu.VMEM((tm, D), BF16)],
    )
    return pl.pallas_call(
        _moe_ffn_kernel,
        grid_spec=grid_spec,
        out_shape=jax.ShapeDtypeStruct((n_slots, D), F32),
        compiler_params=_cparams(("parallel", "arbitrary")),
        name="moe_experts",
    )(tile_expert, n_used, xs, g, wg, wu, wd)


def _moe_combine_kernel(slot_ref, x_ref, y_hbm, route_ref, gf_ref, o_ref, ybuf, sem, *, tc, T, final_norm):
    i = pl.program_id(0)

    def slot_row(k, r):
        s = slot_ref[k * T + i * tc + r]
        return pltpu.make_async_copy(y_hbm.at[pl.ds(s, 1)], ybuf.at[k, pl.ds(r, 1)], sem.at[0])

    for start in (True, False):
        for k in range(TOP_K):
            _for_range(0, tc, lambda r: slot_row(k, r).start() if start else slot_row(k, r).wait(), unroll=8)

    w = route_ref[...]
    out = x_ref[...]
    for k in range(TOP_K):
        out = out + w[:, ROUTE_WEIGHT + k:ROUTE_WEIGHT + k + 1] * ybuf[k]
    if final_norm:
        out = _rms_rows(out, gf_ref[...])
    o_ref[...] = out


def moe_combine(x, y, route, plan, g_final, *, final_norm, tc=256):
    T, D = x.shape
    tc = min(tc, T)
    grid_spec = pltpu.PrefetchScalarGridSpec(
        num_scalar_prefetch=1,
        grid=(T // tc,),
        in_specs=[pl.BlockSpec((tc, D), lambda i, s: (i, 0)),
                  pl.BlockSpec(memory_space=pl.ANY),
                  pl.BlockSpec((tc, LANES), lambda i, s: (i, 0)),
                  pl.BlockSpec((1, D), lambda i, s: (0, 0))],
        out_specs=pl.BlockSpec((tc, D), lambda i, s: (i, 0)),
        scratch_shapes=[pltpu.VMEM((TOP_K, tc, D), F32), pltpu.SemaphoreType.DMA((1,))],
    )
    return pl.pallas_call(
        functools.partial(_moe_combine_kernel, tc=tc, T=T, final_norm=final_norm),
        grid_spec=grid_spec,
        out_shape=jax.ShapeDtypeStruct((T, D), F32),
        compiler_params=_cparams(("arbitrary",)),
        name="moe_combine",
    )(plan[2], x, y, route, g_final)


def moe_block(x, g, router, wg, wu, wd, g_final, *, final_norm, tm=512):
    T, D = x.shape
    tm = min(tm, T)
    router_pad = jnp.zeros((D, LANES), F32).at[:, :N_EXPERTS].set(router)
    route = moe_route(x, g, router_pad)
    plan = _moe_plan(route, T, tm)
    xs = moe_dispatch(x, plan, tm=tm)
    y = moe_experts(xs, g, plan, wg.astype(BF16), wu.astype(BF16), wd.astype(BF16), tm=tm)
    return moe_combine(x, y, route, plan, g_final, final_norm=final_norm)


def _split_w_in(w):
    D = w.shape[0]
    n_fox = 3 * FOX_HEADS * HEAD_DIM
    n_dil = 3 * DIL_HEADS * HEAD_DIM
    n_nsa = (NSA_Q_HEADS + NSA_BRANCHES * 2 * NSA_KV_HEADS) * HEAD_DIM
    n_g = NSA_BRANCHES * NSA_Q_HEADS
    a0 = n_fox
    a1 = a0 + FOX_HEADS
    a2 = a1 + n_dil
    a3 = a2 + n_nsa
    a4 = a3 + n_g
    group_cols = DIL_HEADS_PER_GROUP * HEAD_DIM
    dil = w[:, a1:a2].reshape(D, 3, len(DIL_GROUPS), group_cols)
    dil_sets = [dil[:, :, gi].reshape(D, 3 * group_cols).astype(BF16) for gi in range(len(DIL_GROUPS))]
    main = jnp.concatenate([w[:, a4:].astype(BF16), w[:, :a0].astype(BF16), dil_sets[0],
                            w[:, a2:a3].astype(BF16)], axis=1)
    small = jnp.concatenate([w[:, a0:a1], w[:, a3:a4],
                             jnp.zeros((D, LANES - FOX_HEADS - n_g), w.dtype)], axis=1).astype(BF16)
    return main, dil_sets[1:], small


def mixing_block(x, B, S, norm_g, w_in, forget_bias, cmp_pe, cmp_w1, cmp_w2, wa, wb, wc, w_out, tables):
    T, D = x.shape
    w_main, w_dil, w_small = _split_w_in(w_in)
    g = norm_g.reshape(1, D)
    proj, h = rms_proj(x, g, w_main, BF16, tm=1024, tn=1024)
    small = slab_proj(h, w_small, F32, tm=1024, tn=LANES)[0]
    dil_views = [proj_view(h, w, DIL_GROUPS[gi + 1][1], tm=1024, tn=4 * LANES) for gi, w in enumerate(w_dil)]

    bias_row = jnp.zeros((1, LANES), F32).at[0, :FOX_HEADS].set(forget_bias)
    c = logf_cumsum(small, bias_row, B, S)
    c_row = c[:, :FOX_HEADS].T.reshape(FOX_HEADS, 1, T)
    fox_o = fox_attention(proj, c, c_row, B, S)

    dil = [dilated_group(proj, CB_DIL, tables["dil"][0], 0, B, S)]
    dil += [dilated_group(v, 0, tables["dil"][gi + 1], gi + 1, B, S) for gi, v in enumerate(dil_views)]
    dil_o = [d[0] for d in dil]
    dil_lse = [d[1] for d in dil]

    kvc = nsa_compress(proj, cmp_pe, cmp_w1.astype(BF16), cmp_w2.astype(BF16), B, S)
    cmp_o, sel_t = nsa_cmp_select(proj, kvc, B, S)
    sel = jnp.swapaxes(sel_t, 2, 3).astype(BF16)
    slc_o = nsa_selected(proj, sel, tables["slc"], B, S, tq=tables["slc_tq"], tk=tables["slc_tk"])
    win_o = nsa_window(proj, tables["win"], B, S, tq=tables["win_tq"])

    merged = merge_branches(proj, small, fox_o, dil_o, dil_lse, cmp_o, slc_o, win_o,
                            wa.astype(BF16), wb.astype(BF16), wc.astype(BF16))
    return matmul_residual(merged, w_out.astype(BF16), x)


def bias_tables(rel_bias, S):
    slc_tq, slc_tk = min(256, S), min(512, S)
    win_tq = min(256, S)
    rel_c = rel_bias[:, DIL_HEADS:]
    return {
        "dil": dilated_bias_tables(rel_bias[:, :DIL_HEADS]),
        "slc": slc_bias_table(rel_c, S, slc_tq, slc_tk), "slc_tq": slc_tq, "slc_tk": slc_tk,
        "win": win_bias_table(rel_c, win_tq, min(WIN // win_tq + 1, 3)), "win_tq": win_tq,
    }


def kernel(x, rel_bias, norm_mix_g, norm_ffn_g, norm_final_g, w_in, fox_forget_bias, cmp_pe_k, cmp_w1_k, cmp_w2_k, cmp_pe_v, cmp_w1_v, cmp_w2_v, w_branch_a, w_branch_b, w_branch_c, w_out, ffn_w_gate, ffn_w_up, ffn_w_down, moe_router, moe_w_gate, moe_w_up, moe_w_down):
    B, S, D = x.shape
    if D != D_MODEL:
        raise ValueError(f"column-block layout is built for d_model={D_MODEL}, got {D}")
    T = B * S
    depth = w_in.shape[0]
    tables = bias_tables(rel_bias, S)
    g_final = norm_final_g.reshape(1, D)
    xt = x.reshape(T, D)
    for l in range(depth):
        xt = mixing_block(
            xt, B, S, norm_mix_g[l], w_in[l], fox_forget_bias[l],
            jnp.stack([cmp_pe_k[l], cmp_pe_v[l]]), jnp.stack([cmp_w1_k[l], cmp_w1_v[l]]),
            jnp.stack([cmp_w2_k[l], cmp_w2_v[l]]),
            w_branch_a[l], w_branch_b[l], w_branch_c[l], w_out[l], tables)
        g = norm_ffn_g[l].reshape(1, D)
        last = l == depth - 1
        j = l // 2
        if l % 2 == 0:
            xt = ffn(xt, g, ffn_w_gate[j], ffn_w_up[j], ffn_w_down[j], g_final, final_norm=last)
        else:
            xt = moe_block(xt, g, moe_router[j], moe_w_gate[j], moe_w_up[j], moe_w_down[j], g_final,
                           final_norm=last)
    return xt.reshape(B, S, D)
```

```python
import functools
import math

import jax
import jax.numpy as jnp
import numpy as np
from jax import lax
from jax.experimental import pallas as pl
from jax.experimental.pallas import tpu as pltpu

F32 = jnp.float32
BF16 = jnp.bfloat16
HIGHEST = lax.Precision.HIGHEST
NEG_INF = float("-inf")

LANES = 128
HEAD_DIM = 128
ATTN_SCALE = HEAD_DIM ** -0.5
RMS_EPS = 1e-6
VMEM_LIMIT_BYTES = 58 * 1024 * 1024

D_MODEL = 2048
FOX_HEADS = 8
DIL_GROUPS = ((128, 1), (512, 4), (2048, 16))
DIL_HEADS_PER_GROUP = 4
DIL_HEADS = DIL_HEADS_PER_GROUP * len(DIL_GROUPS)
DIL_TAPS = 128
NSA_Q_HEADS = 8
NSA_KV_HEADS = 2
NSA_GQA = NSA_Q_HEADS // NSA_KV_HEADS
NSA_BRANCHES = 3
CMP_BLOCK = 32
CMP_STRIDE = 16
CMP_HIDDEN = 256
SLC_BLOCK = 64
SLC_COUNT = 16
WIN = 512
REL_BUCKETS = 32
REL_MAX_EXACT = 16
REL_MAX_DIST = 2048
N_EXPERTS = 8
TOP_K = 2
N_BRANCHES = 3

CB_MERGE = 0
CB_FOX = CB_MERGE + N_BRANCHES * (D_MODEL // LANES)
CB_DIL = CB_FOX + 3 * FOX_HEADS
CB_NSAQ = CB_DIL + 3 * DIL_HEADS_PER_GROUP
CB_NSAKV = CB_NSAQ + NSA_Q_HEADS
CB_END = CB_NSAKV + NSA_BRANCHES * 2 * NSA_KV_HEADS
SMALL_FOX_F = 0
SMALL_NSA_G = FOX_HEADS
ROUTE_CHOICE = 0
ROUTE_WEIGHT = N_EXPERTS
MOE_ZERO_ROWS = 64


def _cparams(semantics):
    return pltpu.CompilerParams(dimension_semantics=semantics,
                                vmem_limit_bytes=VMEM_LIMIT_BYTES)


def _t5_bucket_np(dist):
    dist = np.maximum(dist, 0)
    d = np.maximum(dist, 1).astype(np.float32)
    log_ratio = np.log(d / np.float32(REL_MAX_EXACT)) / np.float32(math.log(REL_MAX_DIST / REL_MAX_EXACT))
    large = REL_MAX_EXACT + (log_ratio * np.float32(REL_BUCKETS - REL_MAX_EXACT)).astype(np.int32)
    large = np.minimum(large, REL_BUCKETS - 1)
    return np.where(dist < REL_MAX_EXACT, dist, large).astype(np.int32)


def _toeplitz_kernel(v_ref, o_ref, *, rows, cols):
    x = jnp.broadcast_to(v_ref[...], (rows, v_ref.shape[-1]))
    o_ref[...] = pltpu.roll(x, 0, 1, stride=1, stride_axis=0)[:, :cols]


def _toeplitz_blocks(fn, offsets, rows, cols):
    lx = -(-(rows + cols) // LANES) * LANES
    m = np.arange(lx)
    rel = np.where(m < cols, -m, lx - m)
    v = jnp.stack([fn(c + rel) for c in offsets]).astype(F32)
    n_off, H = v.shape[:2]
    return pl.pallas_call(
        functools.partial(_toeplitz_kernel, rows=rows, cols=cols),
        grid=(n_off, H),
        in_specs=[pl.BlockSpec((None, None, 1, lx), lambda o, h: (o, h, 0, 0))],
        out_specs=pl.BlockSpec((None, None, rows, cols), lambda o, h: (o, h, 0, 0)),
        out_shape=jax.ShapeDtypeStruct((n_off, H, rows, cols), F32),
        compiler_params=_cparams(("parallel", "parallel")),
        name="toeplitz_table",
    )(v.reshape(n_off, H, 1, lx))


def _rms_rows(x, g):
    inv = lax.rsqrt(jnp.mean(x * x, axis=-1, keepdims=True) + RMS_EPS)
    return (x * inv) * g


def _rms_proj_kernel(x_ref, g_ref, w_ref, o_ref, h_ref, *, ncb):
    @pl.when(pl.program_id(1) == 0)
    def _():
        h_ref[...] = _rms_rows(x_ref[...], g_ref[...]).astype(BF16)

    res = jnp.dot(h_ref[...], w_ref[...], preferred_element_type=F32)
    for c in range(ncb):
        o_ref[c] = res[:, c * LANES:(c + 1) * LANES].astype(o_ref.dtype)


def rms_proj(x, g, w, out_dtype, *, tm, tn):
    T, D = x.shape
    N = w.shape[1]
    tm, tn = min(tm, T), min(tn, N)
    ncb = tn // LANES
    return pl.pallas_call(
        functools.partial(_rms_proj_kernel, ncb=ncb),
        grid=(T // tm, N // tn),
        in_specs=[pl.BlockSpec((tm, D), lambda i, j: (i, 0)),
                  pl.BlockSpec((1, D), lambda i, j: (0, 0)),
                  pl.BlockSpec((D, tn), lambda i, j: (0, j))],
        out_specs=[pl.BlockSpec((ncb, tm, LANES), lambda i, j: (j, i, 0)),
                   pl.BlockSpec((tm, D), lambda i, j: (i, 0))],
        out_shape=[jax.ShapeDtypeStruct((N // LANES, T, LANES), out_dtype),
                   jax.ShapeDtypeStruct((T, D), BF16)],
        compiler_params=_cparams(("parallel", "arbitrary")),
        name="rms_proj",
    )(x, g, w)


def _proj_kernel(h_ref, w_ref, o_ref, *, ncb):
    res = jnp.dot(h_ref[...], w_ref[...], preferred_element_type=F32)
    for c in range(ncb):
        o_ref[c] = res[:, c * LANES:(c + 1) * LANES].astype(o_ref.dtype)


def slab_proj(h, w, out_dtype, *, tm, tn):
    T, D = h.shape
    N = w.shape[1]
    tm, tn = min(tm, T), min(tn, N)
    ncb = tn // LANES
    return pl.pallas_call(
        functools.partial(_proj_kernel, ncb=ncb),
        grid=(T // tm, N // tn),
        in_specs=[pl.BlockSpec((tm, D), lambda i, j: (i, 0)),
                  pl.BlockSpec((D, tn), lambda i, j: (0, j))],
        out_specs=pl.BlockSpec((ncb, tm, LANES), lambda i, j: (j, i, 0)),
        out_shape=jax.ShapeDtypeStruct((N // LANES, T, LANES), out_dtype),
        compiler_params=_cparams(("parallel", "parallel")),
        name="slab_proj",
    )(h, w)


def _proj_view_kernel(h_ref, w_ref, o_ref, res_ref, *, ncb, dil):
    res = jnp.dot(h_ref[...], w_ref[...], preferred_element_type=F32)
    rows = res_ref.shape[1] // dil
    for c in range(ncb):
        res_ref[c] = res[:, c * LANES:(c + 1) * LANES]
        for r in range(dil):
            part = res_ref[c, pl.ds(r, rows, stride=dil), :]
            o_ref[c, :, r * LANES:(r + 1) * LANES] = part.astype(o_ref.dtype)


def proj_view(h, w, dil, *, tm, tn):
    T, D = h.shape
    N = w.shape[1]
    tm, tn = min(tm, T), min(tn, N)
    ncb = tn // LANES
    return pl.pallas_call(
        functools.partial(_proj_view_kernel, ncb=ncb, dil=dil),
        grid=(T // tm, N // tn),
        in_specs=[pl.BlockSpec((tm, D), lambda i, j: (i, 0)),
                  pl.BlockSpec((D, tn), lambda i, j: (0, j))],
        out_specs=pl.BlockSpec((ncb, tm // dil, dil * LANES), lambda i, j: (j, i, 0)),
        out_shape=jax.ShapeDtypeStruct((N // LANES, T // dil, dil * LANES), BF16),
        scratch_shapes=[pltpu.VMEM((ncb, tm, LANES), F32)],
        compiler_params=_cparams(("parallel", "parallel")),
        name="proj_view",
    )(h, w)


CUMSUM_BLOCK = 256


def _logf_cumsum_kernel(f_ref, b_ref, c_ref, *, nblk):
    row = lax.broadcasted_iota(jnp.int32, (CUMSUM_BLOCK, CUMSUM_BLOCK), 0)
    col = lax.broadcasted_iota(jnp.int32, (CUMSUM_BLOCK, CUMSUM_BLOCK), 1)
    tri = jnp.where(col <= row, 1.0, 0.0).astype(F32)

    def body(i, carry):
        sl = pl.ds(pl.multiple_of(i * CUMSUM_BLOCK, CUMSUM_BLOCK), CUMSUM_BLOCK)
        z = f_ref[sl, :] + b_ref[...]
        logf = jnp.minimum(z, 0.0) - jnp.log1p(jnp.exp(-jnp.abs(z)))
        cs = jnp.dot(tri, logf, preferred_element_type=F32, precision=HIGHEST) + carry
        c_ref[sl, :] = cs
        return cs[CUMSUM_BLOCK - 1:CUMSUM_BLOCK, :]

    lax.fori_loop(0, nblk, body, jnp.zeros((1, LANES), F32))


def logf_cumsum(small, bias_row, B, S):
    T = B * S
    return pl.pallas_call(
        functools.partial(_logf_cumsum_kernel, nblk=S // CUMSUM_BLOCK),
        grid=(B,),
        in_specs=[pl.BlockSpec((S, LANES), lambda b: (b, 0)),
                  pl.BlockSpec((1, LANES), lambda b: (0, 0))],
        out_specs=pl.BlockSpec((S, LANES), lambda b: (b, 0)),
        out_shape=jax.ShapeDtypeStruct((T, LANES), F32),
        compiler_params=_cparams(("parallel",)),
        name="logf_cumsum",
    )(small, bias_row)


FLASH_ROW_CHUNK = 128


LOG2E = math.log2(math.e)


def _lane_tile(x, n):
    return jnp.concatenate([x] * n, axis=1)


def _flash_update(s2, v, m_ref, l_ref, acc_ref):
    m_old = m_ref[...]
    m_new = jnp.maximum(m_old, jnp.max(s2, axis=-1, keepdims=True))
    m_safe = jnp.where(m_new == NEG_INF, 0.0, m_new)
    alpha = jnp.exp2(m_old - m_safe)
    p = jnp.exp2(s2 - _lane_tile(m_safe, s2.shape[1] // LANES))
    l_ref[...] = alpha * l_ref[...] + jnp.sum(p, axis=-1, keepdims=True)
    acc_ref[...] = alpha * acc_ref[...] + jnp.dot(p.astype(BF16), v, preferred_element_type=F32)
    m_ref[...] = m_new


def _flash_init(m_ref, l_ref, acc_ref):
    m_ref[...] = jnp.full(m_ref.shape, NEG_INF, F32)
    l_ref[...] = jnp.zeros(l_ref.shape, F32)
    acc_ref[...] = jnp.zeros(acc_ref.shape, F32)


def _flash_result(l_ref, acc_ref):
    l = l_ref[...]
    return acc_ref[...] / jnp.where(l > 0.0, l, 1.0)


def _qk(q, k):
    return lax.dot_general(q, k, (((1,), (1,)), ((), ())), preferred_element_type=F32)


def _fox_kernel(qi_ref, ki_ref, q_ref, k_ref, v_ref, cq_ref, ck_ref, o_ref, m_ref, l_ref, acc_ref, cqb_ref,
                *, tq, tk, rc, nb):
    p = pl.program_id(1)
    qi, ki = qi_ref[p], ki_ref[p]

    @pl.when(ki == 0)
    def _():
        _flash_init(m_ref, l_ref, acc_ref)
        lane = lax.broadcasted_iota(jnp.int32, cq_ref.shape, 2)
        mine = jnp.sum(jnp.where(lane == pl.program_id(0), cq_ref[...], 0.0), axis=-1, keepdims=True)
        cqb_ref[...] = jnp.broadcast_to(mine * LOG2E, cqb_ref.shape)

    def step(b, causal_mask):
        ck = ck_ref[b] * LOG2E
        chunks = [pl.ds(c * rc, rc) for c in range(tq // rc)]
        widths = [min(tk, (c + 1) * rc) if causal_mask and tq == tk else tk for c in range(tq // rc)]
        qk = [_qk(q_ref[b, rows, :], k_ref[b, 0:w, :]) for rows, w in zip(chunks, widths)]
        for c, (rows, w) in enumerate(zip(chunks, widths)):
            s = qk[c] * (ATTN_SCALE * LOG2E) + (_lane_tile(cqb_ref[b, rows, :], w // LANES) - ck[:, 0:w])
            if causal_mask:
                qpos = qi * tq + c * rc + lax.broadcasted_iota(jnp.int32, (rc, w), 0)
                kpos = ki * tk + lax.broadcasted_iota(jnp.int32, (rc, w), 1)
                s = jnp.where(kpos <= qpos, s, NEG_INF)
            _flash_update(s, v_ref[b, 0:w, :], m_ref.at[b, rows], l_ref.at[b, rows], acc_ref.at[b, rows])

    crosses_diagonal = (ki + 1) * tk - 1 > qi * tq

    @pl.when(crosses_diagonal)
    def _():
        _for_range(0, nb, lambda b: step(b, True))

    @pl.when(jnp.logical_not(crosses_diagonal))
    def _():
        _for_range(0, nb, lambda b: step(b, False))

    @pl.when(ki == ((qi + 1) * tq - 1) // tk)
    def _():
        o_ref[...] = _flash_result(l_ref, acc_ref).astype(o_ref.dtype)


def _causal_pairs(nq, tq, tk):
    qs, ks = [], []
    for qi in range(nq):
        for ki in range(((qi + 1) * tq - 1) // tk + 1):
            qs.append(qi)
            ks.append(ki)
    return jnp.asarray(qs, jnp.int32), jnp.asarray(ks, jnp.int32)


def fox_attention(proj, c_packed, c_row, B, S, *, tq=512, tk=512):
    tq, tk = min(tq, S), min(tk, S)
    H = FOX_HEADS
    T = B * S
    qis, kis = _causal_pairs(S // tq, tq, tk)
    proj4 = proj.reshape(proj.shape[0], B, S, LANES)
    grid_spec = pltpu.PrefetchScalarGridSpec(
        num_scalar_prefetch=2,
        grid=(H, int(qis.shape[0])),
        in_specs=[
            pl.BlockSpec((None, B, tq, LANES), lambda h, p, qi, ki: (CB_FOX + h, 0, qi[p], 0)),
            pl.BlockSpec((None, B, tk, LANES), lambda h, p, qi, ki: (CB_FOX + H + h, 0, ki[p], 0)),
            pl.BlockSpec((None, B, tk, LANES), lambda h, p, qi, ki: (CB_FOX + 2 * H + h, 0, ki[p], 0)),
            pl.BlockSpec((B, tq, LANES), lambda h, p, qi, ki: (0, qi[p], 0)),
            pl.BlockSpec((None, B, 1, tk), lambda h, p, qi, ki: (h, 0, 0, ki[p])),
        ],
        out_specs=pl.BlockSpec((None, B, tq, LANES), lambda h, p, qi, ki: (h, 0, qi[p], 0)),
        scratch_shapes=[pltpu.VMEM((B, tq, LANES), F32)] * 4,
    )
    out = pl.pallas_call(
        functools.partial(_fox_kernel, tq=tq, tk=tk, rc=min(FLASH_ROW_CHUNK, tq), nb=B),
        grid_spec=grid_spec,
        out_shape=jax.ShapeDtypeStruct((H, B, S, LANES), BF16),
        compiler_params=_cparams(("parallel", "arbitrary")),
        name="fox_attention",
    )(qis, kis, proj4, proj4, proj4, c_packed.reshape(B, S, LANES), c_row.reshape(H, B, 1, S))
    return out.reshape(H, T, LANES)


DIL_ROWS_PER_STEP = 512


def _dil_kernel(q_ref, kp_ref, kc_ref, vp_ref, vc_ref, bias_ref, o_ref, lse_ref, *, dil, tu, nsub, hp):
    ui = pl.program_id(2)
    kcol = lax.broadcasted_iota(jnp.int32, (tu, 2 * tu), 1)
    first_ok = jnp.logical_or(ui > 0, kcol >= tu)
    units = [(c, slice(r * LANES, (r + 1) * LANES)) for c in range(nsub) for r in range(dil)]

    def window(prev_ref, cur_ref, h, c, sl):
        if c == 0:
            return jnp.concatenate([prev_ref[h, :, sl], cur_ref[h, 0:tu, sl]], axis=0)
        return cur_ref[h, (c - 1) * tu:(c + 1) * tu, sl]

    def head(h, carry):
        bias = bias_ref[h]
        bias_first = jnp.where(first_ok, bias, NEG_INF)
        qk = [_qk(q_ref[h, c * tu:(c + 1) * tu, sl], window(kp_ref, kc_ref, h, c, sl)) for c, sl in units]
        for i, (c, sl) in enumerate(units):
            rows = slice(c * tu, (c + 1) * tu)
            s = qk[i] * ATTN_SCALE + (bias_first if c == 0 else bias)
            m = jnp.max(s, axis=-1, keepdims=True)
            e = jnp.exp(s - m)
            den = jnp.sum(e, axis=-1, keepdims=True)
            o = jnp.dot(e.astype(BF16), window(vp_ref, vc_ref, h, c, sl), preferred_element_type=F32) / den
            o_ref[h, rows, sl] = o.astype(o_ref.dtype)
            lse_ref[h, rows, sl] = jnp.broadcast_to(m + jnp.log(den), (tu, LANES))
        return carry

    lax.fori_loop(0, hp, head, 0)


def dilated_group(view, c0, bias_tbl, group, B, S):
    dil = DIL_GROUPS[group][1]
    tu = DIL_TAPS
    Hg = DIL_HEADS_PER_GROUP
    T = B * S
    rows = min(DIL_ROWS_PER_STEP, S // dil)
    nsub = rows // tu
    nstep = S // dil // rows
    hp = max(1, Hg // max(1, dil // 4))
    cq, ck, cv = c0 // hp, (c0 + Hg) // hp, (c0 + 2 * Hg) // hp
    cur_blk = (hp, rows, dil * LANES)
    prev_blk = (hp, tu, dil * LANES)
    prev = lambda c: (lambda b, h, u: (c + h, jnp.maximum((b * nstep + u) * nsub - 1, 0), 0))
    cur = lambda c: (lambda b, h, u: (c + h, b * nstep + u, 0))
    o, lse = pl.pallas_call(
        functools.partial(_dil_kernel, dil=dil, tu=tu, nsub=nsub, hp=hp),
        grid=(B, Hg // hp, nstep),
        in_specs=[pl.BlockSpec(cur_blk, cur(cq)),
                  pl.BlockSpec(prev_blk, prev(ck)), pl.BlockSpec(cur_blk, cur(ck)),
                  pl.BlockSpec(prev_blk, prev(cv)), pl.BlockSpec(cur_blk, cur(cv)),
                  pl.BlockSpec((hp, tu, 2 * tu), lambda b, h, u: (h, 0, 0))],
        out_specs=[pl.BlockSpec(cur_blk, cur(0)), pl.BlockSpec(cur_blk, cur(0))],
        out_shape=[jax.ShapeDtypeStruct((Hg, T // dil, dil * LANES), BF16),
                   jax.ShapeDtypeStruct((Hg, T // dil, dil * LANES), F32)],
        compiler_params=_cparams(("parallel", "parallel", "arbitrary")),
        name=f"dilated_group{group}",
    )(view, view, view, view, view, bias_tbl)
    return o, lse


def dilated_bias_tables(rel_bias_b):
    tu = DIL_TAPS
    tables = []
    for g, (_, dil) in enumerate(DIL_GROUPS):
        heads = rel_bias_b[:, g * DIL_HEADS_PER_GROUP:(g + 1) * DIL_HEADS_PER_GROUP]

        def fn(taps, heads=heads, dil=dil):
            valid = (taps >= 0) & (taps <= DIL_TAPS)
            vals = heads[_t5_bucket_np(np.where(valid, taps, 0) * dil)].T
            return jnp.where(valid[None, :], vals, NEG_INF)

        tables.append(_toeplitz_blocks(fn, [tu], tu, 2 * tu)[0])
    return tables


def _compress_kernel(x_ref, pelo_ref, pehi_ref, w1a_ref, w1b_ref, w2_ref, o_ref, *, nchunk):
    x = x_ref[...].astype(F32)
    u0 = jnp.dot((x + pelo_ref[...]).astype(BF16), w1a_ref[...], preferred_element_type=F32)
    u1 = jnp.dot((x + pehi_ref[...]).astype(BF16), w1b_ref[...], preferred_element_type=F32)
    pre = u0 + pltpu.roll(u1, nchunk - 1, 0)
    hid = jax.nn.gelu(pre)
    out = jnp.dot(hid.astype(BF16), w2_ref[...], preferred_element_type=F32)
    row = lax.broadcasted_iota(jnp.int32, out.shape, 0)
    o_ref[...] = jnp.where(row < nchunk - 1, out, 0.0).astype(o_ref.dtype)


def nsa_compress(proj, pe, w1, w2, B, S):
    nchunk = S // CMP_STRIDE
    half = CMP_STRIDE * HEAD_DIM
    Hkv = NSA_KV_HEADS
    x = proj[CB_NSAKV:CB_NSAKV + 2 * Hkv].reshape(2, Hkv, B, nchunk, half)
    pe_lo = pe[:, :CMP_STRIDE].reshape(2, 1, half)
    pe_hi = pe[:, CMP_STRIDE:].reshape(2, 1, half)
    return pl.pallas_call(
        functools.partial(_compress_kernel, nchunk=nchunk),
        grid=(2, B, Hkv),
        in_specs=[pl.BlockSpec((None, None, None, nchunk, half), lambda t, b, h: (t, h, b, 0, 0)),
                  pl.BlockSpec((None, 1, half), lambda t, b, h: (t, 0, 0)),
                  pl.BlockSpec((None, 1, half), lambda t, b, h: (t, 0, 0)),
                  pl.BlockSpec((None, half, CMP_HIDDEN), lambda t, b, h: (t, 0, 0)),
                  pl.BlockSpec((None, half, CMP_HIDDEN), lambda t, b, h: (t, 1, 0)),
                  pl.BlockSpec((None, CMP_HIDDEN, HEAD_DIM), lambda t, b, h: (t, 0, 0))],
        out_specs=pl.BlockSpec((None, None, None, nchunk, HEAD_DIM), lambda t, b, h: (t, b, h, 0, 0)),
        out_shape=jax.ShapeDtypeStruct((2, B, Hkv, nchunk, HEAD_DIM), BF16),
        compiler_params=_cparams(("parallel", "parallel", "parallel")),
        name="nsa_compress",
    )(x, pe_lo, pe_hi, w1, w1, w2)


def _masked_softmax(s, mask, axis):
    s = jnp.where(mask, s, NEG_INF)
    m = jnp.max(s, axis=axis, keepdims=True)
    m = jnp.where(m == NEG_INF, 0.0, m)
    e = jnp.exp(s - m)
    den = jnp.sum(e, axis=axis, keepdims=True)
    return e / jnp.where(den > 0.0, den, 1.0)


def _cmp_select_kernel(q_ref, kc_ref, vc_ref, mt_ref, o_ref, sel_ref, *, tq, nchunk, n_slc, n_sel):
    q0 = pl.program_id(2) * tq
    kc = kc_ref[...]
    vc = vc_ref[...]
    pos_c = q0 + lax.broadcasted_iota(jnp.int32, (nchunk, tq), 1)
    end_c = lax.broadcasted_iota(jnp.int32, (nchunk, tq), 0) * CMP_STRIDE + (CMP_BLOCK - 1)
    vis_c = end_c <= pos_c
    imp = jnp.zeros((nchunk, tq), F32)
    qk = [_qk(kc, q_ref[g]) for g in range(NSA_GQA)]
    for g in range(NSA_GQA):
        p = _masked_softmax(qk[g] * ATTN_SCALE, vis_c, 0)
        o = lax.dot_general(p.astype(BF16), vc, (((0,), (0,)), ((), ())), preferred_element_type=F32)
        o_ref[g] = o.astype(o_ref.dtype)
        imp = imp + p
    p_slc = jnp.dot(mt_ref[...], imp, preferred_element_type=F32, precision=HIGHEST)

    blk = lax.broadcasted_iota(jnp.int32, (n_slc, tq), 0)
    cur = (q0 + lax.broadcasted_iota(jnp.int32, (n_slc, tq), 1)) // SLC_BLOCK
    forced = (blk == 0) | (blk == cur) | (blk == cur - 1)
    allowed = blk <= cur
    score = jnp.where(forced, 1e30, jnp.where(allowed, p_slc, -1.0))
    chosen = jnp.zeros((n_slc, tq), F32)
    for _ in range(n_sel):
        top = jnp.max(score, axis=0, keepdims=True)
        first = jnp.min(jnp.where(score == top, blk, n_slc), axis=0, keepdims=True)
        hit = blk == first
        chosen = jnp.where(hit, 1.0, chosen)
        score = jnp.where(hit, -2.0, score)
    sel_ref[...] = jnp.where(allowed, chosen, 0.0)


def nsa_cmp_select(proj, kvc, B, S, *, tq=512):
    tq = min(tq, S)
    nq = S // tq
    T = B * S
    nchunk = S // CMP_STRIDE
    n_slc = S // SLC_BLOCK
    n_sel = min(SLC_COUNT, n_slc)
    ratio, n_inner = SLC_BLOCK // CMP_STRIDE, CMP_BLOCK // CMP_STRIDE
    mt = np.zeros((n_slc, nchunk), np.float32)
    for j in range(n_slc):
        for m in range(ratio):
            for n in range(n_inner):
                c = ratio * j + m - n
                if 0 <= c < nchunk - 1:
                    mt[j, c] += 1.0
    G = NSA_GQA
    kv_spec = lambda t: pl.BlockSpec((None, None, None, nchunk, HEAD_DIM), lambda b, h, i: (t, b, h, 0, 0))
    return pl.pallas_call(
        functools.partial(_cmp_select_kernel, tq=tq, nchunk=nchunk, n_slc=n_slc, n_sel=n_sel),
        grid=(B, NSA_KV_HEADS, nq),
        in_specs=[pl.BlockSpec((G, tq, LANES), lambda b, h, i: (CB_NSAQ // G + h, b * nq + i, 0)),
                  kv_spec(0), kv_spec(1),
                  pl.BlockSpec((n_slc, nchunk), lambda b, h, i: (0, 0))],
        out_specs=[pl.BlockSpec((G, tq, LANES), lambda b, h, i: (h, b * nq + i, 0)),
                   pl.BlockSpec((None, None, n_slc, tq), lambda b, h, i: (b, h, 0, i))],
        out_shape=[jax.ShapeDtypeStruct((NSA_Q_HEADS, T, LANES), BF16),
                   jax.ShapeDtypeStruct((B, NSA_KV_HEADS, n_slc, S), F32)],
        compiler_params=_cparams(("parallel", "parallel", "arbitrary")),
        name="nsa_cmp_select",
    )(proj, kvc, kvc, jnp.asarray(mt))


def _slc_kernel(qi_ref, ki_ref, q_ref, k_ref, v_ref, sel_ref, bias_ref, o_ref, m_ref, l_ref, acc_ref,
                *, tq, tk, n_slc, nb):
    p = pl.program_id(1)
    qi, ki = qi_ref[p], ki_ref[p]

    @pl.when(ki == 0)
    def _():
        _flash_init(m_ref, l_ref, acc_ref)

    blk_of_key = ki * (tk // SLC_BLOCK) + lax.broadcasted_iota(jnp.int32, (n_slc, tk), 1) // SLC_BLOCK
    expand = jnp.where(lax.broadcasted_iota(jnp.int32, (n_slc, tk), 0) == blk_of_key, 1.0, 0.0).astype(BF16)

    def step(b, causal_mask):
        picked = jnp.dot(sel_ref[b], expand, preferred_element_type=F32)
        keep = picked > 0.5
        if causal_mask:
            qpos = qi * tq + lax.broadcasted_iota(jnp.int32, (tq, tk), 0)
            kpos = ki * tk + lax.broadcasted_iota(jnp.int32, (tq, tk), 1)
            keep = jnp.where(kpos <= qpos, picked, 0.0) > 0.5
        k, v = k_ref[b], v_ref[b]
        qk = [_qk(q_ref[g, b], k) for g in range(NSA_GQA)]
        for g in range(NSA_GQA):
            s = qk[g] * (ATTN_SCALE * LOG2E) + bias_ref[g]
            s = jnp.where(keep, s, NEG_INF)
            _flash_update(s, v, m_ref.at[g, b], l_ref.at[g, b], acc_ref.at[g, b])

    crosses_diagonal = (ki + 1) * tk - 1 > qi * tq

    @pl.when(crosses_diagonal)
    def _():
        _for_range(0, nb, lambda b: step(b, True))

    @pl.when(jnp.logical_not(crosses_diagonal))
    def _():
        _for_range(0, nb, lambda b: step(b, False))

    @pl.when(ki == ((qi + 1) * tq - 1) // tk)
    def _():
        o_ref[...] = _flash_result(l_ref, acc_ref).astype(o_ref.dtype)


def slc_bias_table(rel_bias_c, S, tq, tk):
    buckets = _t5_bucket_np(np.arange(S + tk))
    not_last = np.nonzero(buckets != REL_BUCKETS - 1)[0]
    far_start = int(not_last[-1]) + 1 if not_last.size else 0
    n_delta = min(S // tq, -(-(far_start + tk - 1) // tq) + 1)

    def fn(d):
        return rel_bias_c[_t5_bucket_np(d)].T * LOG2E

    return _toeplitz_blocks(fn, [dl * tq for dl in range(n_delta)], tq, tk)


def nsa_selected(proj, sel, bias_tbl, B, S, *, tq=256, tk=512):
    tq, tk = min(tq, S), min(tk, S)
    T = B * S
    n_slc = S // SLC_BLOCK
    G = NSA_GQA
    ck = CB_NSAKV + (1 * 2 + 0) * NSA_KV_HEADS
    cv = CB_NSAKV + (1 * 2 + 1) * NSA_KV_HEADS
    qis, kis = _causal_pairs(S // tq, tq, tk)
    n_delta = bias_tbl.shape[0]
    proj4 = proj.reshape(proj.shape[0], B, S, LANES)
    grid_spec = pltpu.PrefetchScalarGridSpec(
        num_scalar_prefetch=2,
        grid=(NSA_KV_HEADS, int(qis.shape[0])),
        in_specs=[
            pl.BlockSpec((G, B, tq, LANES), lambda h, p, qi, ki: (CB_NSAQ // G + h, 0, qi[p], 0)),
            pl.BlockSpec((None, B, tk, LANES), lambda h, p, qi, ki: (ck + h, 0, ki[p], 0)),
            pl.BlockSpec((None, B, tk, LANES), lambda h, p, qi, ki: (cv + h, 0, ki[p], 0)),
            pl.BlockSpec((B, None, tq, n_slc), lambda h, p, qi, ki: (0, h, qi[p], 0)),
            pl.BlockSpec((None, G, tq, tk),
                         lambda h, p, qi, ki: (jnp.minimum(qi[p] - ki[p] * (tk // tq), n_delta - 1), h, 0, 0)),
        ],
        out_specs=pl.BlockSpec((G, B, tq, LANES), lambda h, p, qi, ki: (h, 0, qi[p], 0)),
        scratch_shapes=[pltpu.VMEM((G, B, tq, LANES), F32)] * 3,
    )
    out = pl.pallas_call(
        functools.partial(_slc_kernel, tq=tq, tk=tk, n_slc=n_slc, nb=B),
        grid_spec=grid_spec,
        out_shape=jax.ShapeDtypeStruct((NSA_Q_HEADS, B, S, LANES), BF16),
        compiler_params=_cparams(("parallel", "arbitrary")),
        name="nsa_selected",
    )(qis, kis, proj4, proj4, proj4, sel, bias_tbl)
    return out.reshape(NSA_Q_HEADS, T, LANES)


def _win_kernel(q_ref, k0_ref, k1_ref, k2_ref, v0_ref, v1_ref, v2_ref, bias_ref, o_ref, *, tq, nkb):
    qi = pl.program_id(2)
    k = jnp.concatenate([r[...] for r in (k0_ref, k1_ref, k2_ref)][-nkb:], axis=0)
    v = jnp.concatenate([r[...] for r in (v0_ref, v1_ref, v2_ref)][-nkb:], axis=0)
    qk = [_qk(q_ref[g], k) for g in range(NSA_GQA)]

    def heads(clip_start):
        for g in range(NSA_GQA):
            s = qk[g] * ATTN_SCALE + bias_ref[g]
            if clip_start:
                kpos = (qi - (nkb - 1)) * tq + lax.broadcasted_iota(jnp.int32, (tq, nkb * tq), 1)
                s = jnp.where(kpos >= 0, s, NEG_INF)
            e = jnp.exp(s - jnp.max(s, axis=-1, keepdims=True))
            p = e / jnp.sum(e, axis=-1, keepdims=True)
            o_ref[g] = jnp.dot(p.astype(BF16), v, preferred_element_type=F32).astype(o_ref.dtype)

    @pl.when(qi < nkb - 1)
    def _():
        heads(True)

    @pl.when(qi >= nkb - 1)
    def _():
        heads(False)


def win_bias_table(rel_bias_c, tq, nkb):
    def fn(dist):
        valid = (dist >= 0) & (dist < WIN)
        return jnp.where(valid[None, :], rel_bias_c[_t5_bucket_np(dist)].T, NEG_INF)

    return _toeplitz_blocks(fn, [(nkb - 1) * tq], tq, nkb * tq)[0]


def nsa_window(proj, bias_tbl, B, S, *, tq=256):
    tq = min(tq, S)
    nq = S // tq
    T = B * S
    G = NSA_GQA
    nkb = min(WIN // tq + 1, 3)
    ck = CB_NSAKV + (2 * 2 + 0) * NSA_KV_HEADS
    cv = CB_NSAKV + (2 * 2 + 1) * NSA_KV_HEADS
    kv = lambda c, back: pl.BlockSpec(
        (None, tq, LANES), lambda b, h, i: (c + h, b * nq + jnp.maximum(i - back, 0), 0))
    return pl.pallas_call(
        functools.partial(_win_kernel, tq=tq, nkb=nkb),
        grid=(B, NSA_KV_HEADS, nq),
        in_specs=[pl.BlockSpec((G, tq, LANES), lambda b, h, i: (CB_NSAQ // G + h, b * nq + i, 0)),
                  kv(ck, 2), kv(ck, 1), kv(ck, 0), kv(cv, 2), kv(cv, 1), kv(cv, 0),
                  pl.BlockSpec((G, tq, nkb * tq), lambda b, h, i: (h, 0, 0))],
        out_specs=pl.BlockSpec((G, tq, LANES), lambda b, h, i: (h, b * nq + i, 0)),
        out_shape=jax.ShapeDtypeStruct((NSA_Q_HEADS, T, LANES), BF16),
        compiler_params=_cparams(("parallel", "parallel", "arbitrary")),
        name="nsa_window",
    )(proj, proj, proj, proj, proj, proj, proj, bias_tbl)


def _sigmoid(x):
    return 1.0 / (1.0 + jnp.exp(-x))


def _merge_kernel(fox_ref, d0_ref, d1_ref, d2_ref, l0_ref, l1_ref, l2_ref, cmp_ref, slc_ref, win_ref, small_ref,
                  g0_ref, g1_ref, g2_ref, wa_ref, wb_ref, wc_ref, o_ref, nat_ref, *, ncb):
    tm = o_ref.shape[0]

    def natural(ref, h, dil, slot):
        if dil == 1:
            return ref[h].astype(F32)
        for r in range(dil):
            nat_ref[slot, pl.ds(r, tm // dil, stride=dil), :] = ref[h, :, r * LANES:(r + 1) * LANES].astype(F32)
        return nat_ref[slot]

    ya = jnp.concatenate([fox_ref[h] for h in range(FOX_HEADS)], axis=1)
    yb = []
    for h in range(DIL_HEADS_PER_GROUP):
        outs, lse = [], []
        for g, (o_g, l_g) in enumerate(((d0_ref, l0_ref), (d1_ref, l1_ref), (d2_ref, l2_ref))):
            dil = DIL_GROUPS[g][1]
            slot = (h * len(DIL_GROUPS) + g) * 2
            outs.append(natural(o_g, h, dil, slot))
            lse.append(natural(l_g, h, dil, slot + 1))
        top = jnp.maximum(jnp.maximum(lse[0], lse[1]), lse[2])
        w = [jnp.exp(x - top) for x in lse]
        tot = w[0] + w[1] + w[2]
        y = sum((w[g] / tot) * outs[g] for g in range(len(DIL_GROUPS)))
        yb.append(y.astype(BF16))
    gates = _sigmoid(small_ref[...])
    yc = []
    for h in range(NSA_Q_HEADS):
        y = jnp.zeros(cmp_ref.shape[1:], F32)
        for br, ref in enumerate((cmp_ref, slc_ref, win_ref)):
            col = SMALL_NSA_G + br * NSA_Q_HEADS + h
            y = y + gates[:, col:col + 1] * ref[h].astype(F32)
        yc.append(y.astype(BF16))

    def gate(ref):
        return _sigmoid(jnp.concatenate([ref[c] for c in range(ncb)], axis=1).astype(F32))

    merged = (gate(g0_ref) * jnp.dot(ya, wa_ref[...], preferred_element_type=F32)
              + gate(g1_ref) * jnp.dot(jnp.concatenate(yb, axis=1), wb_ref[...], preferred_element_type=F32)
              + gate(g2_ref) * jnp.dot(jnp.concatenate(yc, axis=1), wc_ref[...], preferred_element_type=F32))
    o_ref[...] = merged.astype(o_ref.dtype)


def merge_branches(proj, small, fox_o, dil_o, dil_lse, cmp_o, slc_o, win_o, wa, wb, wc, *, tm=256):
    T = proj.shape[1]
    D = wa.shape[1]
    tm = min(tm, T)
    ncb = D // LANES
    Hg = DIL_HEADS_PER_GROUP
    heads = lambda n: pl.BlockSpec((n, tm, LANES), lambda i: (0, i, 0))
    gate = lambda b: pl.BlockSpec((ncb, tm, LANES), lambda i: (CB_MERGE // ncb + b, i, 0))
    dil = [pl.BlockSpec((Hg, tm // d, d * LANES), lambda i: (0, i, 0)) for _, d in DIL_GROUPS]
    wspec = lambda k: pl.BlockSpec((k, D), lambda i: (0, 0))
    return pl.pallas_call(
        functools.partial(_merge_kernel, ncb=ncb),
        grid=(T // tm,),
        in_specs=[heads(FOX_HEADS), *dil, *dil, heads(NSA_Q_HEADS), heads(NSA_Q_HEADS), heads(NSA_Q_HEADS),
                  pl.BlockSpec((tm, LANES), lambda i: (i, 0)),
                  gate(0), gate(1), gate(2), wspec(wa.shape[0]), wspec(wb.shape[0]), wspec(wc.shape[0])],
        out_specs=pl.BlockSpec((tm, D), lambda i: (i, 0)),
        out_shape=jax.ShapeDtypeStruct((T, D), BF16),
        scratch_shapes=[pltpu.VMEM((2 * Hg * len(DIL_GROUPS), tm, LANES), F32)],
        compiler_params=_cparams(("parallel",)),
        name="merge_branches",
    )(fox_o, *dil_o, *dil_lse, cmp_o, slc_o, win_o, small, proj, proj, proj, wa, wb, wc)


def _matmul_residual_kernel(a_ref, w_ref, x_ref, o_ref):
    o_ref[...] = x_ref[...] + jnp.dot(a_ref[...], w_ref[...], preferred_element_type=F32)


def matmul_residual(a, w, x, *, tm=1024, tn=1024):
    T, K = a.shape
    N = w.shape[1]
    tm, tn = min(tm, T), min(tn, N)
    return pl.pallas_call(
        _matmul_residual_kernel,
        grid=(T // tm, N // tn),
        in_specs=[pl.BlockSpec((tm, K), lambda i, j: (i, 0)),
                  pl.BlockSpec((K, tn), lambda i, j: (0, j)),
                  pl.BlockSpec((tm, tn), lambda i, j: (i, j))],
        out_specs=pl.BlockSpec((tm, tn), lambda i, j: (i, j)),
        out_shape=jax.ShapeDtypeStruct((T, N), F32),
        compiler_params=_cparams(("parallel", "parallel")),
        name="matmul_residual",
    )(a, w, x)


def _silu(x):
    return x * _sigmoid(x)


def _swiglu_partial(h, wg_ref, wu_ref, wd_ref):
    act = _silu(jnp.dot(h, wg_ref[...].astype(BF16), preferred_element_type=F32)) * \
        jnp.dot(h, wu_ref[...].astype(BF16), preferred_element_type=F32)
    return jnp.dot(act.astype(BF16), wd_ref[...].astype(BF16), preferred_element_type=F32)


def _ffn_kernel(x_ref, g_ref, wg_ref, wu_ref, wd_ref, gf_ref, o_ref, h_ref, *, final_norm):
    f = pl.program_id(1)

    @pl.when(f == 0)
    def _():
        h_ref[...] = _rms_rows(x_ref[...], g_ref[...]).astype(BF16)
        o_ref[...] = x_ref[...]

    o_ref[...] += _swiglu_partial(h_ref[...], wg_ref, wu_ref, wd_ref)

    if final_norm:
        @pl.when(f == pl.num_programs(1) - 1)
        def _():
            o_ref[...] = _rms_rows(o_ref[...], gf_ref[...])


def ffn(x, g, wg, wu, wd, g_final, *, final_norm, tm=1024, tf=256):
    T, D = x.shape
    F = wg.shape[1]
    tm, tf = min(tm, T), min(tf, F)
    return pl.pallas_call(
        functools.partial(_ffn_kernel, final_norm=final_norm),
        grid=(T // tm, F // tf),
        in_specs=[pl.BlockSpec((tm, D), lambda i, f: (i, 0)),
                  pl.BlockSpec((1, D), lambda i, f: (0, 0)),
                  pl.BlockSpec((D, tf), lambda i, f: (0, f)),
                  pl.BlockSpec((D, tf), lambda i, f: (0, f)),
                  pl.BlockSpec((tf, D), lambda i, f: (f, 0)),
                  pl.BlockSpec((1, D), lambda i, f: (0, 0))],
        out_specs=pl.BlockSpec((tm, D), lambda i, f: (i, 0)),
        out_shape=jax.ShapeDtypeStruct((T, D), F32),
        scratch_shapes=[pltpu.VMEM((tm, D), BF16)],
        compiler_params=_cparams(("parallel", "arbitrary")),
        name="ffn",
    )(x, g, wg, wu, wd, g_final)


def _route_kernel(x_ref, g_ref, r_ref, o_ref):
    h = _rms_rows(x_ref[...], g_ref[...])
    logits = jnp.dot(h, r_ref[...], preferred_element_type=F32, precision=HIGHEST)
    lane = lax.broadcasted_iota(jnp.int32, logits.shape, 1)
    logits = jnp.where(lane < N_EXPERTS, logits, NEG_INF)
    v1 = jnp.max(logits, axis=1, keepdims=True)
    i1 = jnp.min(jnp.where(logits == v1, lane, LANES), axis=1, keepdims=True)
    rest = jnp.where(lane == i1, NEG_INF, logits)
    v2 = jnp.max(rest, axis=1, keepdims=True)
    i2 = jnp.min(jnp.where(rest == v2, lane, LANES), axis=1, keepdims=True)
    e2 = jnp.exp(v2 - v1)
    den = 1.0 + e2
    rec = jnp.where(lane == ROUTE_CHOICE + i1, 1.0, jnp.where(lane == ROUTE_CHOICE + i2, 2.0, 0.0))
    rec = jnp.where(lane == ROUTE_WEIGHT, 1.0 / den, jnp.where(lane == ROUTE_WEIGHT + 1, e2 / den, rec))
    o_ref[...] = rec


def moe_route(x, g, router_pad, *, tm=512):
    T, D = x.shape
    tm = min(tm, T)
    return pl.pallas_call(
        _route_kernel,
        grid=(T // tm,),
        in_specs=[pl.BlockSpec((tm, D), lambda i: (i, 0)),
                  pl.BlockSpec((1, D), lambda i: (0, 0)),
                  pl.BlockSpec((D, LANES), lambda i: (0, 0))],
        out_specs=pl.BlockSpec((tm, LANES), lambda i: (i, 0)),
        out_shape=jax.ShapeDtypeStruct((T, LANES), F32),
        compiler_params=_cparams(("parallel",)),
        name="moe_route",
    )(x, g, router_pad)


def _moe_plan(route, T, tm):
    E = N_EXPERTS
    n_tiles = (TOP_K * T + E * tm) // tm
    choice = route[:, ROUTE_CHOICE:ROUTE_CHOICE + E]
    seli = (choice > 0.5).astype(jnp.int32)
    counts = jnp.sum(seli, axis=0)
    padded = ((counts + tm - 1) // tm) * tm
    seg_end = jnp.cumsum(padded)
    seg_start = seg_end - padded
    slot = seg_start[None, :] + jnp.cumsum(seli, axis=0) - 1
    slot_of = jnp.stack([jnp.sum(jnp.where(choice == k + 1.0, slot, 0), axis=1) for k in range(TOP_K)])
    n_used = (seg_end[-1] // tm).astype(jnp.int32)
    tile_start = jnp.arange(n_tiles, dtype=jnp.int32) * tm
    tile_expert = jnp.sum((seg_end[None, :] <= tile_start[:, None]).astype(jnp.int32), axis=1)
    tile_expert = jnp.minimum(tile_expert, E - 1)
    last_expert = jnp.take(tile_expert, n_used - 1)
    tile_expert = jnp.where(jnp.arange(n_tiles) < n_used, tile_expert, last_expert)
    pad_lo = (seg_start + counts).astype(jnp.int32)
    return (tile_expert.astype(jnp.int32), n_used.reshape(1), slot_of.reshape(-1).astype(jnp.int32),
            pad_lo, seg_end.astype(jnp.int32))


def _for_range(lo, hi, fn, unroll=None):
    def body(r, carry):
        fn(r)
        return carry
    lax.fori_loop(lo, hi, body, 0, unroll=unroll)


def _moe_dispatch_kernel(slot_ref, lo_ref, hi_ref, nt_ref, x_ref, xs_hbm, zero_ref, sem, zsem,
                         *, tc, T, chunks_per_tile, n_chunks):
    i = pl.program_id(0)

    def token_row(k, r):
        s = slot_ref[k * T + i * tc + r]
        return pltpu.make_async_copy(x_ref.at[pl.ds(r, 1)], xs_hbm.at[pl.ds(s, 1)], sem.at[0])

    def zero_row(s):
        return pltpu.make_async_copy(zero_ref.at[pl.ds(0, 1)], xs_hbm.at[pl.ds(s, 1)], zsem.at[0])

    def zero_chunk(c):
        rows = pl.ds(pl.multiple_of(c * MOE_ZERO_ROWS, MOE_ZERO_ROWS), MOE_ZERO_ROWS)
        return pltpu.make_async_copy(zero_ref, xs_hbm.at[rows], zsem.at[0])

    @pl.when(i == 0)
    def _():
        zero_ref[...] = jnp.zeros(zero_ref.shape, F32)
        for start in (True, False):
            for e in range(N_EXPERTS):
                _for_range(lo_ref[e], hi_ref[e], lambda s: zero_row(s).start() if start else zero_row(s).wait())
            _for_range(nt_ref[0] * chunks_per_tile, n_chunks,
                       lambda c: zero_chunk(c).start() if start else zero_chunk(c).wait())

    for start in (True, False):
        for k in range(TOP_K):
            _for_range(0, tc, lambda r: token_row(k, r).start() if start else token_row(k, r).wait(), unroll=8)


def moe_dispatch(x, plan, *, tm, tc=512):
    _, n_used, slot_of, pad_lo, pad_hi = plan
    T, D = x.shape
    tc = min(tc, T)
    n_slots = TOP_K * T + N_EXPERTS * tm
    grid_spec = pltpu.PrefetchScalarGridSpec(
        num_scalar_prefetch=4,
        grid=(T // tc,),
        in_specs=[pl.BlockSpec((tc, D), lambda i, s, lo, hi, nt: (i, 0))],
        out_specs=pl.BlockSpec(memory_space=pl.ANY),
        scratch_shapes=[pltpu.VMEM((MOE_ZERO_ROWS, D), F32),
                        pltpu.SemaphoreType.DMA((1,)), pltpu.SemaphoreType.DMA((1,))],
    )
    return pl.pallas_call(
        functools.partial(_moe_dispatch_kernel, tc=tc, T=T, chunks_per_tile=tm // MOE_ZERO_ROWS,
                          n_chunks=n_slots // MOE_ZERO_ROWS),
        grid_spec=grid_spec,
        out_shape=jax.ShapeDtypeStruct((n_slots, D), F32),
        compiler_params=_cparams(("arbitrary",)),
        name="moe_dispatch",
    )(slot_of, pad_lo, pad_hi, n_used, x)


def _moe_ffn_kernel(te_ref, nt_ref, xs_ref, g_ref, wg_ref, wu_ref, wd_ref, y_ref, h_ref):
    i, f = pl.program_id(0), pl.program_id(1)
    used = i < nt_ref[0]

    @pl.when(used & (f == 0))
    def _():
        h_ref[...] = _rms_rows(xs_ref[...], g_ref[...]).astype(BF16)
        y_ref[...] = _swiglu_partial(h_ref[...], wg_ref, wu_ref, wd_ref)

    @pl.when(used & (f > 0))
    def _():
        y_ref[...] += _swiglu_partial(h_ref[...], wg_ref, wu_ref, wd_ref)

    @pl.when(jnp.logical_not(used) & (f == 0))
    def _():
        y_ref[...] = jnp.zeros(y_ref.shape, F32)


def moe_experts(xs, g, plan, wg, wu, wd, *, tm, tf=1024):
    tile_expert, n_used = plan[0], plan[1]
    n_slots, D = xs.shape
    F = wg.shape[2]
    tf = min(tf, F)
    nf = F // tf

    def fidx(i, f, nt):
        return jnp.where(i < nt[0], f, nf - 1)

    grid_spec = pltpu.PrefetchScalarGridSpec(
        num_scalar_prefetch=2,
        grid=(n_slots // tm, nf),
        in_specs=[
            pl.BlockSpec((tm, D), lambda i, f, te, nt: (jnp.minimum(i, nt[0] - 1), 0)),
            pl.BlockSpec((1, D), lambda i, f, te, nt: (0, 0)),
            pl.BlockSpec((None, D, tf), lambda i, f, te, nt: (te[i], 0, fidx(i, f, nt))),
            pl.BlockSpec((None, D, tf), lambda i, f, te, nt: (te[i], 0, fidx(i, f, nt))),
            pl.BlockSpec((None, tf, D), lambda i, f, te, nt: (te[i], fidx(i, f, nt), 0)),
        ],
        out_specs=pl.BlockSpec((tm, D), lambda i, f, te, nt: (i, 0)),
        scratch_shapes=[pltpu.VMEM((tm, D), BF16)],
    )
    return pl.pallas_call(
        _moe_ffn_kernel,
        grid_spec=grid_spec,
        out_shape=jax.ShapeDtypeStruct((n_slots, D), F32),
        compiler_params=_cparams(("parallel", "arbitrary")),
        name="moe_experts",
    )(tile_expert, n_used, xs, g, wg, wu, wd)


def _moe_combine_kernel(slot_ref, x_ref, y_hbm, route_ref, gf_ref, o_ref, ybuf, sem, *, tc, T, final_norm):
    i = pl.program_id(0)

    def slot_row(k, r):
        s = slot_ref[k * T + i * tc + r]
        return pltpu.make_async_copy(y_hbm.at[pl.ds(s, 1)], ybuf.at[k, pl.ds(r, 1)], sem.at[0])

    for start in (True, False):
        for k in range(TOP_K):
            _for_range(0, tc, lambda r: slot_row(k, r).start() if start else slot_row(k, r).wait(), unroll=8)

    w = route_ref[...]
    out = x_ref[...]
    for k in range(TOP_K):
        out = out + w[:, ROUTE_WEIGHT + k:ROUTE_WEIGHT + k + 1] * ybuf[k]
    if final_norm:
        out = _rms_rows(out, gf_ref[...])
    o_ref[...] = out


def moe_combine(x, y, route, plan, g_final, *, final_norm, tc=256):
    T, D = x.shape
    tc = min(tc, T)
    grid_spec = pltpu.PrefetchScalarGridSpec(
        num_scalar_prefetch=1,
        grid=(T // tc,),
        in_specs=[pl.BlockSpec((tc, D), lambda i, s: (i, 0)),
                  pl.BlockSpec(memory_space=pl.ANY),
                  pl.BlockSpec((tc, LANES), lambda i, s: (i, 0)),
                  pl.BlockSpec((1, D), lambda i, s: (0, 0))],
        out_specs=pl.BlockSpec((tc, D), lambda i, s: (i, 0)),
        scratch_shapes=[pltpu.VMEM((TOP_K, tc, D), F32), pltpu.SemaphoreType.DMA((1,))],
    )
    return pl.pallas_call(
        functools.partial(_moe_combine_kernel, tc=tc, T=T, final_norm=final_norm),
        grid_spec=grid_spec,
        out_shape=jax.ShapeDtypeStruct((T, D), F32),
        compiler_params=_cparams(("arbitrary",)),
        name="moe_combine",
    )(plan[2], x, y, route, g_final)


def moe_block(x, g, router, wg, wu, wd, g_final, *, final_norm, tm=512):
    T, D = x.shape
    tm = min(tm, T)
    router_pad = jnp.zeros((D, LANES), F32).at[:, :N_EXPERTS].set(router)
    route = moe_route(x, g, router_pad)
    plan = _moe_plan(route, T, tm)
    xs = moe_dispatch(x, plan, tm=tm)
    y = moe_experts(xs, g, plan, wg.astype(BF16), wu.astype(BF16), wd, tm=tm)
    return moe_combine(x, y, route, plan, g_final, final_norm=final_norm)


def _split_w_in(w):
    D = w.shape[0]
    n_fox = 3 * FOX_HEADS * HEAD_DIM
    n_dil = 3 * DIL_HEADS * HEAD_DIM
    n_nsa = (NSA_Q_HEADS + NSA_BRANCHES * 2 * NSA_KV_HEADS) * HEAD_DIM
    n_g = NSA_BRANCHES * NSA_Q_HEADS
    a0 = n_fox
    a1 = a0 + FOX_HEADS
    a2 = a1 + n_dil
    a3 = a2 + n_nsa
    a4 = a3 + n_g
    group_cols = DIL_HEADS_PER_GROUP * HEAD_DIM
    dil = w[:, a1:a2].reshape(D, 3, len(DIL_GROUPS), group_cols)
    dil_sets = [dil[:, :, gi].reshape(D, 3 * group_cols).astype(BF16) for gi in range(len(DIL_GROUPS))]
    main = jnp.concatenate([w[:, a4:].astype(BF16), w[:, :a0].astype(BF16), dil_sets[0],
                            w[:, a2:a3].astype(BF16)], axis=1)
    small = jnp.concatenate([w[:, a0:a1], w[:, a3:a4],
                             jnp.zeros((D, LANES - FOX_HEADS - n_g), w.dtype)], axis=1).astype(BF16)
    return main, dil_sets[1:], small


def mixing_block(x, B, S, norm_g, w_in, forget_bias, cmp_pe, cmp_w1, cmp_w2, wa, wb, wc, w_out, tables):
    T, D = x.shape
    w_main, w_dil, w_small = _split_w_in(w_in)
    g = norm_g.reshape(1, D)
    proj, h = rms_proj(x, g, w_main, BF16, tm=1024, tn=1024)
    small = slab_proj(h, w_small, F32, tm=1024, tn=LANES)[0]
    dil_views = [proj_view(h, w, DIL_GROUPS[gi + 1][1], tm=1024, tn=4 * LANES) for gi, w in enumerate(w_dil)]

    bias_row = jnp.zeros((1, LANES), F32).at[0, :FOX_HEADS].set(forget_bias)
    c = logf_cumsum(small, bias_row, B, S)
    c_row = c[:, :FOX_HEADS].T.reshape(FOX_HEADS, 1, T)
    fox_o = fox_attention(proj, c, c_row, B, S)

    dil = [dilated_group(proj, CB_DIL, tables["dil"][0], 0, B, S)]
    dil += [dilated_group(v, 0, tables["dil"][gi + 1], gi + 1, B, S) for gi, v in enumerate(dil_views)]
    dil_o = [d[0] for d in dil]
    dil_lse = [d[1] for d in dil]

    kvc = nsa_compress(proj, cmp_pe, cmp_w1.astype(BF16), cmp_w2.astype(BF16), B, S)
    cmp_o, sel_t = nsa_cmp_select(proj, kvc, B, S)
    sel = jnp.swapaxes(sel_t, 2, 3).astype(BF16)
    slc_o = nsa_selected(proj, sel, tables["slc"], B, S, tq=tables["slc_tq"], tk=tables["slc_tk"])
    win_o = nsa_window(proj, tables["win"], B, S, tq=tables["win_tq"])

    merged = merge_branches(proj, small, fox_o, dil_o, dil_lse, cmp_o, slc_o, win_o,
                            wa.astype(BF16), wb.astype(BF16), wc.astype(BF16))
    return matmul_residual(merged, w_out.astype(BF16), x)


def bias_tables(rel_bias, S):
    slc_tq, slc_tk = min(256, S), min(512, S)
    win_tq = min(256, S)
    rel_c = rel_bias[:, DIL_HEADS:]
    return {
        "dil": dilated_bias_tables(rel_bias[:, :DIL_HEADS]),
        "slc": slc_bias_table(rel_c, S, slc_tq, slc_tk), "slc_tq": slc_tq, "slc_tk": slc_tk,
        "win": win_bias_table(rel_c, win_tq, min(WIN // win_tq + 1, 3)), "win_tq": win_tq,
    }


def kernel(x, rel_bias, norm_mix_g, norm_ffn_g, norm_final_g, w_in, fox_forget_bias, cmp_pe_k, cmp_w1_k, cmp_w2_k, cmp_pe_v, cmp_w1_v, cmp_w2_v, w_branch_a, w_branch_b, w_branch_c, w_out, ffn_w_gate, ffn_w_up, ffn_w_down, moe_router, moe_w_gate, moe_w_up, moe_w_down):
    B, S, D = x.shape
    if D != D_MODEL:
        raise ValueError(f"column-block layout is built for d_model={D_MODEL}, got {D}")
    T = B * S
    depth = w_in.shape[0]
    tables = bias_tables(rel_bias, S)
    g_final = norm_final_g.reshape(1, D)
    xt = x.reshape(T, D)
    for l in range(depth):
        xt = mixing_block(
            xt, B, S, norm_mix_g[l], w_in[l], fox_forget_bias[l],
            jnp.stack([cmp_pe_k[l], cmp_pe_v[l]]), jnp.stack([cmp_w1_k[l], cmp_w1_v[l]]),
            jnp.stack([cmp_w2_k[l], cmp_w2_v[l]]),
            w_branch_a[l], w_branch_b[l], w_branch_c[l], w_out[l], tables)
        g = norm_ffn_g[l].reshape(1, D)
        last = l == depth - 1
        j = l // 2
        if l % 2 == 0:
            xt = ffn(xt, g, ffn_w_gate[j], ffn_w_up[j], ffn_w_down[j], g_final, final_norm=last)
        else:
            xt = moe_block(xt, g, moe_router[j], moe_w_gate[j], moe_w_up[j], moe_w_down[j], g_final,
                           final_norm=last)
    return xt.reshape(B, S, D)
```

```python
import functools
import math

import jax
import jax.numpy as jnp
import numpy as np
from jax import lax
from jax.experimental import pallas as pl
from jax.experimental.pallas import tpu as pltpu

F32 = jnp.float32
BF16 = jnp.bfloat16
HIGHEST = lax.Precision.HIGHEST
NEG_INF = float("-inf")

LANES = 128
HEAD_DIM = 128
ATTN_SCALE = HEAD_DIM ** -0.5
RMS_EPS = 1e-6
VMEM_LIMIT_BYTES = 58 * 1024 * 1024

D_MODEL = 2048
FOX_HEADS = 8
DIL_GROUPS = ((128, 1), (512, 4), (2048, 16))
DIL_HEADS_PER_GROUP = 4
DIL_HEADS = DIL_HEADS_PER_GROUP * len(DIL_GROUPS)
DIL_TAPS = 128
NSA_Q_HEADS = 8
NSA_KV_HEADS = 2
NSA_GQA = NSA_Q_HEADS // NSA_KV_HEADS
NSA_BRANCHES = 3
CMP_BLOCK = 32
CMP_STRIDE = 16
CMP_HIDDEN = 256
SLC_BLOCK = 64
SLC_COUNT = 16
WIN = 512
REL_BUCKETS = 32
REL_MAX_EXACT = 16
REL_MAX_DIST = 2048
N_EXPERTS = 8
TOP_K = 2
N_BRANCHES = 3

CB_MERGE = 0
CB_FOX = CB_MERGE + N_BRANCHES * (D_MODEL // LANES)
CB_DIL = CB_FOX + 3 * FOX_HEADS
CB_NSAQ = CB_DIL + 3 * DIL_HEADS_PER_GROUP
CB_NSAKV = CB_NSAQ + NSA_Q_HEADS
CB_END = CB_NSAKV + NSA_BRANCHES * 2 * NSA_KV_HEADS
SMALL_FOX_F = 0
SMALL_NSA_G = FOX_HEADS
ROUTE_CHOICE = 0
ROUTE_WEIGHT = N_EXPERTS
MOE_ZERO_ROWS = 64


def _cparams(semantics):
    return pltpu.CompilerParams(dimension_semantics=semantics,
                                vmem_limit_bytes=VMEM_LIMIT_BYTES)


def _t5_bucket_np(dist):
    dist = np.maximum(dist, 0)
    d = np.maximum(dist, 1).astype(np.float32)
    log_ratio = np.log(d / np.float32(REL_MAX_EXACT)) / np.float32(math.log(REL_MAX_DIST / REL_MAX_EXACT))
    large = REL_MAX_EXACT + (log_ratio * np.float32(REL_BUCKETS - REL_MAX_EXACT)).astype(np.int32)
    large = np.minimum(large, REL_BUCKETS - 1)
    return np.where(dist < REL_MAX_EXACT, dist, large).astype(np.int32)


def _toeplitz_kernel(v_ref, o_ref, *, rows, cols):
    x = jnp.broadcast_to(v_ref[...], (rows, v_ref.shape[-1]))
    o_ref[...] = pltpu.roll(x, 0, 1, stride=1, stride_axis=0)[:, :cols]


def _toeplitz_blocks(fn, offsets, rows, cols):
    lx = -(-(rows + cols) // LANES) * LANES
    m = np.arange(lx)
    rel = np.where(m < cols, -m, lx - m)
    v = jnp.stack([fn(c + rel) for c in offsets]).astype(F32)
    n_off, H = v.shape[:2]
    return pl.pallas_call(
        functools.partial(_toeplitz_kernel, rows=rows, cols=cols),
        grid=(n_off, H),
        in_specs=[pl.BlockSpec((None, None, 1, lx), lambda o, h: (o, h, 0, 0))],
        out_specs=pl.BlockSpec((None, None, rows, cols), lambda o, h: (o, h, 0, 0)),
        out_shape=jax.ShapeDtypeStruct((n_off, H, rows, cols), F32),
        compiler_params=_cparams(("parallel", "parallel")),
        name="toeplitz_table",
    )(v.reshape(n_off, H, 1, lx))


def _rms_rows(x, g):
    inv = lax.rsqrt(jnp.mean(x * x, axis=-1, keepdims=True) + RMS_EPS)
    return (x * inv) * g


def _rms_proj_kernel(x_ref, g_ref, w_ref, o_ref, h_ref, *, ncb):
    @pl.when(pl.program_id(1) == 0)
    def _():
        h_ref[...] = _rms_rows(x_ref[...], g_ref[...]).astype(BF16)

    res = jnp.dot(h_ref[...], w_ref[...], preferred_element_type=F32)
    for c in range(ncb):
        o_ref[c] = res[:, c * LANES:(c + 1) * LANES].astype(o_ref.dtype)


def rms_proj(x, g, w, out_dtype, *, tm, tn):
    T, D = x.shape
    N = w.shape[1]
    tm, tn = min(tm, T), min(tn, N)
    ncb = tn // LANES
    return pl.pallas_call(
        functools.partial(_rms_proj_kernel, ncb=ncb),
        grid=(T // tm, N // tn),
        in_specs=[pl.BlockSpec((tm, D), lambda i, j: (i, 0)),
                  pl.BlockSpec((1, D), lambda i, j: (0, 0)),
                  pl.BlockSpec((D, tn), lambda i, j: (0, j))],
        out_specs=[pl.BlockSpec((ncb, tm, LANES), lambda i, j: (j, i, 0)),
                   pl.BlockSpec((tm, D), lambda i, j: (i, 0))],
        out_shape=[jax.ShapeDtypeStruct((N // LANES, T, LANES), out_dtype),
                   jax.ShapeDtypeStruct((T, D), BF16)],
        compiler_params=_cparams(("parallel", "arbitrary")),
        name="rms_proj",
    )(x, g, w)


def _proj_kernel(h_ref, w_ref, o_ref, *, ncb):
    res = jnp.dot(h_ref[...], w_ref[...], preferred_element_type=F32)
    for c in range(ncb):
        o_ref[c] = res[:, c * LANES:(c + 1) * LANES].astype(o_ref.dtype)


def slab_proj(h, w, out_dtype, *, tm, tn):
    T, D = h.shape
    N = w.shape[1]
    tm, tn = min(tm, T), min(tn, N)
    ncb = tn // LANES
    return pl.pallas_call(
        functools.partial(_proj_kernel, ncb=ncb),
        grid=(T // tm, N // tn),
        in_specs=[pl.BlockSpec((tm, D), lambda i, j: (i, 0)),
                  pl.BlockSpec((D, tn), lambda i, j: (0, j))],
        out_specs=pl.BlockSpec((ncb, tm, LANES), lambda i, j: (j, i, 0)),
        out_shape=jax.ShapeDtypeStruct((N // LANES, T, LANES), out_dtype),
        compiler_params=_cparams(("parallel", "parallel")),
        name="slab_proj",
    )(h, w)


def _proj_view_kernel(h_ref, w_ref, o_ref, res_ref, *, ncb, dil):
    res = jnp.dot(h_ref[...], w_ref[...], preferred_element_type=F32)
    rows = res_ref.shape[1] // dil
    for c in range(ncb):
        res_ref[c] = res[:, c * LANES:(c + 1) * LANES]
        for r in range(dil):
            part = res_ref[c, pl.ds(r, rows, stride=dil), :]
            o_ref[c, :, r * LANES:(r + 1) * LANES] = part.astype(o_ref.dtype)


def proj_view(h, w, dil, *, tm, tn):
    T, D = h.shape
    N = w.shape[1]
    tm, tn = min(tm, T), min(tn, N)
    ncb = tn // LANES
    return pl.pallas_call(
        functools.partial(_proj_view_kernel, ncb=ncb, dil=dil),
        grid=(T // tm, N // tn),
        in_specs=[pl.BlockSpec((tm, D), lambda i, j: (i, 0)),
                  pl.BlockSpec((D, tn), lambda i, j: (0, j))],
        out_specs=pl.BlockSpec((ncb, tm // dil, dil * LANES), lambda i, j: (j, i, 0)),
        out_shape=jax.ShapeDtypeStruct((N // LANES, T // dil, dil * LANES), BF16),
        scratch_shapes=[pltpu.VMEM((ncb, tm, LANES), F32)],
        compiler_params=_cparams(("parallel", "parallel")),
        name="proj_view",
    )(h, w)


CUMSUM_BLOCK = 256


def _logf_cumsum_kernel(f_ref, b_ref, c_ref, *, nblk):
    row = lax.broadcasted_iota(jnp.int32, (CUMSUM_BLOCK, CUMSUM_BLOCK), 0)
    col = lax.broadcasted_iota(jnp.int32, (CUMSUM_BLOCK, CUMSUM_BLOCK), 1)
    tri = jnp.where(col <= row, 1.0, 0.0).astype(F32)

    def body(i, carry):
        sl = pl.ds(pl.multiple_of(i * CUMSUM_BLOCK, CUMSUM_BLOCK), CUMSUM_BLOCK)
        z = f_ref[sl, :] + b_ref[...]
        logf = jnp.minimum(z, 0.0) - jnp.log1p(jnp.exp(-jnp.abs(z)))
        cs = jnp.dot(tri, logf, preferred_element_type=F32, precision=HIGHEST) + carry
        c_ref[sl, :] = cs
        return cs[CUMSUM_BLOCK - 1:CUMSUM_BLOCK, :]

    lax.fori_loop(0, nblk, body, jnp.zeros((1, LANES), F32))


def logf_cumsum(small, bias_row, B, S):
    T = B * S
    return pl.pallas_call(
        functools.partial(_logf_cumsum_kernel, nblk=S // CUMSUM_BLOCK),
        grid=(B,),
        in_specs=[pl.BlockSpec((S, LANES), lambda b: (b, 0)),
                  pl.BlockSpec((1, LANES), lambda b: (0, 0))],
        out_specs=pl.BlockSpec((S, LANES), lambda b: (b, 0)),
        out_shape=jax.ShapeDtypeStruct((T, LANES), F32),
        compiler_params=_cparams(("parallel",)),
        name="logf_cumsum",
    )(small, bias_row)


FLASH_ROW_CHUNK = 128


LOG2E = math.log2(math.e)


def _lane_tile(x, n):
    return jnp.concatenate([x] * n, axis=1)


def _flash_update(s2, v, m_ref, l_ref, acc_ref):
    m_old = m_ref[...]
    m_new = jnp.maximum(m_old, jnp.max(s2, axis=-1, keepdims=True))
    m_safe = jnp.where(m_new == NEG_INF, 0.0, m_new)
    alpha = jnp.exp2(m_old - m_safe)
    p = jnp.exp2(s2 - _lane_tile(m_safe, s2.shape[1] // LANES))
    l_ref[...] = alpha * l_ref[...] + jnp.sum(p, axis=-1, keepdims=True)
    acc_ref[...] = alpha * acc_ref[...] + jnp.dot(p.astype(BF16), v, preferred_element_type=F32)
    m_ref[...] = m_new


def _flash_init(m_ref, l_ref, acc_ref):
    m_ref[...] = jnp.full(m_ref.shape, NEG_INF, F32)
    l_ref[...] = jnp.zeros(l_ref.shape, F32)
    acc_ref[...] = jnp.zeros(acc_ref.shape, F32)


def _flash_result(l_ref, acc_ref):
    l = l_ref[...]
    return acc_ref[...] / jnp.where(l > 0.0, l, 1.0)


def _qk(q, k):
    return lax.dot_general(q, k, (((1,), (1,)), ((), ())), preferred_element_type=F32)


def _fox_kernel(qi_ref, ki_ref, q_ref, k_ref, v_ref, cq_ref, ck_ref, o_ref, m_ref, l_ref, acc_ref, cqb_ref,
                *, tq, tk, rc, nb):
    p = pl.program_id(1)
    qi, ki = qi_ref[p], ki_ref[p]

    @pl.when(ki == 0)
    def _():
        _flash_init(m_ref, l_ref, acc_ref)
        lane = lax.broadcasted_iota(jnp.int32, cq_ref.shape, 2)
        mine = jnp.sum(jnp.where(lane == pl.program_id(0), cq_ref[...], 0.0), axis=-1, keepdims=True)
        cqb_ref[...] = jnp.broadcast_to(mine * LOG2E, cqb_ref.shape)

    def step(b, causal_mask):
        ck = ck_ref[b] * LOG2E
        chunks = [pl.ds(c * rc, rc) for c in range(tq // rc)]
        widths = [min(tk, (c + 1) * rc) if causal_mask and tq == tk else tk for c in range(tq // rc)]
        qk = [_qk(q_ref[b, rows, :], k_ref[b, 0:w, :]) for rows, w in zip(chunks, widths)]
        for c, (rows, w) in enumerate(zip(chunks, widths)):
            s = qk[c] * (ATTN_SCALE * LOG2E) + (_lane_tile(cqb_ref[b, rows, :], w // LANES) - ck[:, 0:w])
            if causal_mask:
                qpos = qi * tq + c * rc + lax.broadcasted_iota(jnp.int32, (rc, w), 0)
                kpos = ki * tk + lax.broadcasted_iota(jnp.int32, (rc, w), 1)
                s = jnp.where(kpos <= qpos, s, NEG_INF)
            _flash_update(s, v_ref[b, 0:w, :], m_ref.at[b, rows], l_ref.at[b, rows], acc_ref.at[b, rows])

    crosses_diagonal = (ki + 1) * tk - 1 > qi * tq

    @pl.when(crosses_diagonal)
    def _():
        _for_range(0, nb, lambda b: step(b, True))

    @pl.when(jnp.logical_not(crosses_diagonal))
    def _():
        _for_range(0, nb, lambda b: step(b, False))

    @pl.when(ki == ((qi + 1) * tq - 1) // tk)
    def _():
        o_ref[...] = _flash_result(l_ref, acc_ref).astype(o_ref.dtype)


def _causal_pairs(nq, tq, tk):
    qs, ks = [], []
    for qi in range(nq):
        for ki in range(((qi + 1) * tq - 1) // tk + 1):
            qs.append(qi)
            ks.append(ki)
    return jnp.asarray(qs, jnp.int32), jnp.asarray(ks, jnp.int32)


def fox_attention(proj, c_packed, c_row, B, S, *, tq=512, tk=512):
    tq, tk = min(tq, S), min(tk, S)
    H = FOX_HEADS
    T = B * S
    qis, kis = _causal_pairs(S // tq, tq, tk)
    proj4 = proj.reshape(proj.shape[0], B, S, LANES)
    grid_spec = pltpu.PrefetchScalarGridSpec(
        num_scalar_prefetch=2,
        grid=(H, int(qis.shape[0])),
        in_specs=[
            pl.BlockSpec((None, B, tq, LANES), lambda h, p, qi, ki: (CB_FOX + h, 0, qi[p], 0)),
            pl.BlockSpec((None, B, tk, LANES), lambda h, p, qi, ki: (CB_FOX + H + h, 0, ki[p], 0)),
            pl.BlockSpec((None, B, tk, LANES), lambda h, p, qi, ki: (CB_FOX + 2 * H + h, 0, ki[p], 0)),
            pl.BlockSpec((B, tq, LANES), lambda h, p, qi, ki: (0, qi[p], 0)),
            pl.BlockSpec((None, B, 1, tk), lambda h, p, qi, ki: (h, 0, 0, ki[p])),
        ],
        out_specs=pl.BlockSpec((None, B, tq, LANES), lambda h, p, qi, ki: (h, 0, qi[p], 0)),
        scratch_shapes=[pltpu.VMEM((B, tq, LANES), F32)] * 4,
    )
    out = pl.pallas_call(
        functools.partial(_fox_kernel, tq=tq, tk=tk, rc=min(FLASH_ROW_CHUNK, tq), nb=B),
        grid_spec=grid_spec,
        out_shape=jax.ShapeDtypeStruct((H, B, S, LANES), BF16),
        compiler_params=_cparams(("parallel", "arbitrary")),
        name="fox_attention",
    )(qis, kis, proj4, proj4, proj4, c_packed.reshape(B, S, LANES), c_row.reshape(H, B, 1, S))
    return out.reshape(H, T, LANES)


DIL_ROWS_PER_STEP = 512


def _dil_kernel(q_ref, kp_ref, kc_ref, vp_ref, vc_ref, bias_ref, o_ref, lse_ref, *, dil, tu, nsub, hp):
    ui = pl.program_id(2)
    kcol = lax.broadcasted_iota(jnp.int32, (tu, 2 * tu), 1)
    first_ok = jnp.logical_or(ui > 0, kcol >= tu)
    units = [(c, slice(r * LANES, (r + 1) * LANES)) for c in range(nsub) for r in range(dil)]

    def window(prev_ref, cur_ref, h, c, sl):
        if c == 0:
            return jnp.concatenate([prev_ref[h, :, sl], cur_ref[h, 0:tu, sl]], axis=0)
        return cur_ref[h, (c - 1) * tu:(c + 1) * tu, sl]

    def head(h, carry):
        bias = bias_ref[h]
        bias_first = jnp.where(first_ok, bias, NEG_INF)
        qk = [_qk(q_ref[h, c * tu:(c + 1) * tu, sl], window(kp_ref, kc_ref, h, c, sl)) for c, sl in units]
        for i, (c, sl) in enumerate(units):
            rows = slice(c * tu, (c + 1) * tu)
            s = qk[i] * ATTN_SCALE + (bias_first if c == 0 else bias)
            m = jnp.max(s, axis=-1, keepdims=True)
            e = jnp.exp(s - m)
            den = jnp.sum(e, axis=-1, keepdims=True)
            o = jnp.dot(e.astype(BF16), window(vp_ref, vc_ref, h, c, sl), preferred_element_type=F32) / den
            o_ref[h, rows, sl] = o.astype(o_ref.dtype)
            lse_ref[h, rows, sl] = jnp.broadcast_to(m + jnp.log(den), (tu, LANES))
        return carry

    lax.fori_loop(0, hp, head, 0)


def dilated_group(view, c0, bias_tbl, group, B, S):
    dil = DIL_GROUPS[group][1]
    tu = DIL_TAPS
    Hg = DIL_HEADS_PER_GROUP
    T = B * S
    rows = min(DIL_ROWS_PER_STEP, S // dil)
    nsub = rows // tu
    nstep = S // dil // rows
    hp = max(1, Hg // max(1, dil // 4))
    cq, ck, cv = c0 // hp, (c0 + Hg) // hp, (c0 + 2 * Hg) // hp
    cur_blk = (hp, rows, dil * LANES)
    prev_blk = (hp, tu, dil * LANES)
    prev = lambda c: (lambda b, h, u: (c + h, jnp.maximum((b * nstep + u) * nsub - 1, 0), 0))
    cur = lambda c: (lambda b, h, u: (c + h, b * nstep + u, 0))
    o, lse = pl.pallas_call(
        functools.partial(_dil_kernel, dil=dil, tu=tu, nsub=nsub, hp=hp),
        grid=(B, Hg // hp, nstep),
        in_specs=[pl.BlockSpec(cur_blk, cur(cq)),
                  pl.BlockSpec(prev_blk, prev(ck)), pl.BlockSpec(cur_blk, cur(ck)),
                  pl.BlockSpec(prev_blk, prev(cv)), pl.BlockSpec(cur_blk, cur(cv)),
                  pl.BlockSpec((hp, tu, 2 * tu), lambda b, h, u: (h, 0, 0))],
        out_specs=[pl.BlockSpec(cur_blk, cur(0)), pl.BlockSpec(cur_blk, cur(0))],
        out_shape=[jax.ShapeDtypeStruct((Hg, T // dil, dil * LANES), BF16),
                   jax.ShapeDtypeStruct((Hg, T // dil, dil * LANES), F32)],
        compiler_params=_cparams(("parallel", "parallel", "arbitrary")),
        name=f"dilated_group{group}",
    )(view, view, view, view, view, bias_tbl)
    return o, lse


def dilated_bias_tables(rel_bias_b):
    tu = DIL_TAPS
    tables = []
    for g, (_, dil) in enumerate(DIL_GROUPS):
        heads = rel_bias_b[:, g * DIL_HEADS_PER_GROUP:(g + 1) * DIL_HEADS_PER_GROUP]

        def fn(taps, heads=heads, dil=dil):
            valid = (taps >= 0) & (taps <= DIL_TAPS)
            vals = heads[_t5_bucket_np(np.where(valid, taps, 0) * dil)].T
            return jnp.where(valid[None, :], vals, NEG_INF)

        tables.append(_toeplitz_blocks(fn, [tu], tu, 2 * tu)[0])
    return tables


def _compress_kernel(x_ref, pelo_ref, pehi_ref, w1a_ref, w1b_ref, w2_ref, o_ref, *, nchunk):
    x = x_ref[...].astype(F32)
    u0 = jnp.dot((x + pelo_ref[...]).astype(BF16), w1a_ref[...], preferred_element_type=F32)
    u1 = jnp.dot((x + pehi_ref[...]).astype(BF16), w1b_ref[...], preferred_element_type=F32)
    pre = u0 + pltpu.roll(u1, nchunk - 1, 0)
    hid = jax.nn.gelu(pre)
    out = jnp.dot(hid.astype(BF16), w2_ref[...], preferred_element_type=F32)
    row = lax.broadcasted_iota(jnp.int32, out.shape, 0)
    o_ref[...] = jnp.where(row < nchunk - 1, out, 0.0).astype(o_ref.dtype)


def nsa_compress(proj, pe, w1, w2, B, S):
    nchunk = S // CMP_STRIDE
    half = CMP_STRIDE * HEAD_DIM
    Hkv = NSA_KV_HEADS
    x = proj[CB_NSAKV:CB_NSAKV + 2 * Hkv].reshape(2, Hkv, B, nchunk, half)
    pe_lo = pe[:, :CMP_STRIDE].reshape(2, 1, half)
    pe_hi = pe[:, CMP_STRIDE:].reshape(2, 1, half)
    return pl.pallas_call(
        functools.partial(_compress_kernel, nchunk=nchunk),
        grid=(2, B, Hkv),
        in_specs=[pl.BlockSpec((None, None, None, nchunk, half), lambda t, b, h: (t, h, b, 0, 0)),
                  pl.BlockSpec((None, 1, half), lambda t, b, h: (t, 0, 0)),
                  pl.BlockSpec((None, 1, half), lambda t, b, h: (t, 0, 0)),
                  pl.BlockSpec((None, half, CMP_HIDDEN), lambda t, b, h: (t, 0, 0)),
                  pl.BlockSpec((None, half, CMP_HIDDEN), lambda t, b, h: (t, 1, 0)),
                  pl.BlockSpec((None, CMP_HIDDEN, HEAD_DIM), lambda t, b, h: (t, 0, 0))],
        out_specs=pl.BlockSpec((None, None, None, nchunk, HEAD_DIM), lambda t, b, h: (t, b, h, 0, 0)),
        out_shape=jax.ShapeDtypeStruct((2, B, Hkv, nchunk, HEAD_DIM), BF16),
        compiler_params=_cparams(("parallel", "parallel", "parallel")),
        name="nsa_compress",
    )(x, pe_lo, pe_hi, w1, w1, w2)


def _masked_softmax(s, mask, axis):
    s = jnp.where(mask, s, NEG_INF)
    m = jnp.max(s, axis=axis, keepdims=True)
    m = jnp.where(m == NEG_INF, 0.0, m)
    e = jnp.exp(s - m)
    den = jnp.sum(e, axis=axis, keepdims=True)
    return e / jnp.where(den > 0.0, den, 1.0)


def _cmp_select_kernel(q_ref, kc_ref, vc_ref, mt_ref, o_ref, sel_ref, *, tq, nchunk, n_slc, n_sel):
    q0 = pl.program_id(2) * tq
    kc = kc_ref[...]
    vc = vc_ref[...]
    pos_c = q0 + lax.broadcasted_iota(jnp.int32, (nchunk, tq), 1)
    end_c = lax.broadcasted_iota(jnp.int32, (nchunk, tq), 0) * CMP_STRIDE + (CMP_BLOCK - 1)
    vis_c = end_c <= pos_c
    imp = jnp.zeros((nchunk, tq), F32)
    qk = [_qk(kc, q_ref[g]) for g in range(NSA_GQA)]
    for g in range(NSA_GQA):
        p = _masked_softmax(qk[g] * ATTN_SCALE, vis_c, 0)
        o = lax.dot_general(p.astype(BF16), vc, (((0,), (0,)), ((), ())), preferred_element_type=F32)
        o_ref[g] = o.astype(o_ref.dtype)
        imp = imp + p
    p_slc = jnp.dot(mt_ref[...], imp, preferred_element_type=F32, precision=HIGHEST)

    blk = lax.broadcasted_iota(jnp.int32, (n_slc, tq), 0)
    cur = (q0 + lax.broadcasted_iota(jnp.int32, (n_slc, tq), 1)) // SLC_BLOCK
    forced = (blk == 0) | (blk == cur) | (blk == cur - 1)
    allowed = blk <= cur
    score = jnp.where(forced, 1e30, jnp.where(allowed, p_slc, -1.0))
    chosen = jnp.zeros((n_slc, tq), F32)
    for _ in range(n_sel):
        top = jnp.max(score, axis=0, keepdims=True)
        first = jnp.min(jnp.where(score == top, blk, n_slc), axis=0, keepdims=True)
        hit = blk == first
        chosen = jnp.where(hit, 1.0, chosen)
        score = jnp.where(hit, -2.0, score)
    sel_ref[...] = jnp.where(allowed, chosen, 0.0)


def nsa_cmp_select(proj, kvc, B, S, *, tq=512):
    tq = min(tq, S)
    nq = S // tq
    T = B * S
    nchunk = S // CMP_STRIDE
    n_slc = S // SLC_BLOCK
    n_sel = min(SLC_COUNT, n_slc)
    ratio, n_inner = SLC_BLOCK // CMP_STRIDE, CMP_BLOCK // CMP_STRIDE
    mt = np.zeros((n_slc, nchunk), np.float32)
    for j in range(n_slc):
        for m in range(ratio):
            for n in range(n_inner):
                c = ratio * j + m - n
                if 0 <= c < nchunk - 1:
                    mt[j, c] += 1.0
    G = NSA_GQA
    kv_spec = lambda t: pl.BlockSpec((None, None, None, nchunk, HEAD_DIM), lambda b, h, i: (t, b, h, 0, 0))
    return pl.pallas_call(
        functools.partial(_cmp_select_kernel, tq=tq, nchunk=nchunk, n_slc=n_slc, n_sel=n_sel),
        grid=(B, NSA_KV_HEADS, nq),
        in_specs=[pl.BlockSpec((G, tq, LANES), lambda b, h, i: (CB_NSAQ // G + h, b * nq + i, 0)),
                  kv_spec(0), kv_spec(1),
                  pl.BlockSpec((n_slc, nchunk), lambda b, h, i: (0, 0))],
        out_specs=[pl.BlockSpec((G, tq, LANES), lambda b, h, i: (h, b * nq + i, 0)),
                   pl.BlockSpec((None, None, n_slc, tq), lambda b, h, i: (b, h, 0, i))],
        out_shape=[jax.ShapeDtypeStruct((NSA_Q_HEADS, T, LANES), BF16),
                   jax.ShapeDtypeStruct((B, NSA_KV_HEADS, n_slc, S), F32)],
        compiler_params=_cparams(("parallel", "parallel", "arbitrary")),
        name="nsa_cmp_select",
    )(proj, kvc, kvc, jnp.asarray(mt))


def _slc_kernel(qi_ref, ki_ref, q_ref, k_ref, v_ref, sel_ref, bias_ref, o_ref, m_ref, l_ref, acc_ref,
                *, tq, tk, n_slc, nb):
    p = pl.program_id(1)
    qi, ki = qi_ref[p], ki_ref[p]

    @pl.when(ki == 0)
    def _():
        _flash_init(m_ref, l_ref, acc_ref)

    blk_of_key = ki * (tk // SLC_BLOCK) + lax.broadcasted_iota(jnp.int32, (n_slc, tk), 1) // SLC_BLOCK
    expand = jnp.where(lax.broadcasted_iota(jnp.int32, (n_slc, tk), 0) == blk_of_key, 1.0, 0.0).astype(BF16)

    def step(b, causal_mask):
        picked = jnp.dot(sel_ref[b], expand, preferred_element_type=F32)
        keep = picked > 0.5
        if causal_mask:
            qpos = qi * tq + lax.broadcasted_iota(jnp.int32, (tq, tk), 0)
            kpos = ki * tk + lax.broadcasted_iota(jnp.int32, (tq, tk), 1)
            keep = jnp.where(kpos <= qpos, picked, 0.0) > 0.5
        k, v = k_ref[b], v_ref[b]
        qk = [_qk(q_ref[g, b], k) for g in range(NSA_GQA)]
        for g in range(NSA_GQA):
            s = qk[g] * (ATTN_SCALE * LOG2E) + bias_ref[g]
            s = jnp.where(keep, s, NEG_INF)
            _flash_update(s, v, m_ref.at[g, b], l_ref.at[g, b], acc_ref.at[g, b])

    crosses_diagonal = (ki + 1) * tk - 1 > qi * tq

    @pl.when(crosses_diagonal)
    def _():
        _for_range(0, nb, lambda b: step(b, True))

    @pl.when(jnp.logical_not(crosses_diagonal))
    def _():
        _for_range(0, nb, lambda b: step(b, False))

    @pl.when(ki == ((qi + 1) * tq - 1) // tk)
    def _():
        o_ref[...] = _flash_result(l_ref, acc_ref).astype(o_ref.dtype)


def slc_bias_table(rel_bias_c, S, tq, tk):
    buckets = _t5_bucket_np(np.arange(S + tk))
    not_last = np.nonzero(buckets != REL_BUCKETS - 1)[0]
    far_start = int(not_last[-1]) + 1 if not_last.size else 0
    n_delta = min(S // tq, -(-(far_start + tk - 1) // tq) + 1)

    def fn(d):
        return rel_bias_c[_t5_bucket_np(d)].T * LOG2E

    return _toeplitz_blocks(fn, [dl * tq for dl in range(n_delta)], tq, tk)


def nsa_selected(proj, sel, bias_tbl, B, S, *, tq=256, tk=512):
    tq, tk = min(tq, S), min(tk, S)
    T = B * S
    n_slc = S // SLC_BLOCK
    G = NSA_GQA
    ck = CB_NSAKV + (1 * 2 + 0) * NSA_KV_HEADS
    cv = CB_NSAKV + (1 * 2 + 1) * NSA_KV_HEADS
    qis, kis = _causal_pairs(S // tq, tq, tk)
    n_delta = bias_tbl.shape[0]
    proj4 = proj.reshape(proj.shape[0], B, S, LANES)
    grid_spec = pltpu.PrefetchScalarGridSpec(
        num_scalar_prefetch=2,
        grid=(NSA_KV_HEADS, int(qis.shape[0])),
        in_specs=[
            pl.BlockSpec((G, B, tq, LANES), lambda h, p, qi, ki: (CB_NSAQ // G + h, 0, qi[p], 0)),
            pl.BlockSpec((None, B, tk, LANES), lambda h, p, qi, ki: (ck + h, 0, ki[p], 0)),
            pl.BlockSpec((None, B, tk, LANES), lambda h, p, qi, ki: (cv + h, 0, ki[p], 0)),
            pl.BlockSpec((B, None, tq, n_slc), lambda h, p, qi, ki: (0, h, qi[p], 0)),
            pl.BlockSpec((None, G, tq, tk),
                         lambda h, p, qi, ki: (jnp.minimum(qi[p] - ki[p] * (tk // tq), n_delta - 1), h, 0, 0)),
        ],
        out_specs=pl.BlockSpec((G, B, tq, LANES), lambda h, p, qi, ki: (h, 0, qi[p], 0)),
        scratch_shapes=[pltpu.VMEM((G, B, tq, LANES), F32)] * 3,
    )
    out = pl.pallas_call(
        functools.partial(_slc_kernel, tq=tq, tk=tk, n_slc=n_slc, nb=B),
        grid_spec=grid_spec,
        out_shape=jax.ShapeDtypeStruct((NSA_Q_HEADS, B, S, LANES), BF16),
        compiler_params=_cparams(("parallel", "arbitrary")),
        name="nsa_selected",
    )(qis, kis, proj4, proj4, proj4, sel, bias_tbl)
    return out.reshape(NSA_Q_HEADS, T, LANES)


def _win_kernel(q_ref, k0_ref, k1_ref, k2_ref, v0_ref, v1_ref, v2_ref, bias_ref, o_ref, *, tq, nkb):
    qi = pl.program_id(2)
    k = jnp.concatenate([r[...] for r in (k0_ref, k1_ref, k2_ref)][-nkb:], axis=0)
    v = jnp.concatenate([r[...] for r in (v0_ref, v1_ref, v2_ref)][-nkb:], axis=0)
    kpos = (qi - (nkb - 1)) * tq + lax.broadcasted_iota(jnp.int32, (tq, nkb * tq), 1)
    qk = [_qk(q_ref[g], k) for g in range(NSA_GQA)]
    for g in range(NSA_GQA):
        s = qk[g] * ATTN_SCALE + bias_ref[g]
        p = _masked_softmax(s, kpos >= 0, -1)
        o_ref[g] = jnp.dot(p.astype(BF16), v, preferred_element_type=F32).astype(o_ref.dtype)


def win_bias_table(rel_bias_c, tq, nkb):
    def fn(dist):
        valid = (dist >= 0) & (dist < WIN)
        return jnp.where(valid[None, :], rel_bias_c[_t5_bucket_np(dist)].T, NEG_INF)

    return _toeplitz_blocks(fn, [(nkb - 1) * tq], tq, nkb * tq)[0]


def nsa_window(proj, bias_tbl, B, S, *, tq=256):
    tq = min(tq, S)
    nq = S // tq
    T = B * S
    G = NSA_GQA
    nkb = min(WIN // tq + 1, 3)
    ck = CB_NSAKV + (2 * 2 + 0) * NSA_KV_HEADS
    cv = CB_NSAKV + (2 * 2 + 1) * NSA_KV_HEADS
    kv = lambda c, back: pl.BlockSpec(
        (None, tq, LANES), lambda b, h, i: (c + h, b * nq + jnp.maximum(i - back, 0), 0))
    return pl.pallas_call(
        functools.partial(_win_kernel, tq=tq, nkb=nkb),
        grid=(B, NSA_KV_HEADS, nq),
        in_specs=[pl.BlockSpec((G, tq, LANES), lambda b, h, i: (CB_NSAQ // G + h, b * nq + i, 0)),
                  kv(ck, 2), kv(ck, 1), kv(ck, 0), kv(cv, 2), kv(cv, 1), kv(cv, 0),
                  pl.BlockSpec((G, tq, nkb * tq), lambda b, h, i: (h, 0, 0))],
        out_specs=pl.BlockSpec((G, tq, LANES), lambda b, h, i: (h, b * nq + i, 0)),
        out_shape=jax.ShapeDtypeStruct((NSA_Q_HEADS, T, LANES), BF16),
        compiler_params=_cparams(("parallel", "parallel", "arbitrary")),
        name="nsa_window",
    )(proj, proj, proj, proj, proj, proj, proj, bias_tbl)


def _sigmoid(x):
    return 1.0 / (1.0 + jnp.exp(-x))


def _merge_kernel(fox_ref, d0_ref, d1_ref, d2_ref, l0_ref, l1_ref, l2_ref, cmp_ref, slc_ref, win_ref, small_ref,
                  g0_ref, g1_ref, g2_ref, wa_ref, wb_ref, wc_ref, o_ref, nat_ref, *, ncb):
    tm = o_ref.shape[0]

    def natural(ref, h, dil, slot):
        if dil == 1:
            return ref[h].astype(F32)
        for r in range(dil):
            nat_ref[slot, pl.ds(r, tm // dil, stride=dil), :] = ref[h, :, r * LANES:(r + 1) * LANES].astype(F32)
        return nat_ref[slot]

    ya = jnp.concatenate([fox_ref[h] for h in range(FOX_HEADS)], axis=1)
    yb = []
    for h in range(DIL_HEADS_PER_GROUP):
        outs, lse = [], []
        for g, (o_g, l_g) in enumerate(((d0_ref, l0_ref), (d1_ref, l1_ref), (d2_ref, l2_ref))):
            dil = DIL_GROUPS[g][1]
            slot = (h * len(DIL_GROUPS) + g) * 2
            outs.append(natural(o_g, h, dil, slot))
            lse.append(natural(l_g, h, dil, slot + 1))
        top = jnp.maximum(jnp.maximum(lse[0], lse[1]), lse[2])
        w = [jnp.exp(x - top) for x in lse]
        tot = w[0] + w[1] + w[2]
        y = sum((w[g] / tot) * outs[g] for g in range(len(DIL_GROUPS)))
        yb.append(y.astype(BF16))
    gates = _sigmoid(small_ref[...])
    yc = []
    for h in range(NSA_Q_HEADS):
        y = jnp.zeros(cmp_ref.shape[1:], F32)
        for br, ref in enumerate((cmp_ref, slc_ref, win_ref)):
            col = SMALL_NSA_G + br * NSA_Q_HEADS + h
            y = y + gates[:, col:col + 1] * ref[h].astype(F32)
        yc.append(y.astype(BF16))

    def gate(ref):
        return _sigmoid(jnp.concatenate([ref[c] for c in range(ncb)], axis=1).astype(F32))

    merged = (gate(g0_ref) * jnp.dot(ya, wa_ref[...], preferred_element_type=F32)
              + gate(g1_ref) * jnp.dot(jnp.concatenate(yb, axis=1), wb_ref[...], preferred_element_type=F32)
              + gate(g2_ref) * jnp.dot(jnp.concatenate(yc, axis=1), wc_ref[...], preferred_element_type=F32))
    o_ref[...] = merged.astype(o_ref.dtype)


def merge_branches(proj, small, fox_o, dil_o, dil_lse, cmp_o, slc_o, win_o, wa, wb, wc, *, tm=256):
    T = proj.shape[1]
    D = wa.shape[1]
    tm = min(tm, T)
    ncb = D // LANES
    Hg = DIL_HEADS_PER_GROUP
    heads = lambda n: pl.BlockSpec((n, tm, LANES), lambda i: (0, i, 0))
    gate = lambda b: pl.BlockSpec((ncb, tm, LANES), lambda i: (CB_MERGE // ncb + b, i, 0))
    dil = [pl.BlockSpec((Hg, tm // d, d * LANES), lambda i: (0, i, 0)) for _, d in DIL_GROUPS]
    wspec = lambda k: pl.BlockSpec((k, D), lambda i: (0, 0))
    return pl.pallas_call(
        functools.partial(_merge_kernel, ncb=ncb),
        grid=(T // tm,),
        in_specs=[heads(FOX_HEADS), *dil, *dil, heads(NSA_Q_HEADS), heads(NSA_Q_HEADS), heads(NSA_Q_HEADS),
                  pl.BlockSpec((tm, LANES), lambda i: (i, 0)),
                  gate(0), gate(1), gate(2), wspec(wa.shape[0]), wspec(wb.shape[0]), wspec(wc.shape[0])],
        out_specs=pl.BlockSpec((tm, D), lambda i: (i, 0)),
        out_shape=jax.ShapeDtypeStruct((T, D), BF16),
        scratch_shapes=[pltpu.VMEM((2 * Hg * len(DIL_GROUPS), tm, LANES), F32)],
        compiler_params=_cparams(("parallel",)),
        name="merge_branches",
    )(fox_o, *dil_o, *dil_lse, cmp_o, slc_o, win_o, small, proj, proj, proj, wa, wb, wc)


def _matmul_residual_kernel(a_ref, w_ref, x_ref, o_ref):
    o_ref[...] = x_ref[...] + jnp.dot(a_ref[...], w_ref[...], preferred_element_type=F32)


def matmul_residual(a, w, x, *, tm=1024, tn=1024):
    T, K = a.shape
    N = w.shape[1]
    tm, tn = min(tm, T), min(tn, N)
    return pl.pallas_call(
        _matmul_residual_kernel,
        grid=(T // tm, N // tn),
        in_specs=[pl.BlockSpec((tm, K), lambda i, j: (i, 0)),
                  pl.BlockSpec((K, tn), lambda i, j: (0, j)),
                  pl.BlockSpec((tm, tn), lambda i, j: (i, j))],
        out_specs=pl.BlockSpec((tm, tn), lambda i, j: (i, j)),
        out_shape=jax.ShapeDtypeStruct((T, N), F32),
        compiler_params=_cparams(("parallel", "parallel")),
        name="matmul_residual",
    )(a, w, x)


def _silu(x):
    return x * _sigmoid(x)


def _swiglu_partial(h, wg_ref, wu_ref, wd_ref):
    act = _silu(jnp.dot(h, wg_ref[...].astype(BF16), preferred_element_type=F32)) * \
        jnp.dot(h, wu_ref[...].astype(BF16), preferred_element_type=F32)
    return jnp.dot(act.astype(BF16), wd_ref[...].astype(BF16), preferred_element_type=F32)


def _ffn_kernel(x_ref, g_ref, wg_ref, wu_ref, wd_ref, gf_ref, o_ref, h_ref, *, final_norm):
    f = pl.program_id(1)

    @pl.when(f == 0)
    def _():
        h_ref[...] = _rms_rows(x_ref[...], g_ref[...]).astype(BF16)
        o_ref[...] = x_ref[...]

    o_ref[...] += _swiglu_partial(h_ref[...], wg_ref, wu_ref, wd_ref)

    if final_norm:
        @pl.when(f == pl.num_programs(1) - 1)
        def _():
            o_ref[...] = _rms_rows(o_ref[...], gf_ref[...])


def ffn(x, g, wg, wu, wd, g_final, *, final_norm, tm=1024, tf=256):
    T, D = x.shape
    F = wg.shape[1]
    tm, tf = min(tm, T), min(tf, F)
    return pl.pallas_call(
        functools.partial(_ffn_kernel, final_norm=final_norm),
        grid=(T // tm, F // tf),
        in_specs=[pl.BlockSpec((tm, D), lambda i, f: (i, 0)),
                  pl.BlockSpec((1, D), lambda i, f: (0, 0)),
                  pl.BlockSpec((D, tf), lambda i, f: (0, f)),
                  pl.BlockSpec((D, tf), lambda i, f: (0, f)),
                  pl.BlockSpec((tf, D), lambda i, f: (f, 0)),
                  pl.BlockSpec((1, D), lambda i, f: (0, 0))],
        out_specs=pl.BlockSpec((tm, D), lambda i, f: (i, 0)),
        out_shape=jax.ShapeDtypeStruct((T, D), F32),
        scratch_shapes=[pltpu.VMEM((tm, D), BF16)],
        compiler_params=_cparams(("parallel", "arbitrary")),
        name="ffn",
    )(x, g, wg, wu, wd, g_final)


def _route_kernel(x_ref, g_ref, r_ref, o_ref):
    h = _rms_rows(x_ref[...], g_ref[...])
    logits = jnp.dot(h, r_ref[...], preferred_element_type=F32, precision=HIGHEST)
    lane = lax.broadcasted_iota(jnp.int32, logits.shape, 1)
    logits = jnp.where(lane < N_EXPERTS, logits, NEG_INF)
    v1 = jnp.max(logits, axis=1, keepdims=True)
    i1 = jnp.min(jnp.where(logits == v1, lane, LANES), axis=1, keepdims=True)
    rest = jnp.where(lane == i1, NEG_INF, logits)
    v2 = jnp.max(rest, axis=1, keepdims=True)
    i2 = jnp.min(jnp.where(rest == v2, lane, LANES), axis=1, keepdims=True)
    e2 = jnp.exp(v2 - v1)
    den = 1.0 + e2
    rec = jnp.where(lane == ROUTE_CHOICE + i1, 1.0, jnp.where(lane == ROUTE_CHOICE + i2, 2.0, 0.0))
    rec = jnp.where(lane == ROUTE_WEIGHT, 1.0 / den, jnp.where(lane == ROUTE_WEIGHT + 1, e2 / den, rec))
    o_ref[...] = rec


def moe_route(x, g, router_pad, *, tm=512):
    T, D = x.shape
    tm = min(tm, T)
    return pl.pallas_call(
        _route_kernel,
        grid=(T // tm,),
        in_specs=[pl.BlockSpec((tm, D), lambda i: (i, 0)),
                  pl.BlockSpec((1, D), lambda i: (0, 0)),
                  pl.BlockSpec((D, LANES), lambda i: (0, 0))],
        out_specs=pl.BlockSpec((tm, LANES), lambda i: (i, 0)),
        out_shape=jax.ShapeDtypeStruct((T, LANES), F32),
        compiler_params=_cparams(("parallel",)),
        name="moe_route",
    )(x, g, router_pad)


def _moe_plan(route, T, tm):
    E = N_EXPERTS
    n_tiles = (TOP_K * T + E * tm) // tm
    choice = route[:, ROUTE_CHOICE:ROUTE_CHOICE + E]
    seli = (choice > 0.5).astype(jnp.int32)
    counts = jnp.sum(seli, axis=0)
    padded = ((counts + tm - 1) // tm) * tm
    seg_end = jnp.cumsum(padded)
    seg_start = seg_end - padded
    slot = seg_start[None, :] + jnp.cumsum(seli, axis=0) - 1
    slot_of = jnp.stack([jnp.sum(jnp.where(choice == k + 1.0, slot, 0), axis=1) for k in range(TOP_K)])
    n_used = (seg_end[-1] // tm).astype(jnp.int32)
    tile_start = jnp.arange(n_tiles, dtype=jnp.int32) * tm
    tile_expert = jnp.sum((seg_end[None, :] <= tile_start[:, None]).astype(jnp.int32), axis=1)
    tile_expert = jnp.minimum(tile_expert, E - 1)
    last_expert = jnp.take(tile_expert, n_used - 1)
    tile_expert = jnp.where(jnp.arange(n_tiles) < n_used, tile_expert, last_expert)
    pad_lo = (seg_start + counts).astype(jnp.int32)
    return (tile_expert.astype(jnp.int32), n_used.reshape(1), slot_of.reshape(-1).astype(jnp.int32),
            pad_lo, seg_end.astype(jnp.int32))


def _for_range(lo, hi, fn, unroll=None):
    def body(r, carry):
        fn(r)
        return carry
    lax.fori_loop(lo, hi, body, 0, unroll=unroll)


def _moe_dispatch_kernel(slot_ref, lo_ref, hi_ref, nt_ref, x_ref, xs_hbm, zero_ref, sem, zsem,
                         *, tc, T, chunks_per_tile, n_chunks):
    i = pl.program_id(0)

    def token_row(k, r):
        s = slot_ref[k * T + i * tc + r]
        return pltpu.make_async_copy(x_ref.at[pl.ds(r, 1)], xs_hbm.at[pl.ds(s, 1)], sem.at[0])

    def zero_row(s):
        return pltpu.make_async_copy(zero_ref.at[pl.ds(0, 1)], xs_hbm.at[pl.ds(s, 1)], zsem.at[0])

    def zero_chunk(c):
        rows = pl.ds(pl.multiple_of(c * MOE_ZERO_ROWS, MOE_ZERO_ROWS), MOE_ZERO_ROWS)
        return pltpu.make_async_copy(zero_ref, xs_hbm.at[rows], zsem.at[0])

    @pl.when(i == 0)
    def _():
        zero_ref[...] = jnp.zeros(zero_ref.shape, F32)
        for start in (True, False):
            for e in range(N_EXPERTS):
                _for_range(lo_ref[e], hi_ref[e], lambda s: zero_row(s).start() if start else zero_row(s).wait())
            _for_range(nt_ref[0] * chunks_per_tile, n_chunks,
                       lambda c: zero_chunk(c).start() if start else zero_chunk(c).wait())

    for k in range(TOP_K):
        _for_range(0, tc, lambda r: token_row(k, r).start(), unroll=8)
    for k in range(TOP_K):
        pltpu.make_async_copy(x_ref, xs_hbm.at[pl.ds(0, tc)], sem.at[0]).wait()


def moe_dispatch(x, plan, *, tm, tc=512):
    _, n_used, slot_of, pad_lo, pad_hi = plan
    T, D = x.shape
    tc = min(tc, T)
    n_slots = TOP_K * T + N_EXPERTS * tm
    grid_spec = pltpu.PrefetchScalarGridSpec(
        num_scalar_prefetch=4,
        grid=(T // tc,),
        in_specs=[pl.BlockSpec((tc, D), lambda i, s, lo, hi, nt: (i, 0))],
        out_specs=pl.BlockSpec(memory_space=pl.ANY),
        scratch_shapes=[pltpu.VMEM((MOE_ZERO_ROWS, D), F32),
                        pltpu.SemaphoreType.DMA((1,)), pltpu.SemaphoreType.DMA((1,))],
    )
    return pl.pallas_call(
        functools.partial(_moe_dispatch_kernel, tc=tc, T=T, chunks_per_tile=tm // MOE_ZERO_ROWS,
                          n_chunks=n_slots // MOE_ZERO_ROWS),
        grid_spec=grid_spec,
        out_shape=jax.ShapeDtypeStruct((n_slots, D), F32),
        compiler_params=_cparams(("arbitrary",)),
        name="moe_dispatch",
    )(slot_of, pad_lo, pad_hi, n_used, x)


def _moe_ffn_kernel(te_ref, nt_ref, xs_ref, g_ref, wg_ref, wu_ref, wd_ref, y_ref, h_ref):
    i, f = pl.program_id(0), pl.program_id(1)
    used = i < nt_ref[0]

    @pl.when(used & (f == 0))
    def _():
        h_ref[...] = _rms_rows(xs_ref[...], g_ref[...]).astype(BF16)
        y_ref[...] = _swiglu_partial(h_ref[...], wg_ref, wu_ref, wd_ref)

    @pl.when(used & (f > 0))
    def _():
        y_ref[...] += _swiglu_partial(h_ref[...], wg_ref, wu_ref, wd_ref)

    @pl.when(jnp.logical_not(used) & (f == 0))
    def _():
        y_ref[...] = jnp.zeros(y_ref.shape, F32)


def moe_experts(xs, g, plan, wg, wu, wd, *, tm, tf=1024):
    tile_expert, n_used = plan[0], plan[1]
    n_slots, D = xs.shape
    F = wg.shape[2]
    tf = min(tf, F)
    nf = F // tf

    def fidx(i, f, nt):
        return jnp.where(i < nt[0], f, nf - 1)

    grid_spec = pltpu.PrefetchScalarGridSpec(
        num_scalar_prefetch=2,
        grid=(n_slots // tm, nf),
        in_specs=[
            pl.BlockSpec((tm, D), lambda i, f, te, nt: (jnp.minimum(i, nt[0] - 1), 0)),
            pl.BlockSpec((1, D), lambda i, f, te, nt: (0, 0)),
            pl.BlockSpec((None, D, tf), lambda i, f, te, nt: (te[i], 0, fidx(i, f, nt))),
            pl.BlockSpec((None, D, tf), lambda i, f, te, nt: (te[i], 0, fidx(i, f, nt))),
            pl.BlockSpec((None, tf, D), lambda i, f, te, nt: (te[i], fidx(i, f, nt), 0)),
        ],
        out_specs=pl.BlockSpec((tm, D), lambda i, f, te, nt: (i, 0)),
        scratch_shapes=[pltpu.VMEM((tm, D), BF16)],
    )
    return pl.pallas_call(
        _moe_ffn_kernel,
        grid_spec=grid_spec,
        out_shape=jax.ShapeDtypeStruct((n_slots, D), F32),
        compiler_params=_cparams(("parallel", "arbitrary")),
        name="moe_experts",
    )(tile_expert, n_used, xs, g, wg, wu, wd)


def _moe_combine_kernel(slot_ref, x_ref, y_hbm, route_ref, gf_ref, o_ref, ybuf, sem, *, tc, T, final_norm):
    i = pl.program_id(0)

    def slot_row(k, r):
        s = slot_ref[k * T + i * tc + r]
        return pltpu.make_async_copy(y_hbm.at[pl.ds(s, 1)], ybuf.at[k, pl.ds(r, 1)], sem.at[0])

    for k in range(TOP_K):
        _for_range(0, tc, lambda r: slot_row(k, r).start(), unroll=8)
    for k in range(TOP_K):
        pltpu.make_async_copy(y_hbm.at[pl.ds(0, tc)], ybuf.at[k], sem.at[0]).wait()

    w = route_ref[...]
    out = x_ref[...]
    for k in range(TOP_K):
        out = out + w[:, ROUTE_WEIGHT + k:ROUTE_WEIGHT + k + 1] * ybuf[k]
    if final_norm:
        out = _rms_rows(out, gf_ref[...])
    o_ref[...] = out


def moe_combine(x, y, route, plan, g_final, *, final_norm, tc=256):
    T, D = x.shape
    tc = min(tc, T)
    grid_spec = pltpu.PrefetchScalarGridSpec(
        num_scalar_prefetch=1,
        grid=(T // tc,),
        in_specs=[pl.BlockSpec((tc, D), lambda i, s: (i, 0)),
                  pl.BlockSpec(memory_space=pl.ANY),
                  pl.BlockSpec((tc, LANES), lambda i, s: (i, 0)),
                  pl.BlockSpec((1, D), lambda i, s: (0, 0))],
        out_specs=pl.BlockSpec((tc, D), lambda i, s: (i, 0)),
        scratch_shapes=[pltpu.VMEM((TOP_K, tc, D), F32), pltpu.SemaphoreType.DMA((1,))],
    )
    return pl.pallas_call(
        functools.partial(_moe_combine_kernel, tc=tc, T=T, final_norm=final_norm),
        grid_spec=grid_spec,
        out_shape=jax.ShapeDtypeStruct((T, D), F32),
        compiler_params=_cparams(("arbitrary",)),
        name="moe_combine",
    )(plan[2], x, y, route, g_final)


def moe_block(x, g, router, wg, wu, wd, g_final, *, final_norm, tm=512):
    T, D = x.shape
    tm = min(tm, T)
    router_pad = jnp.zeros((D, LANES), F32).at[:, :N_EXPERTS].set(router)
    route = moe_route(x, g, router_pad)
    plan = _moe_plan(route, T, tm)
    xs = moe_dispatch(x, plan, tm=tm)
    y = moe_experts(xs, g, plan, wg.astype(BF16), wu.astype(BF16), wd, tm=tm)
    return moe_combine(x, y, route, plan, g_final, final_norm=final_norm)


def _split_w_in(w):
    D = w.shape[0]
    n_fox = 3 * FOX_HEADS * HEAD_DIM
    n_dil = 3 * DIL_HEADS * HEAD_DIM
    n_nsa = (NSA_Q_HEADS + NSA_BRANCHES * 2 * NSA_KV_HEADS) * HEAD_DIM
    n_g = NSA_BRANCHES * NSA_Q_HEADS
    a0 = n_fox
    a1 = a0 + FOX_HEADS
    a2 = a1 + n_dil
    a3 = a2 + n_nsa
    a4 = a3 + n_g
    group_cols = DIL_HEADS_PER_GROUP * HEAD_DIM
    dil = w[:, a1:a2].reshape(D, 3, len(DIL_GROUPS), group_cols)
    dil_sets = [dil[:, :, gi].reshape(D, 3 * group_cols).astype(BF16) for gi in range(len(DIL_GROUPS))]
    main = jnp.concatenate([w[:, a4:].astype(BF16), w[:, :a0].astype(BF16), dil_sets[0],
                            w[:, a2:a3].astype(BF16)], axis=1)
    small = jnp.concatenate([w[:, a0:a1], w[:, a3:a4],
                             jnp.zeros((D, LANES - FOX_HEADS - n_g), w.dtype)], axis=1).astype(BF16)
    return main, dil_sets[1:], small


def mixing_block(x, B, S, norm_g, w_in, forget_bias, cmp_pe, cmp_w1, cmp_w2, wa, wb, wc, w_out, tables):
    T, D = x.shape
    w_main, w_dil, w_small = _split_w_in(w_in)
    g = norm_g.reshape(1, D)
    proj, h = rms_proj(x, g, w_main, BF16, tm=1024, tn=1024)
    small = slab_proj(h, w_small, F32, tm=1024, tn=LANES)[0]
    dil_views = [proj_view(h, w, DIL_GROUPS[gi + 1][1], tm=1024, tn=w.shape[1]) for gi, w in enumerate(w_dil)]

    bias_row = jnp.zeros((1, LANES), F32).at[0, :FOX_HEADS].set(forget_bias)
    c = logf_cumsum(small, bias_row, B, S)
    c_row = c[:, :FOX_HEADS].T.reshape(FOX_HEADS, 1, T)
    fox_o = fox_attention(proj, c, c_row, B, S)

    dil = [dilated_group(proj, CB_DIL, tables["dil"][0], 0, B, S)]
    dil += [dilated_group(v, 0, tables["dil"][gi + 1], gi + 1, B, S) for gi, v in enumerate(dil_views)]
    dil_o = [d[0] for d in dil]
    dil_lse = [d[1] for d in dil]

    kvc = nsa_compress(proj, cmp_pe, cmp_w1.astype(BF16), cmp_w2.astype(BF16), B, S)
    cmp_o, sel_t = nsa_cmp_select(proj, kvc, B, S)
    sel = jnp.swapaxes(sel_t, 2, 3).astype(BF16)
    slc_o = nsa_selected(proj, sel, tables["slc"], B, S, tq=tables["slc_tq"], tk=tables["slc_tk"])
    win_o = nsa_window(proj, tables["win"], B, S, tq=tables["win_tq"])

    merged = merge_branches(proj, small, fox_o, dil_o, dil_lse, cmp_o, slc_o, win_o,
                            wa.astype(BF16), wb.astype(BF16), wc.astype(BF16))
    return matmul_residual(merged, w_out.astype(BF16), x)


def bias_tables(rel_bias, S):
    slc_tq, slc_tk = min(256, S), min(512, S)
    win_tq = min(256, S)
    rel_c = rel_bias[:, DIL_HEADS:]
    return {
        "dil": dilated_bias_tables(rel_bias[:, :DIL_HEADS]),
        "slc": slc_bias_table(rel_c, S, slc_tq, slc_tk), "slc_tq": slc_tq, "slc_tk": slc_tk,
        "win": win_bias_table(rel_c, win_tq, min(WIN // win_tq + 1, 3)), "win_tq": win_tq,
    }


def kernel(x, rel_bias, norm_mix_g, norm_ffn_g, norm_final_g, w_in, fox_forget_bias, cmp_pe_k, cmp_w1_k, cmp_w2_k, cmp_pe_v, cmp_w1_v, cmp_w2_v, w_branch_a, w_branch_b, w_branch_c, w_out, ffn_w_gate, ffn_w_up, ffn_w_down, moe_router, moe_w_gate, moe_w_up, moe_w_down):
    B, S, D = x.shape
    if D != D_MODEL:
        raise ValueError(f"column-block layout is built for d_model={D_MODEL}, got {D}")
    T = B * S
    depth = w_in.shape[0]
    tables = bias_tables(rel_bias, S)
    g_final = norm_final_g.reshape(1, D)
    xt = x.reshape(T, D)
    for l in range(depth):
        xt = mixing_block(
            xt, B, S, norm_mix_g[l], w_in[l], fox_forget_bias[l],
            jnp.stack([cmp_pe_k[l], cmp_pe_v[l]]), jnp.stack([cmp_w1_k[l], cmp_w1_v[l]]),
            jnp.stack([cmp_w2_k[l], cmp_w2_v[l]]),
            w_branch_a[l], w_branch_b[l], w_branch_c[l], w_out[l], tables)
        g = norm_ffn_g[l].reshape(1, D)
        last = l == depth - 1
        j = l // 2
        if l % 2 == 0:
            xt = ffn(xt, g, ffn_w_gate[j], ffn_w_up[j], ffn_w_down[j], g_final, final_norm=last)
        else:
            xt = moe_block(xt, g, moe_router[j], moe_w_gate[j], moe_w_up[j], moe_w_down[j], g_final,
                           final_norm=last)
    return xt.reshape(B, S, D)
```

```python
import functools
import math

import jax
import jax.numpy as jnp
import numpy as np
from jax import lax
from jax.experimental import pallas as pl
from jax.experimental.pallas import tpu as pltpu

F32 = jnp.float32
BF16 = jnp.bfloat16
HIGHEST = lax.Precision.HIGHEST
NEG_INF = float("-inf")

LANES = 128
HEAD_DIM = 128
ATTN_SCALE = HEAD_DIM ** -0.5
RMS_EPS = 1e-6
VMEM_LIMIT_BYTES = 58 * 1024 * 1024

D_MODEL = 2048
FOX_HEADS = 8
DIL_GROUPS = ((128, 1), (512, 4), (2048, 16))
DIL_HEADS_PER_GROUP = 4
DIL_HEADS = DIL_HEADS_PER_GROUP * len(DIL_GROUPS)
DIL_TAPS = 128
NSA_Q_HEADS = 8
NSA_KV_HEADS = 2
NSA_GQA = NSA_Q_HEADS // NSA_KV_HEADS
NSA_BRANCHES = 3
CMP_BLOCK = 32
CMP_STRIDE = 16
CMP_HIDDEN = 256
SLC_BLOCK = 64
SLC_COUNT = 16
WIN = 512
REL_BUCKETS = 32
REL_MAX_EXACT = 16
REL_MAX_DIST = 2048
N_EXPERTS = 8
TOP_K = 2
N_BRANCHES = 3

CB_MERGE = 0
CB_FOX = CB_MERGE + N_BRANCHES * (D_MODEL // LANES)
CB_DIL = CB_FOX + 3 * FOX_HEADS
CB_NSAQ = CB_DIL + 3 * DIL_HEADS_PER_GROUP
CB_NSAKV = CB_NSAQ + NSA_Q_HEADS
CB_END = CB_NSAKV + NSA_BRANCHES * 2 * NSA_KV_HEADS
SMALL_FOX_F = 0
SMALL_NSA_G = FOX_HEADS
ROUTE_CHOICE = 0
ROUTE_WEIGHT = N_EXPERTS
MOE_ZERO_ROWS = 64


def _cparams(semantics):
    return pltpu.CompilerParams(dimension_semantics=semantics,
                                vmem_limit_bytes=VMEM_LIMIT_BYTES)


def _t5_bucket_np(dist):
    dist = np.maximum(dist, 0)
    d = np.maximum(dist, 1).astype(np.float32)
    log_ratio = np.log(d / np.float32(REL_MAX_EXACT)) / np.float32(math.log(REL_MAX_DIST / REL_MAX_EXACT))
    large = REL_MAX_EXACT + (log_ratio * np.float32(REL_BUCKETS - REL_MAX_EXACT)).astype(np.int32)
    large = np.minimum(large, REL_BUCKETS - 1)
    return np.where(dist < REL_MAX_EXACT, dist, large).astype(np.int32)


def _toeplitz_kernel(v_ref, o_ref, *, rows, cols):
    x = jnp.broadcast_to(v_ref[...], (rows, v_ref.shape[-1]))
    o_ref[...] = pltpu.roll(x, 0, 1, stride=1, stride_axis=0)[:, :cols]


def _toeplitz_blocks(fn, offsets, rows, cols):
    lx = -(-(rows + cols) // LANES) * LANES
    m = np.arange(lx)
    rel = np.where(m < cols, -m, lx - m)
    v = jnp.stack([fn(c + rel) for c in offsets]).astype(F32)
    n_off, H = v.shape[:2]
    return pl.pallas_call(
        functools.partial(_toeplitz_kernel, rows=rows, cols=cols),
        grid=(n_off, H),
        in_specs=[pl.BlockSpec((None, None, 1, lx), lambda o, h: (o, h, 0, 0))],
        out_specs=pl.BlockSpec((None, None, rows, cols), lambda o, h: (o, h, 0, 0)),
        out_shape=jax.ShapeDtypeStruct((n_off, H, rows, cols), F32),
        compiler_params=_cparams(("parallel", "parallel")),
        name="toeplitz_table",
    )(v.reshape(n_off, H, 1, lx))


def _rms_rows(x, g):
    inv = lax.rsqrt(jnp.mean(x * x, axis=-1, keepdims=True) + RMS_EPS)
    return (x * inv) * g


def _rms_proj_kernel(x_ref, g_ref, w_ref, o_ref, h_ref, *, ncb):
    @pl.when(pl.program_id(1) == 0)
    def _():
        h_ref[...] = _rms_rows(x_ref[...], g_ref[...]).astype(BF16)

    res = jnp.dot(h_ref[...], w_ref[...], preferred_element_type=F32)
    for c in range(ncb):
        o_ref[c] = res[:, c * LANES:(c + 1) * LANES].astype(o_ref.dtype)


def rms_proj(x, g, w, out_dtype, *, tm, tn):
    T, D = x.shape
    N = w.shape[1]
    tm, tn = min(tm, T), min(tn, N)
    ncb = tn // LANES
    return pl.pallas_call(
        functools.partial(_rms_proj_kernel, ncb=ncb),
        grid=(T // tm, N // tn),
        in_specs=[pl.BlockSpec((tm, D), lambda i, j: (i, 0)),
                  pl.BlockSpec((1, D), lambda i, j: (0, 0)),
                  pl.BlockSpec((D, tn), lambda i, j: (0, j))],
        out_specs=[pl.BlockSpec((ncb, tm, LANES), lambda i, j: (j, i, 0)),
                   pl.BlockSpec((tm, D), lambda i, j: (i, 0))],
        out_shape=[jax.ShapeDtypeStruct((N // LANES, T, LANES), out_dtype),
                   jax.ShapeDtypeStruct((T, D), BF16)],
        compiler_params=_cparams(("parallel", "arbitrary")),
        name="rms_proj",
    )(x, g, w)


def _proj_kernel(h_ref, w_ref, o_ref, *, ncb):
    res = jnp.dot(h_ref[...], w_ref[...], preferred_element_type=F32)
    for c in range(ncb):
        o_ref[c] = res[:, c * LANES:(c + 1) * LANES].astype(o_ref.dtype)


def slab_proj(h, w, out_dtype, *, tm, tn):
    T, D = h.shape
    N = w.shape[1]
    tm, tn = min(tm, T), min(tn, N)
    ncb = tn // LANES
    return pl.pallas_call(
        functools.partial(_proj_kernel, ncb=ncb),
        grid=(T // tm, N // tn),
        in_specs=[pl.BlockSpec((tm, D), lambda i, j: (i, 0)),
                  pl.BlockSpec((D, tn), lambda i, j: (0, j))],
        out_specs=pl.BlockSpec((ncb, tm, LANES), lambda i, j: (j, i, 0)),
        out_shape=jax.ShapeDtypeStruct((N // LANES, T, LANES), out_dtype),
        compiler_params=_cparams(("parallel", "parallel")),
        name="slab_proj",
    )(h, w)


def _proj_view_kernel(h_ref, w_ref, o_ref, res_ref, *, ncb, dil):
    res = jnp.dot(h_ref[...], w_ref[...], preferred_element_type=F32)
    rows = res_ref.shape[1] // dil
    for c in range(ncb):
        res_ref[c] = res[:, c * LANES:(c + 1) * LANES]
        for r in range(dil):
            part = res_ref[c, pl.ds(r, rows, stride=dil), :]
            o_ref[c, :, r * LANES:(r + 1) * LANES] = part.astype(o_ref.dtype)


def proj_view(h, w, dil, *, tm, tn):
    T, D = h.shape
    N = w.shape[1]
    tm, tn = min(tm, T), min(tn, N)
    ncb = tn // LANES
    return pl.pallas_call(
        functools.partial(_proj_view_kernel, ncb=ncb, dil=dil),
        grid=(T // tm, N // tn),
        in_specs=[pl.BlockSpec((tm, D), lambda i, j: (i, 0)),
                  pl.BlockSpec((D, tn), lambda i, j: (0, j))],
        out_specs=pl.BlockSpec((ncb, tm // dil, dil * LANES), lambda i, j: (j, i, 0)),
        out_shape=jax.ShapeDtypeStruct((N // LANES, T // dil, dil * LANES), BF16),
        scratch_shapes=[pltpu.VMEM((ncb, tm, LANES), F32)],
        compiler_params=_cparams(("parallel", "parallel")),
        name="proj_view",
    )(h, w)


CUMSUM_BLOCK = 256


def _logf_cumsum_kernel(f_ref, b_ref, c_ref, *, nblk):
    row = lax.broadcasted_iota(jnp.int32, (CUMSUM_BLOCK, CUMSUM_BLOCK), 0)
    col = lax.broadcasted_iota(jnp.int32, (CUMSUM_BLOCK, CUMSUM_BLOCK), 1)
    tri = jnp.where(col <= row, 1.0, 0.0).astype(F32)

    def body(i, carry):
        sl = pl.ds(pl.multiple_of(i * CUMSUM_BLOCK, CUMSUM_BLOCK), CUMSUM_BLOCK)
        z = f_ref[sl, :] + b_ref[...]
        logf = jnp.minimum(z, 0.0) - jnp.log1p(jnp.exp(-jnp.abs(z)))
        cs = jnp.dot(tri, logf, preferred_element_type=F32, precision=HIGHEST) + carry
        c_ref[sl, :] = cs
        return cs[CUMSUM_BLOCK - 1:CUMSUM_BLOCK, :]

    lax.fori_loop(0, nblk, body, jnp.zeros((1, LANES), F32))


def logf_cumsum(small, bias_row, B, S):
    T = B * S
    return pl.pallas_call(
        functools.partial(_logf_cumsum_kernel, nblk=S // CUMSUM_BLOCK),
        grid=(B,),
        in_specs=[pl.BlockSpec((S, LANES), lambda b: (b, 0)),
                  pl.BlockSpec((1, LANES), lambda b: (0, 0))],
        out_specs=pl.BlockSpec((S, LANES), lambda b: (b, 0)),
        out_shape=jax.ShapeDtypeStruct((T, LANES), F32),
        compiler_params=_cparams(("parallel",)),
        name="logf_cumsum",
    )(small, bias_row)


FLASH_ROW_CHUNK = 128


LOG2E = math.log2(math.e)


def _lane_tile(x, n):
    return jnp.concatenate([x] * n, axis=1)


def _flash_update(s2, v, m_ref, l_ref, acc_ref, row_shift=None):
    m_old = m_ref[...]
    m_cur = jnp.max(s2, axis=-1, keepdims=True)
    if row_shift is not None:
        m_cur = m_cur + row_shift
    m_new = jnp.maximum(m_old, m_cur)
    m_safe = jnp.where(m_new == NEG_INF, 0.0, m_new)
    alpha = jnp.exp2(m_old - m_safe)
    origin = m_safe if row_shift is None else m_safe - row_shift
    p = jnp.exp2(s2 - _lane_tile(origin, s2.shape[1] // LANES))
    l_ref[...] = alpha * l_ref[...] + jnp.sum(p, axis=-1, keepdims=True)
    acc_ref[...] = alpha * acc_ref[...] + jnp.dot(p.astype(BF16), v, preferred_element_type=F32)
    m_ref[...] = m_new


def _flash_init(m_ref, l_ref, acc_ref):
    m_ref[...] = jnp.full(m_ref.shape, NEG_INF, F32)
    l_ref[...] = jnp.zeros(l_ref.shape, F32)
    acc_ref[...] = jnp.zeros(acc_ref.shape, F32)


def _flash_result(l_ref, acc_ref):
    l = l_ref[...]
    return acc_ref[...] / jnp.where(l > 0.0, l, 1.0)


def _qk(q, k):
    return lax.dot_general(q, k, (((1,), (1,)), ((), ())), preferred_element_type=F32)


def _fox_kernel(qi_ref, ki_ref, q_ref, k_ref, v_ref, cq_ref, ck_ref, o_ref, m_ref, l_ref, acc_ref, cqb_ref,
                *, tq, tk, rc, nb):
    p = pl.program_id(1)
    qi, ki = qi_ref[p], ki_ref[p]

    @pl.when(ki == 0)
    def _():
        _flash_init(m_ref, l_ref, acc_ref)
        lane = lax.broadcasted_iota(jnp.int32, cq_ref.shape, 2)
        mine = jnp.sum(jnp.where(lane == SMALL_FOX_F + pl.program_id(0), cq_ref[...], 0.0),
                       axis=-1, keepdims=True)
        cqb_ref[...] = jnp.broadcast_to(mine * LOG2E, cqb_ref.shape)

    chunks = [pl.ds(c * rc, rc) for c in range(tq // rc)]

    def tile(causal_mask):
        widths = [min(tk, (c + 1) * rc) if causal_mask and tq == tk else tk for c in range(tq // rc)]

        def products(b):
            return [_qk(q_ref[b, rows, :], k_ref[b, 0:w, :]) for rows, w in zip(chunks, widths)]

        def softmax_pv(b, qk):
            ck = ck_ref[b] * LOG2E
            for c, (rows, w) in enumerate(zip(chunks, widths)):
                s = qk[c] * (ATTN_SCALE * LOG2E) - ck[:, 0:w]
                if causal_mask:
                    qpos = qi * tq + c * rc + lax.broadcasted_iota(jnp.int32, (rc, w), 0)
                    kpos = ki * tk + lax.broadcasted_iota(jnp.int32, (rc, w), 1)
                    s = jnp.where(kpos <= qpos, s, NEG_INF)
                _flash_update(s, v_ref[b, 0:w, :], m_ref.at[b, rows], l_ref.at[b, rows], acc_ref.at[b, rows],
                              row_shift=cqb_ref[b, rows, :])

        ahead = products(0)
        for b in range(nb):
            qk = ahead
            if b + 1 < nb:
                ahead = products(b + 1)
            softmax_pv(b, qk)

    crosses_diagonal = (ki + 1) * tk - 1 > qi * tq

    @pl.when(crosses_diagonal)
    def _():
        tile(True)

    @pl.when(jnp.logical_not(crosses_diagonal))
    def _():
        tile(False)

    @pl.when(ki == ((qi + 1) * tq - 1) // tk)
    def _():
        o_ref[...] = _flash_result(l_ref, acc_ref).astype(o_ref.dtype)


def _causal_pairs(nq, tq, tk):
    qs, ks = [], []
    for qi in range(nq):
        for ki in range(((qi + 1) * tq - 1) // tk + 1):
            qs.append(qi)
            ks.append(ki)
    return jnp.asarray(qs, jnp.int32), jnp.asarray(ks, jnp.int32)


def fox_attention(proj, c_packed, c_row, B, S, *, tq=512, tk=512):
    tq, tk = min(tq, S), min(tk, S)
    H = FOX_HEADS
    T = B * S
    qis, kis = _causal_pairs(S // tq, tq, tk)
    proj4 = proj.reshape(proj.shape[0], B, S, LANES)
    grid_spec = pltpu.PrefetchScalarGridSpec(
        num_scalar_prefetch=2,
        grid=(H, int(qis.shape[0])),
        in_specs=[
            pl.BlockSpec((None, B, tq, LANES), lambda h, p, qi, ki: (CB_FOX + h, 0, qi[p], 0)),
            pl.BlockSpec((None, B, tk, LANES), lambda h, p, qi, ki: (CB_FOX + H + h, 0, ki[p], 0)),
            pl.BlockSpec((None, B, tk, LANES), lambda h, p, qi, ki: (CB_FOX + 2 * H + h, 0, ki[p], 0)),
            pl.BlockSpec((B, tq, LANES), lambda h, p, qi, ki: (0, qi[p], 0)),
            pl.BlockSpec((None, B, 1, tk), lambda h, p, qi, ki: (h, 0, 0, ki[p])),
        ],
        out_specs=pl.BlockSpec((None, B, tq, LANES), lambda h, p, qi, ki: (h, 0, qi[p], 0)),
        scratch_shapes=[pltpu.VMEM((B, tq, LANES), F32)] * 4,
    )
    out = pl.pallas_call(
        functools.partial(_fox_kernel, tq=tq, tk=tk, rc=min(FLASH_ROW_CHUNK, tq), nb=B),
        grid_spec=grid_spec,
        out_shape=jax.ShapeDtypeStruct((H, B, S, LANES), BF16),
        compiler_params=_cparams(("parallel", "arbitrary")),
        name="fox_attention",
    )(qis, kis, proj4, proj4, proj4, c_packed.reshape(B, S, LANES), c_row.reshape(H, B, 1, S))
    return out.reshape(H, T, LANES)


DIL_ROWS_PER_STEP = 512


def _dil_kernel(q_ref, kp_ref, kc_ref, vp_ref, vc_ref, bias_ref, o_ref, lse_ref, *, dil, tu, nsub, hp):
    ui = pl.program_id(2)
    kcol = lax.broadcasted_iota(jnp.int32, (tu, 2 * tu), 1)
    first_ok = jnp.logical_or(ui > 0, kcol >= tu)
    units = [(c, slice(r * LANES, (r + 1) * LANES)) for c in range(nsub) for r in range(dil)]

    def window(prev_ref, cur_ref, h, c, sl):
        if c == 0:
            return jnp.concatenate([prev_ref[h, :, sl], cur_ref[h, 0:tu, sl]], axis=0)
        return cur_ref[h, (c - 1) * tu:(c + 1) * tu, sl]

    def head(h, carry):
        bias = bias_ref[h]
        bias_first = jnp.where(first_ok, bias, NEG_INF)
        qk = [_qk(q_ref[h, c * tu:(c + 1) * tu, sl], window(kp_ref, kc_ref, h, c, sl)) for c, sl in units]
        for i, (c, sl) in enumerate(units):
            rows = slice(c * tu, (c + 1) * tu)
            s = qk[i] * ATTN_SCALE + (bias_first if c == 0 else bias)
            m = jnp.max(s, axis=-1, keepdims=True)
            e = jnp.exp(s - m)
            den = jnp.sum(e, axis=-1, keepdims=True)
            o = jnp.dot(e.astype(BF16), window(vp_ref, vc_ref, h, c, sl), preferred_element_type=F32) / den
            o_ref[h, rows, sl] = o.astype(o_ref.dtype)
            lse_ref[h, rows, sl] = jnp.broadcast_to(m + jnp.log(den), (tu, LANES))
        return carry

    lax.fori_loop(0, hp, head, 0)


def dilated_group(view, c0, bias_tbl, group, B, S):
    dil = DIL_GROUPS[group][1]
    tu = DIL_TAPS
    Hg = DIL_HEADS_PER_GROUP
    T = B * S
    rows = min(DIL_ROWS_PER_STEP, S // dil)
    nsub = rows // tu
    nstep = S // dil // rows
    hp = max(1, Hg // max(1, dil // 4))
    cq, ck, cv = c0 // hp, (c0 + Hg) // hp, (c0 + 2 * Hg) // hp
    cur_blk = (hp, rows, dil * LANES)
    prev_blk = (hp, tu, dil * LANES)
    prev = lambda c: (lambda b, h, u: (c + h, jnp.maximum((b * nstep + u) * nsub - 1, 0), 0))
    cur = lambda c: (lambda b, h, u: (c + h, b * nstep + u, 0))
    o, lse = pl.pallas_call(
        functools.partial(_dil_kernel, dil=dil, tu=tu, nsub=nsub, hp=hp),
        grid=(B, Hg // hp, nstep),
        in_specs=[pl.BlockSpec(cur_blk, cur(cq)),
                  pl.BlockSpec(prev_blk, prev(ck)), pl.BlockSpec(cur_blk, cur(ck)),
                  pl.BlockSpec(prev_blk, prev(cv)), pl.BlockSpec(cur_blk, cur(cv)),
                  pl.BlockSpec((hp, tu, 2 * tu), lambda b, h, u: (h, 0, 0))],
        out_specs=[pl.BlockSpec(cur_blk, cur(0)), pl.BlockSpec(cur_blk, cur(0))],
        out_shape=[jax.ShapeDtypeStruct((Hg, T // dil, dil * LANES), BF16),
                   jax.ShapeDtypeStruct((Hg, T // dil, dil * LANES), F32)],
        compiler_params=_cparams(("parallel", "parallel", "arbitrary")),
        name=f"dilated_group{group}",
    )(view, view, view, view, view, bias_tbl)
    return o, lse


def dilated_bias_tables(rel_bias_b):
    tu = DIL_TAPS
    tables = []
    for g, (_, dil) in enumerate(DIL_GROUPS):
        heads = rel_bias_b[:, g * DIL_HEADS_PER_GROUP:(g + 1) * DIL_HEADS_PER_GROUP]

        def fn(taps, heads=heads, dil=dil):
            valid = (taps >= 0) & (taps <= DIL_TAPS)
            vals = heads[_t5_bucket_np(np.where(valid, taps, 0) * dil)].T
            return jnp.where(valid[None, :], vals, NEG_INF)

        tables.append(_toeplitz_blocks(fn, [tu], tu, 2 * tu)[0])
    return tables


def _compress_kernel(x_ref, pelo_ref, pehi_ref, w1a_ref, w1b_ref, w2_ref, o_ref, *, nchunk):
    x = x_ref[...].astype(F32)
    u0 = jnp.dot((x + pelo_ref[...]).astype(BF16), w1a_ref[...], preferred_element_type=F32)
    u1 = jnp.dot((x + pehi_ref[...]).astype(BF16), w1b_ref[...], preferred_element_type=F32)
    pre = u0 + pltpu.roll(u1, nchunk - 1, 0)
    hid = jax.nn.gelu(pre)
    out = jnp.dot(hid.astype(BF16), w2_ref[...], preferred_element_type=F32)
    row = lax.broadcasted_iota(jnp.int32, out.shape, 0)
    o_ref[...] = jnp.where(row < nchunk - 1, out, 0.0).astype(o_ref.dtype)


def nsa_compress(proj, pe, w1, w2, B, S):
    nchunk = S // CMP_STRIDE
    half = CMP_STRIDE * HEAD_DIM
    Hkv = NSA_KV_HEADS
    x = proj[CB_NSAKV:CB_NSAKV + 2 * Hkv].reshape(2, Hkv, B, nchunk, half)
    pe_lo = pe[:, :CMP_STRIDE].reshape(2, 1, half)
    pe_hi = pe[:, CMP_STRIDE:].reshape(2, 1, half)
    return pl.pallas_call(
        functools.partial(_compress_kernel, nchunk=nchunk),
        grid=(2, B, Hkv),
        in_specs=[pl.BlockSpec((None, None, None, nchunk, half), lambda t, b, h: (t, h, b, 0, 0)),
                  pl.BlockSpec((None, 1, half), lambda t, b, h: (t, 0, 0)),
                  pl.BlockSpec((None, 1, half), lambda t, b, h: (t, 0, 0)),
                  pl.BlockSpec((None, half, CMP_HIDDEN), lambda t, b, h: (t, 0, 0)),
                  pl.BlockSpec((None, half, CMP_HIDDEN), lambda t, b, h: (t, 1, 0)),
                  pl.BlockSpec((None, CMP_HIDDEN, HEAD_DIM), lambda t, b, h: (t, 0, 0))],
        out_specs=pl.BlockSpec((None, None, None, nchunk, HEAD_DIM), lambda t, b, h: (t, b, h, 0, 0)),
        out_shape=jax.ShapeDtypeStruct((2, B, Hkv, nchunk, HEAD_DIM), BF16),
        compiler_params=_cparams(("parallel", "parallel", "parallel")),
        name="nsa_compress",
    )(x, pe_lo, pe_hi, w1, w1, w2)


def _masked_softmax(s, mask, axis):
    s = jnp.where(mask, s, NEG_INF)
    m = jnp.max(s, axis=axis, keepdims=True)
    m = jnp.where(m == NEG_INF, 0.0, m)
    e = jnp.exp(s - m)
    den = jnp.sum(e, axis=axis, keepdims=True)
    return e / jnp.where(den > 0.0, den, 1.0)


def _cmp_select_kernel(q_ref, kc_ref, vc_ref, mt_ref, o_ref, sel_ref, *, tq, nchunk, n_slc, n_sel):
    q0 = pl.program_id(2) * tq
    kc = kc_ref[...]
    vc = vc_ref[...]
    pos_c = q0 + lax.broadcasted_iota(jnp.int32, (nchunk, tq), 1)
    end_c = lax.broadcasted_iota(jnp.int32, (nchunk, tq), 0) * CMP_STRIDE + (CMP_BLOCK - 1)
    vis_c = end_c <= pos_c
    imp = jnp.zeros((nchunk, tq), F32)
    qk = [_qk(kc, q_ref[g]) for g in range(NSA_GQA)]
    for g in range(NSA_GQA):
        p = _masked_softmax(qk[g] * ATTN_SCALE, vis_c, 0)
        o = lax.dot_general(p.astype(BF16), vc, (((0,), (0,)), ((), ())), preferred_element_type=F32)
        o_ref[g] = o.astype(o_ref.dtype)
        imp = imp + p
    p_slc = jnp.dot(mt_ref[...], imp, preferred_element_type=F32, precision=HIGHEST)

    blk = lax.broadcasted_iota(jnp.int32, (n_slc, tq), 0)
    cur = (q0 + lax.broadcasted_iota(jnp.int32, (n_slc, tq), 1)) // SLC_BLOCK
    forced = (blk == 0) | (blk == cur) | (blk == cur - 1)
    allowed = blk <= cur
    score = jnp.where(forced, 1e30, jnp.where(allowed, p_slc, -1.0))
    chosen = jnp.zeros((n_slc, tq), F32)
    for _ in range(n_sel):
        top = jnp.max(score, axis=0, keepdims=True)
        first = jnp.min(jnp.where(score == top, blk, n_slc), axis=0, keepdims=True)
        hit = blk == first
        chosen = jnp.where(hit, 1.0, chosen)
        score = jnp.where(hit, -2.0, score)
    mask = jnp.concatenate([jnp.where(allowed, chosen, 0.0), jnp.zeros((LANES - n_slc, tq), F32)], axis=0)
    sel_ref[...] = mask.T.astype(sel_ref.dtype)


def nsa_cmp_select(proj, kvc, B, S, *, tq=1024):
    tq = min(tq, S)
    nq = S // tq
    T = B * S
    nchunk = S // CMP_STRIDE
    n_slc = S // SLC_BLOCK
    if n_slc > LANES:
        raise ValueError("selection mask is packed into one lane width: needs S <= 64*128")
    n_sel = min(SLC_COUNT, n_slc)
    ratio, n_inner = SLC_BLOCK // CMP_STRIDE, CMP_BLOCK // CMP_STRIDE
    mt = np.zeros((n_slc, nchunk), np.float32)
    for j in range(n_slc):
        for m in range(ratio):
            for n in range(n_inner):
                c = ratio * j + m - n
                if 0 <= c < nchunk - 1:
                    mt[j, c] += 1.0
    G = NSA_GQA
    kv_spec = lambda t: pl.BlockSpec((None, None, None, nchunk, HEAD_DIM), lambda b, h, i: (t, b, h, 0, 0))
    return pl.pallas_call(
        functools.partial(_cmp_select_kernel, tq=tq, nchunk=nchunk, n_slc=n_slc, n_sel=n_sel),
        grid=(B, NSA_KV_HEADS, nq),
        in_specs=[pl.BlockSpec((G, tq, LANES), lambda b, h, i: (CB_NSAQ // G + h, b * nq + i, 0)),
                  kv_spec(0), kv_spec(1),
                  pl.BlockSpec((n_slc, nchunk), lambda b, h, i: (0, 0))],
        out_specs=[pl.BlockSpec((G, tq, LANES), lambda b, h, i: (h, b * nq + i, 0)),
                   pl.BlockSpec((None, None, tq, LANES), lambda b, h, i: (b, h, i, 0))],
        out_shape=[jax.ShapeDtypeStruct((NSA_Q_HEADS, T, LANES), BF16),
                   jax.ShapeDtypeStruct((B, NSA_KV_HEADS, S, LANES), BF16)],
        compiler_params=_cparams(("parallel", "parallel", "arbitrary")),
        name="nsa_cmp_select",
    )(proj, kvc, kvc, jnp.asarray(mt))


def _slc_kernel(qi_ref, ki_ref, q_ref, k_ref, v_ref, sel_ref, bias_ref, o_ref, m_ref, l_ref, acc_ref,
                *, tq, tk, nb):
    p = pl.program_id(1)
    qi, ki = qi_ref[p], ki_ref[p]

    @pl.when(ki == 0)
    def _():
        _flash_init(m_ref, l_ref, acc_ref)

    blk_of_key = ki * (tk // SLC_BLOCK) + lax.broadcasted_iota(jnp.int32, (LANES, tk), 1) // SLC_BLOCK
    expand = jnp.where(lax.broadcasted_iota(jnp.int32, (LANES, tk), 0) == blk_of_key, 1.0, 0.0).astype(BF16)

    def step(b, causal_mask):
        picked = jnp.dot(sel_ref[b], expand, preferred_element_type=F32)
        keep = picked > 0.5
        if causal_mask:
            qpos = qi * tq + lax.broadcasted_iota(jnp.int32, (tq, tk), 0)
            kpos = ki * tk + lax.broadcasted_iota(jnp.int32, (tq, tk), 1)
            keep = jnp.where(kpos <= qpos, picked, 0.0) > 0.5
        k, v = k_ref[b], v_ref[b]
        qk = [_qk(q_ref[g, b], k) for g in range(NSA_GQA)]
        for g in range(NSA_GQA):
            s = qk[g] * (ATTN_SCALE * LOG2E) + bias_ref[g]
            s = jnp.where(keep, s, NEG_INF)
            _flash_update(s, v, m_ref.at[g, b], l_ref.at[g, b], acc_ref.at[g, b])

    crosses_diagonal = (ki + 1) * tk - 1 > qi * tq

    @pl.when(crosses_diagonal)
    def _():
        _for_range(0, nb, lambda b: step(b, True))

    @pl.when(jnp.logical_not(crosses_diagonal))
    def _():
        _for_range(0, nb, lambda b: step(b, False))

    @pl.when(ki == ((qi + 1) * tq - 1) // tk)
    def _():
        o_ref[...] = _flash_result(l_ref, acc_ref).astype(o_ref.dtype)


def slc_bias_table(rel_bias_c, S, tq, tk):
    buckets = _t5_bucket_np(np.arange(S + tk))
    not_last = np.nonzero(buckets != REL_BUCKETS - 1)[0]
    far_start = int(not_last[-1]) + 1 if not_last.size else 0
    n_delta = min(S // tq, -(-(far_start + tk - 1) // tq) + 1)

    def fn(d):
        return rel_bias_c[_t5_bucket_np(d)].T * LOG2E

    return _toeplitz_blocks(fn, [dl * tq for dl in range(n_delta)], tq, tk)


def nsa_selected(proj, sel, bias_tbl, B, S, *, tq=256, tk=512):
    tq, tk = min(tq, S), min(tk, S)
    T = B * S
    G = NSA_GQA
    ck = CB_NSAKV + (1 * 2 + 0) * NSA_KV_HEADS
    cv = CB_NSAKV + (1 * 2 + 1) * NSA_KV_HEADS
    qis, kis = _causal_pairs(S // tq, tq, tk)
    n_delta = bias_tbl.shape[0]
    proj4 = proj.reshape(proj.shape[0], B, S, LANES)
    grid_spec = pltpu.PrefetchScalarGridSpec(
        num_scalar_prefetch=2,
        grid=(NSA_KV_HEADS, int(qis.shape[0])),
        in_specs=[
            pl.BlockSpec((G, B, tq, LANES), lambda h, p, qi, ki: (CB_NSAQ // G + h, 0, qi[p], 0)),
            pl.BlockSpec((None, B, tk, LANES), lambda h, p, qi, ki: (ck + h, 0, ki[p], 0)),
            pl.BlockSpec((None, B, tk, LANES), lambda h, p, qi, ki: (cv + h, 0, ki[p], 0)),
            pl.BlockSpec((B, None, tq, LANES), lambda h, p, qi, ki: (0, h, qi[p], 0)),
            pl.BlockSpec((None, G, tq, tk),
                         lambda h, p, qi, ki: (jnp.minimum(qi[p] - ki[p] * (tk // tq), n_delta - 1), h, 0, 0)),
        ],
        out_specs=pl.BlockSpec((G, B, tq, LANES), lambda h, p, qi, ki: (h, 0, qi[p], 0)),
        scratch_shapes=[pltpu.VMEM((G, B, tq, LANES), F32)] * 3,
    )
    out = pl.pallas_call(
        functools.partial(_slc_kernel, tq=tq, tk=tk, nb=B),
        grid_spec=grid_spec,
        out_shape=jax.ShapeDtypeStruct((NSA_Q_HEADS, B, S, LANES), BF16),
        compiler_params=_cparams(("parallel", "arbitrary")),
        name="nsa_selected",
    )(qis, kis, proj4, proj4, proj4, sel, bias_tbl)
    return out.reshape(NSA_Q_HEADS, T, LANES)


def _win_kernel(q_ref, k0_ref, k1_ref, k2_ref, v0_ref, v1_ref, v2_ref, bias_ref, o_ref, *, tq, nkb):
    qi = pl.program_id(2)
    k = jnp.concatenate([r[...] for r in (k0_ref, k1_ref, k2_ref)][-nkb:], axis=0)
    v = jnp.concatenate([r[...] for r in (v0_ref, v1_ref, v2_ref)][-nkb:], axis=0)
    kpos = (qi - (nkb - 1)) * tq + lax.broadcasted_iota(jnp.int32, (tq, nkb * tq), 1)
    qk = [_qk(q_ref[g], k) for g in range(NSA_GQA)]
    for g in range(NSA_GQA):
        s = qk[g] * ATTN_SCALE + bias_ref[g]
        p = _masked_softmax(s, kpos >= 0, -1)
        o_ref[g] = jnp.dot(p.astype(BF16), v, preferred_element_type=F32).astype(o_ref.dtype)


def win_bias_table(rel_bias_c, tq, nkb):
    def fn(dist):
        valid = (dist >= 0) & (dist < WIN)
        return jnp.where(valid[None, :], rel_bias_c[_t5_bucket_np(dist)].T, NEG_INF)

    return _toeplitz_blocks(fn, [(nkb - 1) * tq], tq, nkb * tq)[0]


def nsa_window(proj, bias_tbl, B, S, *, tq=256):
    tq = min(tq, S)
    nq = S // tq
    T = B * S
    G = NSA_GQA
    nkb = min(WIN // tq + 1, 3)
    ck = CB_NSAKV + (2 * 2 + 0) * NSA_KV_HEADS
    cv = CB_NSAKV + (2 * 2 + 1) * NSA_KV_HEADS
    kv = lambda c, back: pl.BlockSpec(
        (None, tq, LANES), lambda b, h, i: (c + h, b * nq + jnp.maximum(i - back, 0), 0))
    return pl.pallas_call(
        functools.partial(_win_kernel, tq=tq, nkb=nkb),
        grid=(B, NSA_KV_HEADS, nq),
        in_specs=[pl.BlockSpec((G, tq, LANES), lambda b, h, i: (CB_NSAQ // G + h, b * nq + i, 0)),
                  kv(ck, 2), kv(ck, 1), kv(ck, 0), kv(cv, 2), kv(cv, 1), kv(cv, 0),
                  pl.BlockSpec((G, tq, nkb * tq), lambda b, h, i: (h, 0, 0))],
        out_specs=pl.BlockSpec((G, tq, LANES), lambda b, h, i: (h, b * nq + i, 0)),
        out_shape=jax.ShapeDtypeStruct((NSA_Q_HEADS, T, LANES), BF16),
        compiler_params=_cparams(("parallel", "parallel", "arbitrary")),
        name="nsa_window",
    )(proj, proj, proj, proj, proj, proj, proj, bias_tbl)


def _sigmoid(x):
    return 1.0 / (1.0 + jnp.exp(-x))


def _merge_kernel(fox_ref, d0_ref, d1_ref, d2_ref, l0_ref, l1_ref, l2_ref, cmp_ref, slc_ref, win_ref, small_ref,
                  g0_ref, g1_ref, g2_ref, wa_ref, wb_ref, wc_ref, o_ref, nat_ref, *, ncb):
    tm = o_ref.shape[0]

    def natural(ref, h, dil, slot):
        if dil == 1:
            return ref[h].astype(F32)
        for r in range(dil):
            nat_ref[slot, pl.ds(r, tm // dil, stride=dil), :] = ref[h, :, r * LANES:(r + 1) * LANES].astype(F32)
        return nat_ref[slot]

    ya = jnp.concatenate([fox_ref[h] for h in range(FOX_HEADS)], axis=1)
    yb = []
    for h in range(DIL_HEADS_PER_GROUP):
        outs, lse = [], []
        for g, (o_g, l_g) in enumerate(((d0_ref, l0_ref), (d1_ref, l1_ref), (d2_ref, l2_ref))):
            dil = DIL_GROUPS[g][1]
            slot = (h * len(DIL_GROUPS) + g) * 2
            outs.append(natural(o_g, h, dil, slot))
            lse.append(natural(l_g, h, dil, slot + 1))
        top = jnp.maximum(jnp.maximum(lse[0], lse[1]), lse[2])
        w = [jnp.exp(x - top) for x in lse]
        tot = w[0] + w[1] + w[2]
        y = sum((w[g] / tot) * outs[g] for g in range(len(DIL_GROUPS)))
        yb.append(y.astype(BF16))
    gates = _sigmoid(small_ref[...])
    yc = []
    for h in range(NSA_Q_HEADS):
        y = jnp.zeros(cmp_ref.shape[1:], F32)
        for br, ref in enumerate((cmp_ref, slc_ref, win_ref)):
            col = SMALL_NSA_G + br * NSA_Q_HEADS + h
            y = y + gates[:, col:col + 1] * ref[h].astype(F32)
        yc.append(y.astype(BF16))

    def gate(ref):
        return _sigmoid(jnp.concatenate([ref[c] for c in range(ncb)], axis=1).astype(F32))

    merged = (gate(g0_ref) * jnp.dot(ya, wa_ref[...], preferred_element_type=F32)
              + gate(g1_ref) * jnp.dot(jnp.concatenate(yb, axis=1), wb_ref[...], preferred_element_type=F32)
              + gate(g2_ref) * jnp.dot(jnp.concatenate(yc, axis=1), wc_ref[...], preferred_element_type=F32))
    o_ref[...] = merged.astype(o_ref.dtype)


def merge_branches(proj, small, fox_o, dil_o, dil_lse, cmp_o, slc_o, win_o, wa, wb, wc, *, tm=256):
    T = proj.shape[1]
    D = wa.shape[1]
    tm = min(tm, T)
    ncb = D // LANES
    Hg = DIL_HEADS_PER_GROUP
    heads = lambda n: pl.BlockSpec((n, tm, LANES), lambda i: (0, i, 0))
    gate = lambda b: pl.BlockSpec((ncb, tm, LANES), lambda i: (CB_MERGE // ncb + b, i, 0))
    dil = [pl.BlockSpec((Hg, tm // d, d * LANES), lambda i: (0, i, 0)) for _, d in DIL_GROUPS]
    wspec = lambda k: pl.BlockSpec((k, D), lambda i: (0, 0))
    return pl.pallas_call(
        functools.partial(_merge_kernel, ncb=ncb),
        grid=(T // tm,),
        in_specs=[heads(FOX_HEADS), *dil, *dil, heads(NSA_Q_HEADS), heads(NSA_Q_HEADS), heads(NSA_Q_HEADS),
                  pl.BlockSpec((tm, LANES), lambda i: (i, 0)),
                  gate(0), gate(1), gate(2), wspec(wa.shape[0]), wspec(wb.shape[0]), wspec(wc.shape[0])],
        out_specs=pl.BlockSpec((tm, D), lambda i: (i, 0)),
        out_shape=jax.ShapeDtypeStruct((T, D), BF16),
        scratch_shapes=[pltpu.VMEM((2 * Hg * len(DIL_GROUPS), tm, LANES), F32)],
        compiler_params=_cparams(("parallel",)),
        name="merge_branches",
    )(fox_o, *dil_o, *dil_lse, cmp_o, slc_o, win_o, small, proj, proj, proj, wa, wb, wc)


def _matmul_residual_kernel(a_ref, w_ref, x_ref, o_ref):
    o_ref[...] = x_ref[...] + jnp.dot(a_ref[...], w_ref[...], preferred_element_type=F32)


def matmul_residual(a, w, x, *, tm=1024, tn=1024):
    T, K = a.shape
    N = w.shape[1]
    tm, tn = min(tm, T), min(tn, N)
    return pl.pallas_call(
        _matmul_residual_kernel,
        grid=(T // tm, N // tn),
        in_specs=[pl.BlockSpec((tm, K), lambda i, j: (i, 0)),
                  pl.BlockSpec((K, tn), lambda i, j: (0, j)),
                  pl.BlockSpec((tm, tn), lambda i, j: (i, j))],
        out_specs=pl.BlockSpec((tm, tn), lambda i, j: (i, j)),
        out_shape=jax.ShapeDtypeStruct((T, N), F32),
        compiler_params=_cparams(("parallel", "parallel")),
        name="matmul_residual",
    )(a, w, x)


def _silu(x):
    return x * _sigmoid(x)


def _swiglu_partial(h, wg_ref, wu_ref, wd_ref):
    act = _silu(jnp.dot(h, wg_ref[...].astype(BF16), preferred_element_type=F32)) * \
        jnp.dot(h, wu_ref[...].astype(BF16), preferred_element_type=F32)
    return jnp.dot(act.astype(BF16), wd_ref[...].astype(BF16), preferred_element_type=F32)


def _ffn_kernel(x_ref, g_ref, wg_ref, wu_ref, wd_ref, gf_ref, o_ref, h_ref, *, final_norm):
    f = pl.program_id(1)

    @pl.when(f == 0)
    def _():
        h_ref[...] = _rms_rows(x_ref[...], g_ref[...]).astype(BF16)
        o_ref[...] = x_ref[...]

    o_ref[...] += _swiglu_partial(h_ref[...], wg_ref, wu_ref, wd_ref)

    if final_norm:
        @pl.when(f == pl.num_programs(1) - 1)
        def _():
            o_ref[...] = _rms_rows(o_ref[...], gf_ref[...])


def ffn(x, g, wg, wu, wd, g_final, *, final_norm, tm=1024, tf=256):
    T, D = x.shape
    F = wg.shape[1]
    tm, tf = min(tm, T), min(tf, F)
    return pl.pallas_call(
        functools.partial(_ffn_kernel, final_norm=final_norm),
        grid=(T // tm, F // tf),
        in_specs=[pl.BlockSpec((tm, D), lambda i, f: (i, 0)),
                  pl.BlockSpec((1, D), lambda i, f: (0, 0)),
                  pl.BlockSpec((D, tf), lambda i, f: (0, f)),
                  pl.BlockSpec((D, tf), lambda i, f: (0, f)),
                  pl.BlockSpec((tf, D), lambda i, f: (f, 0)),
                  pl.BlockSpec((1, D), lambda i, f: (0, 0))],
        out_specs=pl.BlockSpec((tm, D), lambda i, f: (i, 0)),
        out_shape=jax.ShapeDtypeStruct((T, D), F32),
        scratch_shapes=[pltpu.VMEM((tm, D), BF16)],
        compiler_params=_cparams(("parallel", "arbitrary")),
        name="ffn",
    )(x, g, wg, wu, wd, g_final)


def _route_kernel(x_ref, g_ref, r_ref, o_ref):
    h = _rms_rows(x_ref[...], g_ref[...])
    logits = jnp.dot(h, r_ref[...], preferred_element_type=F32, precision=HIGHEST)
    lane = lax.broadcasted_iota(jnp.int32, logits.shape, 1)
    logits = jnp.where(lane < N_EXPERTS, logits, NEG_INF)
    v1 = jnp.max(logits, axis=1, keepdims=True)
    i1 = jnp.min(jnp.where(logits == v1, lane, LANES), axis=1, keepdims=True)
    rest = jnp.where(lane == i1, NEG_INF, logits)
    v2 = jnp.max(rest, axis=1, keepdims=True)
    i2 = jnp.min(jnp.where(rest == v2, lane, LANES), axis=1, keepdims=True)
    e2 = jnp.exp(v2 - v1)
    den = 1.0 + e2
    rec = jnp.where(lane == ROUTE_CHOICE + i1, 1.0, jnp.where(lane == ROUTE_CHOICE + i2, 2.0, 0.0))
    rec = jnp.where(lane == ROUTE_WEIGHT, 1.0 / den, jnp.where(lane == ROUTE_WEIGHT + 1, e2 / den, rec))
    o_ref[...] = rec


def moe_route(x, g, router_pad, *, tm=1024):
    T, D = x.shape
    tm = min(tm, T)
    return pl.pallas_call(
        _route_kernel,
        grid=(T // tm,),
        in_specs=[pl.BlockSpec((tm, D), lambda i: (i, 0)),
                  pl.BlockSpec((1, D), lambda i: (0, 0)),
                  pl.BlockSpec((D, LANES), lambda i: (0, 0))],
        out_specs=pl.BlockSpec((tm, LANES), lambda i: (i, 0)),
        out_shape=jax.ShapeDtypeStruct((T, LANES), F32),
        compiler_params=_cparams(("parallel",)),
        name="moe_route",
    )(x, g, router_pad)


def _moe_plan(route, T, tm):
    E = N_EXPERTS
    n_tiles = (TOP_K * T + E * tm) // tm
    choice = route[:, ROUTE_CHOICE:ROUTE_CHOICE + E]
    seli = (choice > 0.5).astype(jnp.int32)
    counts = jnp.sum(seli, axis=0)
    padded = ((counts + tm - 1) // tm) * tm
    seg_end = jnp.cumsum(padded)
    seg_start = seg_end - padded
    slot = seg_start[None, :] + jnp.cumsum(seli, axis=0) - 1
    slot_of = jnp.stack([jnp.sum(jnp.where(choice == k + 1.0, slot, 0), axis=1) for k in range(TOP_K)])
    n_used = (seg_end[-1] // tm).astype(jnp.int32)
    tile_start = jnp.arange(n_tiles, dtype=jnp.int32) * tm
    tile_expert = jnp.sum((seg_end[None, :] <= tile_start[:, None]).astype(jnp.int32), axis=1)
    tile_expert = jnp.minimum(tile_expert, E - 1)
    last_expert = jnp.take(tile_expert, n_used - 1)
    tile_expert = jnp.where(jnp.arange(n_tiles) < n_used, tile_expert, last_expert)
    pad_lo = (seg_start + counts).astype(jnp.int32)
    return (tile_expert.astype(jnp.int32), n_used.reshape(1), slot_of.reshape(-1).astype(jnp.int32),
            pad_lo, seg_end.astype(jnp.int32))


def _for_range(lo, hi, fn, unroll=None):
    def body(r, carry):
        fn(r)
        return carry
    lax.fori_loop(lo, hi, body, 0, unroll=unroll)


def _moe_dispatch_kernel(slot_ref, lo_ref, hi_ref, nt_ref, x_ref, xs_hbm, zero_ref, sem, zsem,
                         *, tc, T, chunks_per_tile, n_chunks):
    i = pl.program_id(0)

    def token_row(k, r):
        s = slot_ref[k * T + i * tc + r]
        return pltpu.make_async_copy(x_ref.at[pl.ds(r, 1)], xs_hbm.at[pl.ds(s, 1)], sem.at[0])

    def zero_row(s):
        return pltpu.make_async_copy(zero_ref.at[pl.ds(0, 1)], xs_hbm.at[pl.ds(s, 1)], zsem.at[0])

    def zero_chunk(c):
        rows = pl.ds(pl.multiple_of(c * MOE_ZERO_ROWS, MOE_ZERO_ROWS), MOE_ZERO_ROWS)
        return pltpu.make_async_copy(zero_ref, xs_hbm.at[rows], zsem.at[0])

    @pl.when(i == 0)
    def _():
        zero_ref[...] = jnp.zeros(zero_ref.shape, F32)
        for start in (True, False):
            for e in range(N_EXPERTS):
                _for_range(lo_ref[e], hi_ref[e], lambda s: zero_row(s).start() if start else zero_row(s).wait())
            _for_range(nt_ref[0] * chunks_per_tile, n_chunks,
                       lambda c: zero_chunk(c).start() if start else zero_chunk(c).wait())

    for k in range(TOP_K):
        _for_range(0, tc, lambda r: token_row(k, r).start(), unroll=8)
    for k in range(TOP_K):
        pltpu.make_async_copy(x_ref, xs_hbm.at[pl.ds(0, tc)], sem.at[0]).wait()


def moe_dispatch(x, plan, *, tm, tc=512):
    _, n_used, slot_of, pad_lo, pad_hi = plan
    T, D = x.shape
    tc = min(tc, T)
    n_slots = TOP_K * T + N_EXPERTS * tm
    grid_spec = pltpu.PrefetchScalarGridSpec(
        num_scalar_prefetch=4,
        grid=(T // tc,),
        in_specs=[pl.BlockSpec((tc, D), lambda i, s, lo, hi, nt: (i, 0))],
        out_specs=pl.BlockSpec(memory_space=pl.ANY),
        scratch_shapes=[pltpu.VMEM((MOE_ZERO_ROWS, D), F32),
                        pltpu.SemaphoreType.DMA((1,)), pltpu.SemaphoreType.DMA((1,))],
    )
    return pl.pallas_call(
        functools.partial(_moe_dispatch_kernel, tc=tc, T=T, chunks_per_tile=tm // MOE_ZERO_ROWS,
                          n_chunks=n_slots // MOE_ZERO_ROWS),
        grid_spec=grid_spec,
        out_shape=jax.ShapeDtypeStruct((n_slots, D), F32),
        compiler_params=_cparams(("arbitrary",)),
        name="moe_dispatch",
    )(slot_of, pad_lo, pad_hi, n_used, x)


def _moe_ffn_kernel(te_ref, nt_ref, xs_ref, g_ref, wg_ref, wu_ref, wd_ref, y_ref, h_ref):
    i, f = pl.program_id(0), pl.program_id(1)
    used = i < nt_ref[0]

    @pl.when(used & (f == 0))
    def _():
        h_ref[...] = _rms_rows(xs_ref[...], g_ref[...]).astype(BF16)
        y_ref[...] = _swiglu_partial(h_ref[...], wg_ref, wu_ref, wd_ref)

    @pl.when(used & (f > 0))
    def _():
        y_ref[...] += _swiglu_partial(h_ref[...], wg_ref, wu_ref, wd_ref)

    @pl.when(jnp.logical_not(used) & (f == 0))
    def _():
        y_ref[...] = jnp.zeros(y_ref.shape, F32)


def moe_experts(xs, g, plan, wg, wu, wd, *, tm, tf=1024):
    tile_expert, n_used = plan[0], plan[1]
    n_slots, D = xs.shape
    F = wg.shape[2]
    tf = min(tf, F)
    nf = F // tf

    def fidx(i, f, nt):
        return jnp.where(i < nt[0], f, nf - 1)

    grid_spec = pltpu.PrefetchScalarGridSpec(
        num_scalar_prefetch=2,
        grid=(n_slots // tm, nf),
        in_specs=[
            pl.BlockSpec((tm, D), lambda i, f, te, nt: (jnp.minimum(i, nt[0] - 1), 0)),
            pl.BlockSpec((1, D), lambda i, f, te, nt: (0, 0)),
            pl.BlockSpec((None, D, tf), lambda i, f, te, nt: (te[i], 0, fidx(i, f, nt))),
            pl.BlockSpec((None, D, tf), lambda i, f, te, nt: (te[i], 0, fidx(i, f, nt))),
            pl.BlockSpec((None, tf, D), lambda i, f, te, nt: (te[i], fidx(i, f, nt), 0)),
        ],
        out_specs=pl.BlockSpec((tm, D), lambda i, f, te, nt: (i, 0)),
        scratch_shapes=[pltpu.VMEM((tm, D), BF16)],
    )
    return pl.pallas_call(
        _moe_ffn_kernel,
        grid_spec=grid_spec,
        out_shape=jax.ShapeDtypeStruct((n_slots, D), F32),
        compiler_params=_cparams(("parallel", "arbitrary")),
        name="moe_experts",
    )(tile_expert, n_used, xs, g, wg, wu, wd)


def _moe_combine_kernel(slot_ref, x_ref, y_hbm, route_ref, gf_ref, o_ref, ybuf, sem, *, tc, T, final_norm):
    i = pl.program_id(0)

    def slot_row(k, r):
        s = slot_ref[k * T + i * tc + r]
        return pltpu.make_async_copy(y_hbm.at[pl.ds(s, 1)], ybuf.at[k, pl.ds(r, 1)], sem.at[0])

    for k in range(TOP_K):
        _for_range(0, tc, lambda r: slot_row(k, r).start(), unroll=8)
    for k in range(TOP_K):
        pltpu.make_async_copy(y_hbm.at[pl.ds(0, tc)], ybuf.at[k], sem.at[0]).wait()

    w = route_ref[...]
    out = x_ref[...]
    for k in range(TOP_K):
        out = out + w[:, ROUTE_WEIGHT + k:ROUTE_WEIGHT + k + 1] * ybuf[k]
    if final_norm:
        out = _rms_rows(out, gf_ref[...])
    o_ref[...] = out


def moe_combine(x, y, route, plan, g_final, *, final_norm, tc=256):
    T, D = x.shape
    tc = min(tc, T)
    grid_spec = pltpu.PrefetchScalarGridSpec(
        num_scalar_prefetch=1,
        grid=(T // tc,),
        in_specs=[pl.BlockSpec((tc, D), lambda i, s: (i, 0)),
                  pl.BlockSpec(memory_space=pl.ANY),
                  pl.BlockSpec((tc, LANES), lambda i, s: (i, 0)),
                  pl.BlockSpec((1, D), lambda i, s: (0, 0))],
        out_specs=pl.BlockSpec((tc, D), lambda i, s: (i, 0)),
        scratch_shapes=[pltpu.VMEM((TOP_K, tc, D), F32), pltpu.SemaphoreType.DMA((1,))],
    )
    return pl.pallas_call(
        functools.partial(_moe_combine_kernel, tc=tc, T=T, final_norm=final_norm),
        grid_spec=grid_spec,
        out_shape=jax.ShapeDtypeStruct((T, D), F32),
        compiler_params=_cparams(("arbitrary",)),
        name="moe_combine",
    )(plan[2], x, y, route, g_final)


def moe_block(x, g, router, wg, wu, wd, g_final, *, final_norm, tm=512):
    T, D = x.shape
    tm = min(tm, T)
    router_pad = jnp.zeros((D, LANES), F32).at[:, :N_EXPERTS].set(router)
    route = moe_route(x, g, router_pad)
    plan = _moe_plan(route, T, tm)
    xs = moe_dispatch(x, plan, tm=tm)
    y = moe_experts(xs, g, plan, wg.astype(BF16), wu.astype(BF16), wd, tm=tm)
    return moe_combine(x, y, route, plan, g_final, final_norm=final_norm)


def _split_w_in(w):
    D = w.shape[0]
    n_fox = 3 * FOX_HEADS * HEAD_DIM
    n_dil = 3 * DIL_HEADS * HEAD_DIM
    n_nsa = (NSA_Q_HEADS + NSA_BRANCHES * 2 * NSA_KV_HEADS) * HEAD_DIM
    n_g = NSA_BRANCHES * NSA_Q_HEADS
    a0 = n_fox
    a1 = a0 + FOX_HEADS
    a2 = a1 + n_dil
    a3 = a2 + n_nsa
    a4 = a3 + n_g
    group_cols = DIL_HEADS_PER_GROUP * HEAD_DIM
    dil = w[:, a1:a2].reshape(D, 3, len(DIL_GROUPS), group_cols)
    dil_sets = [dil[:, :, gi].reshape(D, 3 * group_cols).astype(BF16) for gi in range(len(DIL_GROUPS))]
    main = jnp.concatenate([w[:, a4:].astype(BF16), w[:, :a0].astype(BF16), dil_sets[0],
                            w[:, a2:a3].astype(BF16)], axis=1)
    small = jnp.concatenate([w[:, a0:a1], w[:, a3:a4],
                             jnp.zeros((D, LANES - FOX_HEADS - n_g), w.dtype)], axis=1).astype(BF16)
    return main, dil_sets[1:], small


def mixing_block(x, B, S, norm_g, w_in, forget_bias, cmp_pe, cmp_w1, cmp_w2, wa, wb, wc, w_out, tables):
    T, D = x.shape
    w_main, w_dil, w_small = _split_w_in(w_in)
    g = norm_g.reshape(1, D)
    proj, h = rms_proj(x, g, w_main, BF16, tm=1024, tn=1024)
    small = slab_proj(h, w_small, F32, tm=1024, tn=LANES)[0]
    dil_views = [proj_view(h, w, DIL_GROUPS[gi + 1][1], tm=1024, tn=w.shape[1]) for gi, w in enumerate(w_dil)]

    gate_lanes = slice(SMALL_FOX_F, SMALL_FOX_F + FOX_HEADS)
    bias_row = jnp.zeros((1, LANES), F32).at[0, gate_lanes].set(forget_bias)
    c = logf_cumsum(small, bias_row, B, S)
    c_row = c[:, gate_lanes].T.reshape(FOX_HEADS, 1, T)
    fox_o = fox_attention(proj, c, c_row, B, S)

    dil = [dilated_group(proj, CB_DIL, tables["dil"][0], 0, B, S)]
    dil += [dilated_group(v, 0, tables["dil"][gi + 1], gi + 1, B, S) for gi, v in enumerate(dil_views)]
    dil_o = [d[0] for d in dil]
    dil_lse = [d[1] for d in dil]

    kvc = nsa_compress(proj, cmp_pe, cmp_w1.astype(BF16), cmp_w2.astype(BF16), B, S)
    cmp_o, sel = nsa_cmp_select(proj, kvc, B, S)
    slc_o = nsa_selected(proj, sel, tables["slc"], B, S, tq=tables["slc_tq"], tk=tables["slc_tk"])
    win_o = nsa_window(proj, tables["win"], B, S, tq=tables["win_tq"])

    merged = merge_branches(proj, small, fox_o, dil_o, dil_lse, cmp_o, slc_o, win_o,
                            wa.astype(BF16), wb.astype(BF16), wc.astype(BF16))
    return matmul_residual(merged, w_out.astype(BF16), x)


def bias_tables(rel_bias, S):
    slc_tq, slc_tk = min(256, S), min(512, S)
    win_tq = min(256, S)
    rel_c = rel_bias[:, DIL_HEADS:]
    return {
        "dil": dilated_bias_tables(rel_bias[:, :DIL_HEADS]),
        "slc": slc_bias_table(rel_c, S, slc_tq, slc_tk), "slc_tq": slc_tq, "slc_tk": slc_tk,
        "win": win_bias_table(rel_c, win_tq, min(WIN // win_tq + 1, 3)), "win_tq": win_tq,
    }


def kernel(x, rel_bias, norm_mix_g, norm_ffn_g, norm_final_g, w_in, fox_forget_bias, cmp_pe_k, cmp_w1_k, cmp_w2_k, cmp_pe_v, cmp_w1_v, cmp_w2_v, w_branch_a, w_branch_b, w_branch_c, w_out, ffn_w_gate, ffn_w_up, ffn_w_down, moe_router, moe_w_gate, moe_w_up, moe_w_down):
    B, S, D = x.shape
    if D != D_MODEL:
        raise ValueError(f"column-block layout is built for d_model={D_MODEL}, got {D}")
    T = B * S
    depth = w_in.shape[0]
    tables = bias_tables(rel_bias, S)
    g_final = norm_final_g.reshape(1, D)
    xt = x.reshape(T, D)
    for l in range(depth):
        xt = mixing_block(
            xt, B, S, norm_mix_g[l], w_in[l], fox_forget_bias[l],
            jnp.stack([cmp_pe_k[l], cmp_pe_v[l]]), jnp.stack([cmp_w1_k[l], cmp_w1_v[l]]),
            jnp.stack([cmp_w2_k[l], cmp_w2_v[l]]),
            w_branch_a[l], w_branch_b[l], w_branch_c[l], w_out[l], tables)
        g = norm_ffn_g[l].reshape(1, D)
        last = l == depth - 1
        j = l // 2
        if l % 2 == 0:
            xt = ffn(xt, g, ffn_w_gate[j], ffn_w_up[j], ffn_w_down[j], g_final, final_norm=last)
        else:
            xt = moe_block(xt, g, moe_router[j], moe_w_gate[j], moe_w_up[j], moe_w_down[j], g_final,
                           final_norm=last)
    return xt.reshape(B, S, D)
```

```python
import functools
import math

import jax
import jax.numpy as jnp
import numpy as np
from jax import lax
from jax.experimental import pallas as pl
from jax.experimental.pallas import tpu as pltpu

F32 = jnp.float32
BF16 = jnp.bfloat16
HIGHEST = lax.Precision.HIGHEST
NEG_INF = float("-inf")

LANES = 128
HEAD_DIM = 128
ATTN_SCALE = HEAD_DIM ** -0.5
RMS_EPS = 1e-6
VMEM_LIMIT_BYTES = 58 * 1024 * 1024

D_MODEL = 2048
FOX_HEADS = 8
DIL_GROUPS = ((128, 1), (512, 4), (2048, 16))
DIL_HEADS_PER_GROUP = 4
DIL_HEADS = DIL_HEADS_PER_GROUP * len(DIL_GROUPS)
DIL_TAPS = 128
NSA_Q_HEADS = 8
NSA_KV_HEADS = 2
NSA_GQA = NSA_Q_HEADS // NSA_KV_HEADS
NSA_BRANCHES = 3
CMP_BLOCK = 32
CMP_STRIDE = 16
CMP_HIDDEN = 256
SLC_BLOCK = 64
SLC_COUNT = 16
WIN = 512
REL_BUCKETS = 32
REL_MAX_EXACT = 16
REL_MAX_DIST = 2048
N_EXPERTS = 8
TOP_K = 2
N_BRANCHES = 3

CB_MERGE = 0
CB_FOX = CB_MERGE + N_BRANCHES * (D_MODEL // LANES)
CB_DIL = CB_FOX + 3 * FOX_HEADS
CB_NSAQ = CB_DIL + 3 * DIL_HEADS_PER_GROUP
CB_NSAKV = CB_NSAQ + NSA_Q_HEADS
CB_END = CB_NSAKV + NSA_BRANCHES * 2 * NSA_KV_HEADS
SMALL_FOX_F = 0
SMALL_NSA_G = FOX_HEADS
ROUTE_CHOICE = 0
ROUTE_WEIGHT = N_EXPERTS
MOE_ZERO_ROWS = 64


def _cparams(semantics):
    return pltpu.CompilerParams(dimension_semantics=semantics,
                                vmem_limit_bytes=VMEM_LIMIT_BYTES)


def _t5_bucket_np(dist):
    dist = np.maximum(dist, 0)
    d = np.maximum(dist, 1).astype(np.float32)
    log_ratio = np.log(d / np.float32(REL_MAX_EXACT)) / np.float32(math.log(REL_MAX_DIST / REL_MAX_EXACT))
    large = REL_MAX_EXACT + (log_ratio * np.float32(REL_BUCKETS - REL_MAX_EXACT)).astype(np.int32)
    large = np.minimum(large, REL_BUCKETS - 1)
    return np.where(dist < REL_MAX_EXACT, dist, large).astype(np.int32)


def _toeplitz_kernel(v_ref, o_ref, *, rows, cols):
    x = jnp.broadcast_to(v_ref[...], (rows, v_ref.shape[-1]))
    o_ref[...] = pltpu.roll(x, 0, 1, stride=1, stride_axis=0)[:, :cols]


def _toeplitz_blocks(fn, offsets, rows, cols):
    lx = -(-(rows + cols) // LANES) * LANES
    m = np.arange(lx)
    rel = np.where(m < cols, -m, lx - m)
    v = jnp.stack([fn(c + rel) for c in offsets]).astype(F32)
    n_off, H = v.shape[:2]
    return pl.pallas_call(
        functools.partial(_toeplitz_kernel, rows=rows, cols=cols),
        grid=(n_off, H),
        in_specs=[pl.BlockSpec((None, None, 1, lx), lambda o, h: (o, h, 0, 0))],
        out_specs=pl.BlockSpec((None, None, rows, cols), lambda o, h: (o, h, 0, 0)),
        out_shape=jax.ShapeDtypeStruct((n_off, H, rows, cols), F32),
        compiler_params=_cparams(("parallel", "parallel")),
        name="toeplitz_table",
    )(v.reshape(n_off, H, 1, lx))


def _rms_rows(x, g):
    inv = lax.rsqrt(jnp.mean(x * x, axis=-1, keepdims=True) + RMS_EPS)
    return (x * inv) * g


def _rms_proj_kernel(x_ref, g_ref, w_ref, o_ref, h_ref, *, ncb):
    @pl.when(pl.program_id(1) == 0)
    def _():
        h_ref[...] = _rms_rows(x_ref[...], g_ref[...]).astype(BF16)

    res = jnp.dot(h_ref[...], w_ref[...], preferred_element_type=F32)
    for c in range(ncb):
        o_ref[c] = res[:, c * LANES:(c + 1) * LANES].astype(o_ref.dtype)


def rms_proj(x, g, w, out_dtype, *, tm, tn):
    T, D = x.shape
    N = w.shape[1]
    tm, tn = min(tm, T), min(tn, N)
    ncb = tn // LANES
    return pl.pallas_call(
        functools.partial(_rms_proj_kernel, ncb=ncb),
        grid=(T // tm, N // tn),
        in_specs=[pl.BlockSpec((tm, D), lambda i, j: (i, 0)),
                  pl.BlockSpec((1, D), lambda i, j: (0, 0)),
                  pl.BlockSpec((D, tn), lambda i, j: (0, j))],
        out_specs=[pl.BlockSpec((ncb, tm, LANES), lambda i, j: (j, i, 0)),
                   pl.BlockSpec((tm, D), lambda i, j: (i, 0))],
        out_shape=[jax.ShapeDtypeStruct((N // LANES, T, LANES), out_dtype),
                   jax.ShapeDtypeStruct((T, D), BF16)],
        compiler_params=_cparams(("parallel", "arbitrary")),
        name="rms_proj",
    )(x, g, w)


def _proj_kernel(h_ref, w_ref, o_ref, *, ncb):
    res = jnp.dot(h_ref[...], w_ref[...], preferred_element_type=F32)
    for c in range(ncb):
        o_ref[c] = res[:, c * LANES:(c + 1) * LANES].astype(o_ref.dtype)


def slab_proj(h, w, out_dtype, *, tm, tn):
    T, D = h.shape
    N = w.shape[1]
    tm, tn = min(tm, T), min(tn, N)
    ncb = tn // LANES
    return pl.pallas_call(
        functools.partial(_proj_kernel, ncb=ncb),
        grid=(T // tm, N // tn),
        in_specs=[pl.BlockSpec((tm, D), lambda i, j: (i, 0)),
                  pl.BlockSpec((D, tn), lambda i, j: (0, j))],
        out_specs=pl.BlockSpec((ncb, tm, LANES), lambda i, j: (j, i, 0)),
        out_shape=jax.ShapeDtypeStruct((N // LANES, T, LANES), out_dtype),
        compiler_params=_cparams(("parallel", "parallel")),
        name="slab_proj",
    )(h, w)


def _proj_view_kernel(h_ref, w_ref, o_ref, res_ref, *, ncb, dil):
    res = jnp.dot(h_ref[...], w_ref[...], preferred_element_type=F32)
    rows = res_ref.shape[1] // dil
    for c in range(ncb):
        res_ref[c] = res[:, c * LANES:(c + 1) * LANES]
        for r in range(dil):
            part = res_ref[c, pl.ds(r, rows, stride=dil), :]
            o_ref[c, :, r * LANES:(r + 1) * LANES] = part.astype(o_ref.dtype)


def proj_view(h, w, dil, *, tm, tn):
    T, D = h.shape
    N = w.shape[1]
    tm, tn = min(tm, T), min(tn, N)
    ncb = tn // LANES
    return pl.pallas_call(
        functools.partial(_proj_view_kernel, ncb=ncb, dil=dil),
        grid=(T // tm, N // tn),
        in_specs=[pl.BlockSpec((tm, D), lambda i, j: (i, 0)),
                  pl.BlockSpec((D, tn), lambda i, j: (0, j))],
        out_specs=pl.BlockSpec((ncb, tm // dil, dil * LANES), lambda i, j: (j, i, 0)),
        out_shape=jax.ShapeDtypeStruct((N // LANES, T // dil, dil * LANES), BF16),
        scratch_shapes=[pltpu.VMEM((ncb, tm, LANES), F32)],
        compiler_params=_cparams(("parallel", "parallel")),
        name="proj_view",
    )(h, w)


CUMSUM_BLOCK = 256


def _logf_cumsum_kernel(f_ref, b_ref, c_ref, *, nblk):
    row = lax.broadcasted_iota(jnp.int32, (CUMSUM_BLOCK, CUMSUM_BLOCK), 0)
    col = lax.broadcasted_iota(jnp.int32, (CUMSUM_BLOCK, CUMSUM_BLOCK), 1)
    tri = jnp.where(col <= row, 1.0, 0.0).astype(F32)

    def body(i, carry):
        sl = pl.ds(pl.multiple_of(i * CUMSUM_BLOCK, CUMSUM_BLOCK), CUMSUM_BLOCK)
        z = f_ref[sl, :] + b_ref[...]
        logf = jnp.minimum(z, 0.0) - jnp.log1p(jnp.exp(-jnp.abs(z)))
        cs = jnp.dot(tri, logf, preferred_element_type=F32, precision=HIGHEST) + carry
        c_ref[sl, :] = cs
        return cs[CUMSUM_BLOCK - 1:CUMSUM_BLOCK, :]

    lax.fori_loop(0, nblk, body, jnp.zeros((1, LANES), F32))


def logf_cumsum(small, bias_row, B, S):
    T = B * S
    return pl.pallas_call(
        functools.partial(_logf_cumsum_kernel, nblk=S // CUMSUM_BLOCK),
        grid=(B,),
        in_specs=[pl.BlockSpec((S, LANES), lambda b: (b, 0)),
                  pl.BlockSpec((1, LANES), lambda b: (0, 0))],
        out_specs=pl.BlockSpec((S, LANES), lambda b: (b, 0)),
        out_shape=jax.ShapeDtypeStruct((T, LANES), F32),
        compiler_params=_cparams(("parallel",)),
        name="logf_cumsum",
    )(small, bias_row)


FLASH_ROW_CHUNK = 128


LOG2E = math.log2(math.e)


def _lane_tile(x, n):
    return jnp.concatenate([x] * n, axis=1)


def _flash_update(s2, v, m_ref, l_ref, acc_ref, row_shift=None):
    m_old = m_ref[...]
    m_cur = jnp.max(s2, axis=-1, keepdims=True)
    if row_shift is not None:
        m_cur = m_cur + row_shift
    m_new = jnp.maximum(m_old, m_cur)
    m_safe = jnp.where(m_new == NEG_INF, 0.0, m_new)
    alpha = jnp.exp2(m_old - m_safe)
    origin = m_safe if row_shift is None else m_safe - row_shift
    p = jnp.exp2(s2 - _lane_tile(origin, s2.shape[1] // LANES))
    l_ref[...] = alpha * l_ref[...] + jnp.sum(p, axis=-1, keepdims=True)
    acc_ref[...] = alpha * acc_ref[...] + jnp.dot(p.astype(BF16), v, preferred_element_type=F32)
    m_ref[...] = m_new


def _flash_init(m_ref, l_ref, acc_ref):
    m_ref[...] = jnp.full(m_ref.shape, NEG_INF, F32)
    l_ref[...] = jnp.zeros(l_ref.shape, F32)
    acc_ref[...] = jnp.zeros(acc_ref.shape, F32)


def _flash_result(l_ref, acc_ref):
    l = l_ref[...]
    return acc_ref[...] / jnp.where(l > 0.0, l, 1.0)


def _qk(q, k):
    return lax.dot_general(q, k, (((1,), (1,)), ((), ())), preferred_element_type=F32)


def _fox_kernel(qi_ref, ki_ref, q_ref, k_ref, v_ref, cq_ref, ck_ref, o_ref, m_ref, l_ref, acc_ref, cqb_ref,
                *, tq, tk, rc, nb):
    p = pl.program_id(1)
    qi, ki = qi_ref[p], ki_ref[p]

    @pl.when(ki == 0)
    def _():
        _flash_init(m_ref, l_ref, acc_ref)
        lane = lax.broadcasted_iota(jnp.int32, cq_ref.shape, 2)
        mine = jnp.sum(jnp.where(lane == SMALL_FOX_F + pl.program_id(0), cq_ref[...], 0.0),
                       axis=-1, keepdims=True)
        cqb_ref[...] = jnp.broadcast_to(mine * LOG2E, cqb_ref.shape)

    chunks = [pl.ds(c * rc, rc) for c in range(tq // rc)]

    def tile(causal_mask):
        widths = [min(tk, (c + 1) * rc) if causal_mask and tq == tk else tk for c in range(tq // rc)]

        def products(b):
            return [_qk(q_ref[b, rows, :], k_ref[b, 0:w, :]) for rows, w in zip(chunks, widths)]

        def softmax_pv(b, qk):
            ck = ck_ref[b] * LOG2E
            for c, (rows, w) in enumerate(zip(chunks, widths)):
                s = qk[c] * (ATTN_SCALE * LOG2E) - ck[:, 0:w]
                if causal_mask:
                    qpos = qi * tq + c * rc + lax.broadcasted_iota(jnp.int32, (rc, w), 0)
                    kpos = ki * tk + lax.broadcasted_iota(jnp.int32, (rc, w), 1)
                    s = jnp.where(kpos <= qpos, s, NEG_INF)
                _flash_update(s, v_ref[b, 0:w, :], m_ref.at[b, rows], l_ref.at[b, rows], acc_ref.at[b, rows],
                              row_shift=cqb_ref[b, rows, :])

        ahead = products(0)
        for b in range(nb):
            qk = ahead
            if b + 1 < nb:
                ahead = products(b + 1)
            softmax_pv(b, qk)

    crosses_diagonal = (ki + 1) * tk - 1 > qi * tq

    @pl.when(crosses_diagonal)
    def _():
        tile(True)

    @pl.when(jnp.logical_not(crosses_diagonal))
    def _():
        tile(False)

    @pl.when(ki == ((qi + 1) * tq - 1) // tk)
    def _():
        o_ref[...] = _flash_result(l_ref, acc_ref).astype(o_ref.dtype)


def _causal_pairs(nq, tq, tk):
    qs, ks = [], []
    for qi in range(nq):
        for ki in range(((qi + 1) * tq - 1) // tk + 1):
            qs.append(qi)
            ks.append(ki)
    return jnp.asarray(qs, jnp.int32), jnp.asarray(ks, jnp.int32)


def fox_attention(proj, c_packed, c_row, B, S, *, tq=512, tk=512):
    tq, tk = min(tq, S), min(tk, S)
    H = FOX_HEADS
    T = B * S
    qis, kis = _causal_pairs(S // tq, tq, tk)
    proj4 = proj.reshape(proj.shape[0], B, S, LANES)
    grid_spec = pltpu.PrefetchScalarGridSpec(
        num_scalar_prefetch=2,
        grid=(H, int(qis.shape[0])),
        in_specs=[
            pl.BlockSpec((None, B, tq, LANES), lambda h, p, qi, ki: (CB_FOX + h, 0, qi[p], 0)),
            pl.BlockSpec((None, B, tk, LANES), lambda h, p, qi, ki: (CB_FOX + H + h, 0, ki[p], 0)),
            pl.BlockSpec((None, B, tk, LANES), lambda h, p, qi, ki: (CB_FOX + 2 * H + h, 0, ki[p], 0)),
            pl.BlockSpec((B, tq, LANES), lambda h, p, qi, ki: (0, qi[p], 0)),
            pl.BlockSpec((None, B, 1, tk), lambda h, p, qi, ki: (h, 0, 0, ki[p])),
        ],
        out_specs=pl.BlockSpec((None, B, tq, LANES), lambda h, p, qi, ki: (h, 0, qi[p], 0)),
        scratch_shapes=[pltpu.VMEM((B, tq, LANES), F32)] * 4,
    )
    out = pl.pallas_call(
        functools.partial(_fox_kernel, tq=tq, tk=tk, rc=min(FLASH_ROW_CHUNK, tq), nb=B),
        grid_spec=grid_spec,
        out_shape=jax.ShapeDtypeStruct((H, B, S, LANES), BF16),
        compiler_params=_cparams(("parallel", "arbitrary")),
        name="fox_attention",
    )(qis, kis, proj4, proj4, proj4, c_packed.reshape(B, S, LANES), c_row.reshape(H, B, 1, S))
    return out.reshape(H, T, LANES)


DIL_ROWS_PER_STEP = 512


def _dil_kernel(q_ref, kp_ref, kc_ref, vp_ref, vc_ref, bias_ref, o_ref, lse_ref, *, dil, tu, nsub, hp):
    ui = pl.program_id(2)
    kcol = lax.broadcasted_iota(jnp.int32, (tu, 2 * tu), 1)
    first_ok = jnp.logical_or(ui > 0, kcol >= tu)
    units = [(c, slice(r * LANES, (r + 1) * LANES)) for c in range(nsub) for r in range(dil)]

    def window(prev_ref, cur_ref, h, c, sl):
        if c == 0:
            return jnp.concatenate([prev_ref[h, :, sl], cur_ref[h, 0:tu, sl]], axis=0)
        return cur_ref[h, (c - 1) * tu:(c + 1) * tu, sl]

    def head(h, carry):
        bias = bias_ref[h]
        bias_first = jnp.where(first_ok, bias, NEG_INF)
        qk = [_qk(q_ref[h, c * tu:(c + 1) * tu, sl], window(kp_ref, kc_ref, h, c, sl)) for c, sl in units]
        for i, (c, sl) in enumerate(units):
            rows = slice(c * tu, (c + 1) * tu)
            s = qk[i] * ATTN_SCALE + (bias_first if c == 0 else bias)
            m = jnp.max(s, axis=-1, keepdims=True)
            e = jnp.exp(s - m)
            den = jnp.sum(e, axis=-1, keepdims=True)
            o = jnp.dot(e.astype(BF16), window(vp_ref, vc_ref, h, c, sl), preferred_element_type=F32) / den
            o_ref[h, rows, sl] = o.astype(o_ref.dtype)
            lse_ref[h, rows, sl] = jnp.broadcast_to(m + jnp.log(den), (tu, LANES))
        return carry

    lax.fori_loop(0, hp, head, 0)


def dilated_group(view, c0, bias_tbl, group, B, S):
    dil = DIL_GROUPS[group][1]
    tu = DIL_TAPS
    Hg = DIL_HEADS_PER_GROUP
    T = B * S
    rows = min(DIL_ROWS_PER_STEP, S // dil)
    nsub = rows // tu
    nstep = S // dil // rows
    hp = max(1, Hg // max(1, dil // 4))
    cq, ck, cv = c0 // hp, (c0 + Hg) // hp, (c0 + 2 * Hg) // hp
    cur_blk = (hp, rows, dil * LANES)
    prev_blk = (hp, tu, dil * LANES)
    prev = lambda c: (lambda b, h, u: (c + h, jnp.maximum((b * nstep + u) * nsub - 1, 0), 0))
    cur = lambda c: (lambda b, h, u: (c + h, b * nstep + u, 0))
    o, lse = pl.pallas_call(
        functools.partial(_dil_kernel, dil=dil, tu=tu, nsub=nsub, hp=hp),
        grid=(B, Hg // hp, nstep),
        in_specs=[pl.BlockSpec(cur_blk, cur(cq)),
                  pl.BlockSpec(prev_blk, prev(ck)), pl.BlockSpec(cur_blk, cur(ck)),
                  pl.BlockSpec(prev_blk, prev(cv)), pl.BlockSpec(cur_blk, cur(cv)),
                  pl.BlockSpec((hp, tu, 2 * tu), lambda b, h, u: (h, 0, 0))],
        out_specs=[pl.BlockSpec(cur_blk, cur(0)), pl.BlockSpec(cur_blk, cur(0))],
        out_shape=[jax.ShapeDtypeStruct((Hg, T // dil, dil * LANES), BF16),
                   jax.ShapeDtypeStruct((Hg, T // dil, dil * LANES), F32)],
        compiler_params=_cparams(("parallel", "parallel", "arbitrary")),
        name=f"dilated_group{group}",
    )(view, view, view, view, view, bias_tbl)
    return o, lse


def dilated_bias_tables(rel_bias_b):
    tu = DIL_TAPS
    tables = []
    for g, (_, dil) in enumerate(DIL_GROUPS):
        heads = rel_bias_b[:, g * DIL_HEADS_PER_GROUP:(g + 1) * DIL_HEADS_PER_GROUP]

        def fn(taps, heads=heads, dil=dil):
            valid = (taps >= 0) & (taps <= DIL_TAPS)
            vals = heads[_t5_bucket_np(np.where(valid, taps, 0) * dil)].T
            return jnp.where(valid[None, :], vals, NEG_INF)

        tables.append(_toeplitz_blocks(fn, [tu], tu, 2 * tu)[0])
    return tables


def _compress_kernel(x_ref, pelo_ref, pehi_ref, w1a_ref, w1b_ref, w2_ref, o_ref, *, nchunk):
    x = x_ref[...].astype(F32)
    u0 = jnp.dot((x + pelo_ref[...]).astype(BF16), w1a_ref[...], preferred_element_type=F32)
    u1 = jnp.dot((x + pehi_ref[...]).astype(BF16), w1b_ref[...], preferred_element_type=F32)
    pre = u0 + pltpu.roll(u1, nchunk - 1, 0)
    hid = jax.nn.gelu(pre)
    out = jnp.dot(hid.astype(BF16), w2_ref[...], preferred_element_type=F32)
    row = lax.broadcasted_iota(jnp.int32, out.shape, 0)
    o_ref[...] = jnp.where(row < nchunk - 1, out, 0.0).astype(o_ref.dtype)


def nsa_compress(proj, pe, w1, w2, B, S):
    nchunk = S // CMP_STRIDE
    half = CMP_STRIDE * HEAD_DIM
    Hkv = NSA_KV_HEADS
    x = proj[CB_NSAKV:CB_NSAKV + 2 * Hkv].reshape(2, Hkv, B, nchunk, half)
    pe_lo = pe[:, :CMP_STRIDE].reshape(2, 1, half)
    pe_hi = pe[:, CMP_STRIDE:].reshape(2, 1, half)
    return pl.pallas_call(
        functools.partial(_compress_kernel, nchunk=nchunk),
        grid=(2, B, Hkv),
        in_specs=[pl.BlockSpec((None, None, None, nchunk, half), lambda t, b, h: (t, h, b, 0, 0)),
                  pl.BlockSpec((None, 1, half), lambda t, b, h: (t, 0, 0)),
                  pl.BlockSpec((None, 1, half), lambda t, b, h: (t, 0, 0)),
                  pl.BlockSpec((None, half, CMP_HIDDEN), lambda t, b, h: (t, 0, 0)),
                  pl.BlockSpec((None, half, CMP_HIDDEN), lambda t, b, h: (t, 1, 0)),
                  pl.BlockSpec((None, CMP_HIDDEN, HEAD_DIM), lambda t, b, h: (t, 0, 0))],
        out_specs=pl.BlockSpec((None, None, None, nchunk, HEAD_DIM), lambda t, b, h: (t, b, h, 0, 0)),
        out_shape=jax.ShapeDtypeStruct((2, B, Hkv, nchunk, HEAD_DIM), BF16),
        compiler_params=_cparams(("parallel", "parallel", "parallel")),
        name="nsa_compress",
    )(x, pe_lo, pe_hi, w1, w1, w2)


def _masked_softmax(s, mask, axis):
    s = jnp.where(mask, s, NEG_INF)
    m = jnp.max(s, axis=axis, keepdims=True)
    m = jnp.where(m == NEG_INF, 0.0, m)
    e = jnp.exp(s - m)
    den = jnp.sum(e, axis=axis, keepdims=True)
    return e / jnp.where(den > 0.0, den, 1.0)


def _cmp_select_kernel(q_ref, kc_ref, vc_ref, mt_ref, o_ref, sel_ref, *, tq, nchunk, n_slc, n_sel):
    q0 = pl.program_id(2) * tq
    kc = kc_ref[...]
    vc = vc_ref[...]
    pos_c = q0 + lax.broadcasted_iota(jnp.int32, (nchunk, tq), 1)
    end_c = lax.broadcasted_iota(jnp.int32, (nchunk, tq), 0) * CMP_STRIDE + (CMP_BLOCK - 1)
    vis_c = end_c <= pos_c
    imp = jnp.zeros((nchunk, tq), F32)
    qk = [_qk(kc, q_ref[g]) for g in range(NSA_GQA)]
    for g in range(NSA_GQA):
        p = _masked_softmax(qk[g] * ATTN_SCALE, vis_c, 0)
        o = lax.dot_general(p.astype(BF16), vc, (((0,), (0,)), ((), ())), preferred_element_type=F32)
        o_ref[g] = o.astype(o_ref.dtype)
        imp = imp + p
    p_slc = jnp.dot(mt_ref[...], imp, preferred_element_type=F32, precision=HIGHEST)

    blk = lax.broadcasted_iota(jnp.int32, (n_slc, tq), 0)
    cur = (q0 + lax.broadcasted_iota(jnp.int32, (n_slc, tq), 1)) // SLC_BLOCK
    forced = (blk == 0) | (blk == cur) | (blk == cur - 1)
    allowed = blk <= cur
    score = jnp.where(forced, 1e30, jnp.where(allowed, p_slc, -1.0))
    chosen = jnp.zeros((n_slc, tq), F32)
    for _ in range(n_sel):
        top = jnp.max(score, axis=0, keepdims=True)
        first = jnp.min(jnp.where(score == top, blk, n_slc), axis=0, keepdims=True)
        hit = blk == first
        chosen = jnp.where(hit, 1.0, chosen)
        score = jnp.where(hit, -2.0, score)
    mask = jnp.concatenate([jnp.where(allowed, chosen, 0.0), jnp.zeros((LANES - n_slc, tq), F32)], axis=0)
    sel_ref[...] = mask.T.astype(sel_ref.dtype)


def nsa_cmp_select(proj, kvc, B, S, *, tq=1024):
    tq = min(tq, S)
    nq = S // tq
    T = B * S
    nchunk = S // CMP_STRIDE
    n_slc = S // SLC_BLOCK
    if n_slc > LANES:
        raise ValueError("selection mask is packed into one lane width: needs S <= 64*128")
    n_sel = min(SLC_COUNT, n_slc)
    ratio, n_inner = SLC_BLOCK // CMP_STRIDE, CMP_BLOCK // CMP_STRIDE
    mt = np.zeros((n_slc, nchunk), np.float32)
    for j in range(n_slc):
        for m in range(ratio):
            for n in range(n_inner):
                c = ratio * j + m - n
                if 0 <= c < nchunk - 1:
                    mt[j, c] += 1.0
    G = NSA_GQA
    kv_spec = lambda t: pl.BlockSpec((None, None, None, nchunk, HEAD_DIM), lambda b, h, i: (t, b, h, 0, 0))
    return pl.pallas_call(
        functools.partial(_cmp_select_kernel, tq=tq, nchunk=nchunk, n_slc=n_slc, n_sel=n_sel),
        grid=(B, NSA_KV_HEADS, nq),
        in_specs=[pl.BlockSpec((G, tq, LANES), lambda b, h, i: (CB_NSAQ // G + h, b * nq + i, 0)),
                  kv_spec(0), kv_spec(1),
                  pl.BlockSpec((n_slc, nchunk), lambda b, h, i: (0, 0))],
        out_specs=[pl.BlockSpec((G, tq, LANES), lambda b, h, i: (h, b * nq + i, 0)),
                   pl.BlockSpec((None, None, tq, LANES), lambda b, h, i: (b, h, i, 0))],
        out_shape=[jax.ShapeDtypeStruct((NSA_Q_HEADS, T, LANES), BF16),
                   jax.ShapeDtypeStruct((B, NSA_KV_HEADS, S, LANES), BF16)],
        compiler_params=_cparams(("parallel", "parallel", "arbitrary")),
        name="nsa_cmp_select",
    )(proj, kvc, kvc, jnp.asarray(mt))


def _slc_kernel(qi_ref, ki_ref, q_ref, k_ref, v_ref, sel_ref, bias_ref, o_ref, m_ref, l_ref, acc_ref,
                *, tq, tk, nb):
    p = pl.program_id(1)
    qi, ki = qi_ref[p], ki_ref[p]

    @pl.when(ki == 0)
    def _():
        _flash_init(m_ref, l_ref, acc_ref)

    blk_of_key = ki * (tk // SLC_BLOCK) + lax.broadcasted_iota(jnp.int32, (LANES, tk), 1) // SLC_BLOCK
    expand = jnp.where(lax.broadcasted_iota(jnp.int32, (LANES, tk), 0) == blk_of_key, 1.0, 0.0).astype(BF16)

    def step(b, causal_mask):
        picked = jnp.dot(sel_ref[b], expand, preferred_element_type=F32)
        keep = picked > 0.5
        if causal_mask:
            qpos = qi * tq + lax.broadcasted_iota(jnp.int32, (tq, tk), 0)
            kpos = ki * tk + lax.broadcasted_iota(jnp.int32, (tq, tk), 1)
            keep = jnp.where(kpos <= qpos, picked, 0.0) > 0.5
        k, v = k_ref[b], v_ref[b]
        qk = [_qk(q_ref[g, b], k) for g in range(NSA_GQA)]
        for g in range(NSA_GQA):
            s = qk[g] * (ATTN_SCALE * LOG2E) + bias_ref[g]
            s = jnp.where(keep, s, NEG_INF)
            _flash_update(s, v, m_ref.at[g, b], l_ref.at[g, b], acc_ref.at[g, b])

    crosses_diagonal = (ki + 1) * tk - 1 > qi * tq

    @pl.when(crosses_diagonal)
    def _():
        _for_range(0, nb, lambda b: step(b, True))

    @pl.when(jnp.logical_not(crosses_diagonal))
    def _():
        _for_range(0, nb, lambda b: step(b, False))

    @pl.when(ki == ((qi + 1) * tq - 1) // tk)
    def _():
        o_ref[...] = _flash_result(l_ref, acc_ref).astype(o_ref.dtype)


def slc_bias_table(rel_bias_c, S, tq, tk):
    buckets = _t5_bucket_np(np.arange(S + tk))
    not_last = np.nonzero(buckets != REL_BUCKETS - 1)[0]
    far_start = int(not_last[-1]) + 1 if not_last.size else 0
    n_delta = min(S // tq, -(-(far_start + tk - 1) // tq) + 1)

    def fn(d):
        return rel_bias_c[_t5_bucket_np(d)].T * LOG2E

    return _toeplitz_blocks(fn, [dl * tq for dl in range(n_delta)], tq, tk)


def nsa_selected(proj, sel, bias_tbl, B, S, *, tq=256, tk=512):
    tq, tk = min(tq, S), min(tk, S)
    T = B * S
    G = NSA_GQA
    ck = CB_NSAKV + (1 * 2 + 0) * NSA_KV_HEADS
    cv = CB_NSAKV + (1 * 2 + 1) * NSA_KV_HEADS
    qis, kis = _causal_pairs(S // tq, tq, tk)
    n_delta = bias_tbl.shape[0]
    proj4 = proj.reshape(proj.shape[0], B, S, LANES)
    grid_spec = pltpu.PrefetchScalarGridSpec(
        num_scalar_prefetch=2,
        grid=(NSA_KV_HEADS, int(qis.shape[0])),
        in_specs=[
            pl.BlockSpec((G, B, tq, LANES), lambda h, p, qi, ki: (CB_NSAQ // G + h, 0, qi[p], 0)),
            pl.BlockSpec((None, B, tk, LANES), lambda h, p, qi, ki: (ck + h, 0, ki[p], 0)),
            pl.BlockSpec((None, B, tk, LANES), lambda h, p, qi, ki: (cv + h, 0, ki[p], 0)),
            pl.BlockSpec((B, None, tq, LANES), lambda h, p, qi, ki: (0, h, qi[p], 0)),
            pl.BlockSpec((None, G, tq, tk),
                         lambda h, p, qi, ki: (jnp.minimum(qi[p] - ki[p] * (tk // tq), n_delta - 1), h, 0, 0)),
        ],
        out_specs=pl.BlockSpec((G, B, tq, LANES), lambda h, p, qi, ki: (h, 0, qi[p], 0)),
        scratch_shapes=[pltpu.VMEM((G, B, tq, LANES), F32)] * 3,
    )
    out = pl.pallas_call(
        functools.partial(_slc_kernel, tq=tq, tk=tk, nb=B),
        grid_spec=grid_spec,
        out_shape=jax.ShapeDtypeStruct((NSA_Q_HEADS, B, S, LANES), BF16),
        compiler_params=_cparams(("parallel", "arbitrary")),
        name="nsa_selected",
    )(qis, kis, proj4, proj4, proj4, sel, bias_tbl)
    return out.reshape(NSA_Q_HEADS, T, LANES)


def _win_kernel(q_ref, k0_ref, k1_ref, k2_ref, v0_ref, v1_ref, v2_ref, bias_ref, o_ref, *, tq, nkb):
    qi = pl.program_id(2)
    k = jnp.concatenate([r[...] for r in (k0_ref, k1_ref, k2_ref)][-nkb:], axis=0)
    v = jnp.concatenate([r[...] for r in (v0_ref, v1_ref, v2_ref)][-nkb:], axis=0)
    kpos = (qi - (nkb - 1)) * tq + lax.broadcasted_iota(jnp.int32, (tq, nkb * tq), 1)
    qk = [_qk(q_ref[g], k) for g in range(NSA_GQA)]
    for g in range(NSA_GQA):
        s = qk[g] * ATTN_SCALE + bias_ref[g]
        p = _masked_softmax(s, kpos >= 0, -1)
        o_ref[g] = jnp.dot(p.astype(BF16), v, preferred_element_type=F32).astype(o_ref.dtype)


def win_bias_table(rel_bias_c, tq, nkb):
    def fn(dist):
        valid = (dist >= 0) & (dist < WIN)
        return jnp.where(valid[None, :], rel_bias_c[_t5_bucket_np(dist)].T, NEG_INF)

    return _toeplitz_blocks(fn, [(nkb - 1) * tq], tq, nkb * tq)[0]


def nsa_window(proj, bias_tbl, B, S, *, tq=256):
    tq = min(tq, S)
    nq = S // tq
    T = B * S
    G = NSA_GQA
    nkb = min(WIN // tq + 1, 3)
    ck = CB_NSAKV + (2 * 2 + 0) * NSA_KV_HEADS
    cv = CB_NSAKV + (2 * 2 + 1) * NSA_KV_HEADS
    kv = lambda c, back: pl.BlockSpec(
        (None, tq, LANES), lambda b, h, i: (c + h, b * nq + jnp.maximum(i - back, 0), 0))
    return pl.pallas_call(
        functools.partial(_win_kernel, tq=tq, nkb=nkb),
        grid=(B, NSA_KV_HEADS, nq),
        in_specs=[pl.BlockSpec((G, tq, LANES), lambda b, h, i: (CB_NSAQ // G + h, b * nq + i, 0)),
                  kv(ck, 2), kv(ck, 1), kv(ck, 0), kv(cv, 2), kv(cv, 1), kv(cv, 0),
                  pl.BlockSpec((G, tq, nkb * tq), lambda b, h, i: (h, 0, 0))],
        out_specs=pl.BlockSpec((G, tq, LANES), lambda b, h, i: (h, b * nq + i, 0)),
        out_shape=jax.ShapeDtypeStruct((NSA_Q_HEADS, T, LANES), BF16),
        compiler_params=_cparams(("parallel", "parallel", "arbitrary")),
        name="nsa_window",
    )(proj, proj, proj, proj, proj, proj, proj, bias_tbl)


def _sigmoid(x):
    return 1.0 / (1.0 + jnp.exp(-x))


def _merge_kernel(fox_ref, d0_ref, d1_ref, d2_ref, l0_ref, l1_ref, l2_ref, cmp_ref, slc_ref, win_ref, small_ref,
                  g0_ref, g1_ref, g2_ref, wa_ref, wb_ref, wc_ref, wo_ref, x_ref, o_ref, nat_ref, *, ncb):
    tm = o_ref.shape[0]

    def natural(ref, h, dil, slot):
        if dil == 1:
            return ref[h].astype(F32)
        for r in range(dil):
            nat_ref[slot, pl.ds(r, tm // dil, stride=dil), :] = ref[h, :, r * LANES:(r + 1) * LANES].astype(F32)
        return nat_ref[slot]

    ya = jnp.concatenate([fox_ref[h] for h in range(FOX_HEADS)], axis=1)
    yb = []
    for h in range(DIL_HEADS_PER_GROUP):
        outs, lse = [], []
        for g, (o_g, l_g) in enumerate(((d0_ref, l0_ref), (d1_ref, l1_ref), (d2_ref, l2_ref))):
            dil = DIL_GROUPS[g][1]
            slot = (h * len(DIL_GROUPS) + g) * 2
            outs.append(natural(o_g, h, dil, slot))
            lse.append(natural(l_g, h, dil, slot + 1))
        top = jnp.maximum(jnp.maximum(lse[0], lse[1]), lse[2])
        w = [jnp.exp(x - top) for x in lse]
        tot = w[0] + w[1] + w[2]
        y = sum((w[g] / tot) * outs[g] for g in range(len(DIL_GROUPS)))
        yb.append(y.astype(BF16))
    gates = _sigmoid(small_ref[...])
    yc = []
    for h in range(NSA_Q_HEADS):
        y = jnp.zeros(cmp_ref.shape[1:], F32)
        for br, ref in enumerate((cmp_ref, slc_ref, win_ref)):
            col = SMALL_NSA_G + br * NSA_Q_HEADS + h
            y = y + gates[:, col:col + 1] * ref[h].astype(F32)
        yc.append(y.astype(BF16))

    def gate(ref):
        return _sigmoid(jnp.concatenate([ref[c] for c in range(ncb)], axis=1).astype(F32))

    merged = (gate(g0_ref) * jnp.dot(ya, wa_ref[...], preferred_element_type=F32)
              + gate(g1_ref) * jnp.dot(jnp.concatenate(yb, axis=1), wb_ref[...], preferred_element_type=F32)
              + gate(g2_ref) * jnp.dot(jnp.concatenate(yc, axis=1), wc_ref[...], preferred_element_type=F32))
    o_ref[...] = x_ref[...] + jnp.dot(merged.astype(BF16), wo_ref[...], preferred_element_type=F32)


def merge_branches(x, proj, small, fox_o, dil_o, dil_lse, cmp_o, slc_o, win_o, wa, wb, wc, w_out, *, tm=256):
    T, D = x.shape
    tm = min(tm, T)
    ncb = D // LANES
    Hg = DIL_HEADS_PER_GROUP
    heads = lambda n: pl.BlockSpec((n, tm, LANES), lambda i: (0, i, 0))
    gate = lambda b: pl.BlockSpec((ncb, tm, LANES), lambda i: (CB_MERGE // ncb + b, i, 0))
    dil = [pl.BlockSpec((Hg, tm // d, d * LANES), lambda i: (0, i, 0)) for _, d in DIL_GROUPS]
    wspec = lambda w: pl.BlockSpec(w.shape, lambda i: (0, 0), pipeline_mode=pl.Buffered(1))
    return pl.pallas_call(
        functools.partial(_merge_kernel, ncb=ncb),
        grid=(T // tm,),
        in_specs=[heads(FOX_HEADS), *dil, *dil, heads(NSA_Q_HEADS), heads(NSA_Q_HEADS), heads(NSA_Q_HEADS),
                  pl.BlockSpec((tm, LANES), lambda i: (i, 0)),
                  gate(0), gate(1), gate(2), wspec(wa), wspec(wb), wspec(wc), wspec(w_out),
                  pl.BlockSpec((tm, D), lambda i: (i, 0))],
        out_specs=pl.BlockSpec((tm, D), lambda i: (i, 0)),
        out_shape=jax.ShapeDtypeStruct((T, D), F32),
        scratch_shapes=[pltpu.VMEM((2 * Hg * len(DIL_GROUPS), tm, LANES), F32)],
        compiler_params=_cparams(("parallel",)),
        name="merge_branches",
    )(fox_o, *dil_o, *dil_lse, cmp_o, slc_o, win_o, small, proj, proj, proj, wa, wb, wc, w_out, x)


def _silu(x):
    return x * _sigmoid(x)


def _swiglu_partial(h, wg_ref, wu_ref, wd_ref):
    act = _silu(jnp.dot(h, wg_ref[...].astype(BF16), preferred_element_type=F32)) * \
        jnp.dot(h, wu_ref[...].astype(BF16), preferred_element_type=F32)
    return jnp.dot(act.astype(BF16), wd_ref[...].astype(BF16), preferred_element_type=F32)


def _ffn_kernel(x_ref, g_ref, wg_ref, wu_ref, wd_ref, gf_ref, o_ref, h_ref, *, final_norm):
    f = pl.program_id(1)

    @pl.when(f == 0)
    def _():
        h_ref[...] = _rms_rows(x_ref[...], g_ref[...]).astype(BF16)
        o_ref[...] = x_ref[...]

    o_ref[...] += _swiglu_partial(h_ref[...], wg_ref, wu_ref, wd_ref)

    if final_norm:
        @pl.when(f == pl.num_programs(1) - 1)
        def _():
            o_ref[...] = _rms_rows(o_ref[...], gf_ref[...])


def ffn(x, g, wg, wu, wd, g_final, *, final_norm, tm=1024, tf=256):
    T, D = x.shape
    F = wg.shape[1]
    tm, tf = min(tm, T), min(tf, F)
    return pl.pallas_call(
        functools.partial(_ffn_kernel, final_norm=final_norm),
        grid=(T // tm, F // tf),
        in_specs=[pl.BlockSpec((tm, D), lambda i, f: (i, 0)),
                  pl.BlockSpec((1, D), lambda i, f: (0, 0)),
                  pl.BlockSpec((D, tf), lambda i, f: (0, f)),
                  pl.BlockSpec((D, tf), lambda i, f: (0, f)),
                  pl.BlockSpec((tf, D), lambda i, f: (f, 0)),
                  pl.BlockSpec((1, D), lambda i, f: (0, 0))],
        out_specs=pl.BlockSpec((tm, D), lambda i, f: (i, 0)),
        out_shape=jax.ShapeDtypeStruct((T, D), F32),
        scratch_shapes=[pltpu.VMEM((tm, D), BF16)],
        compiler_params=_cparams(("parallel", "arbitrary")),
        name="ffn",
    )(x, g, wg, wu, wd, g_final)


def _route_kernel(x_ref, g_ref, r_ref, o_ref):
    h = _rms_rows(x_ref[...], g_ref[...])
    logits = jnp.dot(h, r_ref[...], preferred_element_type=F32, precision=HIGHEST)
    lane = lax.broadcasted_iota(jnp.int32, logits.shape, 1)
    logits = jnp.where(lane < N_EXPERTS, logits, NEG_INF)
    v1 = jnp.max(logits, axis=1, keepdims=True)
    i1 = jnp.min(jnp.where(logits == v1, lane, LANES), axis=1, keepdims=True)
    rest = jnp.where(lane == i1, NEG_INF, logits)
    v2 = jnp.max(rest, axis=1, keepdims=True)
    i2 = jnp.min(jnp.where(rest == v2, lane, LANES), axis=1, keepdims=True)
    e2 = jnp.exp(v2 - v1)
    den = 1.0 + e2
    rec = jnp.where(lane == ROUTE_CHOICE + i1, 1.0, jnp.where(lane == ROUTE_CHOICE + i2, 2.0, 0.0))
    rec = jnp.where(lane == ROUTE_WEIGHT, 1.0 / den, jnp.where(lane == ROUTE_WEIGHT + 1, e2 / den, rec))
    o_ref[...] = rec


def moe_route(x, g, router_pad, *, tm=1024):
    T, D = x.shape
    tm = min(tm, T)
    return pl.pallas_call(
        _route_kernel,
        grid=(T // tm,),
        in_specs=[pl.BlockSpec((tm, D), lambda i: (i, 0)),
                  pl.BlockSpec((1, D), lambda i: (0, 0)),
                  pl.BlockSpec((D, LANES), lambda i: (0, 0))],
        out_specs=pl.BlockSpec((tm, LANES), lambda i: (i, 0)),
        out_shape=jax.ShapeDtypeStruct((T, LANES), F32),
        compiler_params=_cparams(("parallel",)),
        name="moe_route",
    )(x, g, router_pad)


def _moe_plan(route, T, tm):
    E = N_EXPERTS
    n_tiles = (TOP_K * T + E * tm) // tm
    choice = route[:, ROUTE_CHOICE:ROUTE_CHOICE + E]
    seli = (choice > 0.5).astype(jnp.int32)
    counts = jnp.sum(seli, axis=0)
    padded = ((counts + tm - 1) // tm) * tm
    seg_end = jnp.cumsum(padded)
    seg_start = seg_end - padded
    slot = seg_start[None, :] + jnp.cumsum(seli, axis=0) - 1
    slot_of = jnp.stack([jnp.sum(jnp.where(choice == k + 1.0, slot, 0), axis=1) for k in range(TOP_K)])
    n_used = (seg_end[-1] // tm).astype(jnp.int32)
    tile_start = jnp.arange(n_tiles, dtype=jnp.int32) * tm
    tile_expert = jnp.sum((seg_end[None, :] <= tile_start[:, None]).astype(jnp.int32), axis=1)
    tile_expert = jnp.minimum(tile_expert, E - 1)
    last_expert = jnp.take(tile_expert, n_used - 1)
    tile_expert = jnp.where(jnp.arange(n_tiles) < n_used, tile_expert, last_expert)
    pad_lo = (seg_start + counts).astype(jnp.int32)
    return (tile_expert.astype(jnp.int32), n_used.reshape(1), slot_of.reshape(-1).astype(jnp.int32),
            pad_lo, seg_end.astype(jnp.int32))


def _for_range(lo, hi, fn, unroll=None):
    def body(r, carry):
        fn(r)
        return carry
    lax.fori_loop(lo, hi, body, 0, unroll=unroll)


def _moe_dispatch_kernel(slot_ref, lo_ref, hi_ref, nt_ref, x_ref, xs_hbm, zero_ref, sem, zsem,
                         *, tc, T, chunks_per_tile, n_chunks):
    i = pl.program_id(0)

    def token_row(k, r):
        s = slot_ref[k * T + i * tc + r]
        return pltpu.make_async_copy(x_ref.at[pl.ds(r, 1)], xs_hbm.at[pl.ds(s, 1)], sem.at[0])

    def zero_row(s):
        return pltpu.make_async_copy(zero_ref.at[pl.ds(0, 1)], xs_hbm.at[pl.ds(s, 1)], zsem.at[0])

    def zero_chunk(c):
        rows = pl.ds(pl.multiple_of(c * MOE_ZERO_ROWS, MOE_ZERO_ROWS), MOE_ZERO_ROWS)
        return pltpu.make_async_copy(zero_ref, xs_hbm.at[rows], zsem.at[0])

    @pl.when(i == 0)
    def _():
        zero_ref[...] = jnp.zeros(zero_ref.shape, F32)
        for start in (True, False):
            for e in range(N_EXPERTS):
                _for_range(lo_ref[e], hi_ref[e], lambda s: zero_row(s).start() if start else zero_row(s).wait())
            _for_range(nt_ref[0] * chunks_per_tile, n_chunks,
                       lambda c: zero_chunk(c).start() if start else zero_chunk(c).wait())

    for k in range(TOP_K):
        _for_range(0, tc, lambda r: token_row(k, r).start(), unroll=8)
    for k in range(TOP_K):
        pltpu.make_async_copy(x_ref, xs_hbm.at[pl.ds(0, tc)], sem.at[0]).wait()


def moe_dispatch(x, plan, *, tm, tc=512):
    _, n_used, slot_of, pad_lo, pad_hi = plan
    T, D = x.shape
    tc = min(tc, T)
    n_slots = TOP_K * T + N_EXPERTS * tm
    grid_spec = pltpu.PrefetchScalarGridSpec(
        num_scalar_prefetch=4,
        grid=(T // tc,),
        in_specs=[pl.BlockSpec((tc, D), lambda i, s, lo, hi, nt: (i, 0))],
        out_specs=pl.BlockSpec(memory_space=pl.ANY),
        scratch_shapes=[pltpu.VMEM((MOE_ZERO_ROWS, D), F32),
                        pltpu.SemaphoreType.DMA((1,)), pltpu.SemaphoreType.DMA((1,))],
    )
    return pl.pallas_call(
        functools.partial(_moe_dispatch_kernel, tc=tc, T=T, chunks_per_tile=tm // MOE_ZERO_ROWS,
                          n_chunks=n_slots // MOE_ZERO_ROWS),
        grid_spec=grid_spec,
        out_shape=jax.ShapeDtypeStruct((n_slots, D), F32),
        compiler_params=_cparams(("arbitrary",)),
        name="moe_dispatch",
    )(slot_of, pad_lo, pad_hi, n_used, x)


def _moe_ffn_kernel(te_ref, nt_ref, xs_ref, g_ref, wg_ref, wu_ref, wd_ref, y_ref, h_ref):
    i, f = pl.program_id(0), pl.program_id(1)
    used = i < nt_ref[0]

    @pl.when(used & (f == 0))
    def _():
        h_ref[...] = _rms_rows(xs_ref[...], g_ref[...]).astype(BF16)
        y_ref[...] = _swiglu_partial(h_ref[...], wg_ref, wu_ref, wd_ref)

    @pl.when(used & (f > 0))
    def _():
        y_ref[...] += _swiglu_partial(h_ref[...], wg_ref, wu_ref, wd_ref)

    @pl.when(jnp.logical_not(used) & (f == 0))
    def _():
        y_ref[...] = jnp.zeros(y_ref.shape, F32)


def moe_experts(xs, g, plan, wg, wu, wd, *, tm, tf=1024):
    tile_expert, n_used = plan[0], plan[1]
    n_slots, D = xs.shape
    F = wg.shape[2]
    tf = min(tf, F)
    nf = F // tf

    def fidx(i, f, nt):
        return jnp.where(i < nt[0], f, nf - 1)

    grid_spec = pltpu.PrefetchScalarGridSpec(
        num_scalar_prefetch=2,
        grid=(n_slots // tm, nf),
        in_specs=[
            pl.BlockSpec((tm, D), lambda i, f, te, nt: (jnp.minimum(i, nt[0] - 1), 0)),
            pl.BlockSpec((1, D), lambda i, f, te, nt: (0, 0)),
            pl.BlockSpec((None, D, tf), lambda i, f, te, nt: (te[i], 0, fidx(i, f, nt))),
            pl.BlockSpec((None, D, tf), lambda i, f, te, nt: (te[i], 0, fidx(i, f, nt))),
            pl.BlockSpec((None, tf, D), lambda i, f, te, nt: (te[i], fidx(i, f, nt), 0)),
        ],
        out_specs=pl.BlockSpec((tm, D), lambda i, f, te, nt: (i, 0)),
        scratch_shapes=[pltpu.VMEM((tm, D), BF16)],
    )
    return pl.pallas_call(
        _moe_ffn_kernel,
        grid_spec=grid_spec,
        out_shape=jax.ShapeDtypeStruct((n_slots, D), F32),
        compiler_params=_cparams(("parallel", "arbitrary")),
        name="moe_experts",
    )(tile_expert, n_used, xs, g, wg, wu, wd)


def _moe_combine_kernel(slot_ref, x_ref, y_hbm, route_ref, gf_ref, o_ref, ybuf, sem, *, tc, T, final_norm):
    i = pl.program_id(0)

    def slot_row(k, r):
        s = slot_ref[k * T + i * tc + r]
        return pltpu.make_async_copy(y_hbm.at[pl.ds(s, 1)], ybuf.at[k, pl.ds(r, 1)], sem.at[0])

    for k in range(TOP_K):
        _for_range(0, tc, lambda r: slot_row(k, r).start(), unroll=8)
    for k in range(TOP_K):
        pltpu.make_async_copy(y_hbm.at[pl.ds(0, tc)], ybuf.at[k], sem.at[0]).wait()

    w = route_ref[...]
    out = x_ref[...]
    for k in range(TOP_K):
        out = out + w[:, ROUTE_WEIGHT + k:ROUTE_WEIGHT + k + 1] * ybuf[k]
    if final_norm:
        out = _rms_rows(out, gf_ref[...])
    o_ref[...] = out


def moe_combine(x, y, route, plan, g_final, *, final_norm, tc=256):
    T, D = x.shape
    tc = min(tc, T)
    grid_spec = pltpu.PrefetchScalarGridSpec(
        num_scalar_prefetch=1,
        grid=(T // tc,),
        in_specs=[pl.BlockSpec((tc, D), lambda i, s: (i, 0)),
                  pl.BlockSpec(memory_space=pl.ANY),
                  pl.BlockSpec((tc, LANES), lambda i, s: (i, 0)),
                  pl.BlockSpec((1, D), lambda i, s: (0, 0))],
        out_specs=pl.BlockSpec((tc, D), lambda i, s: (i, 0)),
        scratch_shapes=[pltpu.VMEM((TOP_K, tc, D), F32), pltpu.SemaphoreType.DMA((1,))],
    )
    return pl.pallas_call(
        functools.partial(_moe_combine_kernel, tc=tc, T=T, final_norm=final_norm),
        grid_spec=grid_spec,
        out_shape=jax.ShapeDtypeStruct((T, D), F32),
        compiler_params=_cparams(("arbitrary",)),
        name="moe_combine",
    )(plan[2], x, y, route, g_final)


def moe_block(x, g, router, wg, wu, wd, g_final, *, final_norm, tm=512):
    T, D = x.shape
    tm = min(tm, T)
    router_pad = jnp.zeros((D, LANES), F32).at[:, :N_EXPERTS].set(router)
    route = moe_route(x, g, router_pad)
    plan = _moe_plan(route, T, tm)
    xs = moe_dispatch(x, plan, tm=tm)
    y = moe_experts(xs, g, plan, wg.astype(BF16), wu.astype(BF16), wd, tm=tm)
    return moe_combine(x, y, route, plan, g_final, final_norm=final_norm)


def _split_w_in(w):
    D = w.shape[0]
    n_fox = 3 * FOX_HEADS * HEAD_DIM
    n_dil = 3 * DIL_HEADS * HEAD_DIM
    n_nsa = (NSA_Q_HEADS + NSA_BRANCHES * 2 * NSA_KV_HEADS) * HEAD_DIM
    n_g = NSA_BRANCHES * NSA_Q_HEADS
    a0 = n_fox
    a1 = a0 + FOX_HEADS
    a2 = a1 + n_dil
    a3 = a2 + n_nsa
    a4 = a3 + n_g
    group_cols = DIL_HEADS_PER_GROUP * HEAD_DIM
    dil = w[:, a1:a2].reshape(D, 3, len(DIL_GROUPS), group_cols)
    dil_sets = [dil[:, :, gi].reshape(D, 3 * group_cols).astype(BF16) for gi in range(len(DIL_GROUPS))]
    main = jnp.concatenate([w[:, a4:].astype(BF16), w[:, :a0].astype(BF16), dil_sets[0],
                            w[:, a2:a3].astype(BF16)], axis=1)
    small = jnp.concatenate([w[:, a0:a1], w[:, a3:a4],
                             jnp.zeros((D, LANES - FOX_HEADS - n_g), w.dtype)], axis=1).astype(BF16)
    return main, dil_sets[1:], small


def mixing_block(x, B, S, norm_g, w_in, forget_bias, cmp_pe, cmp_w1, cmp_w2, wa, wb, wc, w_out, tables):
    T, D = x.shape
    w_main, w_dil, w_small = _split_w_in(w_in)
    g = norm_g.reshape(1, D)
    proj, h = rms_proj(x, g, w_main, BF16, tm=1024, tn=1024)
    small = slab_proj(h, w_small, F32, tm=1024, tn=LANES)[0]
    dil_views = [proj_view(h, w, DIL_GROUPS[gi + 1][1], tm=1024, tn=w.shape[1]) for gi, w in enumerate(w_dil)]

    gate_lanes = slice(SMALL_FOX_F, SMALL_FOX_F + FOX_HEADS)
    bias_row = jnp.zeros((1, LANES), F32).at[0, gate_lanes].set(forget_bias)
    c = logf_cumsum(small, bias_row, B, S)
    c_row = c[:, gate_lanes].T.reshape(FOX_HEADS, 1, T)
    fox_o = fox_attention(proj, c, c_row, B, S)

    dil = [dilated_group(proj, CB_DIL, tables["dil"][0], 0, B, S)]
    dil += [dilated_group(v, 0, tables["dil"][gi + 1], gi + 1, B, S) for gi, v in enumerate(dil_views)]
    dil_o = [d[0] for d in dil]
    dil_lse = [d[1] for d in dil]

    kvc = nsa_compress(proj, cmp_pe, cmp_w1.astype(BF16), cmp_w2.astype(BF16), B, S)
    cmp_o, sel = nsa_cmp_select(proj, kvc, B, S)
    slc_o = nsa_selected(proj, sel, tables["slc"], B, S, tq=tables["slc_tq"], tk=tables["slc_tk"])
    win_o = nsa_window(proj, tables["win"], B, S, tq=tables["win_tq"])

    return merge_branches(x, proj, small, fox_o, dil_o, dil_lse, cmp_o, slc_o, win_o,
                          wa.astype(BF16), wb.astype(BF16), wc.astype(BF16), w_out.astype(BF16))


def bias_tables(rel_bias, S):
    slc_tq, slc_tk = min(256, S), min(512, S)
    win_tq = min(256, S)
    rel_c = rel_bias[:, DIL_HEADS:]
    return {
        "dil": dilated_bias_tables(rel_bias[:, :DIL_HEADS]),
        "slc": slc_bias_table(rel_c, S, slc_tq, slc_tk), "slc_tq": slc_tq, "slc_tk": slc_tk,
        "win": win_bias_table(rel_c, win_tq, min(WIN // win_tq + 1, 3)), "win_tq": win_tq,
    }


def kernel(x, rel_bias, norm_mix_g, norm_ffn_g, norm_final_g, w_in, fox_forget_bias, cmp_pe_k, cmp_w1_k, cmp_w2_k, cmp_pe_v, cmp_w1_v, cmp_w2_v, w_branch_a, w_branch_b, w_branch_c, w_out, ffn_w_gate, ffn_w_up, ffn_w_down, moe_router, moe_w_gate, moe_w_up, moe_w_down):
    B, S, D = x.shape
    if D != D_MODEL:
        raise ValueError(f"column-block layout is built for d_model={D_MODEL}, got {D}")
    T = B * S
    depth = w_in.shape[0]
    tables = bias_tables(rel_bias, S)
    g_final = norm_final_g.reshape(1, D)
    xt = x.reshape(T, D)
    for l in range(depth):
        xt = mixing_block(
            xt, B, S, norm_mix_g[l], w_in[l], fox_forget_bias[l],
            jnp.stack([cmp_pe_k[l], cmp_pe_v[l]]), jnp.stack([cmp_w1_k[l], cmp_w1_v[l]]),
            jnp.stack([cmp_w2_k[l], cmp_w2_v[l]]),
            w_branch_a[l], w_branch_b[l], w_branch_c[l], w_out[l], tables)
        g = norm_ffn_g[l].reshape(1, D)
        last = l == depth - 1
        j = l // 2
        if l % 2 == 0:
            xt = ffn(xt, g, ffn_w_gate[j], ffn_w_up[j], ffn_w_down[j], g_final, final_norm=last)
        else:
            xt = moe_block(xt, g, moe_router[j], moe_w_gate[j], moe_w_up[j], moe_w_down[j], g_final,
                           final_norm=last)
    return xt.reshape(B, S, D)
```

```python
import functools
import math

import jax
import jax.numpy as jnp
import numpy as np
from jax import lax
from jax.experimental import pallas as pl
from jax.experimental.pallas import tpu as pltpu

F32 = jnp.float32
BF16 = jnp.bfloat16
HIGHEST = lax.Precision.HIGHEST
NEG_INF = float("-inf")

LANES = 128
HEAD_DIM = 128
ATTN_SCALE = HEAD_DIM ** -0.5
RMS_EPS = 1e-6
VMEM_LIMIT_BYTES = 58 * 1024 * 1024

D_MODEL = 2048
FOX_HEADS = 8
DIL_GROUPS = ((128, 1), (512, 4), (2048, 16))
DIL_HEADS_PER_GROUP = 4
DIL_HEADS = DIL_HEADS_PER_GROUP * len(DIL_GROUPS)
DIL_TAPS = 128
NSA_Q_HEADS = 8
NSA_KV_HEADS = 2
NSA_GQA = NSA_Q_HEADS // NSA_KV_HEADS
NSA_BRANCHES = 3
CMP_BLOCK = 32
CMP_STRIDE = 16
CMP_HIDDEN = 256
SLC_BLOCK = 64
SLC_COUNT = 16
WIN = 512
REL_BUCKETS = 32
REL_MAX_EXACT = 16
REL_MAX_DIST = 2048
N_EXPERTS = 8
TOP_K = 2
N_BRANCHES = 3

CB_MERGE = 0
CB_FOX = CB_MERGE + N_BRANCHES * (D_MODEL // LANES)
CB_DIL = CB_FOX + 3 * FOX_HEADS
CB_NSAQ = CB_DIL + 3 * DIL_HEADS_PER_GROUP
CB_NSAKV = CB_NSAQ + NSA_Q_HEADS
CB_END = CB_NSAKV + NSA_BRANCHES * 2 * NSA_KV_HEADS
SMALL_FOX_F = 0
SMALL_NSA_G = FOX_HEADS
ROUTE_CHOICE = 0
ROUTE_WEIGHT = N_EXPERTS
MOE_ZERO_ROWS = 64


def _cparams(semantics):
    return pltpu.CompilerParams(dimension_semantics=semantics,
                                vmem_limit_bytes=VMEM_LIMIT_BYTES)


def _t5_bucket_np(dist):
    dist = np.maximum(dist, 0)
    d = np.maximum(dist, 1).astype(np.float32)
    log_ratio = np.log(d / np.float32(REL_MAX_EXACT)) / np.float32(math.log(REL_MAX_DIST / REL_MAX_EXACT))
    large = REL_MAX_EXACT + (log_ratio * np.float32(REL_BUCKETS - REL_MAX_EXACT)).astype(np.int32)
    large = np.minimum(large, REL_BUCKETS - 1)
    return np.where(dist < REL_MAX_EXACT, dist, large).astype(np.int32)


def _toeplitz_kernel(v_ref, o_ref, *, rows, cols):
    x = jnp.broadcast_to(v_ref[...], (rows, v_ref.shape[-1]))
    o_ref[...] = pltpu.roll(x, 0, 1, stride=1, stride_axis=0)[:, :cols]


def _toeplitz_blocks(fn, offsets, rows, cols):
    lx = -(-(rows + cols) // LANES) * LANES
    m = np.arange(lx)
    rel = np.where(m < cols, -m, lx - m)
    v = jnp.stack([fn(c + rel) for c in offsets]).astype(F32)
    n_off, H = v.shape[:2]
    return pl.pallas_call(
        functools.partial(_toeplitz_kernel, rows=rows, cols=cols),
        grid=(n_off, H),
        in_specs=[pl.BlockSpec((None, None, 1, lx), lambda o, h: (o, h, 0, 0))],
        out_specs=pl.BlockSpec((None, None, rows, cols), lambda o, h: (o, h, 0, 0)),
        out_shape=jax.ShapeDtypeStruct((n_off, H, rows, cols), F32),
        compiler_params=_cparams(("parallel", "parallel")),
        name="toeplitz_table",
    )(v.reshape(n_off, H, 1, lx))


def _rms_rows(x, g):
    inv = lax.rsqrt(jnp.mean(x * x, axis=-1, keepdims=True) + RMS_EPS)
    return (x * inv) * g


def _rms_proj_kernel(x_ref, g_ref, w_ref, *rest, ncb, cast_cols):
    if cast_cols:
        src_ref, o_ref, h_ref, dst_ref = rest
    else:
        o_ref, h_ref = rest

    @pl.when(pl.program_id(1) == 0)
    def _():
        h_ref[...] = _rms_rows(x_ref[...], g_ref[...]).astype(BF16)

    res = jnp.dot(h_ref[...], w_ref[...], preferred_element_type=F32)
    for c in range(ncb):
        o_ref[c] = res[:, c * LANES:(c + 1) * LANES].astype(o_ref.dtype)

    if cast_cols:
        @pl.when(pl.program_id(1) < cast_cols)
        def _():
            dst_ref[...] = src_ref[...].astype(BF16)


def rms_proj(x, g, w, out_dtype, *, tm, tn, cast=None):
    T, D = x.shape
    N = w.shape[1]
    tm, tn = min(tm, T), min(tn, N)
    ncb = tn // LANES
    steps_i, steps_j = T // tm, N // tn
    in_specs = [pl.BlockSpec((tm, D), lambda i, j: (i, 0)),
                pl.BlockSpec((1, D), lambda i, j: (0, 0)),
                pl.BlockSpec((D, tn), lambda i, j: (0, j))]
    out_specs = [pl.BlockSpec((ncb, tm, LANES), lambda i, j: (j, i, 0)),
                 pl.BlockSpec((tm, D), lambda i, j: (i, 0))]
    out_shape = [jax.ShapeDtypeStruct((N // LANES, T, LANES), out_dtype),
                 jax.ShapeDtypeStruct((T, D), BF16)]
    operands = [x, g, w]
    cast_cols = 0
    if cast is not None:
        flat = cast.reshape(-1, cast.shape[-1])
        cast_cols = 1 << (steps_j.bit_length() - 1)
        rows = flat.shape[0] // (steps_i * cast_cols)
        if rows * steps_i * cast_cols != flat.shape[0] or rows % 16:
            raise ValueError("cast operand does not split into whole bf16 tiles over the grid")
        block = pl.BlockSpec((rows, flat.shape[1]), lambda i, j: (i * cast_cols + jnp.minimum(j, cast_cols - 1), 0))
        in_specs.append(block)
        out_specs.append(block)
        out_shape.append(jax.ShapeDtypeStruct(flat.shape, BF16))
        operands.append(flat)
    outs = pl.pallas_call(
        functools.partial(_rms_proj_kernel, ncb=ncb, cast_cols=cast_cols),
        grid=(steps_i, steps_j),
        in_specs=in_specs,
        out_specs=out_specs,
        out_shape=out_shape,
        compiler_params=_cparams(("parallel", "arbitrary")),
        name="rms_proj",
    )(*operands)
    if cast is None:
        return outs[0], outs[1], None
    return outs[0], outs[1], outs[2].reshape(cast.shape)


def _proj_kernel(h_ref, w_ref, o_ref, *, ncb):
    res = jnp.dot(h_ref[...], w_ref[...], preferred_element_type=F32)
    for c in range(ncb):
        o_ref[c] = res[:, c * LANES:(c + 1) * LANES].astype(o_ref.dtype)


def slab_proj(h, w, out_dtype, *, tm, tn):
    T, D = h.shape
    N = w.shape[1]
    tm, tn = min(tm, T), min(tn, N)
    ncb = tn // LANES
    return pl.pallas_call(
        functools.partial(_proj_kernel, ncb=ncb),
        grid=(T // tm, N // tn),
        in_specs=[pl.BlockSpec((tm, D), lambda i, j: (i, 0)),
                  pl.BlockSpec((D, tn), lambda i, j: (0, j))],
        out_specs=pl.BlockSpec((ncb, tm, LANES), lambda i, j: (j, i, 0)),
        out_shape=jax.ShapeDtypeStruct((N // LANES, T, LANES), out_dtype),
        compiler_params=_cparams(("parallel", "parallel")),
        name="slab_proj",
    )(h, w)


def _proj_view_kernel(h_ref, w_ref, o_ref, res_ref, *, ncb, dil):
    res = jnp.dot(h_ref[...], w_ref[...], preferred_element_type=F32)
    rows = res_ref.shape[1] // dil
    for c in range(ncb):
        res_ref[c] = res[:, c * LANES:(c + 1) * LANES]
        for r in range(dil):
            part = res_ref[c, pl.ds(r, rows, stride=dil), :]
            o_ref[c, :, r * LANES:(r + 1) * LANES] = part.astype(o_ref.dtype)


def proj_view(h, w, dil, *, tm, tn):
    T, D = h.shape
    N = w.shape[1]
    tm, tn = min(tm, T), min(tn, N)
    ncb = tn // LANES
    return pl.pallas_call(
        functools.partial(_proj_view_kernel, ncb=ncb, dil=dil),
        grid=(T // tm, N // tn),
        in_specs=[pl.BlockSpec((tm, D), lambda i, j: (i, 0)),
                  pl.BlockSpec((D, tn), lambda i, j: (0, j))],
        out_specs=pl.BlockSpec((ncb, tm // dil, dil * LANES), lambda i, j: (j, i, 0)),
        out_shape=jax.ShapeDtypeStruct((N // LANES, T // dil, dil * LANES), BF16),
        scratch_shapes=[pltpu.VMEM((ncb, tm, LANES), F32)],
        compiler_params=_cparams(("parallel", "parallel")),
        name="proj_view",
    )(h, w)


CUMSUM_BLOCK = 256


def _logf_cumsum_kernel(f_ref, b_ref, c_ref, *, nblk):
    row = lax.broadcasted_iota(jnp.int32, (CUMSUM_BLOCK, CUMSUM_BLOCK), 0)
    col = lax.broadcasted_iota(jnp.int32, (CUMSUM_BLOCK, CUMSUM_BLOCK), 1)
    tri = jnp.where(col <= row, 1.0, 0.0).astype(F32)

    def body(i, carry):
        sl = pl.ds(pl.multiple_of(i * CUMSUM_BLOCK, CUMSUM_BLOCK), CUMSUM_BLOCK)
        z = f_ref[sl, :] + b_ref[...]
        logf = jnp.minimum(z, 0.0) - jnp.log1p(jnp.exp(-jnp.abs(z)))
        cs = jnp.dot(tri, logf, preferred_element_type=F32, precision=HIGHEST) + carry
        c_ref[sl, :] = cs
        return cs[CUMSUM_BLOCK - 1:CUMSUM_BLOCK, :]

    lax.fori_loop(0, nblk, body, jnp.zeros((1, LANES), F32))


def logf_cumsum(small, bias_row, B, S):
    T = B * S
    return pl.pallas_call(
        functools.partial(_logf_cumsum_kernel, nblk=S // CUMSUM_BLOCK),
        grid=(B,),
        in_specs=[pl.BlockSpec((S, LANES), lambda b: (b, 0)),
                  pl.BlockSpec((1, LANES), lambda b: (0, 0))],
        out_specs=pl.BlockSpec((S, LANES), lambda b: (b, 0)),
        out_shape=jax.ShapeDtypeStruct((T, LANES), F32),
        compiler_params=_cparams(("parallel",)),
        name="logf_cumsum",
    )(small, bias_row)


FLASH_ROW_CHUNK = 128


LOG2E = math.log2(math.e)


def _lane_tile(x, n):
    return jnp.concatenate([x] * n, axis=1)


def _flash_update(s2, v, m_ref, l_ref, acc_ref, row_shift=None):
    m_old = m_ref[...]
    m_cur = jnp.max(s2, axis=-1, keepdims=True)
    if row_shift is not None:
        m_cur = m_cur + row_shift
    m_new = jnp.maximum(m_old, m_cur)
    m_safe = jnp.where(m_new == NEG_INF, 0.0, m_new)
    alpha = jnp.exp2(m_old - m_safe)
    origin = m_safe if row_shift is None else m_safe - row_shift
    p = jnp.exp2(s2 - _lane_tile(origin, s2.shape[1] // LANES))
    l_ref[...] = alpha * l_ref[...] + jnp.sum(p, axis=-1, keepdims=True)
    acc_ref[...] = alpha * acc_ref[...] + jnp.dot(p.astype(BF16), v, preferred_element_type=F32)
    m_ref[...] = m_new


def _flash_init(m_ref, l_ref, acc_ref):
    m_ref[...] = jnp.full(m_ref.shape, NEG_INF, F32)
    l_ref[...] = jnp.zeros(l_ref.shape, F32)
    acc_ref[...] = jnp.zeros(acc_ref.shape, F32)


def _flash_result(l_ref, acc_ref):
    l = l_ref[...]
    return acc_ref[...] / jnp.where(l > 0.0, l, 1.0)


def _qk(q, k):
    return lax.dot_general(q, k, (((1,), (1,)), ((), ())), preferred_element_type=F32)


def _fox_kernel(qi_ref, ki_ref, q_ref, k_ref, v_ref, cq_ref, ck_ref, o_ref, m_ref, l_ref, acc_ref, cqb_ref,
                *, tq, tk, rc, nb):
    p = pl.program_id(1)
    qi, ki = qi_ref[p], ki_ref[p]

    @pl.when(ki == 0)
    def _():
        _flash_init(m_ref, l_ref, acc_ref)
        lane = lax.broadcasted_iota(jnp.int32, cq_ref.shape, 2)
        mine = jnp.sum(jnp.where(lane == SMALL_FOX_F + pl.program_id(0), cq_ref[...], 0.0),
                       axis=-1, keepdims=True)
        cqb_ref[...] = jnp.broadcast_to(mine * LOG2E, cqb_ref.shape)

    chunks = [pl.ds(c * rc, rc) for c in range(tq // rc)]

    def tile(causal_mask):
        widths = [min(tk, (c + 1) * rc) if causal_mask and tq == tk else tk for c in range(tq // rc)]

        def products(b):
            return [_qk(q_ref[b, rows, :], k_ref[b, 0:w, :]) for rows, w in zip(chunks, widths)]

        def softmax_pv(b, qk):
            ck = ck_ref[b] * LOG2E
            for c, (rows, w) in enumerate(zip(chunks, widths)):
                s = qk[c] * (ATTN_SCALE * LOG2E) - ck[:, 0:w]
                if causal_mask:
                    qpos = qi * tq + c * rc + lax.broadcasted_iota(jnp.int32, (rc, w), 0)
                    kpos = ki * tk + lax.broadcasted_iota(jnp.int32, (rc, w), 1)
                    s = jnp.where(kpos <= qpos, s, NEG_INF)
                _flash_update(s, v_ref[b, 0:w, :], m_ref.at[b, rows], l_ref.at[b, rows], acc_ref.at[b, rows],
                              row_shift=cqb_ref[b, rows, :])

        ahead = products(0)
        for b in range(nb):
            qk = ahead
            if b + 1 < nb:
                ahead = products(b + 1)
            softmax_pv(b, qk)

    crosses_diagonal = (ki + 1) * tk - 1 > qi * tq

    @pl.when(crosses_diagonal)
    def _():
        tile(True)

    @pl.when(jnp.logical_not(crosses_diagonal))
    def _():
        tile(False)

    @pl.when(ki == ((qi + 1) * tq - 1) // tk)
    def _():
        o_ref[...] = _flash_result(l_ref, acc_ref).astype(o_ref.dtype)


def _causal_pairs(nq, tq, tk):
    qs, ks = [], []
    for qi in range(nq):
        for ki in range(((qi + 1) * tq - 1) // tk + 1):
            qs.append(qi)
            ks.append(ki)
    return jnp.asarray(qs, jnp.int32), jnp.asarray(ks, jnp.int32)


def fox_attention(proj, c_packed, c_row, B, S, *, tq=512, tk=512):
    tq, tk = min(tq, S), min(tk, S)
    H = FOX_HEADS
    T = B * S
    qis, kis = _causal_pairs(S // tq, tq, tk)
    proj4 = proj.reshape(proj.shape[0], B, S, LANES)
    grid_spec = pltpu.PrefetchScalarGridSpec(
        num_scalar_prefetch=2,
        grid=(H, int(qis.shape[0])),
        in_specs=[
            pl.BlockSpec((None, B, tq, LANES), lambda h, p, qi, ki: (CB_FOX + h, 0, qi[p], 0)),
            pl.BlockSpec((None, B, tk, LANES), lambda h, p, qi, ki: (CB_FOX + H + h, 0, ki[p], 0)),
            pl.BlockSpec((None, B, tk, LANES), lambda h, p, qi, ki: (CB_FOX + 2 * H + h, 0, ki[p], 0)),
            pl.BlockSpec((B, tq, LANES), lambda h, p, qi, ki: (0, qi[p], 0)),
            pl.BlockSpec((None, B, 1, tk), lambda h, p, qi, ki: (h, 0, 0, ki[p])),
        ],
        out_specs=pl.BlockSpec((None, B, tq, LANES), lambda h, p, qi, ki: (h, 0, qi[p], 0)),
        scratch_shapes=[pltpu.VMEM((B, tq, LANES), F32)] * 4,
    )
    out = pl.pallas_call(
        functools.partial(_fox_kernel, tq=tq, tk=tk, rc=min(FLASH_ROW_CHUNK, tq), nb=B),
        grid_spec=grid_spec,
        out_shape=jax.ShapeDtypeStruct((H, B, S, LANES), BF16),
        compiler_params=_cparams(("parallel", "arbitrary")),
        name="fox_attention",
    )(qis, kis, proj4, proj4, proj4, c_packed.reshape(B, S, LANES), c_row.reshape(H, B, 1, S))
    return out.reshape(H, T, LANES)


DIL_ROWS_PER_STEP = 512


def _dil_kernel(q_ref, kp_ref, kc_ref, vp_ref, vc_ref, bias_ref, o_ref, lse_ref, *, dil, tu, nsub, hp):
    ui = pl.program_id(2)
    kcol = lax.broadcasted_iota(jnp.int32, (tu, 2 * tu), 1)
    first_ok = jnp.logical_or(ui > 0, kcol >= tu)
    units = [(c, slice(r * LANES, (r + 1) * LANES)) for c in range(nsub) for r in range(dil)]

    def window(prev_ref, cur_ref, h, c, sl):
        if c == 0:
            return jnp.concatenate([prev_ref[h, :, sl], cur_ref[h, 0:tu, sl]], axis=0)
        return cur_ref[h, (c - 1) * tu:(c + 1) * tu, sl]

    def head(h, carry):
        bias = bias_ref[h]
        bias_first = jnp.where(first_ok, bias, NEG_INF)
        qk = [_qk(q_ref[h, c * tu:(c + 1) * tu, sl], window(kp_ref, kc_ref, h, c, sl)) for c, sl in units]
        for i, (c, sl) in enumerate(units):
            rows = slice(c * tu, (c + 1) * tu)
            s = qk[i] * ATTN_SCALE + (bias_first if c == 0 else bias)
            m = jnp.max(s, axis=-1, keepdims=True)
            e = jnp.exp(s - m)
            den = jnp.sum(e, axis=-1, keepdims=True)
            o = jnp.dot(e.astype(BF16), window(vp_ref, vc_ref, h, c, sl), preferred_element_type=F32) / den
            o_ref[h, rows, sl] = o.astype(o_ref.dtype)
            lse_ref[h, rows, sl] = jnp.broadcast_to(m + jnp.log(den), (tu, LANES))
        return carry

    lax.fori_loop(0, hp, head, 0)


def dilated_group(view, c0, bias_tbl, group, B, S):
    dil = DIL_GROUPS[group][1]
    tu = DIL_TAPS
    Hg = DIL_HEADS_PER_GROUP
    T = B * S
    rows = min(DIL_ROWS_PER_STEP, S // dil)
    nsub = rows // tu
    nstep = S // dil // rows
    hp = max(1, Hg // max(1, dil // 4))
    cq, ck, cv = c0 // hp, (c0 + Hg) // hp, (c0 + 2 * Hg) // hp
    cur_blk = (hp, rows, dil * LANES)
    prev_blk = (hp, tu, dil * LANES)
    prev = lambda c: (lambda b, h, u: (c + h, jnp.maximum((b * nstep + u) * nsub - 1, 0), 0))
    cur = lambda c: (lambda b, h, u: (c + h, b * nstep + u, 0))
    o, lse = pl.pallas_call(
        functools.partial(_dil_kernel, dil=dil, tu=tu, nsub=nsub, hp=hp),
        grid=(B, Hg // hp, nstep),
        in_specs=[pl.BlockSpec(cur_blk, cur(cq)),
                  pl.BlockSpec(prev_blk, prev(ck)), pl.BlockSpec(cur_blk, cur(ck)),
                  pl.BlockSpec(prev_blk, prev(cv)), pl.BlockSpec(cur_blk, cur(cv)),
                  pl.BlockSpec((hp, tu, 2 * tu), lambda b, h, u: (h, 0, 0))],
        out_specs=[pl.BlockSpec(cur_blk, cur(0)), pl.BlockSpec(cur_blk, cur(0))],
        out_shape=[jax.ShapeDtypeStruct((Hg, T // dil, dil * LANES), BF16),
                   jax.ShapeDtypeStruct((Hg, T // dil, dil * LANES), F32)],
        compiler_params=_cparams(("parallel", "parallel", "arbitrary")),
        name=f"dilated_group{group}",
    )(view, view, view, view, view, bias_tbl)
    return o, lse


def dilated_bias_tables(rel_bias_b):
    tu = DIL_TAPS
    tables = []
    for g, (_, dil) in enumerate(DIL_GROUPS):
        heads = rel_bias_b[:, g * DIL_HEADS_PER_GROUP:(g + 1) * DIL_HEADS_PER_GROUP]

        def fn(taps, heads=heads, dil=dil):
            valid = (taps >= 0) & (taps <= DIL_TAPS)
            vals = heads[_t5_bucket_np(np.where(valid, taps, 0) * dil)].T
            return jnp.where(valid[None, :], vals, NEG_INF)

        tables.append(_toeplitz_blocks(fn, [tu], tu, 2 * tu)[0])
    return tables


def _compress_kernel(x_ref, pelo_ref, pehi_ref, w1a_ref, w1b_ref, w2_ref, o_ref, *, nchunk):
    x = x_ref[...].astype(F32)
    u0 = jnp.dot((x + pelo_ref[...]).astype(BF16), w1a_ref[...], preferred_element_type=F32)
    u1 = jnp.dot((x + pehi_ref[...]).astype(BF16), w1b_ref[...], preferred_element_type=F32)
    pre = u0 + pltpu.roll(u1, nchunk - 1, 0)
    hid = jax.nn.gelu(pre)
    out = jnp.dot(hid.astype(BF16), w2_ref[...], preferred_element_type=F32)
    row = lax.broadcasted_iota(jnp.int32, out.shape, 0)
    o_ref[...] = jnp.where(row < nchunk - 1, out, 0.0).astype(o_ref.dtype)


def nsa_compress(proj, pe, w1, w2, B, S):
    nchunk = S // CMP_STRIDE
    half = CMP_STRIDE * HEAD_DIM
    Hkv = NSA_KV_HEADS
    x = proj[CB_NSAKV:CB_NSAKV + 2 * Hkv].reshape(2, Hkv, B, nchunk, half)
    pe_lo = pe[:, :CMP_STRIDE].reshape(2, 1, half)
    pe_hi = pe[:, CMP_STRIDE:].reshape(2, 1, half)
    return pl.pallas_call(
        functools.partial(_compress_kernel, nchunk=nchunk),
        grid=(2, B, Hkv),
        in_specs=[pl.BlockSpec((None, None, None, nchunk, half), lambda t, b, h: (t, h, b, 0, 0)),
                  pl.BlockSpec((None, 1, half), lambda t, b, h: (t, 0, 0)),
                  pl.BlockSpec((None, 1, half), lambda t, b, h: (t, 0, 0)),
                  pl.BlockSpec((None, half, CMP_HIDDEN), lambda t, b, h: (t, 0, 0)),
                  pl.BlockSpec((None, half, CMP_HIDDEN), lambda t, b, h: (t, 1, 0)),
                  pl.BlockSpec((None, CMP_HIDDEN, HEAD_DIM), lambda t, b, h: (t, 0, 0))],
        out_specs=pl.BlockSpec((None, None, None, nchunk, HEAD_DIM), lambda t, b, h: (t, b, h, 0, 0)),
        out_shape=jax.ShapeDtypeStruct((2, B, Hkv, nchunk, HEAD_DIM), BF16),
        compiler_params=_cparams(("parallel", "parallel", "parallel")),
        name="nsa_compress",
    )(x, pe_lo, pe_hi, w1, w1, w2)


def _masked_softmax(s, mask, axis):
    s = jnp.where(mask, s, NEG_INF)
    m = jnp.max(s, axis=axis, keepdims=True)
    m = jnp.where(m == NEG_INF, 0.0, m)
    e = jnp.exp(s - m)
    den = jnp.sum(e, axis=axis, keepdims=True)
    return e / jnp.where(den > 0.0, den, 1.0)


def _cmp_select_kernel(q_ref, kc_ref, vc_ref, mt_ref, o_ref, sel_ref, *, tq, nchunk, n_slc, n_sel):
    q0 = pl.program_id(2) * tq
    kc = kc_ref[...]
    vc = vc_ref[...]
    pos_c = q0 + lax.broadcasted_iota(jnp.int32, (nchunk, tq), 1)
    end_c = lax.broadcasted_iota(jnp.int32, (nchunk, tq), 0) * CMP_STRIDE + (CMP_BLOCK - 1)
    vis_c = end_c <= pos_c
    imp = jnp.zeros((nchunk, tq), F32)
    qk = [_qk(kc, q_ref[g]) for g in range(NSA_GQA)]
    for g in range(NSA_GQA):
        p = _masked_softmax(qk[g] * ATTN_SCALE, vis_c, 0)
        o = lax.dot_general(p.astype(BF16), vc, (((0,), (0,)), ((), ())), preferred_element_type=F32)
        o_ref[g] = o.astype(o_ref.dtype)
        imp = imp + p
    p_slc = jnp.dot(mt_ref[...], imp, preferred_element_type=F32, precision=HIGHEST)

    blk = lax.broadcasted_iota(jnp.int32, (n_slc, tq), 0)
    cur = (q0 + lax.broadcasted_iota(jnp.int32, (n_slc, tq), 1)) // SLC_BLOCK
    forced = (blk == 0) | (blk == cur) | (blk == cur - 1)
    allowed = blk <= cur
    score = jnp.where(forced, 1e30, jnp.where(allowed, p_slc, -1.0))
    chosen = jnp.zeros((n_slc, tq), F32)
    for _ in range(n_sel):
        top = jnp.max(score, axis=0, keepdims=True)
        first = jnp.min(jnp.where(score == top, blk, n_slc), axis=0, keepdims=True)
        hit = blk == first
        chosen = jnp.where(hit, 1.0, chosen)
        score = jnp.where(hit, -2.0, score)
    mask = jnp.concatenate([jnp.where(allowed, chosen, 0.0), jnp.zeros((LANES - n_slc, tq), F32)], axis=0)
    sel_ref[...] = mask.T.astype(sel_ref.dtype)


def nsa_cmp_select(proj, kvc, B, S, *, tq=1024):
    tq = min(tq, S)
    nq = S // tq
    T = B * S
    nchunk = S // CMP_STRIDE
    n_slc = S // SLC_BLOCK
    if n_slc > LANES:
        raise ValueError("selection mask is packed into one lane width: needs S <= 64*128")
    n_sel = min(SLC_COUNT, n_slc)
    ratio, n_inner = SLC_BLOCK // CMP_STRIDE, CMP_BLOCK // CMP_STRIDE
    mt = np.zeros((n_slc, nchunk), np.float32)
    for j in range(n_slc):
        for m in range(ratio):
            for n in range(n_inner):
                c = ratio * j + m - n
                if 0 <= c < nchunk - 1:
                    mt[j, c] += 1.0
    G = NSA_GQA
    kv_spec = lambda t: pl.BlockSpec((None, None, None, nchunk, HEAD_DIM), lambda b, h, i: (t, b, h, 0, 0))
    return pl.pallas_call(
        functools.partial(_cmp_select_kernel, tq=tq, nchunk=nchunk, n_slc=n_slc, n_sel=n_sel),
        grid=(B, NSA_KV_HEADS, nq),
        in_specs=[pl.BlockSpec((G, tq, LANES), lambda b, h, i: (CB_NSAQ // G + h, b * nq + i, 0)),
                  kv_spec(0), kv_spec(1),
                  pl.BlockSpec((n_slc, nchunk), lambda b, h, i: (0, 0))],
        out_specs=[pl.BlockSpec((G, tq, LANES), lambda b, h, i: (h, b * nq + i, 0)),
                   pl.BlockSpec((None, None, tq, LANES), lambda b, h, i: (b, h, i, 0))],
        out_shape=[jax.ShapeDtypeStruct((NSA_Q_HEADS, T, LANES), BF16),
                   jax.ShapeDtypeStruct((B, NSA_KV_HEADS, S, LANES), BF16)],
        compiler_params=_cparams(("parallel", "parallel", "arbitrary")),
        name="nsa_cmp_select",
    )(proj, kvc, kvc, jnp.asarray(mt))


def _slc_kernel(qi_ref, ki_ref, q_ref, k_ref, v_ref, sel_ref, bias_ref, o_ref, m_ref, l_ref, acc_ref,
                *, tq, tk, nb):
    p = pl.program_id(1)
    qi, ki = qi_ref[p], ki_ref[p]

    @pl.when(ki == 0)
    def _():
        _flash_init(m_ref, l_ref, acc_ref)

    blk_of_key = ki * (tk // SLC_BLOCK) + lax.broadcasted_iota(jnp.int32, (LANES, tk), 1) // SLC_BLOCK
    expand = jnp.where(lax.broadcasted_iota(jnp.int32, (LANES, tk), 0) == blk_of_key, 1.0, 0.0).astype(BF16)

    def step(b, causal_mask):
        picked = jnp.dot(sel_ref[b], expand, preferred_element_type=F32)
        keep = picked > 0.5
        if causal_mask:
            qpos = qi * tq + lax.broadcasted_iota(jnp.int32, (tq, tk), 0)
            kpos = ki * tk + lax.broadcasted_iota(jnp.int32, (tq, tk), 1)
            keep = jnp.where(kpos <= qpos, picked, 0.0) > 0.5
        k, v = k_ref[b], v_ref[b]
        qk = [_qk(q_ref[g, b], k) for g in range(NSA_GQA)]
        for g in range(NSA_GQA):
            s = qk[g] * (ATTN_SCALE * LOG2E) + bias_ref[g]
            s = jnp.where(keep, s, NEG_INF)
            _flash_update(s, v, m_ref.at[g, b], l_ref.at[g, b], acc_ref.at[g, b])

    crosses_diagonal = (ki + 1) * tk - 1 > qi * tq

    @pl.when(crosses_diagonal)
    def _():
        _for_range(0, nb, lambda b: step(b, True))

    @pl.when(jnp.logical_not(crosses_diagonal))
    def _():
        _for_range(0, nb, lambda b: step(b, False))

    @pl.when(ki == ((qi + 1) * tq - 1) // tk)
    def _():
        o_ref[...] = _flash_result(l_ref, acc_ref).astype(o_ref.dtype)


def slc_bias_table(rel_bias_c, S, tq, tk):
    buckets = _t5_bucket_np(np.arange(S + tk))
    not_last = np.nonzero(buckets != REL_BUCKETS - 1)[0]
    far_start = int(not_last[-1]) + 1 if not_last.size else 0
    n_delta = min(S // tq, -(-(far_start + tk - 1) // tq) + 1)

    def fn(d):
        return rel_bias_c[_t5_bucket_np(d)].T * LOG2E

    return _toeplitz_blocks(fn, [dl * tq for dl in range(n_delta)], tq, tk)


def nsa_selected(proj, sel, bias_tbl, B, S, *, tq=256, tk=512):
    tq, tk = min(tq, S), min(tk, S)
    T = B * S
    G = NSA_GQA
    ck = CB_NSAKV + (1 * 2 + 0) * NSA_KV_HEADS
    cv = CB_NSAKV + (1 * 2 + 1) * NSA_KV_HEADS
    qis, kis = _causal_pairs(S // tq, tq, tk)
    n_delta = bias_tbl.shape[0]
    proj4 = proj.reshape(proj.shape[0], B, S, LANES)
    grid_spec = pltpu.PrefetchScalarGridSpec(
        num_scalar_prefetch=2,
        grid=(NSA_KV_HEADS, int(qis.shape[0])),
        in_specs=[
            pl.BlockSpec((G, B, tq, LANES), lambda h, p, qi, ki: (CB_NSAQ // G + h, 0, qi[p], 0)),
            pl.BlockSpec((None, B, tk, LANES), lambda h, p, qi, ki: (ck + h, 0, ki[p], 0)),
            pl.BlockSpec((None, B, tk, LANES), lambda h, p, qi, ki: (cv + h, 0, ki[p], 0)),
            pl.BlockSpec((B, None, tq, LANES), lambda h, p, qi, ki: (0, h, qi[p], 0)),
            pl.BlockSpec((None, G, tq, tk),
                         lambda h, p, qi, ki: (jnp.minimum(qi[p] - ki[p] * (tk // tq), n_delta - 1), h, 0, 0)),
        ],
        out_specs=pl.BlockSpec((G, B, tq, LANES), lambda h, p, qi, ki: (h, 0, qi[p], 0)),
        scratch_shapes=[pltpu.VMEM((G, B, tq, LANES), F32)] * 3,
    )
    out = pl.pallas_call(
        functools.partial(_slc_kernel, tq=tq, tk=tk, nb=B),
        grid_spec=grid_spec,
        out_shape=jax.ShapeDtypeStruct((NSA_Q_HEADS, B, S, LANES), BF16),
        compiler_params=_cparams(("parallel", "arbitrary")),
        name="nsa_selected",
    )(qis, kis, proj4, proj4, proj4, sel, bias_tbl)
    return out.reshape(NSA_Q_HEADS, T, LANES)


def _win_kernel(q_ref, k0_ref, k1_ref, k2_ref, v0_ref, v1_ref, v2_ref, bias_ref, o_ref, *, tq, nkb):
    qi = pl.program_id(2)
    k = jnp.concatenate([r[...] for r in (k0_ref, k1_ref, k2_ref)][-nkb:], axis=0)
    v = jnp.concatenate([r[...] for r in (v0_ref, v1_ref, v2_ref)][-nkb:], axis=0)
    kpos = (qi - (nkb - 1)) * tq + lax.broadcasted_iota(jnp.int32, (tq, nkb * tq), 1)
    qk = [_qk(q_ref[g], k) for g in range(NSA_GQA)]
    for g in range(NSA_GQA):
        s = qk[g] * ATTN_SCALE + bias_ref[g]
        p = _masked_softmax(s, kpos >= 0, -1)
        o_ref[g] = jnp.dot(p.astype(BF16), v, preferred_element_type=F32).astype(o_ref.dtype)


def win_bias_table(rel_bias_c, tq, nkb):
    def fn(dist):
        valid = (dist >= 0) & (dist < WIN)
        return jnp.where(valid[None, :], rel_bias_c[_t5_bucket_np(dist)].T, NEG_INF)

    return _toeplitz_blocks(fn, [(nkb - 1) * tq], tq, nkb * tq)[0]


def nsa_window(proj, bias_tbl, B, S, *, tq=256):
    tq = min(tq, S)
    nq = S // tq
    T = B * S
    G = NSA_GQA
    nkb = min(WIN // tq + 1, 3)
    ck = CB_NSAKV + (2 * 2 + 0) * NSA_KV_HEADS
    cv = CB_NSAKV + (2 * 2 + 1) * NSA_KV_HEADS
    kv = lambda c, back: pl.BlockSpec(
        (None, tq, LANES), lambda b, h, i: (c + h, b * nq + jnp.maximum(i - back, 0), 0))
    return pl.pallas_call(
        functools.partial(_win_kernel, tq=tq, nkb=nkb),
        grid=(B, NSA_KV_HEADS, nq),
        in_specs=[pl.BlockSpec((G, tq, LANES), lambda b, h, i: (CB_NSAQ // G + h, b * nq + i, 0)),
                  kv(ck, 2), kv(ck, 1), kv(ck, 0), kv(cv, 2), kv(cv, 1), kv(cv, 0),
                  pl.BlockSpec((G, tq, nkb * tq), lambda b, h, i: (h, 0, 0))],
        out_specs=pl.BlockSpec((G, tq, LANES), lambda b, h, i: (h, b * nq + i, 0)),
        out_shape=jax.ShapeDtypeStruct((NSA_Q_HEADS, T, LANES), BF16),
        compiler_params=_cparams(("parallel", "parallel", "arbitrary")),
        name="nsa_window",
    )(proj, proj, proj, proj, proj, proj, proj, bias_tbl)


def _sigmoid(x):
    return 1.0 / (1.0 + jnp.exp(-x))


def _merge_kernel(fox_ref, d0_ref, d1_ref, d2_ref, l0_ref, l1_ref, l2_ref, cmp_ref, slc_ref, win_ref, small_ref,
                  g0_ref, g1_ref, g2_ref, wa_ref, wb_ref, wc_ref, wo_ref, x_ref, o_ref, nat_ref, *, ncb):
    tm = o_ref.shape[0]

    def natural(ref, h, dil, slot):
        if dil == 1:
            return ref[h].astype(F32)
        for r in range(dil):
            nat_ref[slot, pl.ds(r, tm // dil, stride=dil), :] = ref[h, :, r * LANES:(r + 1) * LANES].astype(F32)
        return nat_ref[slot]

    ya = jnp.concatenate([fox_ref[h] for h in range(FOX_HEADS)], axis=1)
    yb = []
    for h in range(DIL_HEADS_PER_GROUP):
        outs, lse = [], []
        for g, (o_g, l_g) in enumerate(((d0_ref, l0_ref), (d1_ref, l1_ref), (d2_ref, l2_ref))):
            dil = DIL_GROUPS[g][1]
            slot = (h * len(DIL_GROUPS) + g) * 2
            outs.append(natural(o_g, h, dil, slot))
            lse.append(natural(l_g, h, dil, slot + 1))
        top = jnp.maximum(jnp.maximum(lse[0], lse[1]), lse[2])
        w = [jnp.exp(x - top) for x in lse]
        tot = w[0] + w[1] + w[2]
        y = sum((w[g] / tot) * outs[g] for g in range(len(DIL_GROUPS)))
        yb.append(y.astype(BF16))
    gates = _sigmoid(small_ref[...])
    yc = []
    for h in range(NSA_Q_HEADS):
        y = jnp.zeros(cmp_ref.shape[1:], F32)
        for br, ref in enumerate((cmp_ref, slc_ref, win_ref)):
            col = SMALL_NSA_G + br * NSA_Q_HEADS + h
            y = y + gates[:, col:col + 1] * ref[h].astype(F32)
        yc.append(y.astype(BF16))

    def gate(ref):
        return _sigmoid(jnp.concatenate([ref[c] for c in range(ncb)], axis=1).astype(F32))

    merged = (gate(g0_ref) * jnp.dot(ya, wa_ref[...], preferred_element_type=F32)
              + gate(g1_ref) * jnp.dot(jnp.concatenate(yb, axis=1), wb_ref[...], preferred_element_type=F32)
              + gate(g2_ref) * jnp.dot(jnp.concatenate(yc, axis=1), wc_ref[...], preferred_element_type=F32))
    o_ref[...] = x_ref[...] + jnp.dot(merged.astype(BF16), wo_ref[...], preferred_element_type=F32)


def merge_branches(x, proj, small, fox_o, dil_o, dil_lse, cmp_o, slc_o, win_o, wa, wb, wc, w_out, *, tm=256):
    T, D = x.shape
    tm = min(tm, T)
    ncb = D // LANES
    Hg = DIL_HEADS_PER_GROUP
    heads = lambda n: pl.BlockSpec((n, tm, LANES), lambda i: (0, i, 0))
    gate = lambda b: pl.BlockSpec((ncb, tm, LANES), lambda i: (CB_MERGE // ncb + b, i, 0))
    dil = [pl.BlockSpec((Hg, tm // d, d * LANES), lambda i: (0, i, 0)) for _, d in DIL_GROUPS]
    wspec = lambda w: pl.BlockSpec(w.shape, lambda i: (0, 0), pipeline_mode=pl.Buffered(1))
    return pl.pallas_call(
        functools.partial(_merge_kernel, ncb=ncb),
        grid=(T // tm,),
        in_specs=[heads(FOX_HEADS), *dil, *dil, heads(NSA_Q_HEADS), heads(NSA_Q_HEADS), heads(NSA_Q_HEADS),
                  pl.BlockSpec((tm, LANES), lambda i: (i, 0)),
                  gate(0), gate(1), gate(2), wspec(wa), wspec(wb), wspec(wc), wspec(w_out),
                  pl.BlockSpec((tm, D), lambda i: (i, 0))],
        out_specs=pl.BlockSpec((tm, D), lambda i: (i, 0)),
        out_shape=jax.ShapeDtypeStruct((T, D), F32),
        scratch_shapes=[pltpu.VMEM((2 * Hg * len(DIL_GROUPS), tm, LANES), F32)],
        compiler_params=_cparams(("parallel",)),
        name="merge_branches",
    )(fox_o, *dil_o, *dil_lse, cmp_o, slc_o, win_o, small, proj, proj, proj, wa, wb, wc, w_out, x)


def _silu(x):
    return x * _sigmoid(x)


def _swiglu_partial(h, wg_ref, wu_ref, wd_ref):
    act = _silu(jnp.dot(h, wg_ref[...].astype(BF16), preferred_element_type=F32)) * \
        jnp.dot(h, wu_ref[...].astype(BF16), preferred_element_type=F32)
    return jnp.dot(act.astype(BF16), wd_ref[...].astype(BF16), preferred_element_type=F32)


def _ffn_kernel(x_ref, g_ref, wg_ref, wu_ref, wd_ref, gf_ref, o_ref, h_ref, *, final_norm):
    f = pl.program_id(1)

    @pl.when(f == 0)
    def _():
        h_ref[...] = _rms_rows(x_ref[...], g_ref[...]).astype(BF16)
        o_ref[...] = x_ref[...]

    o_ref[...] += _swiglu_partial(h_ref[...], wg_ref, wu_ref, wd_ref)

    if final_norm:
        @pl.when(f == pl.num_programs(1) - 1)
        def _():
            o_ref[...] = _rms_rows(o_ref[...], gf_ref[...])


def ffn(x, g, wg, wu, wd, g_final, *, final_norm, tm=1024, tf=256):
    T, D = x.shape
    F = wg.shape[1]
    tm, tf = min(tm, T), min(tf, F)
    return pl.pallas_call(
        functools.partial(_ffn_kernel, final_norm=final_norm),
        grid=(T // tm, F // tf),
        in_specs=[pl.BlockSpec((tm, D), lambda i, f: (i, 0)),
                  pl.BlockSpec((1, D), lambda i, f: (0, 0)),
                  pl.BlockSpec((D, tf), lambda i, f: (0, f)),
                  pl.BlockSpec((D, tf), lambda i, f: (0, f)),
                  pl.BlockSpec((tf, D), lambda i, f: (f, 0)),
                  pl.BlockSpec((1, D), lambda i, f: (0, 0))],
        out_specs=pl.BlockSpec((tm, D), lambda i, f: (i, 0)),
        out_shape=jax.ShapeDtypeStruct((T, D), F32),
        scratch_shapes=[pltpu.VMEM((tm, D), BF16)],
        compiler_params=_cparams(("parallel", "arbitrary")),
        name="ffn",
    )(x, g, wg, wu, wd, g_final)


def _route_kernel(x_ref, g_ref, r_ref, o_ref):
    h = _rms_rows(x_ref[...], g_ref[...])
    logits = jnp.dot(h, r_ref[...], preferred_element_type=F32, precision=HIGHEST)
    lane = lax.broadcasted_iota(jnp.int32, logits.shape, 1)
    logits = jnp.where(lane < N_EXPERTS, logits, NEG_INF)
    v1 = jnp.max(logits, axis=1, keepdims=True)
    i1 = jnp.min(jnp.where(logits == v1, lane, LANES), axis=1, keepdims=True)
    rest = jnp.where(lane == i1, NEG_INF, logits)
    v2 = jnp.max(rest, axis=1, keepdims=True)
    i2 = jnp.min(jnp.where(rest == v2, lane, LANES), axis=1, keepdims=True)
    e2 = jnp.exp(v2 - v1)
    den = 1.0 + e2
    rec = jnp.where(lane == ROUTE_CHOICE + i1, 1.0, jnp.where(lane == ROUTE_CHOICE + i2, 2.0, 0.0))
    rec = jnp.where(lane == ROUTE_WEIGHT, 1.0 / den, jnp.where(lane == ROUTE_WEIGHT + 1, e2 / den, rec))
    o_ref[...] = rec


def moe_route(x, g, router_pad, *, tm=1024):
    T, D = x.shape
    tm = min(tm, T)
    return pl.pallas_call(
        _route_kernel,
        grid=(T // tm,),
        in_specs=[pl.BlockSpec((tm, D), lambda i: (i, 0)),
                  pl.BlockSpec((1, D), lambda i: (0, 0)),
                  pl.BlockSpec((D, LANES), lambda i: (0, 0))],
        out_specs=pl.BlockSpec((tm, LANES), lambda i: (i, 0)),
        out_shape=jax.ShapeDtypeStruct((T, LANES), F32),
        compiler_params=_cparams(("parallel",)),
        name="moe_route",
    )(x, g, router_pad)


def _moe_plan(route, T, tm):
    E = N_EXPERTS
    n_tiles = (TOP_K * T + E * tm) // tm
    choice = route[:, ROUTE_CHOICE:ROUTE_CHOICE + E]
    seli = (choice > 0.5).astype(jnp.int32)
    counts = jnp.sum(seli, axis=0)
    padded = ((counts + tm - 1) // tm) * tm
    seg_end = jnp.cumsum(padded)
    seg_start = seg_end - padded
    slot = seg_start[None, :] + jnp.cumsum(seli, axis=0) - 1
    slot_of = jnp.stack([jnp.sum(jnp.where(choice == k + 1.0, slot, 0), axis=1) for k in range(TOP_K)])
    n_used = (seg_end[-1] // tm).astype(jnp.int32)
    tile_start = jnp.arange(n_tiles, dtype=jnp.int32) * tm
    tile_expert = jnp.sum((seg_end[None, :] <= tile_start[:, None]).astype(jnp.int32), axis=1)
    tile_expert = jnp.minimum(tile_expert, E - 1)
    last_expert = jnp.take(tile_expert, n_used - 1)
    tile_expert = jnp.where(jnp.arange(n_tiles) < n_used, tile_expert, last_expert)
    pad_lo = (seg_start + counts).astype(jnp.int32)
    return (tile_expert.astype(jnp.int32), n_used.reshape(1), slot_of.reshape(-1).astype(jnp.int32),
            pad_lo, seg_end.astype(jnp.int32))


def _for_range(lo, hi, fn, unroll=None):
    def body(r, carry):
        fn(r)
        return carry
    lax.fori_loop(lo, hi, body, 0, unroll=unroll)


def _moe_dispatch_kernel(slot_ref, lo_ref, hi_ref, nt_ref, x_ref, xs_hbm, zero_ref, sem, zsem,
                         *, tc, T, chunks_per_tile, n_chunks):
    i = pl.program_id(0)

    def token_row(k, r):
        s = slot_ref[k * T + i * tc + r]
        return pltpu.make_async_copy(x_ref.at[pl.ds(r, 1)], xs_hbm.at[pl.ds(s, 1)], sem.at[0])

    def zero_row(s):
        return pltpu.make_async_copy(zero_ref.at[pl.ds(0, 1)], xs_hbm.at[pl.ds(s, 1)], zsem.at[0])

    def zero_chunk(c):
        rows = pl.ds(pl.multiple_of(c * MOE_ZERO_ROWS, MOE_ZERO_ROWS), MOE_ZERO_ROWS)
        return pltpu.make_async_copy(zero_ref, xs_hbm.at[rows], zsem.at[0])

    @pl.when(i == 0)
    def _():
        zero_ref[...] = jnp.zeros(zero_ref.shape, F32)
        for start in (True, False):
            for e in range(N_EXPERTS):
                _for_range(lo_ref[e], hi_ref[e], lambda s: zero_row(s).start() if start else zero_row(s).wait())
            _for_range(nt_ref[0] * chunks_per_tile, n_chunks,
                       lambda c: zero_chunk(c).start() if start else zero_chunk(c).wait())

    for k in range(TOP_K):
        _for_range(0, tc, lambda r: token_row(k, r).start(), unroll=8)
    for k in range(TOP_K):
        pltpu.make_async_copy(x_ref, xs_hbm.at[pl.ds(0, tc)], sem.at[0]).wait()


def moe_dispatch(x, plan, *, tm, tc=512):
    _, n_used, slot_of, pad_lo, pad_hi = plan
    T, D = x.shape
    tc = min(tc, T)
    n_slots = TOP_K * T + N_EXPERTS * tm
    grid_spec = pltpu.PrefetchScalarGridSpec(
        num_scalar_prefetch=4,
        grid=(T // tc,),
        in_specs=[pl.BlockSpec((tc, D), lambda i, s, lo, hi, nt: (i, 0))],
        out_specs=pl.BlockSpec(memory_space=pl.ANY),
        scratch_shapes=[pltpu.VMEM((MOE_ZERO_ROWS, D), F32),
                        pltpu.SemaphoreType.DMA((1,)), pltpu.SemaphoreType.DMA((1,))],
    )
    return pl.pallas_call(
        functools.partial(_moe_dispatch_kernel, tc=tc, T=T, chunks_per_tile=tm // MOE_ZERO_ROWS,
                          n_chunks=n_slots // MOE_ZERO_ROWS),
        grid_spec=grid_spec,
        out_shape=jax.ShapeDtypeStruct((n_slots, D), F32),
        compiler_params=_cparams(("arbitrary",)),
        name="moe_dispatch",
    )(slot_of, pad_lo, pad_hi, n_used, x)


def _moe_ffn_kernel(te_ref, nt_ref, xs_ref, g_ref, wg_ref, wu_ref, wd_ref, y_ref, h_ref):
    i, f = pl.program_id(0), pl.program_id(1)
    used = i < nt_ref[0]

    @pl.when(used & (f == 0))
    def _():
        h_ref[...] = _rms_rows(xs_ref[...], g_ref[...]).astype(BF16)
        y_ref[...] = _swiglu_partial(h_ref[...], wg_ref, wu_ref, wd_ref)

    @pl.when(used & (f > 0))
    def _():
        y_ref[...] += _swiglu_partial(h_ref[...], wg_ref, wu_ref, wd_ref)

    @pl.when(jnp.logical_not(used) & (f == 0))
    def _():
        y_ref[...] = jnp.zeros(y_ref.shape, F32)


def moe_experts(xs, g, plan, wg, wu, wd, *, tm, tf=1024):
    tile_expert, n_used = plan[0], plan[1]
    n_slots, D = xs.shape
    F = wg.shape[2]
    tf = min(tf, F)
    nf = F // tf

    def fidx(i, f, nt):
        return jnp.where(i < nt[0], f, nf - 1)

    grid_spec = pltpu.PrefetchScalarGridSpec(
        num_scalar_prefetch=2,
        grid=(n_slots // tm, nf),
        in_specs=[
            pl.BlockSpec((tm, D), lambda i, f, te, nt: (jnp.minimum(i, nt[0] - 1), 0)),
            pl.BlockSpec((1, D), lambda i, f, te, nt: (0, 0)),
            pl.BlockSpec((None, D, tf), lambda i, f, te, nt: (te[i], 0, fidx(i, f, nt))),
            pl.BlockSpec((None, D, tf), lambda i, f, te, nt: (te[i], 0, fidx(i, f, nt))),
            pl.BlockSpec((None, tf, D), lambda i, f, te, nt: (te[i], fidx(i, f, nt), 0)),
        ],
        out_specs=pl.BlockSpec((tm, D), lambda i, f, te, nt: (i, 0)),
        scratch_shapes=[pltpu.VMEM((tm, D), BF16)],
    )
    return pl.pallas_call(
        _moe_ffn_kernel,
        grid_spec=grid_spec,
        out_shape=jax.ShapeDtypeStruct((n_slots, D), F32),
        compiler_params=_cparams(("parallel", "arbitrary")),
        name="moe_experts",
    )(tile_expert, n_used, xs, g, wg, wu, wd)


def _moe_combine_kernel(slot_ref, x_ref, y_hbm, route_ref, gf_ref, o_ref, ybuf, sem, *, tc, T, final_norm):
    i = pl.program_id(0)

    def slot_row(k, r):
        s = slot_ref[k * T + i * tc + r]
        return pltpu.make_async_copy(y_hbm.at[pl.ds(s, 1)], ybuf.at[k, pl.ds(r, 1)], sem.at[0])

    for k in range(TOP_K):
        _for_range(0, tc, lambda r: slot_row(k, r).start(), unroll=8)
    for k in range(TOP_K):
        pltpu.make_async_copy(y_hbm.at[pl.ds(0, tc)], ybuf.at[k], sem.at[0]).wait()

    w = route_ref[...]
    out = x_ref[...]
    for k in range(TOP_K):
        out = out + w[:, ROUTE_WEIGHT + k:ROUTE_WEIGHT + k + 1] * ybuf[k]
    if final_norm:
        out = _rms_rows(out, gf_ref[...])
    o_ref[...] = out


def moe_combine(x, y, route, plan, g_final, *, final_norm, tc=256):
    T, D = x.shape
    tc = min(tc, T)
    grid_spec = pltpu.PrefetchScalarGridSpec(
        num_scalar_prefetch=1,
        grid=(T // tc,),
        in_specs=[pl.BlockSpec((tc, D), lambda i, s: (i, 0)),
                  pl.BlockSpec(memory_space=pl.ANY),
                  pl.BlockSpec((tc, LANES), lambda i, s: (i, 0)),
                  pl.BlockSpec((1, D), lambda i, s: (0, 0))],
        out_specs=pl.BlockSpec((tc, D), lambda i, s: (i, 0)),
        scratch_shapes=[pltpu.VMEM((TOP_K, tc, D), F32), pltpu.SemaphoreType.DMA((1,))],
    )
    return pl.pallas_call(
        functools.partial(_moe_combine_kernel, tc=tc, T=T, final_norm=final_norm),
        grid_spec=grid_spec,
        out_shape=jax.ShapeDtypeStruct((T, D), F32),
        compiler_params=_cparams(("arbitrary",)),
        name="moe_combine",
    )(plan[2], x, y, route, g_final)


def moe_block(x, g, router, wg, wu, wd, g_final, *, final_norm, tm=512):
    T, D = x.shape
    tm = min(tm, T)
    router_pad = jnp.zeros((D, LANES), F32).at[:, :N_EXPERTS].set(router)
    route = moe_route(x, g, router_pad)
    plan = _moe_plan(route, T, tm)
    xs = moe_dispatch(x, plan, tm=tm)
    y = moe_experts(xs, g, plan, wg.astype(BF16), wu.astype(BF16), wd, tm=tm)
    return moe_combine(x, y, route, plan, g_final, final_norm=final_norm)


def _split_w_in(w):
    D = w.shape[0]
    n_fox = 3 * FOX_HEADS * HEAD_DIM
    n_dil = 3 * DIL_HEADS * HEAD_DIM
    n_nsa = (NSA_Q_HEADS + NSA_BRANCHES * 2 * NSA_KV_HEADS) * HEAD_DIM
    n_g = NSA_BRANCHES * NSA_Q_HEADS
    a0 = n_fox
    a1 = a0 + FOX_HEADS
    a2 = a1 + n_dil
    a3 = a2 + n_nsa
    a4 = a3 + n_g
    group_cols = DIL_HEADS_PER_GROUP * HEAD_DIM
    dil = w[:, a1:a2].reshape(D, 3, len(DIL_GROUPS), group_cols)
    dil_sets = [dil[:, :, gi].reshape(D, 3 * group_cols).astype(BF16) for gi in range(len(DIL_GROUPS))]
    main = jnp.concatenate([w[:, a4:].astype(BF16), w[:, :a0].astype(BF16), dil_sets[0],
                            w[:, a2:a3].astype(BF16)], axis=1)
    small = jnp.concatenate([w[:, a0:a1], w[:, a3:a4],
                             jnp.zeros((D, LANES - FOX_HEADS - n_g), w.dtype)], axis=1).astype(BF16)
    return main, dil_sets[1:], small


def mixing_block(x, B, S, norm_g, w_in, forget_bias, cmp_pe, cmp_w1, cmp_w2, wa, wb, wc, w_out, tables,
                 cast=None):
    T, D = x.shape
    w_main, w_dil, w_small = _split_w_in(w_in)
    g = norm_g.reshape(1, D)
    proj, h, cast_bf16 = rms_proj(x, g, w_main, BF16, tm=1024, tn=1024, cast=cast)
    small = slab_proj(h, w_small, F32, tm=1024, tn=LANES)[0]
    dil_views = [proj_view(h, w, DIL_GROUPS[gi + 1][1], tm=1024, tn=w.shape[1]) for gi, w in enumerate(w_dil)]

    gate_lanes = slice(SMALL_FOX_F, SMALL_FOX_F + FOX_HEADS)
    bias_row = jnp.zeros((1, LANES), F32).at[0, gate_lanes].set(forget_bias)
    c = logf_cumsum(small, bias_row, B, S)
    c_row = c[:, gate_lanes].T.reshape(FOX_HEADS, 1, T)
    fox_o = fox_attention(proj, c, c_row, B, S)

    dil = [dilated_group(proj, CB_DIL, tables["dil"][0], 0, B, S)]
    dil += [dilated_group(v, 0, tables["dil"][gi + 1], gi + 1, B, S) for gi, v in enumerate(dil_views)]
    dil_o = [d[0] for d in dil]
    dil_lse = [d[1] for d in dil]

    kvc = nsa_compress(proj, cmp_pe, cmp_w1.astype(BF16), cmp_w2.astype(BF16), B, S)
    cmp_o, sel = nsa_cmp_select(proj, kvc, B, S)
    slc_o = nsa_selected(proj, sel, tables["slc"], B, S, tq=tables["slc_tq"], tk=tables["slc_tk"])
    win_o = nsa_window(proj, tables["win"], B, S, tq=tables["win_tq"])

    out = merge_branches(x, proj, small, fox_o, dil_o, dil_lse, cmp_o, slc_o, win_o,
                         wa.astype(BF16), wb.astype(BF16), wc.astype(BF16), w_out.astype(BF16))
    return out, cast_bf16


def bias_tables(rel_bias, S):
    slc_tq, slc_tk = min(256, S), min(512, S)
    win_tq = min(256, S)
    rel_c = rel_bias[:, DIL_HEADS:]
    return {
        "dil": dilated_bias_tables(rel_bias[:, :DIL_HEADS]),
        "slc": slc_bias_table(rel_c, S, slc_tq, slc_tk), "slc_tq": slc_tq, "slc_tk": slc_tk,
        "win": win_bias_table(rel_c, win_tq, min(WIN // win_tq + 1, 3)), "win_tq": win_tq,
    }


def kernel(x, rel_bias, norm_mix_g, norm_ffn_g, norm_final_g, w_in, fox_forget_bias, cmp_pe_k, cmp_w1_k, cmp_w2_k, cmp_pe_v, cmp_w1_v, cmp_w2_v, w_branch_a, w_branch_b, w_branch_c, w_out, ffn_w_gate, ffn_w_up, ffn_w_down, moe_router, moe_w_gate, moe_w_up, moe_w_down):
    B, S, D = x.shape
    if D != D_MODEL:
        raise ValueError(f"column-block layout is built for d_model={D_MODEL}, got {D}")
    T = B * S
    depth = w_in.shape[0]
    tables = bias_tables(rel_bias, S)
    g_final = norm_final_g.reshape(1, D)
    xt = x.reshape(T, D)
    side_cast = {}
    for l in range(1, depth, 2):
        side_cast[l - 1] = moe_w_gate[l // 2]
        side_cast[l] = moe_w_up[l // 2]
    rounded = {}
    for l in range(depth):
        xt, rounded[l] = mixing_block(
            xt, B, S, norm_mix_g[l], w_in[l], fox_forget_bias[l],
            jnp.stack([cmp_pe_k[l], cmp_pe_v[l]]), jnp.stack([cmp_w1_k[l], cmp_w1_v[l]]),
            jnp.stack([cmp_w2_k[l], cmp_w2_v[l]]),
            w_branch_a[l], w_branch_b[l], w_branch_c[l], w_out[l], tables, cast=side_cast.get(l))
        g = norm_ffn_g[l].reshape(1, D)
        last = l == depth - 1
        j = l // 2
        if l % 2 == 0:
            xt = ffn(xt, g, ffn_w_gate[j], ffn_w_up[j], ffn_w_down[j], g_final, final_norm=last)
        else:
            xt = moe_block(xt, g, moe_router[j], rounded[l - 1], rounded[l], moe_w_down[j], g_final,
                           final_norm=last)
    return xt.reshape(B, S, D)
```

```python
import functools
import math

import jax
import jax.numpy as jnp
import numpy as np
from jax import lax
from jax.experimental import pallas as pl
from jax.experimental.pallas import tpu as pltpu

F32 = jnp.float32
BF16 = jnp.bfloat16
HIGHEST = lax.Precision.HIGHEST
NEG_INF = float("-inf")

LANES = 128
HEAD_DIM = 128
ATTN_SCALE = HEAD_DIM ** -0.5
RMS_EPS = 1e-6
VMEM_LIMIT_BYTES = 58 * 1024 * 1024

D_MODEL = 2048
FOX_HEADS = 8
DIL_GROUPS = ((128, 1), (512, 4), (2048, 16))
DIL_HEADS_PER_GROUP = 4
DIL_HEADS = DIL_HEADS_PER_GROUP * len(DIL_GROUPS)
DIL_TAPS = 128
NSA_Q_HEADS = 8
NSA_KV_HEADS = 2
NSA_GQA = NSA_Q_HEADS // NSA_KV_HEADS
NSA_BRANCHES = 3
CMP_BLOCK = 32
CMP_STRIDE = 16
CMP_HIDDEN = 256
SLC_BLOCK = 64
SLC_COUNT = 16
WIN = 512
REL_BUCKETS = 32
REL_MAX_EXACT = 16
REL_MAX_DIST = 2048
N_EXPERTS = 8
TOP_K = 2
N_BRANCHES = 3

CB_MERGE = 0
CB_FOX = CB_MERGE + N_BRANCHES * (D_MODEL // LANES)
CB_DIL = CB_FOX + 3 * FOX_HEADS
CB_NSAQ = CB_DIL + 3 * DIL_HEADS_PER_GROUP
CB_NSAKV = CB_NSAQ + NSA_Q_HEADS
CB_END = CB_NSAKV + NSA_BRANCHES * 2 * NSA_KV_HEADS
SMALL_FOX_F = 0
SMALL_NSA_G = FOX_HEADS
ROUTE_CHOICE = 0
ROUTE_WEIGHT = N_EXPERTS
MOE_ZERO_ROWS = 64


def _cparams(semantics):
    return pltpu.CompilerParams(dimension_semantics=semantics,
                                vmem_limit_bytes=VMEM_LIMIT_BYTES)


def _t5_bucket_np(dist):
    dist = np.maximum(dist, 0)
    d = np.maximum(dist, 1).astype(np.float32)
    log_ratio = np.log(d / np.float32(REL_MAX_EXACT)) / np.float32(math.log(REL_MAX_DIST / REL_MAX_EXACT))
    large = REL_MAX_EXACT + (log_ratio * np.float32(REL_BUCKETS - REL_MAX_EXACT)).astype(np.int32)
    large = np.minimum(large, REL_BUCKETS - 1)
    return np.where(dist < REL_MAX_EXACT, dist, large).astype(np.int32)


def _toeplitz_kernel(v_ref, o_ref, *, rows, cols):
    x = jnp.broadcast_to(v_ref[...], (rows, v_ref.shape[-1]))
    o_ref[...] = pltpu.roll(x, 0, 1, stride=1, stride_axis=0)[:, :cols]


def _toeplitz_blocks(fn, offsets, rows, cols):
    lx = -(-(rows + cols) // LANES) * LANES
    m = np.arange(lx)
    rel = np.where(m < cols, -m, lx - m)
    v = jnp.stack([fn(c + rel) for c in offsets]).astype(F32)
    n_off, H = v.shape[:2]
    return pl.pallas_call(
        functools.partial(_toeplitz_kernel, rows=rows, cols=cols),
        grid=(n_off, H),
        in_specs=[pl.BlockSpec((None, None, 1, lx), lambda o, h: (o, h, 0, 0))],
        out_specs=pl.BlockSpec((None, None, rows, cols), lambda o, h: (o, h, 0, 0)),
        out_shape=jax.ShapeDtypeStruct((n_off, H, rows, cols), F32),
        compiler_params=_cparams(("parallel", "parallel")),
        name="toeplitz_table",
    )(v.reshape(n_off, H, 1, lx))


def _rms_rows(x, g):
    inv = lax.rsqrt(jnp.mean(x * x, axis=-1, keepdims=True) + RMS_EPS)
    return (x * inv) * g


def _rms_proj_kernel(x_ref, g_ref, w_ref, *rest, ncb, cast_cols):
    if cast_cols:
        src_ref, o_ref, h_ref, dst_ref = rest
    else:
        o_ref, h_ref = rest

    @pl.when(pl.program_id(1) == 0)
    def _():
        h_ref[...] = _rms_rows(x_ref[...], g_ref[...]).astype(BF16)

    res = jnp.dot(h_ref[...], w_ref[...], preferred_element_type=F32)
    for c in range(ncb):
        o_ref[c] = res[:, c * LANES:(c + 1) * LANES].astype(o_ref.dtype)

    if cast_cols:
        @pl.when(pl.program_id(1) < cast_cols)
        def _():
            dst_ref[...] = src_ref[...].astype(BF16)


def rms_proj(x, g, w, out_dtype, *, tm, tn, cast=None):
    T, D = x.shape
    N = w.shape[1]
    tm, tn = min(tm, T), min(tn, N)
    ncb = tn // LANES
    steps_i, steps_j = T // tm, N // tn
    in_specs = [pl.BlockSpec((tm, D), lambda i, j: (i, 0)),
                pl.BlockSpec((1, D), lambda i, j: (0, 0)),
                pl.BlockSpec((D, tn), lambda i, j: (0, j))]
    out_specs = [pl.BlockSpec((ncb, tm, LANES), lambda i, j: (j, i, 0)),
                 pl.BlockSpec((tm, D), lambda i, j: (i, 0))]
    out_shape = [jax.ShapeDtypeStruct((N // LANES, T, LANES), out_dtype),
                 jax.ShapeDtypeStruct((T, D), BF16)]
    operands = [x, g, w]
    cast_cols = 0
    if cast is not None:
        flat = cast.reshape(-1, cast.shape[-1])
        cast_cols = 1 << (steps_j.bit_length() - 1)
        rows = flat.shape[0] // (steps_i * cast_cols)
        if rows * steps_i * cast_cols != flat.shape[0] or rows % 16:
            raise ValueError("cast operand does not split into whole bf16 tiles over the grid")
        block = pl.BlockSpec((rows, flat.shape[1]), lambda i, j: (i * cast_cols + jnp.minimum(j, cast_cols - 1), 0))
        in_specs.append(block)
        out_specs.append(block)
        out_shape.append(jax.ShapeDtypeStruct(flat.shape, BF16))
        operands.append(flat)
    outs = pl.pallas_call(
        functools.partial(_rms_proj_kernel, ncb=ncb, cast_cols=cast_cols),
        grid=(steps_i, steps_j),
        in_specs=in_specs,
        out_specs=out_specs,
        out_shape=out_shape,
        compiler_params=_cparams(("parallel", "arbitrary")),
        name="rms_proj",
    )(*operands)
    if cast is None:
        return outs[0], outs[1], None
    return outs[0], outs[1], outs[2].reshape(cast.shape)


def _proj_kernel(h_ref, w_ref, o_ref, *, ncb):
    res = jnp.dot(h_ref[...], w_ref[...], preferred_element_type=F32)
    for c in range(ncb):
        o_ref[c] = res[:, c * LANES:(c + 1) * LANES].astype(o_ref.dtype)


def slab_proj(h, w, out_dtype, *, tm, tn):
    T, D = h.shape
    N = w.shape[1]
    tm, tn = min(tm, T), min(tn, N)
    ncb = tn // LANES
    return pl.pallas_call(
        functools.partial(_proj_kernel, ncb=ncb),
        grid=(T // tm, N // tn),
        in_specs=[pl.BlockSpec((tm, D), lambda i, j: (i, 0)),
                  pl.BlockSpec((D, tn), lambda i, j: (0, j))],
        out_specs=pl.BlockSpec((ncb, tm, LANES), lambda i, j: (j, i, 0)),
        out_shape=jax.ShapeDtypeStruct((N // LANES, T, LANES), out_dtype),
        compiler_params=_cparams(("parallel", "parallel")),
        name="slab_proj",
    )(h, w)


def _proj_view_kernel(h_ref, w_ref, o_ref, res_ref, *, ncb, dil):
    res = jnp.dot(h_ref[...], w_ref[...], preferred_element_type=F32)
    rows = res_ref.shape[1] // dil
    for c in range(ncb):
        res_ref[c] = res[:, c * LANES:(c + 1) * LANES]
        for r in range(dil):
            part = res_ref[c, pl.ds(r, rows, stride=dil), :]
            o_ref[c, :, r * LANES:(r + 1) * LANES] = part.astype(o_ref.dtype)


def proj_view(h, w, dil, *, tm, tn):
    T, D = h.shape
    N = w.shape[1]
    tm, tn = min(tm, T), min(tn, N)
    ncb = tn // LANES
    return pl.pallas_call(
        functools.partial(_proj_view_kernel, ncb=ncb, dil=dil),
        grid=(T // tm, N // tn),
        in_specs=[pl.BlockSpec((tm, D), lambda i, j: (i, 0)),
                  pl.BlockSpec((D, tn), lambda i, j: (0, j))],
        out_specs=pl.BlockSpec((ncb, tm // dil, dil * LANES), lambda i, j: (j, i, 0)),
        out_shape=jax.ShapeDtypeStruct((N // LANES, T // dil, dil * LANES), BF16),
        scratch_shapes=[pltpu.VMEM((ncb, tm, LANES), F32)],
        compiler_params=_cparams(("parallel", "parallel")),
        name="proj_view",
    )(h, w)


CUMSUM_BLOCK = 256


def _logf_cumsum_kernel(f_ref, b_ref, c_ref, *, nblk):
    row = lax.broadcasted_iota(jnp.int32, (CUMSUM_BLOCK, CUMSUM_BLOCK), 0)
    col = lax.broadcasted_iota(jnp.int32, (CUMSUM_BLOCK, CUMSUM_BLOCK), 1)
    tri = jnp.where(col <= row, 1.0, 0.0).astype(F32)

    def body(i, carry):
        sl = pl.ds(pl.multiple_of(i * CUMSUM_BLOCK, CUMSUM_BLOCK), CUMSUM_BLOCK)
        z = f_ref[sl, :] + b_ref[...]
        logf = jnp.minimum(z, 0.0) - jnp.log1p(jnp.exp(-jnp.abs(z)))
        cs = jnp.dot(tri, logf, preferred_element_type=F32, precision=HIGHEST) + carry
        c_ref[sl, :] = cs
        return cs[CUMSUM_BLOCK - 1:CUMSUM_BLOCK, :]

    lax.fori_loop(0, nblk, body, jnp.zeros((1, LANES), F32))


def logf_cumsum(small, bias_row, B, S):
    T = B * S
    return pl.pallas_call(
        functools.partial(_logf_cumsum_kernel, nblk=S // CUMSUM_BLOCK),
        grid=(B,),
        in_specs=[pl.BlockSpec((S, LANES), lambda b: (b, 0)),
                  pl.BlockSpec((1, LANES), lambda b: (0, 0))],
        out_specs=pl.BlockSpec((S, LANES), lambda b: (b, 0)),
        out_shape=jax.ShapeDtypeStruct((T, LANES), F32),
        compiler_params=_cparams(("parallel",)),
        name="logf_cumsum",
    )(small, bias_row)


FLASH_ROW_CHUNK = 128


LOG2E = math.log2(math.e)


def _lane_tile(x, n):
    return jnp.concatenate([x] * n, axis=1)


def _flash_update(s2, v, m_ref, l_ref, acc_ref, row_shift=None):
    m_old = m_ref[...]
    m_cur = jnp.max(s2, axis=-1, keepdims=True)
    if row_shift is not None:
        m_cur = m_cur + row_shift
    m_new = jnp.maximum(m_old, m_cur)
    m_safe = jnp.where(m_new == NEG_INF, 0.0, m_new)
    alpha = jnp.exp2(m_old - m_safe)
    origin = m_safe if row_shift is None else m_safe - row_shift
    p = jnp.exp2(s2 - _lane_tile(origin, s2.shape[1] // LANES))
    l_ref[...] = alpha * l_ref[...] + jnp.sum(p, axis=-1, keepdims=True)
    acc_ref[...] = alpha * acc_ref[...] + jnp.dot(p.astype(BF16), v, preferred_element_type=F32)
    m_ref[...] = m_new


def _flash_init(m_ref, l_ref, acc_ref):
    m_ref[...] = jnp.full(m_ref.shape, NEG_INF, F32)
    l_ref[...] = jnp.zeros(l_ref.shape, F32)
    acc_ref[...] = jnp.zeros(acc_ref.shape, F32)


def _flash_result(l_ref, acc_ref):
    l = l_ref[...]
    return acc_ref[...] / jnp.where(l > 0.0, l, 1.0)


def _qk(q, k):
    return lax.dot_general(q, k, (((1,), (1,)), ((), ())), preferred_element_type=F32)


def _fox_kernel(qi_ref, ki_ref, q_ref, k_ref, v_ref, cq_ref, ck_ref, o_ref, m_ref, l_ref, acc_ref, cqb_ref,
                *, tq, tk, rc, nb):
    p = pl.program_id(1)
    qi, ki = qi_ref[p], ki_ref[p]

    @pl.when(ki == 0)
    def _():
        _flash_init(m_ref, l_ref, acc_ref)
        lane = lax.broadcasted_iota(jnp.int32, cq_ref.shape, 2)
        mine = jnp.sum(jnp.where(lane == SMALL_FOX_F + pl.program_id(0), cq_ref[...], 0.0),
                       axis=-1, keepdims=True)
        cqb_ref[...] = jnp.broadcast_to(mine * LOG2E, cqb_ref.shape)

    chunks = [pl.ds(c * rc, rc) for c in range(tq // rc)]

    def tile(causal_mask):
        widths = [min(tk, (c + 1) * rc) if causal_mask and tq == tk else tk for c in range(tq // rc)]

        def products(b):
            return [_qk(q_ref[b, rows, :], k_ref[b, 0:w, :]) for rows, w in zip(chunks, widths)]

        def softmax_pv(b, qk):
            ck = ck_ref[b] * LOG2E
            for c, (rows, w) in enumerate(zip(chunks, widths)):
                s = qk[c] * (ATTN_SCALE * LOG2E) - ck[:, 0:w]
                if causal_mask:
                    qpos = qi * tq + c * rc + lax.broadcasted_iota(jnp.int32, (rc, w), 0)
                    kpos = ki * tk + lax.broadcasted_iota(jnp.int32, (rc, w), 1)
                    s = jnp.where(kpos <= qpos, s, NEG_INF)
                _flash_update(s, v_ref[b, 0:w, :], m_ref.at[b, rows], l_ref.at[b, rows], acc_ref.at[b, rows],
                              row_shift=cqb_ref[b, rows, :])

        ahead = products(0)
        for b in range(nb):
            qk = ahead
            if b + 1 < nb:
                ahead = products(b + 1)
            softmax_pv(b, qk)

    crosses_diagonal = (ki + 1) * tk - 1 > qi * tq

    @pl.when(crosses_diagonal)
    def _():
        tile(True)

    @pl.when(jnp.logical_not(crosses_diagonal))
    def _():
        tile(False)

    @pl.when(ki == ((qi + 1) * tq - 1) // tk)
    def _():
        o_ref[...] = _flash_result(l_ref, acc_ref).astype(o_ref.dtype)


def _causal_pairs(nq, tq, tk):
    qs, ks = [], []
    for qi in range(nq):
        for ki in range(((qi + 1) * tq - 1) // tk + 1):
            qs.append(qi)
            ks.append(ki)
    return jnp.asarray(qs, jnp.int32), jnp.asarray(ks, jnp.int32)


def fox_attention(proj, c_packed, c_row, B, S, *, tq=512, tk=512):
    tq, tk = min(tq, S), min(tk, S)
    H = FOX_HEADS
    T = B * S
    qis, kis = _causal_pairs(S // tq, tq, tk)
    proj4 = proj.reshape(proj.shape[0], B, S, LANES)
    grid_spec = pltpu.PrefetchScalarGridSpec(
        num_scalar_prefetch=2,
        grid=(H, int(qis.shape[0])),
        in_specs=[
            pl.BlockSpec((None, B, tq, LANES), lambda h, p, qi, ki: (CB_FOX + h, 0, qi[p], 0)),
            pl.BlockSpec((None, B, tk, LANES), lambda h, p, qi, ki: (CB_FOX + H + h, 0, ki[p], 0)),
            pl.BlockSpec((None, B, tk, LANES), lambda h, p, qi, ki: (CB_FOX + 2 * H + h, 0, ki[p], 0)),
            pl.BlockSpec((B, tq, LANES), lambda h, p, qi, ki: (0, qi[p], 0)),
            pl.BlockSpec((None, B, 1, tk), lambda h, p, qi, ki: (h, 0, 0, ki[p])),
        ],
        out_specs=pl.BlockSpec((None, B, tq, LANES), lambda h, p, qi, ki: (h, 0, qi[p], 0)),
        scratch_shapes=[pltpu.VMEM((B, tq, LANES), F32)] * 4,
    )
    out = pl.pallas_call(
        functools.partial(_fox_kernel, tq=tq, tk=tk, rc=min(FLASH_ROW_CHUNK, tq), nb=B),
        grid_spec=grid_spec,
        out_shape=jax.ShapeDtypeStruct((H, B, S, LANES), BF16),
        compiler_params=_cparams(("parallel", "arbitrary")),
        name="fox_attention",
    )(qis, kis, proj4, proj4, proj4, c_packed.reshape(B, S, LANES), c_row.reshape(H, B, 1, S))
    return out.reshape(H, T, LANES)


DIL_ROWS_PER_STEP = 512


def _dil_kernel(q_ref, kp_ref, kc_ref, vp_ref, vc_ref, bias_ref, o_ref, lse_ref, *, dil, tu, nsub, hp):
    ui = pl.program_id(2)
    kcol = lax.broadcasted_iota(jnp.int32, (tu, 2 * tu), 1)
    first_ok = jnp.logical_or(ui > 0, kcol >= tu)
    units = [(c, slice(r * LANES, (r + 1) * LANES)) for c in range(nsub) for r in range(dil)]

    def window(prev_ref, cur_ref, h, c, sl):
        if c == 0:
            return jnp.concatenate([prev_ref[h, :, sl], cur_ref[h, 0:tu, sl]], axis=0)
        return cur_ref[h, (c - 1) * tu:(c + 1) * tu, sl]

    def head(h, carry):
        bias = bias_ref[h]
        bias_first = jnp.where(first_ok, bias, NEG_INF)
        qk = [_qk(q_ref[h, c * tu:(c + 1) * tu, sl], window(kp_ref, kc_ref, h, c, sl)) for c, sl in units]
        for i, (c, sl) in enumerate(units):
            rows = slice(c * tu, (c + 1) * tu)
            s = qk[i] * ATTN_SCALE + (bias_first if c == 0 else bias)
            m = jnp.max(s, axis=-1, keepdims=True)
            e = jnp.exp(s - m)
            den = jnp.sum(e, axis=-1, keepdims=True)
            o = jnp.dot(e.astype(BF16), window(vp_ref, vc_ref, h, c, sl), preferred_element_type=F32) / den
            o_ref[h, rows, sl] = o.astype(o_ref.dtype)
            lse_ref[h, rows, sl] = jnp.broadcast_to(m + jnp.log(den), (tu, LANES))
        return carry

    lax.fori_loop(0, hp, head, 0)


def dilated_group(view, c0, bias_tbl, group, B, S):
    dil = DIL_GROUPS[group][1]
    tu = DIL_TAPS
    Hg = DIL_HEADS_PER_GROUP
    T = B * S
    rows = min(DIL_ROWS_PER_STEP, S // dil)
    nsub = rows // tu
    nstep = S // dil // rows
    hp = max(1, Hg // max(1, dil // 4))
    cq, ck, cv = c0 // hp, (c0 + Hg) // hp, (c0 + 2 * Hg) // hp
    cur_blk = (hp, rows, dil * LANES)
    prev_blk = (hp, tu, dil * LANES)
    prev = lambda c: (lambda b, h, u: (c + h, jnp.maximum((b * nstep + u) * nsub - 1, 0), 0))
    cur = lambda c: (lambda b, h, u: (c + h, b * nstep + u, 0))
    o, lse = pl.pallas_call(
        functools.partial(_dil_kernel, dil=dil, tu=tu, nsub=nsub, hp=hp),
        grid=(B, Hg // hp, nstep),
        in_specs=[pl.BlockSpec(cur_blk, cur(cq)),
                  pl.BlockSpec(prev_blk, prev(ck)), pl.BlockSpec(cur_blk, cur(ck)),
                  pl.BlockSpec(prev_blk, prev(cv)), pl.BlockSpec(cur_blk, cur(cv)),
                  pl.BlockSpec((hp, tu, 2 * tu), lambda b, h, u: (h, 0, 0))],
        out_specs=[pl.BlockSpec(cur_blk, cur(0)), pl.BlockSpec(cur_blk, cur(0))],
        out_shape=[jax.ShapeDtypeStruct((Hg, T // dil, dil * LANES), BF16),
                   jax.ShapeDtypeStruct((Hg, T // dil, dil * LANES), F32)],
        compiler_params=_cparams(("parallel", "parallel", "arbitrary")),
        name=f"dilated_group{group}",
    )(view, view, view, view, view, bias_tbl)
    return o, lse


def dilated_bias_tables(rel_bias_b):
    tu = DIL_TAPS
    tables = []
    for g, (_, dil) in enumerate(DIL_GROUPS):
        heads = rel_bias_b[:, g * DIL_HEADS_PER_GROUP:(g + 1) * DIL_HEADS_PER_GROUP]

        def fn(taps, heads=heads, dil=dil):
            valid = (taps >= 0) & (taps <= DIL_TAPS)
            vals = heads[_t5_bucket_np(np.where(valid, taps, 0) * dil)].T
            return jnp.where(valid[None, :], vals, NEG_INF)

        tables.append(_toeplitz_blocks(fn, [tu], tu, 2 * tu)[0])
    return tables


def _compress_kernel(x_ref, pelo_ref, pehi_ref, w1a_ref, w1b_ref, w2_ref, o_ref, *, nchunk):
    x = x_ref[...].astype(F32)
    u0 = jnp.dot((x + pelo_ref[...]).astype(BF16), w1a_ref[...], preferred_element_type=F32)
    u1 = jnp.dot((x + pehi_ref[...]).astype(BF16), w1b_ref[...], preferred_element_type=F32)
    pre = u0 + pltpu.roll(u1, nchunk - 1, 0)
    hid = jax.nn.gelu(pre)
    out = jnp.dot(hid.astype(BF16), w2_ref[...], preferred_element_type=F32)
    row = lax.broadcasted_iota(jnp.int32, out.shape, 0)
    o_ref[...] = jnp.where(row < nchunk - 1, out, 0.0).astype(o_ref.dtype)


def nsa_compress(proj, pe, w1, w2, B, S):
    nchunk = S // CMP_STRIDE
    half = CMP_STRIDE * HEAD_DIM
    Hkv = NSA_KV_HEADS
    x = proj[CB_NSAKV:CB_NSAKV + 2 * Hkv].reshape(2, Hkv, B, nchunk, half)
    pe_lo = pe[:, :CMP_STRIDE].reshape(2, 1, half)
    pe_hi = pe[:, CMP_STRIDE:].reshape(2, 1, half)
    return pl.pallas_call(
        functools.partial(_compress_kernel, nchunk=nchunk),
        grid=(2, B, Hkv),
        in_specs=[pl.BlockSpec((None, None, None, nchunk, half), lambda t, b, h: (t, h, b, 0, 0)),
                  pl.BlockSpec((None, 1, half), lambda t, b, h: (t, 0, 0)),
                  pl.BlockSpec((None, 1, half), lambda t, b, h: (t, 0, 0)),
                  pl.BlockSpec((None, half, CMP_HIDDEN), lambda t, b, h: (t, 0, 0)),
                  pl.BlockSpec((None, half, CMP_HIDDEN), lambda t, b, h: (t, 1, 0)),
                  pl.BlockSpec((None, CMP_HIDDEN, HEAD_DIM), lambda t, b, h: (t, 0, 0))],
        out_specs=pl.BlockSpec((None, None, None, nchunk, HEAD_DIM), lambda t, b, h: (t, b, h, 0, 0)),
        out_shape=jax.ShapeDtypeStruct((2, B, Hkv, nchunk, HEAD_DIM), BF16),
        compiler_params=_cparams(("parallel", "parallel", "parallel")),
        name="nsa_compress",
    )(x, pe_lo, pe_hi, w1, w1, w2)


def _masked_softmax(s, mask, axis):
    s = jnp.where(mask, s, NEG_INF)
    m = jnp.max(s, axis=axis, keepdims=True)
    m = jnp.where(m == NEG_INF, 0.0, m)
    e = jnp.exp(s - m)
    den = jnp.sum(e, axis=axis, keepdims=True)
    return e / jnp.where(den > 0.0, den, 1.0)


def _cmp_select_kernel(q_ref, kc_ref, vc_ref, mt_ref, o_ref, sel_ref, *, tq, nchunk, n_slc, n_sel):
    q0 = pl.program_id(2) * tq
    kc = kc_ref[...]
    vc = vc_ref[...]
    pos_c = q0 + lax.broadcasted_iota(jnp.int32, (nchunk, tq), 1)
    end_c = lax.broadcasted_iota(jnp.int32, (nchunk, tq), 0) * CMP_STRIDE + (CMP_BLOCK - 1)
    vis_c = end_c <= pos_c
    imp = jnp.zeros((nchunk, tq), F32)
    qk = [_qk(kc, q_ref[g]) for g in range(NSA_GQA)]
    for g in range(NSA_GQA):
        p = _masked_softmax(qk[g] * ATTN_SCALE, vis_c, 0)
        o = lax.dot_general(p.astype(BF16), vc, (((0,), (0,)), ((), ())), preferred_element_type=F32)
        o_ref[g] = o.astype(o_ref.dtype)
        imp = imp + p
    p_slc = jnp.dot(mt_ref[...], imp, preferred_element_type=F32, precision=HIGHEST)

    blk = lax.broadcasted_iota(jnp.int32, (n_slc, tq), 0)
    cur = (q0 + lax.broadcasted_iota(jnp.int32, (n_slc, tq), 1)) // SLC_BLOCK
    forced = (blk == 0) | (blk == cur) | (blk == cur - 1)
    allowed = blk <= cur
    score = jnp.where(forced, 1e30, jnp.where(allowed, p_slc, -1.0))
    chosen = jnp.zeros((n_slc, tq), F32)
    for _ in range(n_sel):
        top = jnp.max(score, axis=0, keepdims=True)
        first = jnp.min(jnp.where(score == top, blk, n_slc), axis=0, keepdims=True)
        hit = blk == first
        chosen = jnp.where(hit, 1.0, chosen)
        score = jnp.where(hit, -2.0, score)
    mask = jnp.concatenate([jnp.where(allowed, chosen, 0.0), jnp.zeros((LANES - n_slc, tq), F32)], axis=0)
    sel_ref[...] = mask.T.astype(sel_ref.dtype)


def nsa_cmp_select(proj, kvc, B, S, *, tq=1024):
    tq = min(tq, S)
    nq = S // tq
    T = B * S
    nchunk = S // CMP_STRIDE
    n_slc = S // SLC_BLOCK
    if n_slc > LANES:
        raise ValueError("selection mask is packed into one lane width: needs S <= 64*128")
    n_sel = min(SLC_COUNT, n_slc)
    ratio, n_inner = SLC_BLOCK // CMP_STRIDE, CMP_BLOCK // CMP_STRIDE
    mt = np.zeros((n_slc, nchunk), np.float32)
    for j in range(n_slc):
        for m in range(ratio):
            for n in range(n_inner):
                c = ratio * j + m - n
                if 0 <= c < nchunk - 1:
                    mt[j, c] += 1.0
    G = NSA_GQA
    kv_spec = lambda t: pl.BlockSpec((None, None, None, nchunk, HEAD_DIM), lambda b, h, i: (t, b, h, 0, 0))
    return pl.pallas_call(
        functools.partial(_cmp_select_kernel, tq=tq, nchunk=nchunk, n_slc=n_slc, n_sel=n_sel),
        grid=(B, NSA_KV_HEADS, nq),
        in_specs=[pl.BlockSpec((G, tq, LANES), lambda b, h, i: (CB_NSAQ // G + h, b * nq + i, 0)),
                  kv_spec(0), kv_spec(1),
                  pl.BlockSpec((n_slc, nchunk), lambda b, h, i: (0, 0))],
        out_specs=[pl.BlockSpec((G, tq, LANES), lambda b, h, i: (h, b * nq + i, 0)),
                   pl.BlockSpec((None, None, tq, LANES), lambda b, h, i: (b, h, i, 0))],
        out_shape=[jax.ShapeDtypeStruct((NSA_Q_HEADS, T, LANES), BF16),
                   jax.ShapeDtypeStruct((B, NSA_KV_HEADS, S, LANES), BF16)],
        compiler_params=_cparams(("parallel", "parallel", "arbitrary")),
        name="nsa_cmp_select",
    )(proj, kvc, kvc, jnp.asarray(mt))


def _slc_kernel(qi_ref, ki_ref, q_ref, k_ref, v_ref, sel_ref, bias_ref, o_ref, m_ref, l_ref, acc_ref,
                *, tq, tk, nb):
    p = pl.program_id(1)
    qi, ki = qi_ref[p], ki_ref[p]

    @pl.when(ki == 0)
    def _():
        _flash_init(m_ref, l_ref, acc_ref)

    blk_of_key = ki * (tk // SLC_BLOCK) + lax.broadcasted_iota(jnp.int32, (LANES, tk), 1) // SLC_BLOCK
    expand = jnp.where(lax.broadcasted_iota(jnp.int32, (LANES, tk), 0) == blk_of_key, 1.0, 0.0).astype(BF16)

    def step(b, causal_mask):
        picked = jnp.dot(sel_ref[b], expand, preferred_element_type=F32)
        keep = picked > 0.5
        if causal_mask:
            qpos = qi * tq + lax.broadcasted_iota(jnp.int32, (tq, tk), 0)
            kpos = ki * tk + lax.broadcasted_iota(jnp.int32, (tq, tk), 1)
            keep = jnp.where(kpos <= qpos, picked, 0.0) > 0.5
        k, v = k_ref[b], v_ref[b]
        qk = [_qk(q_ref[g, b], k) for g in range(NSA_GQA)]
        for g in range(NSA_GQA):
            s = qk[g] * (ATTN_SCALE * LOG2E) + bias_ref[g]
            s = jnp.where(keep, s, NEG_INF)
            _flash_update(s, v, m_ref.at[g, b], l_ref.at[g, b], acc_ref.at[g, b])

    crosses_diagonal = (ki + 1) * tk - 1 > qi * tq

    @pl.when(crosses_diagonal)
    def _():
        _for_range(0, nb, lambda b: step(b, True))

    @pl.when(jnp.logical_not(crosses_diagonal))
    def _():
        _for_range(0, nb, lambda b: step(b, False))

    @pl.when(ki == ((qi + 1) * tq - 1) // tk)
    def _():
        o_ref[...] = _flash_result(l_ref, acc_ref).astype(o_ref.dtype)


def slc_bias_table(rel_bias_c, S, tq, tk):
    buckets = _t5_bucket_np(np.arange(S + tk))
    not_last = np.nonzero(buckets != REL_BUCKETS - 1)[0]
    far_start = int(not_last[-1]) + 1 if not_last.size else 0
    n_delta = min(S // tq, -(-(far_start + tk - 1) // tq) + 1)

    def fn(d):
        return rel_bias_c[_t5_bucket_np(d)].T * LOG2E

    return _toeplitz_blocks(fn, [dl * tq for dl in range(n_delta)], tq, tk)


def nsa_selected(proj, sel, bias_tbl, B, S, *, tq=256, tk=512):
    tq, tk = min(tq, S), min(tk, S)
    T = B * S
    G = NSA_GQA
    ck = CB_NSAKV + (1 * 2 + 0) * NSA_KV_HEADS
    cv = CB_NSAKV + (1 * 2 + 1) * NSA_KV_HEADS
    qis, kis = _causal_pairs(S // tq, tq, tk)
    n_delta = bias_tbl.shape[0]
    proj4 = proj.reshape(proj.shape[0], B, S, LANES)
    grid_spec = pltpu.PrefetchScalarGridSpec(
        num_scalar_prefetch=2,
        grid=(NSA_KV_HEADS, int(qis.shape[0])),
        in_specs=[
            pl.BlockSpec((G, B, tq, LANES), lambda h, p, qi, ki: (CB_NSAQ // G + h, 0, qi[p], 0)),
            pl.BlockSpec((None, B, tk, LANES), lambda h, p, qi, ki: (ck + h, 0, ki[p], 0)),
            pl.BlockSpec((None, B, tk, LANES), lambda h, p, qi, ki: (cv + h, 0, ki[p], 0)),
            pl.BlockSpec((B, None, tq, LANES), lambda h, p, qi, ki: (0, h, qi[p], 0)),
            pl.BlockSpec((None, G, tq, tk),
                         lambda h, p, qi, ki: (jnp.minimum(qi[p] - ki[p] * (tk // tq), n_delta - 1), h, 0, 0)),
        ],
        out_specs=pl.BlockSpec((G, B, tq, LANES), lambda h, p, qi, ki: (h, 0, qi[p], 0)),
        scratch_shapes=[pltpu.VMEM((G, B, tq, LANES), F32)] * 3,
    )
    out = pl.pallas_call(
        functools.partial(_slc_kernel, tq=tq, tk=tk, nb=B),
        grid_spec=grid_spec,
        out_shape=jax.ShapeDtypeStruct((NSA_Q_HEADS, B, S, LANES), BF16),
        compiler_params=_cparams(("parallel", "arbitrary")),
        name="nsa_selected",
    )(qis, kis, proj4, proj4, proj4, sel, bias_tbl)
    return out.reshape(NSA_Q_HEADS, T, LANES)


def _win_kernel(q_ref, k0_ref, k1_ref, k2_ref, v0_ref, v1_ref, v2_ref, bias_ref, o_ref, *, tq, nkb):
    qi = pl.program_id(2)
    k = jnp.concatenate([r[...] for r in (k0_ref, k1_ref, k2_ref)][-nkb:], axis=0)
    v = jnp.concatenate([r[...] for r in (v0_ref, v1_ref, v2_ref)][-nkb:], axis=0)
    kpos = (qi - (nkb - 1)) * tq + lax.broadcasted_iota(jnp.int32, (tq, nkb * tq), 1)
    qk = [_qk(q_ref[g], k) for g in range(NSA_GQA)]
    for g in range(NSA_GQA):
        s = qk[g] * ATTN_SCALE + bias_ref[g]
        p = _masked_softmax(s, kpos >= 0, -1)
        o_ref[g] = jnp.dot(p.astype(BF16), v, preferred_element_type=F32).astype(o_ref.dtype)


def win_bias_table(rel_bias_c, tq, nkb):
    def fn(dist):
        valid = (dist >= 0) & (dist < WIN)
        return jnp.where(valid[None, :], rel_bias_c[_t5_bucket_np(dist)].T, NEG_INF)

    return _toeplitz_blocks(fn, [(nkb - 1) * tq], tq, nkb * tq)[0]


def nsa_window(proj, bias_tbl, B, S, *, tq=256):
    tq = min(tq, S)
    nq = S // tq
    T = B * S
    G = NSA_GQA
    nkb = min(WIN // tq + 1, 3)
    ck = CB_NSAKV + (2 * 2 + 0) * NSA_KV_HEADS
    cv = CB_NSAKV + (2 * 2 + 1) * NSA_KV_HEADS
    kv = lambda c, back: pl.BlockSpec(
        (None, tq, LANES), lambda b, h, i: (c + h, b * nq + jnp.maximum(i - back, 0), 0))
    return pl.pallas_call(
        functools.partial(_win_kernel, tq=tq, nkb=nkb),
        grid=(B, NSA_KV_HEADS, nq),
        in_specs=[pl.BlockSpec((G, tq, LANES), lambda b, h, i: (CB_NSAQ // G + h, b * nq + i, 0)),
                  kv(ck, 2), kv(ck, 1), kv(ck, 0), kv(cv, 2), kv(cv, 1), kv(cv, 0),
                  pl.BlockSpec((G, tq, nkb * tq), lambda b, h, i: (h, 0, 0))],
        out_specs=pl.BlockSpec((G, tq, LANES), lambda b, h, i: (h, b * nq + i, 0)),
        out_shape=jax.ShapeDtypeStruct((NSA_Q_HEADS, T, LANES), BF16),
        compiler_params=_cparams(("parallel", "parallel", "arbitrary")),
        name="nsa_window",
    )(proj, proj, proj, proj, proj, proj, proj, bias_tbl)


def _sigmoid(x):
    return 1.0 / (1.0 + jnp.exp(-x))


def _merge_kernel(fox_ref, d0_ref, d1_ref, d2_ref, l0_ref, l1_ref, l2_ref, cmp_ref, slc_ref, win_ref, small_ref,
                  g0_ref, g1_ref, g2_ref, wa_ref, wb_ref, wc_ref, wo_ref, x_ref, o_ref, nat_ref, *, ncb):
    tm = o_ref.shape[0]

    def natural(ref, h, dil, slot):
        if dil == 1:
            return ref[h].astype(F32)
        for r in range(dil):
            nat_ref[slot, pl.ds(r, tm // dil, stride=dil), :] = ref[h, :, r * LANES:(r + 1) * LANES].astype(F32)
        return nat_ref[slot]

    ya = jnp.concatenate([fox_ref[h] for h in range(FOX_HEADS)], axis=1)
    yb = []
    for h in range(DIL_HEADS_PER_GROUP):
        outs, lse = [], []
        for g, (o_g, l_g) in enumerate(((d0_ref, l0_ref), (d1_ref, l1_ref), (d2_ref, l2_ref))):
            dil = DIL_GROUPS[g][1]
            slot = (h * len(DIL_GROUPS) + g) * 2
            outs.append(natural(o_g, h, dil, slot))
            lse.append(natural(l_g, h, dil, slot + 1))
        top = jnp.maximum(jnp.maximum(lse[0], lse[1]), lse[2])
        w = [jnp.exp(x - top) for x in lse]
        tot = w[0] + w[1] + w[2]
        y = sum((w[g] / tot) * outs[g] for g in range(len(DIL_GROUPS)))
        yb.append(y.astype(BF16))
    gates = _sigmoid(small_ref[...])
    yc = []
    for h in range(NSA_Q_HEADS):
        y = jnp.zeros(cmp_ref.shape[1:], F32)
        for br, ref in enumerate((cmp_ref, slc_ref, win_ref)):
            col = SMALL_NSA_G + br * NSA_Q_HEADS + h
            y = y + gates[:, col:col + 1] * ref[h].astype(F32)
        yc.append(y.astype(BF16))

    def gate(ref):
        return _sigmoid(jnp.concatenate([ref[c] for c in range(ncb)], axis=1).astype(F32))

    merged = (gate(g0_ref) * jnp.dot(ya, wa_ref[...], preferred_element_type=F32)
              + gate(g1_ref) * jnp.dot(jnp.concatenate(yb, axis=1), wb_ref[...], preferred_element_type=F32)
              + gate(g2_ref) * jnp.dot(jnp.concatenate(yc, axis=1), wc_ref[...], preferred_element_type=F32))
    o_ref[...] = x_ref[...] + jnp.dot(merged.astype(BF16), wo_ref[...], preferred_element_type=F32)


def merge_branches(x, proj, small, fox_o, dil_o, dil_lse, cmp_o, slc_o, win_o, wa, wb, wc, w_out, *, tm=256):
    T, D = x.shape
    tm = min(tm, T)
    ncb = D // LANES
    Hg = DIL_HEADS_PER_GROUP
    heads = lambda n: pl.BlockSpec((n, tm, LANES), lambda i: (0, i, 0))
    gate = lambda b: pl.BlockSpec((ncb, tm, LANES), lambda i: (CB_MERGE // ncb + b, i, 0))
    dil = [pl.BlockSpec((Hg, tm // d, d * LANES), lambda i: (0, i, 0)) for _, d in DIL_GROUPS]
    wspec = lambda w: pl.BlockSpec(w.shape, lambda i: (0, 0), pipeline_mode=pl.Buffered(1))
    return pl.pallas_call(
        functools.partial(_merge_kernel, ncb=ncb),
        grid=(T // tm,),
        in_specs=[heads(FOX_HEADS), *dil, *dil, heads(NSA_Q_HEADS), heads(NSA_Q_HEADS), heads(NSA_Q_HEADS),
                  pl.BlockSpec((tm, LANES), lambda i: (i, 0)),
                  gate(0), gate(1), gate(2), wspec(wa), wspec(wb), wspec(wc), wspec(w_out),
                  pl.BlockSpec((tm, D), lambda i: (i, 0))],
        out_specs=pl.BlockSpec((tm, D), lambda i: (i, 0)),
        out_shape=jax.ShapeDtypeStruct((T, D), F32),
        scratch_shapes=[pltpu.VMEM((2 * Hg * len(DIL_GROUPS), tm, LANES), F32)],
        compiler_params=_cparams(("parallel",)),
        name="merge_branches",
    )(fox_o, *dil_o, *dil_lse, cmp_o, slc_o, win_o, small, proj, proj, proj, wa, wb, wc, w_out, x)


def _silu(x):
    return x * _sigmoid(x)


def _swiglu_partial(h, wg_ref, wu_ref, wd_ref):
    act = _silu(jnp.dot(h, wg_ref[...].astype(BF16), preferred_element_type=F32)) * \
        jnp.dot(h, wu_ref[...].astype(BF16), preferred_element_type=F32)
    return jnp.dot(act.astype(BF16), wd_ref[...].astype(BF16), preferred_element_type=F32)


def _ffn_kernel(x_ref, g_ref, wg_ref, wu_ref, wd_ref, gf_ref, o_ref, h_ref, *, final_norm):
    f = pl.program_id(1)

    @pl.when(f == 0)
    def _():
        h_ref[...] = _rms_rows(x_ref[...], g_ref[...]).astype(BF16)
        o_ref[...] = x_ref[...]

    o_ref[...] += _swiglu_partial(h_ref[...], wg_ref, wu_ref, wd_ref)

    if final_norm:
        @pl.when(f == pl.num_programs(1) - 1)
        def _():
            o_ref[...] = _rms_rows(o_ref[...], gf_ref[...])


def ffn(x, g, wg, wu, wd, g_final, *, final_norm, tm=1024, tf=256):
    T, D = x.shape
    F = wg.shape[1]
    tm, tf = min(tm, T), min(tf, F)
    return pl.pallas_call(
        functools.partial(_ffn_kernel, final_norm=final_norm),
        grid=(T // tm, F // tf),
        in_specs=[pl.BlockSpec((tm, D), lambda i, f: (i, 0)),
                  pl.BlockSpec((1, D), lambda i, f: (0, 0)),
                  pl.BlockSpec((D, tf), lambda i, f: (0, f)),
                  pl.BlockSpec((D, tf), lambda i, f: (0, f)),
                  pl.BlockSpec((tf, D), lambda i, f: (f, 0)),
                  pl.BlockSpec((1, D), lambda i, f: (0, 0))],
        out_specs=pl.BlockSpec((tm, D), lambda i, f: (i, 0)),
        out_shape=jax.ShapeDtypeStruct((T, D), F32),
        scratch_shapes=[pltpu.VMEM((tm, D), BF16)],
        compiler_params=_cparams(("parallel", "arbitrary")),
        name="ffn",
    )(x, g, wg, wu, wd, g_final)


def _route_kernel(x_ref, g_ref, r_ref, o_ref):
    h = _rms_rows(x_ref[...], g_ref[...])
    logits = jnp.dot(h, r_ref[...], preferred_element_type=F32, precision=HIGHEST)
    lane = lax.broadcasted_iota(jnp.int32, logits.shape, 1)
    logits = jnp.where(lane < N_EXPERTS, logits, NEG_INF)
    v1 = jnp.max(logits, axis=1, keepdims=True)
    i1 = jnp.min(jnp.where(logits == v1, lane, LANES), axis=1, keepdims=True)
    rest = jnp.where(lane == i1, NEG_INF, logits)
    v2 = jnp.max(rest, axis=1, keepdims=True)
    i2 = jnp.min(jnp.where(rest == v2, lane, LANES), axis=1, keepdims=True)
    e2 = jnp.exp(v2 - v1)
    den = 1.0 + e2
    rec = jnp.where(lane == ROUTE_CHOICE + i1, 1.0, jnp.where(lane == ROUTE_CHOICE + i2, 2.0, 0.0))
    rec = jnp.where(lane == ROUTE_WEIGHT, 1.0 / den, jnp.where(lane == ROUTE_WEIGHT + 1, e2 / den, rec))
    o_ref[...] = rec


def moe_route(x, g, router_pad, *, tm=1024):
    T, D = x.shape
    tm = min(tm, T)
    return pl.pallas_call(
        _route_kernel,
        grid=(T // tm,),
        in_specs=[pl.BlockSpec((tm, D), lambda i: (i, 0)),
                  pl.BlockSpec((1, D), lambda i: (0, 0)),
                  pl.BlockSpec((D, LANES), lambda i: (0, 0))],
        out_specs=pl.BlockSpec((tm, LANES), lambda i: (i, 0)),
        out_shape=jax.ShapeDtypeStruct((T, LANES), F32),
        compiler_params=_cparams(("parallel",)),
        name="moe_route",
    )(x, g, router_pad)


def _moe_plan(route, T, tm):
    E = N_EXPERTS
    n_tiles = (TOP_K * T + E * tm) // tm
    choice = route[:, ROUTE_CHOICE:ROUTE_CHOICE + E]
    seli = (choice > 0.5).astype(jnp.int32)
    counts = jnp.sum(seli, axis=0)
    padded = ((counts + tm - 1) // tm) * tm
    seg_end = jnp.cumsum(padded)
    seg_start = seg_end - padded
    slot = seg_start[None, :] + jnp.cumsum(seli, axis=0) - 1
    slot_of = jnp.stack([jnp.sum(jnp.where(choice == k + 1.0, slot, 0), axis=1) for k in range(TOP_K)])
    n_used = (seg_end[-1] // tm).astype(jnp.int32)
    tile_start = jnp.arange(n_tiles, dtype=jnp.int32) * tm
    tile_expert = jnp.sum((seg_end[None, :] <= tile_start[:, None]).astype(jnp.int32), axis=1)
    tile_expert = jnp.minimum(tile_expert, E - 1)
    last_expert = jnp.take(tile_expert, n_used - 1)
    tile_expert = jnp.where(jnp.arange(n_tiles) < n_used, tile_expert, last_expert)
    pad_lo = (seg_start + counts).astype(jnp.int32)
    return (tile_expert.astype(jnp.int32), n_used.reshape(1), slot_of.reshape(-1).astype(jnp.int32),
            pad_lo, seg_end.astype(jnp.int32))


ROW_DMA_UNROLL = 8


def _for_range(lo, hi, fn, unroll=None):
    def body(r, carry):
        fn(r)
        return carry
    lax.fori_loop(lo, hi, body, 0, unroll=unroll)


def _moe_dispatch_kernel(slot_ref, lo_ref, hi_ref, nt_ref, x_ref, xs_hbm, zero_ref, sem, zsem,
                         *, tc, T, chunks_per_tile, n_chunks):
    i = pl.program_id(0)

    def token_row(k, r):
        s = slot_ref[k * T + i * tc + r]
        return pltpu.make_async_copy(x_ref.at[pl.ds(r, 1)], xs_hbm.at[pl.ds(s, 1)], sem.at[0])

    def zero_row(s):
        return pltpu.make_async_copy(zero_ref.at[pl.ds(0, 1)], xs_hbm.at[pl.ds(s, 1)], zsem.at[0])

    def zero_chunk(c):
        rows = pl.ds(pl.multiple_of(c * MOE_ZERO_ROWS, MOE_ZERO_ROWS), MOE_ZERO_ROWS)
        return pltpu.make_async_copy(zero_ref, xs_hbm.at[rows], zsem.at[0])

    @pl.when(i == 0)
    def _():
        zero_ref[...] = jnp.zeros(zero_ref.shape, F32)
        for start in (True, False):
            for e in range(N_EXPERTS):
                _for_range(lo_ref[e], hi_ref[e], lambda s: zero_row(s).start() if start else zero_row(s).wait())
            _for_range(nt_ref[0] * chunks_per_tile, n_chunks,
                       lambda c: zero_chunk(c).start() if start else zero_chunk(c).wait())

    for k in range(TOP_K):
        _for_range(0, tc // ROW_DMA_UNROLL, lambda g: [
            token_row(k, g * ROW_DMA_UNROLL + j).start(priority=j % 2) for j in range(ROW_DMA_UNROLL)])
    for k in range(TOP_K):
        pltpu.make_async_copy(x_ref, xs_hbm.at[pl.ds(0, tc)], sem.at[0]).wait()


def moe_dispatch(x, plan, *, tm, tc=512):
    _, n_used, slot_of, pad_lo, pad_hi = plan
    T, D = x.shape
    tc = min(tc, T)
    n_slots = TOP_K * T + N_EXPERTS * tm
    grid_spec = pltpu.PrefetchScalarGridSpec(
        num_scalar_prefetch=4,
        grid=(T // tc,),
        in_specs=[pl.BlockSpec((tc, D), lambda i, s, lo, hi, nt: (i, 0))],
        out_specs=pl.BlockSpec(memory_space=pl.ANY),
        scratch_shapes=[pltpu.VMEM((MOE_ZERO_ROWS, D), F32),
                        pltpu.SemaphoreType.DMA((1,)), pltpu.SemaphoreType.DMA((1,))],
    )
    return pl.pallas_call(
        functools.partial(_moe_dispatch_kernel, tc=tc, T=T, chunks_per_tile=tm // MOE_ZERO_ROWS,
                          n_chunks=n_slots // MOE_ZERO_ROWS),
        grid_spec=grid_spec,
        out_shape=jax.ShapeDtypeStruct((n_slots, D), F32),
        compiler_params=_cparams(("arbitrary",)),
        name="moe_dispatch",
    )(slot_of, pad_lo, pad_hi, n_used, x)


def _moe_ffn_kernel(te_ref, nt_ref, xs_ref, g_ref, wg_ref, wu_ref, wd_ref, y_ref, h_ref):
    i, f = pl.program_id(0), pl.program_id(1)
    used = i < nt_ref[0]

    @pl.when(used & (f == 0))
    def _():
        h_ref[...] = _rms_rows(xs_ref[...], g_ref[...]).astype(BF16)
        y_ref[...] = _swiglu_partial(h_ref[...], wg_ref, wu_ref, wd_ref)

    @pl.when(used & (f > 0))
    def _():
        y_ref[...] += _swiglu_partial(h_ref[...], wg_ref, wu_ref, wd_ref)

    @pl.when(jnp.logical_not(used) & (f == 0))
    def _():
        y_ref[...] = jnp.zeros(y_ref.shape, F32)


def moe_experts(xs, g, plan, wg, wu, wd, *, tm, tf=1024):
    tile_expert, n_used = plan[0], plan[1]
    n_slots, D = xs.shape
    F = wg.shape[2]
    tf = min(tf, F)
    nf = F // tf

    def fidx(i, f, nt):
        return jnp.where(i < nt[0], f, nf - 1)

    grid_spec = pltpu.PrefetchScalarGridSpec(
        num_scalar_prefetch=2,
        grid=(n_slots // tm, nf),
        in_specs=[
            pl.BlockSpec((tm, D), lambda i, f, te, nt: (jnp.minimum(i, nt[0] - 1), 0)),
            pl.BlockSpec((1, D), lambda i, f, te, nt: (0, 0)),
            pl.BlockSpec((None, D, tf), lambda i, f, te, nt: (te[i], 0, fidx(i, f, nt))),
            pl.BlockSpec((None, D, tf), lambda i, f, te, nt: (te[i], 0, fidx(i, f, nt))),
            pl.BlockSpec((None, tf, D), lambda i, f, te, nt: (te[i], fidx(i, f, nt), 0)),
        ],
        out_specs=pl.BlockSpec((tm, D), lambda i, f, te, nt: (i, 0)),
        scratch_shapes=[pltpu.VMEM((tm, D), BF16)],
    )
    return pl.pallas_call(
        _moe_ffn_kernel,
        grid_spec=grid_spec,
        out_shape=jax.ShapeDtypeStruct((n_slots, D), F32),
        compiler_params=_cparams(("parallel", "arbitrary")),
        name="moe_experts",
    )(tile_expert, n_used, xs, g, wg, wu, wd)


def _moe_combine_kernel(slot_ref, x_ref, y_hbm, route_ref, gf_ref, o_ref, ybuf, sem, *, tc, T, final_norm):
    i = pl.program_id(0)

    def slot_row(k, r):
        s = slot_ref[k * T + i * tc + r]
        return pltpu.make_async_copy(y_hbm.at[pl.ds(s, 1)], ybuf.at[k, pl.ds(r, 1)], sem.at[0])

    for k in range(TOP_K):
        _for_range(0, tc // ROW_DMA_UNROLL, lambda g: [
            slot_row(k, g * ROW_DMA_UNROLL + j).start(priority=j % 2) for j in range(ROW_DMA_UNROLL)])
    for k in range(TOP_K):
        pltpu.make_async_copy(y_hbm.at[pl.ds(0, tc)], ybuf.at[k], sem.at[0]).wait()

    w = route_ref[...]
    out = x_ref[...]
    for k in range(TOP_K):
        out = out + w[:, ROUTE_WEIGHT + k:ROUTE_WEIGHT + k + 1] * ybuf[k]
    if final_norm:
        out = _rms_rows(out, gf_ref[...])
    o_ref[...] = out


def moe_combine(x, y, route, plan, g_final, *, final_norm, tc=256):
    T, D = x.shape
    tc = min(tc, T)
    grid_spec = pltpu.PrefetchScalarGridSpec(
        num_scalar_prefetch=1,
        grid=(T // tc,),
        in_specs=[pl.BlockSpec((tc, D), lambda i, s: (i, 0)),
                  pl.BlockSpec(memory_space=pl.ANY),
                  pl.BlockSpec((tc, LANES), lambda i, s: (i, 0)),
                  pl.BlockSpec((1, D), lambda i, s: (0, 0))],
        out_specs=pl.BlockSpec((tc, D), lambda i, s: (i, 0)),
        scratch_shapes=[pltpu.VMEM((TOP_K, tc, D), F32), pltpu.SemaphoreType.DMA((1,))],
    )
    return pl.pallas_call(
        functools.partial(_moe_combine_kernel, tc=tc, T=T, final_norm=final_norm),
        grid_spec=grid_spec,
        out_shape=jax.ShapeDtypeStruct((T, D), F32),
        compiler_params=_cparams(("arbitrary",)),
        name="moe_combine",
    )(plan[2], x, y, route, g_final)


def moe_block(x, g, router, wg, wu, wd, g_final, *, final_norm, tm=512):
    T, D = x.shape
    tm = min(tm, T)
    router_pad = jnp.zeros((D, LANES), F32).at[:, :N_EXPERTS].set(router)
    route = moe_route(x, g, router_pad)
    plan = _moe_plan(route, T, tm)
    xs = moe_dispatch(x, plan, tm=tm)
    y = moe_experts(xs, g, plan, wg.astype(BF16), wu.astype(BF16), wd, tm=tm)
    return moe_combine(x, y, route, plan, g_final, final_norm=final_norm)


def _split_w_in(w):
    D = w.shape[0]
    n_fox = 3 * FOX_HEADS * HEAD_DIM
    n_dil = 3 * DIL_HEADS * HEAD_DIM
    n_nsa = (NSA_Q_HEADS + NSA_BRANCHES * 2 * NSA_KV_HEADS) * HEAD_DIM
    n_g = NSA_BRANCHES * NSA_Q_HEADS
    a0 = n_fox
    a1 = a0 + FOX_HEADS
    a2 = a1 + n_dil
    a3 = a2 + n_nsa
    a4 = a3 + n_g
    group_cols = DIL_HEADS_PER_GROUP * HEAD_DIM
    dil = w[:, a1:a2].reshape(D, 3, len(DIL_GROUPS), group_cols)
    dil_sets = [dil[:, :, gi].reshape(D, 3 * group_cols).astype(BF16) for gi in range(len(DIL_GROUPS))]
    main = jnp.concatenate([w[:, a4:].astype(BF16), w[:, :a0].astype(BF16), dil_sets[0],
                            w[:, a2:a3].astype(BF16)], axis=1)
    small = jnp.concatenate([w[:, a0:a1], w[:, a3:a4],
                             jnp.zeros((D, LANES - FOX_HEADS - n_g), w.dtype)], axis=1).astype(BF16)
    return main, dil_sets[1:], small


def mixing_block(x, B, S, norm_g, w_in, forget_bias, cmp_pe, cmp_w1, cmp_w2, wa, wb, wc, w_out, tables,
                 cast=None):
    T, D = x.shape
    w_main, w_dil, w_small = _split_w_in(w_in)
    g = norm_g.reshape(1, D)
    proj, h, cast_bf16 = rms_proj(x, g, w_main, BF16, tm=1024, tn=1024, cast=cast)
    small = slab_proj(h, w_small, F32, tm=1024, tn=LANES)[0]
    dil_views = [proj_view(h, w, DIL_GROUPS[gi + 1][1], tm=1024, tn=w.shape[1]) for gi, w in enumerate(w_dil)]

    gate_lanes = slice(SMALL_FOX_F, SMALL_FOX_F + FOX_HEADS)
    bias_row = jnp.zeros((1, LANES), F32).at[0, gate_lanes].set(forget_bias)
    c = logf_cumsum(small, bias_row, B, S)
    c_row = c[:, gate_lanes].T.reshape(FOX_HEADS, 1, T)
    fox_o = fox_attention(proj, c, c_row, B, S)

    dil = [dilated_group(proj, CB_DIL, tables["dil"][0], 0, B, S)]
    dil += [dilated_group(v, 0, tables["dil"][gi + 1], gi + 1, B, S) for gi, v in enumerate(dil_views)]
    dil_o = [d[0] for d in dil]
    dil_lse = [d[1] for d in dil]

    kvc = nsa_compress(proj, cmp_pe, cmp_w1.astype(BF16), cmp_w2.astype(BF16), B, S)
    cmp_o, sel = nsa_cmp_select(proj, kvc, B, S)
    slc_o = nsa_selected(proj, sel, tables["slc"], B, S, tq=tables["slc_tq"], tk=tables["slc_tk"])
    win_o = nsa_window(proj, tables["win"], B, S, tq=tables["win_tq"])

    out = merge_branches(x, proj, small, fox_o, dil_o, dil_lse, cmp_o, slc_o, win_o,
                         wa.astype(BF16), wb.astype(BF16), wc.astype(BF16), w_out.astype(BF16))
    return out, cast_bf16


def bias_tables(rel_bias, S):
    slc_tq, slc_tk = min(256, S), min(512, S)
    win_tq = min(256, S)
    rel_c = rel_bias[:, DIL_HEADS:]
    return {
        "dil": dilated_bias_tables(rel_bias[:, :DIL_HEADS]),
        "slc": slc_bias_table(rel_c, S, slc_tq, slc_tk), "slc_tq": slc_tq, "slc_tk": slc_tk,
        "win": win_bias_table(rel_c, win_tq, min(WIN // win_tq + 1, 3)), "win_tq": win_tq,
    }


def kernel(x, rel_bias, norm_mix_g, norm_ffn_g, norm_final_g, w_in, fox_forget_bias, cmp_pe_k, cmp_w1_k, cmp_w2_k, cmp_pe_v, cmp_w1_v, cmp_w2_v, w_branch_a, w_branch_b, w_branch_c, w_out, ffn_w_gate, ffn_w_up, ffn_w_down, moe_router, moe_w_gate, moe_w_up, moe_w_down):
    B, S, D = x.shape
    if D != D_MODEL:
        raise ValueError(f"column-block layout is built for d_model={D_MODEL}, got {D}")
    T = B * S
    depth = w_in.shape[0]
    tables = bias_tables(rel_bias, S)
    g_final = norm_final_g.reshape(1, D)
    xt = x.reshape(T, D)
    side_cast = {}
    for l in range(1, depth, 2):
        side_cast[l - 1] = moe_w_gate[l // 2]
        side_cast[l] = moe_w_up[l // 2]
    rounded = {}
    for l in range(depth):
        xt, rounded[l] = mixing_block(
            xt, B, S, norm_mix_g[l], w_in[l], fox_forget_bias[l],
            jnp.stack([cmp_pe_k[l], cmp_pe_v[l]]), jnp.stack([cmp_w1_k[l], cmp_w1_v[l]]),
            jnp.stack([cmp_w2_k[l], cmp_w2_v[l]]),
            w_branch_a[l], w_branch_b[l], w_branch_c[l], w_out[l], tables, cast=side_cast.get(l))
        g = norm_ffn_g[l].reshape(1, D)
        last = l == depth - 1
        j = l // 2
        if l % 2 == 0:
            xt = ffn(xt, g, ffn_w_gate[j], ffn_w_up[j], ffn_w_down[j], g_final, final_norm=last)
        else:
            xt = moe_block(xt, g, moe_router[j], rounded[l - 1], rounded[l], moe_w_down[j], g_final,
                           final_norm=last)
    return xt.reshape(B, S, D)
```
